```python
import math
import jax, jax.numpy as jnp
from jax import lax
import numpy as np

D_MODEL = 1024
BATCH = 8
SEQ = 4096
DEPTH = 4

MEM_LEN = 256
EPS = 1e-6
NEG_INF = -1e30

CONV_DIM = D_MODEL
CONV_KERNEL = 31

SSD_INNER = 2 * D_MODEL
SSD_HEAD_DIM = 64
SSD_HEADS = SSD_INNER // SSD_HEAD_DIM
SSD_GROUPS = 4
SSD_STATE = 128
SSD_CONV = 4
SSD_CHUNK = 128
SSD_XBC = SSD_INNER + 2 * SSD_GROUPS * SSD_STATE

ATTN_HEADS = 16
ATTN_KV_HEADS = 4
ATTN_HEAD_DIM = 64
ATTN_DIM = ATTN_HEADS * ATTN_HEAD_DIM
ATTN_KV_DIM = ATTN_KV_HEADS * ATTN_HEAD_DIM
ATTN_WINDOW = 128
ATTN_BLOCK = 128

REL_BUCKETS = 32
REL_MAX_DIST = 128

XATTN_HEADS = 4
XATTN_HEAD_DIM = D_MODEL // XATTN_HEADS

N_BRANCH = 3
MLP_HIDDEN = 4 * D_MODEL

OFF_CONV = 0
OFF_Z = OFF_CONV + 2 * CONV_DIM
OFF_XBC = OFF_Z + SSD_INNER
OFF_DT = OFF_XBC + SSD_XBC
OFF_Q = OFF_DT + SSD_HEADS
OFF_K = OFF_Q + ATTN_DIM
OFF_V = OFF_K + ATTN_KV_DIM
OFF_GATE = OFF_V + ATTN_KV_DIM
IN_COLS = OFF_GATE + N_BRANCH * D_MODEL

kernel_name = "hybrid_conv_ssd_swa_gated_trunk"


def rms_norm(x, g):
    x32 = x.astype(jnp.float32)
    y = x32 * lax.rsqrt(jnp.mean(x32 * x32, axis=-1, keepdims=True) + EPS)
    return (y * g.astype(jnp.float32)).astype(x.dtype)


def layer_norm(x, g, b):
    x32 = x.astype(jnp.float32)
    mu = jnp.mean(x32, axis=-1, keepdims=True)
    xc = x32 - mu
    y = xc * lax.rsqrt(jnp.mean(xc * xc, axis=-1, keepdims=True) + EPS)
    return (y * g.astype(jnp.float32) + b.astype(jnp.float32)).astype(x.dtype)


def causal_depthwise_conv(x, w, b):
    k, c = w.shape
    y = lax.conv_general_dilated(
        x, w[:, None, :].astype(x.dtype), window_strides=(1,), padding=[(k - 1, 0)],
        dimension_numbers=("NWC", "WIO", "NWC"), feature_group_count=c)
    return y + b.astype(x.dtype)


def conformer_conv_branch(u, dw_w, dw_b, ln_g, ln_b):
    a, gate = jnp.split(u, 2, axis=-1)
    h = a * jax.nn.sigmoid(gate)
    h = causal_depthwise_conv(h, dw_w, dw_b)
    h = layer_norm(h, ln_g, ln_b)
    return jax.nn.silu(h)


def segsum_exp(a):
    t = a.shape[-1]
    cs = jnp.cumsum(a, axis=-1)
    diff = cs[..., :, None] - cs[..., None, :]
    mask = jnp.tril(jnp.ones((t, t), dtype=bool))
    return jnp.where(mask, jnp.exp(jnp.where(mask, diff, 0.0)), 0.0)


def ssd_chunked(x, dt, a, bm, cm):
    b, l, h, p = x.shape
    g, n = bm.shape[-2:]
    r = h // g
    q = SSD_CHUNK
    nc = l // q
    xdt = (x * dt[..., None].astype(x.dtype)).reshape(b, nc, q, g, r, p)
    bc = bm.reshape(b, nc, q, g, n)
    cc = cm.reshape(b, nc, q, g, n)
    da = (dt * a).reshape(b, nc, q, g, r).transpose(0, 1, 3, 4, 2)
    cs = jnp.cumsum(da, axis=-1)
    decay = segsum_exp(da).astype(x.dtype)
    cb = jnp.einsum("bcqgn,bcsgn->bcgqs", cc, bc)
    scores = cb[:, :, :, None] * decay
    y_diag = jnp.einsum("bcgrqs,bcsgrp->bcqgrp", scores, xdt)
    decay_to_end = jnp.exp(cs[..., -1:] - cs).astype(x.dtype)
    chunk_states = jnp.einsum("bcqgn,bcgrq,bcqgrp->bcgrpn", bc, decay_to_end, xdt)
    chunk_decay = jnp.exp(cs[..., -1])

    def step(state, inp):
        s_c, d_c = inp
        return state * d_c[..., None, None] + s_c, state

    init = jnp.zeros((b, g, r, p, n), jnp.float32)
    _, states_in = lax.scan(
        step, init,
        (jnp.moveaxis(chunk_states.astype(jnp.float32), 1, 0), jnp.moveaxis(chunk_decay, 1, 0)))
    states_in = jnp.moveaxis(states_in, 0, 1).astype(x.dtype)
    decay_from_start = jnp.exp(cs).astype(x.dtype)
    y_off = jnp.einsum("bcqgn,bcgrpn,bcgrq->bcqgrp", cc, states_in, decay_from_start)
    return (y_diag + y_off).reshape(b, l, h, p)


def ssd_branch(z, xbc, dt_raw, conv_w, conv_b, dt_bias, a_log, d_skip, norm_g):
    b, l, _ = xbc.shape
    xbc = jax.nn.silu(causal_depthwise_conv(xbc, conv_w, conv_b))
    xs = xbc[..., :SSD_INNER].reshape(b, l, SSD_HEADS, SSD_HEAD_DIM)
    bm = xbc[..., SSD_INNER:SSD_INNER + SSD_GROUPS * SSD_STATE].reshape(b, l, SSD_GROUPS, SSD_STATE)
    cm = xbc[..., SSD_INNER + SSD_GROUPS * SSD_STATE:].reshape(b, l, SSD_GROUPS, SSD_STATE)
    dt = jax.nn.softplus(dt_raw.astype(jnp.float32) + dt_bias.astype(jnp.float32))
    a = -jnp.exp(a_log.astype(jnp.float32))
    y = ssd_chunked(xs, dt, a, bm, cm) + xs * d_skip[:, None].astype(xs.dtype)
    y = y.reshape(b, l, SSD_INNER) * jax.nn.silu(z)
    y = rms_norm(y.reshape(b, l, SSD_GROUPS, SSD_INNER // SSD_GROUPS),
                 norm_g.reshape(SSD_GROUPS, SSD_INNER // SSD_GROUPS))
    return y.reshape(b, l, SSD_INNER)


def t5_band_bias(rel_table):
    qi = jnp.arange(ATTN_BLOCK)[:, None] + ATTN_BLOCK
    kj = jnp.arange(2 * ATTN_BLOCK)[None, :]
    dist = qi - kj
    max_exact = REL_BUCKETS // 2
    d = jnp.maximum(dist, 1).astype(jnp.float32)
    large = max_exact + (jnp.log(d / max_exact) / math.log(REL_MAX_DIST / max_exact)
                         * (REL_BUCKETS - max_exact)).astype(jnp.int32)
    large = jnp.minimum(large, REL_BUCKETS - 1)
    bucket = jnp.where(dist < max_exact, jnp.maximum(dist, 0), large)
    bias = jnp.transpose(rel_table[bucket], (2, 0, 1)).astype(jnp.float32)
    return bias, dist


def swa_branch(q, k, v, q_g, k_g, sinks, rel_bias, band_dist):
    b, l, _ = q.shape
    nb = l // ATTN_BLOCK
    r = ATTN_HEADS // ATTN_KV_HEADS
    q = rms_norm(q.reshape(b, l, ATTN_HEADS, ATTN_HEAD_DIM), q_g)
    k = rms_norm(k.reshape(b, l, ATTN_KV_HEADS, ATTN_HEAD_DIM), k_g)
    v = v.reshape(b, l, ATTN_KV_HEADS, ATTN_HEAD_DIM)

    def band(t):
        tp = jnp.pad(t, ((0, 0), (ATTN_BLOCK, 0), (0, 0), (0, 0)))
        prev = tp[:, :l].reshape(b, nb, ATTN_BLOCK, ATTN_KV_HEADS, ATTN_HEAD_DIM)
        cur = t.reshape(b, nb, ATTN_BLOCK, ATTN_KV_HEADS, ATTN_HEAD_DIM)
        return jnp.concatenate([prev, cur], axis=2)

    kb, vb = band(k), band(v)
    qb = q.reshape(b, nb, ATTN_BLOCK, ATTN_KV_HEADS, r, ATTN_HEAD_DIM)
    logits = jnp.einsum("bnqgrd,bnkgd->bngrqk", qb, kb).astype(jnp.float32) * (ATTN_HEAD_DIM ** -0.5)
    logits = logits + rel_bias.reshape(ATTN_KV_HEADS, r, ATTN_BLOCK, 2 * ATTN_BLOCK)
    key_pos = (jnp.arange(nb)[:, None] * ATTN_BLOCK - ATTN_BLOCK
               + jnp.arange(2 * ATTN_BLOCK)[None, :])
    in_window = (band_dist >= 0) & (band_dist < ATTN_WINDOW)
    mask = in_window[None] & (key_pos >= 0)[:, None, :]
    logits = jnp.where(mask[None, :, None, None], logits, NEG_INF)
    sink = sinks.astype(jnp.float32).reshape(ATTN_KV_HEADS, r)[None, None, :, :, None, None]
    m = jnp.maximum(jnp.max(logits, axis=-1, keepdims=True), sink)
    pexp = jnp.exp(logits - m)
    probs = pexp / (jnp.sum(pexp, axis=-1, keepdims=True) + jnp.exp(sink - m))
    out = jnp.einsum("bngrqk,bnkgd->bnqgrd", probs.astype(v.dtype), vb)
    return out.reshape(b, l, ATTN_DIM)


def memory_cross_attention(h, mem_h, w_q, w_kv, q_g, k_g, w_o):
    b, l, _ = h.shape
    m = mem_h.shape[1]
    q = rms_norm((h @ w_q).reshape(b, l, XATTN_HEADS, XATTN_HEAD_DIM), q_g)
    kv = mem_h @ w_kv
    k = rms_norm(kv[..., :D_MODEL].reshape(b, m, XATTN_HEADS, XATTN_HEAD_DIM), k_g)
    v = kv[..., D_MODEL:].reshape(b, m, XATTN_HEADS, XATTN_HEAD_DIM)
    logits = jnp.einsum("bqhd,bkhd->bhqk", q, k).astype(jnp.float32) * (XATTN_HEAD_DIM ** -0.5)
    probs = jax.nn.softmax(logits, axis=-1)
    out = jnp.einsum("bhqk,bkhd->bqhd", probs.astype(v.dtype), v).reshape(b, l, D_MODEL)
    return out @ w_o


def _fwd_setup_inputs(seed: int = 0) -> dict:
    key = jax.random.key(seed)
    ks = iter(jax.random.split(key, 48))
    f32 = jnp.float32
    L = DEPTH

    def nrm(shape, scale):
        return jax.random.normal(next(ks), shape, f32) * scale

    def gain(shape):
        return 1.0 + nrm(shape, 0.02)

    out_scale = (2.0 * DEPTH) ** -0.5
    dt0 = jnp.exp(jax.random.uniform(next(ks), (L, SSD_HEADS), f32, math.log(1e-3), math.log(1e-1)))
    return {
        "x": nrm((BATCH, SEQ, D_MODEL), 1.0),
        "mem": nrm((BATCH, MEM_LEN, D_MODEL), 1.0),
        "rel_table": nrm((REL_BUCKETS, ATTN_HEADS), 0.1),
        "norm_mix": gain((L, D_MODEL)),
        "w_in": nrm((L, D_MODEL, IN_COLS), D_MODEL ** -0.5),
        "gate_bias": nrm((L, N_BRANCH, D_MODEL), 0.01),
        "conv_dw_w": nrm((L, CONV_KERNEL, CONV_DIM), CONV_KERNEL ** -0.5),
        "conv_dw_b": nrm((L, CONV_DIM), 0.01),
        "conv_ln_g": gain((L, CONV_DIM)),
        "conv_ln_b": nrm((L, CONV_DIM), 0.01),
        "w_conv_out": nrm((L, CONV_DIM, D_MODEL), CONV_DIM ** -0.5),
        "ssd_conv_w": nrm((L, SSD_CONV, SSD_XBC), SSD_CONV ** -0.5),
        "ssd_conv_b": nrm((L, SSD_XBC), 0.01),
        "ssd_dt_bias": dt0 + jnp.log(-jnp.expm1(-dt0)),
        "ssd_A_log": jnp.log(jax.random.uniform(next(ks), (L, SSD_HEADS), f32, 1.0, 16.0)),
        "ssd_D": gain((L, SSD_HEADS)),
        "ssd_norm_g": gain((L, SSD_INNER)),
        "w_ssd_out": nrm((L, SSD_INNER, D_MODEL), SSD_INNER ** -0.5),
        "attn_q_norm": gain((L, ATTN_HEAD_DIM)),
        "attn_k_norm": gain((L, ATTN_HEAD_DIM)),
        "attn_sinks": nrm((L, ATTN_HEADS), 0.5),
        "w_attn_out": nrm((L, ATTN_DIM, D_MODEL), ATTN_DIM ** -0.5),
        "w_mix_out": nrm((L, D_MODEL, D_MODEL), out_scale * D_MODEL ** -0.5),
        "norm_xattn": gain((L, D_MODEL)),
        "norm_mem": gain((L, D_MODEL)),
        "w_xq": nrm((L, D_MODEL, D_MODEL), D_MODEL ** -0.5),
        "w_xkv": nrm((L, D_MODEL, 2 * D_MODEL), D_MODEL ** -0.5),
        "xattn_q_norm": gain((L, XATTN_HEAD_DIM)),
        "xattn_k_norm": gain((L, XATTN_HEAD_DIM)),
        "w_xo": nrm((L, D_MODEL, D_MODEL), out_scale * D_MODEL ** -0.5),
        "norm_mlp": gain((L, D_MODEL)),
        "w_mlp_up": nrm((L, D_MODEL, MLP_HIDDEN), D_MODEL ** -0.5),
        "w_mlp_down": nrm((L, MLP_HIDDEN, D_MODEL), out_scale * MLP_HIDDEN ** -0.5),
    }


def _fwd_reference(x, mem, rel_table, norm_mix, w_in, gate_bias, conv_dw_w, conv_dw_b, conv_ln_g,
              conv_ln_b, w_conv_out, ssd_conv_w, ssd_conv_b, ssd_dt_bias, ssd_A_log, ssd_D,
              ssd_norm_g, w_ssd_out, attn_q_norm, attn_k_norm, attn_sinks, w_attn_out, w_mix_out,
              norm_xattn, norm_mem, w_xq, w_xkv, xattn_q_norm, xattn_k_norm, w_xo, norm_mlp,
              w_mlp_up, w_mlp_down):
    b, l, _ = x.shape
    rel_bias, band_dist = t5_band_bias(rel_table)
    h = x
    for i in range(DEPTH):
        u = rms_norm(h, norm_mix[i])
        proj = u @ w_in[i]
        y_a = conformer_conv_branch(proj[..., OFF_CONV:OFF_Z], conv_dw_w[i], conv_dw_b[i],
                                    conv_ln_g[i], conv_ln_b[i]) @ w_conv_out[i]
        y_b = ssd_branch(proj[..., OFF_Z:OFF_XBC], proj[..., OFF_XBC:OFF_DT], proj[..., OFF_DT:OFF_Q],
                         ssd_conv_w[i], ssd_conv_b[i], ssd_dt_bias[i], ssd_A_log[i], ssd_D[i],
                         ssd_norm_g[i]) @ w_ssd_out[i]
        y_c = swa_branch(proj[..., OFF_Q:OFF_K], proj[..., OFF_K:OFF_V], proj[..., OFF_V:OFF_GATE],
                         attn_q_norm[i], attn_k_norm[i], attn_sinks[i], rel_bias, band_dist) @ w_attn_out[i]
        gates = jax.nn.sigmoid(proj[..., OFF_GATE:IN_COLS].reshape(b, l, N_BRANCH, D_MODEL)
                               + gate_bias[i].astype(proj.dtype))
        merged = gates[..., 0, :] * y_a + gates[..., 1, :] * y_b + gates[..., 2, :] * y_c
        h = h + merged @ w_mix_out[i]
        h = h + memory_cross_attention(rms_norm(h, norm_xattn[i]), rms_norm(mem, norm_mem[i]),
                                       w_xq[i], w_xkv[i], xattn_q_norm[i], xattn_k_norm[i], w_xo[i])
        u = rms_norm(h, norm_mlp[i])
        h = h + jnp.square(jax.nn.relu(u @ w_mlp_up[i])) @ w_mlp_down[i]
    return h


import jax as _jax
import jax.numpy as _jnp

TWIN_FORMAT = 'train_step'
FWD_PARAMS = ['x', 'mem', 'rel_table', 'norm_mix', 'w_in', 'gate_bias', 'conv_dw_w', 'conv_dw_b', 'conv_ln_g', 'conv_ln_b', 'w_conv_out', 'ssd_conv_w', 'ssd_conv_b', 'ssd_dt_bias', 'ssd_A_log', 'ssd_D', 'ssd_norm_g', 'w_ssd_out', 'attn_q_norm', 'attn_k_norm', 'attn_sinks', 'w_attn_out', 'w_mix_out', 'norm_xattn', 'norm_mem', 'w_xq', 'w_xkv', 'xattn_q_norm', 'xattn_k_norm', 'w_xo', 'norm_mlp', 'w_mlp_up', 'w_mlp_down']
TWIN_WEIGHTS = ['rel_table', 'norm_mix', 'w_in', 'gate_bias', 'conv_dw_w', 'conv_dw_b', 'conv_ln_g', 'conv_ln_b', 'w_conv_out', 'ssd_conv_w', 'ssd_conv_b', 'ssd_dt_bias', 'ssd_A_log', 'ssd_D', 'ssd_norm_g', 'w_ssd_out', 'attn_q_norm', 'attn_k_norm', 'attn_sinks', 'w_attn_out', 'w_mix_out', 'norm_xattn', 'norm_mem', 'w_xq', 'w_xkv', 'xattn_q_norm', 'xattn_k_norm', 'w_xo', 'norm_mlp', 'w_mlp_up', 'w_mlp_down']
TWIN_DIFF_INPUT = 'x'
TWIN_INPUTS = ['x', 'mem', 'rel_table', 'norm_mix', 'w_in', 'gate_bias', 'conv_dw_w', 'conv_dw_b', 'conv_ln_g', 'conv_ln_b', 'w_conv_out', 'ssd_conv_w', 'ssd_conv_b', 'ssd_dt_bias', 'ssd_A_log', 'ssd_D', 'ssd_norm_g', 'w_ssd_out', 'attn_q_norm', 'attn_k_norm', 'attn_sinks', 'w_attn_out', 'w_mix_out', 'norm_xattn', 'norm_mem', 'w_xq', 'w_xkv', 'xattn_q_norm', 'xattn_k_norm', 'w_xo', 'norm_mlp', 'w_mlp_up', 'w_mlp_down', 'loss_target', 'm_rel_table', 'm_norm_mix', 'm_w_in', 'm_gate_bias', 'm_conv_dw_w', 'm_conv_dw_b', 'm_conv_ln_g', 'm_conv_ln_b', 'm_w_conv_out', 'm_ssd_conv_w', 'm_ssd_conv_b', 'm_ssd_dt_bias', 'm_ssd_A_log', 'm_ssd_D', 'm_ssd_norm_g', 'm_w_ssd_out', 'm_attn_q_norm', 'm_attn_k_norm', 'm_attn_sinks', 'm_w_attn_out', 'm_w_mix_out', 'm_norm_xattn', 'm_norm_mem', 'm_w_xq', 'm_w_xkv', 'm_xattn_q_norm', 'm_xattn_k_norm', 'm_w_xo', 'm_norm_mlp', 'm_w_mlp_up', 'm_w_mlp_down', 'v_rel_table', 'v_norm_mix', 'v_w_in', 'v_gate_bias', 'v_conv_dw_w', 'v_conv_dw_b', 'v_conv_ln_g', 'v_conv_ln_b', 'v_w_conv_out', 'v_ssd_conv_w', 'v_ssd_conv_b', 'v_ssd_dt_bias', 'v_ssd_A_log', 'v_ssd_D', 'v_ssd_norm_g', 'v_w_ssd_out', 'v_attn_q_norm', 'v_attn_k_norm', 'v_attn_sinks', 'v_w_attn_out', 'v_w_mix_out', 'v_norm_xattn', 'v_norm_mem', 'v_w_xq', 'v_w_xkv', 'v_xattn_q_norm', 'v_xattn_k_norm', 'v_w_xo', 'v_norm_mlp', 'v_w_mlp_up', 'v_w_mlp_down']
TWIN_OUTPUTS = ['loss', 'grad_x', 'grad_rel_table', 'grad_norm_mix', 'grad_w_in', 'grad_gate_bias', 'grad_conv_dw_w', 'grad_conv_dw_b', 'grad_conv_ln_g', 'grad_conv_ln_b', 'grad_w_conv_out', 'grad_ssd_conv_w', 'grad_ssd_conv_b', 'grad_ssd_dt_bias', 'grad_ssd_A_log', 'grad_ssd_D', 'grad_ssd_norm_g', 'grad_w_ssd_out', 'grad_attn_q_norm', 'grad_attn_k_norm', 'grad_attn_sinks', 'grad_w_attn_out', 'grad_w_mix_out', 'grad_norm_xattn', 'grad_norm_mem', 'grad_w_xq', 'grad_w_xkv', 'grad_xattn_q_norm', 'grad_xattn_k_norm', 'grad_w_xo', 'grad_norm_mlp', 'grad_w_mlp_up', 'grad_w_mlp_down', 'delta_rel_table', 'delta_norm_mix', 'delta_w_in', 'delta_gate_bias', 'delta_conv_dw_w', 'delta_conv_dw_b', 'delta_conv_ln_g', 'delta_conv_ln_b', 'delta_w_conv_out', 'delta_ssd_conv_w', 'delta_ssd_conv_b', 'delta_ssd_dt_bias', 'delta_ssd_A_log', 'delta_ssd_D', 'delta_ssd_norm_g', 'delta_w_ssd_out', 'delta_attn_q_norm', 'delta_attn_k_norm', 'delta_attn_sinks', 'delta_w_attn_out', 'delta_w_mix_out', 'delta_norm_xattn', 'delta_norm_mem', 'delta_w_xq', 'delta_w_xkv', 'delta_xattn_q_norm', 'delta_xattn_k_norm', 'delta_w_xo', 'delta_norm_mlp', 'delta_w_mlp_up', 'delta_w_mlp_down', 'new_m_rel_table', 'new_m_norm_mix', 'new_m_w_in', 'new_m_gate_bias', 'new_m_conv_dw_w', 'new_m_conv_dw_b', 'new_m_conv_ln_g', 'new_m_conv_ln_b', 'new_m_w_conv_out', 'new_m_ssd_conv_w', 'new_m_ssd_conv_b', 'new_m_ssd_dt_bias', 'new_m_ssd_A_log', 'new_m_ssd_D', 'new_m_ssd_norm_g', 'new_m_w_ssd_out', 'new_m_attn_q_norm', 'new_m_attn_k_norm', 'new_m_attn_sinks', 'new_m_w_attn_out', 'new_m_w_mix_out', 'new_m_norm_xattn', 'new_m_norm_mem', 'new_m_w_xq', 'new_m_w_xkv', 'new_m_xattn_q_norm', 'new_m_xattn_k_norm', 'new_m_w_xo', 'new_m_norm_mlp', 'new_m_w_mlp_up', 'new_m_w_mlp_down', 'new_v_rel_table', 'new_v_norm_mix', 'new_v_w_in', 'new_v_gate_bias', 'new_v_conv_dw_w', 'new_v_conv_dw_b', 'new_v_conv_ln_g', 'new_v_conv_ln_b', 'new_v_w_conv_out', 'new_v_ssd_conv_w', 'new_v_ssd_conv_b', 'new_v_ssd_dt_bias', 'new_v_ssd_A_log', 'new_v_ssd_D', 'new_v_ssd_norm_g', 'new_v_w_ssd_out', 'new_v_attn_q_norm', 'new_v_attn_k_norm', 'new_v_attn_sinks', 'new_v_w_attn_out', 'new_v_w_mix_out', 'new_v_norm_xattn', 'new_v_norm_mem', 'new_v_w_xq', 'new_v_w_xkv', 'new_v_xattn_q_norm', 'new_v_xattn_k_norm', 'new_v_w_xo', 'new_v_norm_mlp', 'new_v_w_mlp_up', 'new_v_w_mlp_down']
TWIN_LEAF_KINDS = {'loss': 'loss', 'grad_x': 'grad_x', 'grad_rel_table': 'grad_w', 'grad_norm_mix': 'grad_w', 'grad_w_in': 'grad_w', 'grad_gate_bias': 'grad_w', 'grad_conv_dw_w': 'grad_w', 'grad_conv_dw_b': 'grad_w', 'grad_conv_ln_g': 'grad_w', 'grad_conv_ln_b': 'grad_w', 'grad_w_conv_out': 'grad_w', 'grad_ssd_conv_w': 'grad_w', 'grad_ssd_conv_b': 'grad_w', 'grad_ssd_dt_bias': 'grad_w', 'grad_ssd_A_log': 'grad_w', 'grad_ssd_D': 'grad_w', 'grad_ssd_norm_g': 'grad_w', 'grad_w_ssd_out': 'grad_w', 'grad_attn_q_norm': 'grad_w', 'grad_attn_k_norm': 'grad_w', 'grad_attn_sinks': 'grad_w', 'grad_w_attn_out': 'grad_w', 'grad_w_mix_out': 'grad_w', 'grad_norm_xattn': 'grad_w', 'grad_norm_mem': 'grad_w', 'grad_w_xq': 'grad_w', 'grad_w_xkv': 'grad_w', 'grad_xattn_q_norm': 'grad_w', 'grad_xattn_k_norm': 'grad_w', 'grad_w_xo': 'grad_w', 'grad_norm_mlp': 'grad_w', 'grad_w_mlp_up': 'grad_w', 'grad_w_mlp_down': 'grad_w', 'delta_rel_table': 'delta_w', 'delta_norm_mix': 'delta_w', 'delta_w_in': 'delta_w', 'delta_gate_bias': 'delta_w', 'delta_conv_dw_w': 'delta_w', 'delta_conv_dw_b': 'delta_w', 'delta_conv_ln_g': 'delta_w', 'delta_conv_ln_b': 'delta_w', 'delta_w_conv_out': 'delta_w', 'delta_ssd_conv_w': 'delta_w', 'delta_ssd_conv_b': 'delta_w', 'delta_ssd_dt_bias': 'delta_w', 'delta_ssd_A_log': 'delta_w', 'delta_ssd_D': 'delta_w', 'delta_ssd_norm_g': 'delta_w', 'delta_w_ssd_out': 'delta_w', 'delta_attn_q_norm': 'delta_w', 'delta_attn_k_norm': 'delta_w', 'delta_attn_sinks': 'delta_w', 'delta_w_attn_out': 'delta_w', 'delta_w_mix_out': 'delta_w', 'delta_norm_xattn': 'delta_w', 'delta_norm_mem': 'delta_w', 'delta_w_xq': 'delta_w', 'delta_w_xkv': 'delta_w', 'delta_xattn_q_norm': 'delta_w', 'delta_xattn_k_norm': 'delta_w', 'delta_w_xo': 'delta_w', 'delta_norm_mlp': 'delta_w', 'delta_w_mlp_up': 'delta_w', 'delta_w_mlp_down': 'delta_w', 'new_m_rel_table': 'new_m', 'new_m_norm_mix': 'new_m', 'new_m_w_in': 'new_m', 'new_m_gate_bias': 'new_m', 'new_m_conv_dw_w': 'new_m', 'new_m_conv_dw_b': 'new_m', 'new_m_conv_ln_g': 'new_m', 'new_m_conv_ln_b': 'new_m', 'new_m_w_conv_out': 'new_m', 'new_m_ssd_conv_w': 'new_m', 'new_m_ssd_conv_b': 'new_m', 'new_m_ssd_dt_bias': 'new_m', 'new_m_ssd_A_log': 'new_m', 'new_m_ssd_D': 'new_m', 'new_m_ssd_norm_g': 'new_m', 'new_m_w_ssd_out': 'new_m', 'new_m_attn_q_norm': 'new_m', 'new_m_attn_k_norm': 'new_m', 'new_m_attn_sinks': 'new_m', 'new_m_w_attn_out': 'new_m', 'new_m_w_mix_out': 'new_m', 'new_m_norm_xattn': 'new_m', 'new_m_norm_mem': 'new_m', 'new_m_w_xq': 'new_m', 'new_m_w_xkv': 'new_m', 'new_m_xattn_q_norm': 'new_m', 'new_m_xattn_k_norm': 'new_m', 'new_m_w_xo': 'new_m', 'new_m_norm_mlp': 'new_m', 'new_m_w_mlp_up': 'new_m', 'new_m_w_mlp_down': 'new_m', 'new_v_rel_table': 'new_v', 'new_v_norm_mix': 'new_v', 'new_v_w_in': 'new_v', 'new_v_gate_bias': 'new_v', 'new_v_conv_dw_w': 'new_v', 'new_v_conv_dw_b': 'new_v', 'new_v_conv_ln_g': 'new_v', 'new_v_conv_ln_b': 'new_v', 'new_v_w_conv_out': 'new_v', 'new_v_ssd_conv_w': 'new_v', 'new_v_ssd_conv_b': 'new_v', 'new_v_ssd_dt_bias': 'new_v', 'new_v_ssd_A_log': 'new_v', 'new_v_ssd_D': 'new_v', 'new_v_ssd_norm_g': 'new_v', 'new_v_w_ssd_out': 'new_v', 'new_v_attn_q_norm': 'new_v', 'new_v_attn_k_norm': 'new_v', 'new_v_attn_sinks': 'new_v', 'new_v_w_attn_out': 'new_v', 'new_v_w_mix_out': 'new_v', 'new_v_norm_xattn': 'new_v', 'new_v_norm_mem': 'new_v', 'new_v_w_xq': 'new_v', 'new_v_w_xkv': 'new_v', 'new_v_xattn_q_norm': 'new_v', 'new_v_xattn_k_norm': 'new_v', 'new_v_w_xo': 'new_v', 'new_v_norm_mlp': 'new_v', 'new_v_w_mlp_up': 'new_v', 'new_v_w_mlp_down': 'new_v'}


def _forward(args):
    return _fwd_reference(*[args[k] for k in FWD_PARAMS])


def _output_shape():
    out = _jax.eval_shape(lambda: _forward(_fwd_setup_inputs(0)))
    return out.shape, out.dtype

N_MICROBATCH = 1
ADAM_LR = 0.001
ADAM_B1 = 0.9
ADAM_B2 = 0.999
ADAM_EPS = 1e-08
ADAM_WD = 0.01
ADAM_STEP = 10
PER_EXAMPLE_BATCH_AXIS = {'x': 0, 'mem': 0, 'loss_target': 0}
SHARED_INPUTS = []
_WEIGHT_DTYPES = {'rel_table': _jnp.float32, 'norm_mix': _jnp.float32, 'w_in': _jnp.float32, 'gate_bias': _jnp.float32, 'conv_dw_w': _jnp.float32, 'conv_dw_b': _jnp.float32, 'conv_ln_g': _jnp.float32, 'conv_ln_b': _jnp.float32, 'w_conv_out': _jnp.float32, 'ssd_conv_w': _jnp.float32, 'ssd_conv_b': _jnp.float32, 'ssd_dt_bias': _jnp.float32, 'ssd_A_log': _jnp.float32, 'ssd_D': _jnp.float32, 'ssd_norm_g': _jnp.float32, 'w_ssd_out': _jnp.float32, 'attn_q_norm': _jnp.float32, 'attn_k_norm': _jnp.float32, 'attn_sinks': _jnp.float32, 'w_attn_out': _jnp.float32, 'w_mix_out': _jnp.float32, 'norm_xattn': _jnp.float32, 'norm_mem': _jnp.float32, 'w_xq': _jnp.float32, 'w_xkv': _jnp.float32, 'xattn_q_norm': _jnp.float32, 'xattn_k_norm': _jnp.float32, 'w_xo': _jnp.float32, 'norm_mlp': _jnp.float32, 'w_mlp_up': _jnp.float32, 'w_mlp_down': _jnp.float32}
MOMENT_SCALE = {'rel_table': 4.046060e-02, 'norm_mix': 5.675118e-01, 'w_in': 1.464708e-01, 'gate_bias': 3.390453e-01, 'conv_dw_w': 2.455394e-01, 'conv_dw_b': 2.381716e+00, 'conv_ln_g': 1.106271e+00, 'conv_ln_b': 1.436458e+00, 'w_conv_out': 6.007732e-01, 'ssd_conv_w': 2.022414e-01, 'ssd_conv_b': 6.813952e-01, 'ssd_dt_bias': 2.127686e-01, 'ssd_A_log': 1.169684e+00, 'ssd_D': 1.262410e+00, 'ssd_norm_g': 1.004422e+00, 'w_ssd_out': 5.262312e-01, 'attn_q_norm': 2.050647e-01, 'attn_k_norm': 2.064753e-01, 'attn_sinks': 3.615476e-02, 'w_attn_out': 5.081303e-01, 'w_mix_out': 2.665347e+00, 'norm_xattn': 3.654836e-02, 'norm_mem': 1.776119e-01, 'w_xq': 3.587970e-02, 'w_xkv': 1.150051e-01, 'xattn_q_norm': 1.566715e-01, 'xattn_k_norm': 1.570429e-01, 'w_xo': 4.730048e-01, 'norm_mlp': 1.227166e+01, 'w_mlp_up': 4.730457e-01, 'w_mlp_down': 7.500426e+00}


def _to_microbatches(a, axis):
    t = _jnp.moveaxis(a, axis, 0)
    t = t.reshape((N_MICROBATCH, t.shape[0] // N_MICROBATCH) + t.shape[1:])
    return _jnp.moveaxis(t, 1, axis + 1)


def setup_inputs(seed: int = 0) -> dict:
    inp = _fwd_setup_inputs(seed)
    key = _jax.random.fold_in(_jax.random.key(seed), 7919)
    shape, _ = _output_shape()
    out = dict(inp)
    out["loss_target"] = _jax.random.normal(_jax.random.fold_in(key, 0), shape, _jnp.float32)
    for i, name in enumerate(TWIN_WEIGHTS):
        w = inp[name].astype(_jnp.float32)
        if MOMENT_SCALE is None:
            s = _jnp.sqrt(_jnp.mean(_jnp.square(w)) + 1e-30)
        else:
            s = MOMENT_SCALE[name]
        km, kv = _jax.random.split(_jax.random.fold_in(key, i + 1))
        out[name] = w
        out["m_" + name] = s * _jax.random.normal(km, w.shape, _jnp.float32)
        out["v_" + name] = (s * s) * _jax.random.uniform(kv, w.shape, _jnp.float32, 0.5, 1.5)
    if N_MICROBATCH > 1:
        for name, axis in PER_EXAMPLE_BATCH_AXIS.items():
            out[name] = _to_microbatches(out[name], axis)
    return {'x': out['x'], 'mem': out['mem'], 'rel_table': out['rel_table'], 'norm_mix': out['norm_mix'], 'w_in': out['w_in'], 'gate_bias': out['gate_bias'], 'conv_dw_w': out['conv_dw_w'], 'conv_dw_b': out['conv_dw_b'], 'conv_ln_g': out['conv_ln_g'], 'conv_ln_b': out['conv_ln_b'], 'w_conv_out': out['w_conv_out'], 'ssd_conv_w': out['ssd_conv_w'], 'ssd_conv_b': out['ssd_conv_b'], 'ssd_dt_bias': out['ssd_dt_bias'], 'ssd_A_log': out['ssd_A_log'], 'ssd_D': out['ssd_D'], 'ssd_norm_g': out['ssd_norm_g'], 'w_ssd_out': out['w_ssd_out'], 'attn_q_norm': out['attn_q_norm'], 'attn_k_norm': out['attn_k_norm'], 'attn_sinks': out['attn_sinks'], 'w_attn_out': out['w_attn_out'], 'w_mix_out': out['w_mix_out'], 'norm_xattn': out['norm_xattn'], 'norm_mem': out['norm_mem'], 'w_xq': out['w_xq'], 'w_xkv': out['w_xkv'], 'xattn_q_norm': out['xattn_q_norm'], 'xattn_k_norm': out['xattn_k_norm'], 'w_xo': out['w_xo'], 'norm_mlp': out['norm_mlp'], 'w_mlp_up': out['w_mlp_up'], 'w_mlp_down': out['w_mlp_down'], 'loss_target': out['loss_target'], 'm_rel_table': out['m_rel_table'], 'm_norm_mix': out['m_norm_mix'], 'm_w_in': out['m_w_in'], 'm_gate_bias': out['m_gate_bias'], 'm_conv_dw_w': out['m_conv_dw_w'], 'm_conv_dw_b': out['m_conv_dw_b'], 'm_conv_ln_g': out['m_conv_ln_g'], 'm_conv_ln_b': out['m_conv_ln_b'], 'm_w_conv_out': out['m_w_conv_out'], 'm_ssd_conv_w': out['m_ssd_conv_w'], 'm_ssd_conv_b': out['m_ssd_conv_b'], 'm_ssd_dt_bias': out['m_ssd_dt_bias'], 'm_ssd_A_log': out['m_ssd_A_log'], 'm_ssd_D': out['m_ssd_D'], 'm_ssd_norm_g': out['m_ssd_norm_g'], 'm_w_ssd_out': out['m_w_ssd_out'], 'm_attn_q_norm': out['m_attn_q_norm'], 'm_attn_k_norm': out['m_attn_k_norm'], 'm_attn_sinks': out['m_attn_sinks'], 'm_w_attn_out': out['m_w_attn_out'], 'm_w_mix_out': out['m_w_mix_out'], 'm_norm_xattn': out['m_norm_xattn'], 'm_norm_mem': out['m_norm_mem'], 'm_w_xq': out['m_w_xq'], 'm_w_xkv': out['m_w_xkv'], 'm_xattn_q_norm': out['m_xattn_q_norm'], 'm_xattn_k_norm': out['m_xattn_k_norm'], 'm_w_xo': out['m_w_xo'], 'm_norm_mlp': out['m_norm_mlp'], 'm_w_mlp_up': out['m_w_mlp_up'], 'm_w_mlp_down': out['m_w_mlp_down'], 'v_rel_table': out['v_rel_table'], 'v_norm_mix': out['v_norm_mix'], 'v_w_in': out['v_w_in'], 'v_gate_bias': out['v_gate_bias'], 'v_conv_dw_w': out['v_conv_dw_w'], 'v_conv_dw_b': out['v_conv_dw_b'], 'v_conv_ln_g': out['v_conv_ln_g'], 'v_conv_ln_b': out['v_conv_ln_b'], 'v_w_conv_out': out['v_w_conv_out'], 'v_ssd_conv_w': out['v_ssd_conv_w'], 'v_ssd_conv_b': out['v_ssd_conv_b'], 'v_ssd_dt_bias': out['v_ssd_dt_bias'], 'v_ssd_A_log': out['v_ssd_A_log'], 'v_ssd_D': out['v_ssd_D'], 'v_ssd_norm_g': out['v_ssd_norm_g'], 'v_w_ssd_out': out['v_w_ssd_out'], 'v_attn_q_norm': out['v_attn_q_norm'], 'v_attn_k_norm': out['v_attn_k_norm'], 'v_attn_sinks': out['v_attn_sinks'], 'v_w_attn_out': out['v_w_attn_out'], 'v_w_mix_out': out['v_w_mix_out'], 'v_norm_xattn': out['v_norm_xattn'], 'v_norm_mem': out['v_norm_mem'], 'v_w_xq': out['v_w_xq'], 'v_w_xkv': out['v_w_xkv'], 'v_xattn_q_norm': out['v_xattn_q_norm'], 'v_xattn_k_norm': out['v_xattn_k_norm'], 'v_w_xo': out['v_w_xo'], 'v_norm_mlp': out['v_norm_mlp'], 'v_w_mlp_up': out['v_w_mlp_up'], 'v_w_mlp_down': out['v_w_mlp_down']}


def _loss(weights, diff, rest, loss_target):
    with _jax.named_scope("forward"):
        args = {**rest, TWIN_DIFF_INPUT: diff, **{k: w.astype(_WEIGHT_DTYPES[k]) for k, w in weights.items()}}
        y = _forward(args)
    with _jax.named_scope("loss_head"):
        err = _jnp.square(y.astype(_jnp.float32) - loss_target)
        return 0.5 * _jnp.sum(_jnp.mean(err, axis=-1)) if err.ndim else 0.5 * err


def _adamw(w, g, m, v):
    m = ADAM_B1 * m + (1.0 - ADAM_B1) * g
    v = ADAM_B2 * v + (1.0 - ADAM_B2) * _jnp.square(g)
    m_hat = m / (1.0 - ADAM_B1 ** ADAM_STEP)
    v_hat = v / (1.0 - ADAM_B2 ** ADAM_STEP)
    delta = -ADAM_LR * (m_hat / (_jnp.sqrt(v_hat) + ADAM_EPS) + ADAM_WD * w)
    return delta, m, v


def reference(x, mem, rel_table, norm_mix, w_in, gate_bias, conv_dw_w, conv_dw_b, conv_ln_g, conv_ln_b, w_conv_out, ssd_conv_w, ssd_conv_b, ssd_dt_bias, ssd_A_log, ssd_D, ssd_norm_g, w_ssd_out, attn_q_norm, attn_k_norm, attn_sinks, w_attn_out, w_mix_out, norm_xattn, norm_mem, w_xq, w_xkv, xattn_q_norm, xattn_k_norm, w_xo, norm_mlp, w_mlp_up, w_mlp_down, loss_target, m_rel_table, m_norm_mix, m_w_in, m_gate_bias, m_conv_dw_w, m_conv_dw_b, m_conv_ln_g, m_conv_ln_b, m_w_conv_out, m_ssd_conv_w, m_ssd_conv_b, m_ssd_dt_bias, m_ssd_A_log, m_ssd_D, m_ssd_norm_g, m_w_ssd_out, m_attn_q_norm, m_attn_k_norm, m_attn_sinks, m_w_attn_out, m_w_mix_out, m_norm_xattn, m_norm_mem, m_w_xq, m_w_xkv, m_xattn_q_norm, m_xattn_k_norm, m_w_xo, m_norm_mlp, m_w_mlp_up, m_w_mlp_down, v_rel_table, v_norm_mix, v_w_in, v_gate_bias, v_conv_dw_w, v_conv_dw_b, v_conv_ln_g, v_conv_ln_b, v_w_conv_out, v_ssd_conv_w, v_ssd_conv_b, v_ssd_dt_bias, v_ssd_A_log, v_ssd_D, v_ssd_norm_g, v_w_ssd_out, v_attn_q_norm, v_attn_k_norm, v_attn_sinks, v_w_attn_out, v_w_mix_out, v_norm_xattn, v_norm_mem, v_w_xq, v_w_xkv, v_xattn_q_norm, v_xattn_k_norm, v_w_xo, v_norm_mlp, v_w_mlp_up, v_w_mlp_down):
    given = dict(x=x, mem=mem, rel_table=rel_table, norm_mix=norm_mix, w_in=w_in, gate_bias=gate_bias, conv_dw_w=conv_dw_w, conv_dw_b=conv_dw_b, conv_ln_g=conv_ln_g, conv_ln_b=conv_ln_b, w_conv_out=w_conv_out, ssd_conv_w=ssd_conv_w, ssd_conv_b=ssd_conv_b, ssd_dt_bias=ssd_dt_bias, ssd_A_log=ssd_A_log, ssd_D=ssd_D, ssd_norm_g=ssd_norm_g, w_ssd_out=w_ssd_out, attn_q_norm=attn_q_norm, attn_k_norm=attn_k_norm, attn_sinks=attn_sinks, w_attn_out=w_attn_out, w_mix_out=w_mix_out, norm_xattn=norm_xattn, norm_mem=norm_mem, w_xq=w_xq, w_xkv=w_xkv, xattn_q_norm=xattn_q_norm, xattn_k_norm=xattn_k_norm, w_xo=w_xo, norm_mlp=norm_mlp, w_mlp_up=w_mlp_up, w_mlp_down=w_mlp_down, loss_target=loss_target, m_rel_table=m_rel_table, m_norm_mix=m_norm_mix, m_w_in=m_w_in, m_gate_bias=m_gate_bias, m_conv_dw_w=m_conv_dw_w, m_conv_dw_b=m_conv_dw_b, m_conv_ln_g=m_conv_ln_g, m_conv_ln_b=m_conv_ln_b, m_w_conv_out=m_w_conv_out, m_ssd_conv_w=m_ssd_conv_w, m_ssd_conv_b=m_ssd_conv_b, m_ssd_dt_bias=m_ssd_dt_bias, m_ssd_A_log=m_ssd_A_log, m_ssd_D=m_ssd_D, m_ssd_norm_g=m_ssd_norm_g, m_w_ssd_out=m_w_ssd_out, m_attn_q_norm=m_attn_q_norm, m_attn_k_norm=m_attn_k_norm, m_attn_sinks=m_attn_sinks, m_w_attn_out=m_w_attn_out, m_w_mix_out=m_w_mix_out, m_norm_xattn=m_norm_xattn, m_norm_mem=m_norm_mem, m_w_xq=m_w_xq, m_w_xkv=m_w_xkv, m_xattn_q_norm=m_xattn_q_norm, m_xattn_k_norm=m_xattn_k_norm, m_w_xo=m_w_xo, m_norm_mlp=m_norm_mlp, m_w_mlp_up=m_w_mlp_up, m_w_mlp_down=m_w_mlp_down, v_rel_table=v_rel_table, v_norm_mix=v_norm_mix, v_w_in=v_w_in, v_gate_bias=v_gate_bias, v_conv_dw_w=v_conv_dw_w, v_conv_dw_b=v_conv_dw_b, v_conv_ln_g=v_conv_ln_g, v_conv_ln_b=v_conv_ln_b, v_w_conv_out=v_w_conv_out, v_ssd_conv_w=v_ssd_conv_w, v_ssd_conv_b=v_ssd_conv_b, v_ssd_dt_bias=v_ssd_dt_bias, v_ssd_A_log=v_ssd_A_log, v_ssd_D=v_ssd_D, v_ssd_norm_g=v_ssd_norm_g, v_w_ssd_out=v_w_ssd_out, v_attn_q_norm=v_attn_q_norm, v_attn_k_norm=v_attn_k_norm, v_attn_sinks=v_attn_sinks, v_w_attn_out=v_w_attn_out, v_w_mix_out=v_w_mix_out, v_norm_xattn=v_norm_xattn, v_norm_mem=v_norm_mem, v_w_xq=v_w_xq, v_w_xkv=v_w_xkv, v_xattn_q_norm=v_xattn_q_norm, v_xattn_k_norm=v_xattn_k_norm, v_w_xo=v_w_xo, v_norm_mlp=v_norm_mlp, v_w_mlp_up=v_w_mlp_up, v_w_mlp_down=v_w_mlp_down)
    weights = {n: given[n] for n in TWIN_WEIGHTS}
    shared = {n: given[n] for n in SHARED_INPUTS}
    per_example = {n: given[n] for n in ['x', 'mem']}
    grad_fn = _jax.value_and_grad(_loss, argnums=(0, 1))

    def one_microbatch(ex, loss_target):
        ex = dict(ex)
        diff = ex.pop(TWIN_DIFF_INPUT)
        return grad_fn(weights, diff, {**shared, **ex}, loss_target)

    if N_MICROBATCH == 1:
        loss, (grad_w, grad_x) = one_microbatch(per_example, given["loss_target"])
    else:
        def body(carry, xs):
            loss_sum, grad_sum = carry
            l_k, (gw_k, gx_k) = one_microbatch(xs[0], xs[1])
            with _jax.named_scope("update"):
                return (loss_sum + l_k, _jax.tree.map(_jnp.add, grad_sum, gw_k)), gx_k

        init = (_jnp.zeros((), _jnp.float32), _jax.tree.map(_jnp.zeros_like, weights))
        (loss, grad_w), grad_x = _jax.lax.scan(body, init, (per_example, given["loss_target"]))
    with _jax.named_scope("update"):
        delta_w, new_m, new_v = {}, {}, {}
        for n in TWIN_WEIGHTS:
            delta_w[n], new_m[n], new_v[n] = _adamw(weights[n], grad_w[n], given["m_" + n], given["v_" + n])
    return (loss, grad_x, *[grad_w[n] for n in TWIN_WEIGHTS], *[delta_w[n] for n in TWIN_WEIGHTS],
            *[new_m[n] for n in TWIN_WEIGHTS], *[new_v[n] for n in TWIN_WEIGHTS])
```

```python
import functools
import math

import numpy as np
import jax
import jax.numpy as jnp
from jax import lax
from jax.experimental import pallas as pl
from jax.experimental.pallas import tpu as pltpu

f32, bf16 = jnp.float32, jnp.bfloat16

D = 1024
EPS = 1e-6
NEG_INF = -1e30
CONV_K = 31
SSD_INNER = 2048
SSD_HEADS = 32
SSD_P = 64
SSD_GROUPS = 4
SSD_N = 128
SSD_K = 4
CHUNK = 128
SSD_XBC = 3072
ATTN_HEADS = 16
ATTN_KV = 4
ATTN_D = 64
BLK = 128
REL_BUCKETS = 32
XH = 4
XD = 256
MLP = 4096
MEM_LEN = 256

OFF_Z, OFF_XBC, OFF_DT, OFF_Q, OFF_K, OFF_V, OFF_GATE, IN_COLS = 2048, 4096, 7168, 7200, 8224, 8480, 8736, 11808
C_CONV, C_Z, C_XBC, C_Q, C_G, C_K, C_V, C_DT, PCOLS = 0, 2048, 4096, 7168, 8192, 11264, 11520, 11776, 12288

ADAM_LR, ADAM_B1, ADAM_B2, ADAM_EPS, ADAM_WD, ADAM_STEP = 0.001, 0.9, 0.999, 1e-08, 0.01, 10

VMEM_LIMIT_BYTES = 56 * 1024 * 1024
N_CHIPS = 4
BIG = ["w_in", "w_conv_out", "w_ssd_out", "w_attn_out", "w_mix_out", "w_xq", "w_xkv", "w_xo", "w_mlp_up", "w_mlp_down"]
BIG_ROWS = dict(w_in=2952, w_conv_out=256, w_ssd_out=512, w_attn_out=256, w_mix_out=256, w_xq=256, w_xkv=512,
                w_xo=256, w_mlp_up=1024, w_mlp_down=1024)
COL_SHARDED = dict(w_in=IN_COLS, w_xkv=2048, w_mlp_up=4096)
ROWS_L = 7424
HALF_L = ROWS_L // 2
ADD_TILE = HALF_L // 8


def _params(sem=None):
    return pltpu.CompilerParams(dimension_semantics=sem, vmem_limit_bytes=VMEM_LIMIT_BYTES)


NN, NT, TN = ((1,), (0,)), ((1,), (1,)), ((0,), (0,))


def _dg(a, b, dims):
    return lax.dot_general(a.astype(bf16), b.astype(bf16), (dims, ((), ())), preferred_element_type=f32)


@jax.custom_vjp
def dot_nn(a, b):
    return _dg(a, b, NN)


dot_nn.defvjp(lambda a, b: (_dg(a, b, NN), (a, b)), lambda r, g: (_dg(g, r[1], NT), _dg(r[0], g, TN)))


@jax.custom_vjp
def dot_nt(a, b):
    return _dg(a, b, NT)


dot_nt.defvjp(lambda a, b: (_dg(a, b, NT), (a, b)), lambda r, g: (_dg(g, r[1], NN), _dg(g, r[0], TN)))


@jax.custom_vjp
def dot_tn(a, b):
    return _dg(a, b, TN)


dot_tn.defvjp(lambda a, b: (_dg(a, b, TN), (a, b)), lambda r, g: (_dg(r[1], g, NT), _dg(r[0], g, NN)))


def _tri(n, upper):
    r = lax.broadcasted_iota(jnp.int32, (n, n), 0)
    c = lax.broadcasted_iota(jnp.int32, (n, n), 1)
    return jnp.where((c >= r) if upper else (r >= c), 1.0, 0.0).astype(f32)


def _hdot(a, b):
    return lax.dot_general(a, b, (NN, ((), ())), preferred_element_type=f32, precision=lax.Precision.HIGHEST)


@jax.custom_vjp
def cumsum_rows(x):
    return _hdot(_tri(x.shape[0], False), x)


cumsum_rows.defvjp(lambda x: (_hdot(_tri(x.shape[0], False), x), None),
                   lambda _, g: (_hdot(_tri(g.shape[0], True), g),))


def _pick(n, prefs):
    for p in prefs:
        if n % p == 0:
            return p
    return n


def matmul(a, b, mode, name, out_dtype=f32, residual=None):
    if mode == "nn":
        (m, k), n = a.shape, b.shape[1]
    elif mode == "nt":
        (m, k), n = a.shape, b.shape[0]
    else:
        (k, m), n = a.shape, b.shape[1]
    tm = _pick(m, (1024, 512, 256))
    tn = _pick(n, (1024, 512, 256, 128))
    tk = _pick(k, (1024, 512))
    nk = k // tk
    dims = dict(nn=NN, nt=NT, tn=TN)[mode]
    a_spec = pl.BlockSpec((tk, tm), lambda i, j, l: (l, i)) if mode == "tn" else pl.BlockSpec((tm, tk), lambda i, j, l: (i, l))
    b_spec = pl.BlockSpec((tn, tk), lambda i, j, l: (j, l)) if mode == "nt" else pl.BlockSpec((tk, tn), lambda i, j, l: (l, j))
    o_spec = pl.BlockSpec((tm, tn), lambda i, j, l: (i, j))
    has_res = residual is not None

    def body(*refs):
        a_ref, b_ref = refs[0], refs[1]
        r_ref = refs[2] if has_res else None
        o_ref = refs[2 + has_res]
        acc = refs[3 + has_res]
        l = pl.program_id(2)

        @pl.when(l == 0)
        def _():
            acc[...] = jnp.zeros(acc.shape, f32)

        acc[...] += _dg(a_ref[...], b_ref[...], dims)

        @pl.when(l == nk - 1)
        def _():
            r = acc[...]
            if has_res:
                r = r + r_ref[...]
            o_ref[...] = r.astype(o_ref.dtype)

    ins = [a, b] + ([residual] if has_res else [])
    return pl.pallas_call(
        body, name=name, grid=(m // tm, n // tn, nk),
        in_specs=[a_spec, b_spec] + ([o_spec] if has_res else []),
        out_specs=o_spec, out_shape=jax.ShapeDtypeStruct((m, n), out_dtype),
        scratch_shapes=[pltpu.VMEM((tm, tn), f32)],
        compiler_params=_params(("parallel", "parallel", "arbitrary")),
    )(*ins)


def _tile_in_specs(tiled, params, tile):
    specs = [pl.BlockSpec((tile, cols), functools.partial(lambda i, cb: (i, cb), cb=cb)) for (_, cols, cb) in tiled]
    specs += [pl.BlockSpec(p.shape, lambda i: (0, 0)) for p in params]
    return specs


def tile_fwd(name, f, tiled, params, outs, tile):
    rows = tiled[0][0].shape[0]
    nt, npar = len(tiled), len(params)

    def body(*refs):
        xs = [r[...].astype(f32) for r in refs[:nt + npar]]
        res = f(*xs)
        for o_ref, o in zip(refs[nt + npar:], res):
            o_ref[...] = o.astype(o_ref.dtype)

    return pl.pallas_call(
        body, name=name, grid=(rows // tile,),
        in_specs=_tile_in_specs(tiled, params, tile),
        out_specs=[pl.BlockSpec((tile, c), lambda i: (i, 0)) for c, _ in outs],
        out_shape=[jax.ShapeDtypeStruct((rows, c), dt) for c, dt in outs],
        compiler_params=_params(("parallel",)),
    )(*[t[0] for t in tiled], *params)


def tile_bwd(name, f, tiled, params, cots, d_dtypes, tile, addend=None):
    rows = tiled[0][0].shape[0]
    nt, npar, nc = len(tiled), len(params), len(cots)
    has_add = addend is not None

    def body(*refs):
        i = pl.program_id(0)
        xs = [r[...].astype(f32) for r in refs[:nt + npar]]
        gs = tuple(r[...].astype(f32) for r in refs[nt + npar:nt + npar + nc])
        pos = nt + npar + nc
        add_ref = refs[pos] if has_add else None
        pos += has_add
        dt_refs, dp_refs = refs[pos:pos + nt], refs[pos + nt:pos + nt + npar]
        _, vjp = jax.vjp(f, *xs)
        ds = vjp(gs)
        for k in range(nt):
            d = ds[k]
            if has_add and k == 0:
                d = d + add_ref[...]
            dt_refs[k][...] = d.astype(dt_refs[k].dtype)

        @pl.when(i == 0)
        def _():
            for r in dp_refs:
                r[...] = jnp.zeros(r.shape, f32)

        for k in range(npar):
            dp_refs[k][...] += ds[nt + k]

    in_specs = _tile_in_specs(tiled, params, tile)
    in_specs += [pl.BlockSpec((tile, c.shape[1]), lambda i: (i, 0)) for c in cots]
    ins = [t[0] for t in tiled] + list(params) + list(cots)
    if has_add:
        in_specs.append(pl.BlockSpec((tile, addend.shape[1]), lambda i: (i, 0)))
        ins.append(addend)
    out_specs = [pl.BlockSpec((tile, cols), lambda i: (i, 0)) for (_, cols, _) in tiled]
    out_specs += [pl.BlockSpec(p.shape, lambda i: (0, 0)) for p in params]
    out_shape = [jax.ShapeDtypeStruct((rows, cols), dt) for (_, cols, _), dt in zip(tiled, d_dtypes)]
    out_shape += [jax.ShapeDtypeStruct(p.shape, f32) for p in params]
    res = pl.pallas_call(
        body, name=name, grid=(rows // tile,), in_specs=in_specs, out_specs=out_specs, out_shape=out_shape,
        compiler_params=_params(("arbitrary",)),
    )(*ins)
    return res[:nt], res[nt:]


def f_rms(h, g):
    return (h * lax.rsqrt(jnp.mean(h * h, axis=-1, keepdims=True) + EPS) * g,)


def f_ln_silu(x, g, b):
    mu = jnp.mean(x, axis=-1, keepdims=True)
    xc = x - mu
    y = xc * lax.rsqrt(jnp.mean(xc * xc, axis=-1, keepdims=True) + EPS) * g + b
    return (y * jax.nn.sigmoid(y),)


def f_merge(pa, pb, pc, ya, yb, yc, gb):
    ga = jax.nn.sigmoid(pa + gb[:, 0:D])
    gb_ = jax.nn.sigmoid(pb + gb[:, D:2 * D])
    gc = jax.nn.sigmoid(pc + gb[:, 2 * D:3 * D])
    return (ga * ya + gb_ * yb + gc * yc,)


def f_relu2(u):
    return (jnp.square(jnp.maximum(u, 0.0)),)


HALO = 32
CONV_CB = 256
SUB = 128


def _silu_grad(p):
    s = jax.nn.sigmoid(p)
    return s * (1.0 + p * (1.0 - s))


def conv_fwd(name, src, col0, gate_col0, w, b, glu, silu):
    t_len = src.shape[0]
    k_taps, c = w.shape
    tt = min(512, t_len)
    nt_ = t_len // tt
    cb0, gb0 = col0 // CONV_CB, gate_col0 // CONV_CB

    def body(*refs):
        pos = 0
        x_cur, x_prev = refs[0], refs[1]
        pos = 2
        if glu:
            g_cur, g_prev = refs[2], refs[3]
            pos = 4
        w_ref, b_ref = refs[pos], refs[pos + 1]
        pre_ref = refs[pos + 2]
        act_ref = refs[pos + 3] if silu else None
        xp = refs[-1]
        t = pl.program_id(1)
        cur = x_cur[...]
        tail = x_prev[pl.ds(tt - HALO, HALO), :]
        if glu:
            cur = cur * jax.nn.sigmoid(g_cur[...])
            tail = tail * jax.nn.sigmoid(g_prev[pl.ds(tt - HALO, HALO), :])
        xp[pl.ds(0, HALO), :] = jnp.where(t > 0, tail, 0.0)
        xp[pl.ds(HALO, tt), :] = cur
        for s in range(tt // SUB):
            acc = jnp.broadcast_to(b_ref[...], (SUB, CONV_CB))
            for j in range(k_taps):
                acc = acc + xp[pl.ds(s * SUB + HALO - (k_taps - 1) + j, SUB), :] * w_ref[pl.ds(j, 1), :]
            pre_ref[pl.ds(s * SUB, SUB), :] = acc
            if silu:
                act_ref[pl.ds(s * SUB, SUB), :] = acc * jax.nn.sigmoid(acc)

    blk = lambda off: pl.BlockSpec((tt, CONV_CB), functools.partial(lambda j, t, off: (t, off + j), off=off))
    prev = lambda off: pl.BlockSpec((tt, CONV_CB), functools.partial(lambda j, t, off: (jnp.maximum(t - 1, 0), off + j), off=off))
    in_specs, ins = [blk(cb0), prev(cb0)], [src, src]
    if glu:
        in_specs += [blk(gb0), prev(gb0)]
        ins += [src, src]
    in_specs += [pl.BlockSpec((k_taps, CONV_CB), lambda j, t: (0, j)), pl.BlockSpec((1, CONV_CB), lambda j, t: (0, j))]
    ins += [w, b]
    o_spec = pl.BlockSpec((tt, CONV_CB), lambda j, t: (t, j))
    n_out = 2 if silu else 1
    res = pl.pallas_call(
        body, name=name, grid=(c // CONV_CB, nt_), in_specs=in_specs, out_specs=[o_spec] * n_out,
        out_shape=[jax.ShapeDtypeStruct((t_len, c), f32)] * n_out,
        scratch_shapes=[pltpu.VMEM((HALO + tt, CONV_CB), f32)],
        compiler_params=_params(("parallel", "arbitrary")),
    )(*ins)
    return res


def conv_bwd(name, dy, pre, src, col0, gate_col0, w, glu, silu, out_dtype):
    t_len = src.shape[0]
    k_taps, c = w.shape
    tt = min(512, t_len)
    nt_ = t_len // tt
    cb0, gb0 = col0 // CONV_CB, gate_col0 // CONV_CB

    def body(*refs):
        dy_cur, dy_next = refs[0], refs[1]
        pos = 2
        if silu:
            p_cur, p_next = refs[2], refs[3]
            pos = 4
        x_cur, x_prev = refs[pos], refs[pos + 1]
        pos += 2
        if glu:
            g_cur, g_prev = refs[pos], refs[pos + 1]
            pos += 2
        w_ref = refs[pos]
        pos += 1
        n_dx = 2 if glu else 1
        dx_refs = refs[pos:pos + n_dx]
        dw_ref, db_ref = refs[pos + n_dx], refs[pos + n_dx + 1]
        dp, xp = refs[-2], refs[-1]
        t = pl.program_id(1)

        dcur = dy_cur[...]
        dhead = dy_next[pl.ds(0, HALO), :]
        if silu:
            dcur = dcur * _silu_grad(p_cur[...])
            dhead = dhead * _silu_grad(p_next[pl.ds(0, HALO), :])
        dp[pl.ds(0, tt), :] = dcur
        dp[pl.ds(tt, HALO), :] = jnp.where(t < nt_ - 1, dhead, 0.0)
        cur = x_cur[...]
        tail = x_prev[pl.ds(tt - HALO, HALO), :]
        if glu:
            cur = cur * jax.nn.sigmoid(g_cur[...])
            tail = tail * jax.nn.sigmoid(g_prev[pl.ds(tt - HALO, HALO), :])
        xp[pl.ds(0, HALO), :] = jnp.where(t > 0, tail, 0.0)
        xp[pl.ds(HALO, tt), :] = cur

        @pl.when(t == 0)
        def _():
            dw_ref[...] = jnp.zeros(dw_ref.shape, f32)
            db_ref[...] = jnp.zeros(db_ref.shape, f32)

        for s in range(tt // SUB):
            d_sub = dp[pl.ds(s * SUB, SUB), :]
            acc = jnp.zeros((SUB, CONV_CB), f32)
            for j in range(k_taps):
                acc = acc + dp[pl.ds(s * SUB + (k_taps - 1) - j, SUB), :] * w_ref[pl.ds(j, 1), :]
                xs = xp[pl.ds(s * SUB + HALO - (k_taps - 1) + j, SUB), :]
                dw_ref[pl.ds(j, 1), :] += jnp.sum(d_sub * xs, axis=0, keepdims=True)
            db_ref[...] += jnp.sum(d_sub, axis=0, keepdims=True)
            if glu:
                a = x_cur[pl.ds(s * SUB, SUB), :]
                sg = jax.nn.sigmoid(g_cur[pl.ds(s * SUB, SUB), :])
                dx_refs[0][pl.ds(s * SUB, SUB), :] = (acc * sg).astype(out_dtype)
                dx_refs[1][pl.ds(s * SUB, SUB), :] = (acc * a * sg * (1.0 - sg)).astype(out_dtype)
            else:
                dx_refs[0][pl.ds(s * SUB, SUB), :] = acc.astype(out_dtype)

    blk = lambda off: pl.BlockSpec((tt, CONV_CB), functools.partial(lambda j, t, off: (t, off + j), off=off))
    prev = lambda off: pl.BlockSpec((tt, CONV_CB), functools.partial(lambda j, t, off: (jnp.maximum(t - 1, 0), off + j), off=off))
    nxt = pl.BlockSpec((tt, CONV_CB), lambda j, t: (jnp.minimum(t + 1, nt_ - 1), j))
    in_specs, ins = [blk(0), nxt], [dy, dy]
    if silu:
        in_specs += [blk(0), nxt]
        ins += [pre, pre]
    in_specs += [blk(cb0), prev(cb0)]
    ins += [src, src]
    if glu:
        in_specs += [blk(gb0), prev(gb0)]
        ins += [src, src]
    in_specs.append(pl.BlockSpec((k_taps, CONV_CB), lambda j, t: (0, j)))
    ins.append(w)
    n_dx = 2 if glu else 1
    o_spec = pl.BlockSpec((tt, CONV_CB), lambda j, t: (t, j))
    out_specs = [o_spec] * n_dx + [pl.BlockSpec((k_taps, CONV_CB), lambda j, t: (0, j)), pl.BlockSpec((1, CONV_CB), lambda j, t: (0, j))]
    out_shape = [jax.ShapeDtypeStruct((t_len, c), out_dtype)] * n_dx + [jax.ShapeDtypeStruct((k_taps, c), f32), jax.ShapeDtypeStruct((1, c), f32)]
    return pl.pallas_call(
        body, name=name, grid=(c // CONV_CB, nt_), in_specs=in_specs, out_specs=out_specs, out_shape=out_shape,
        scratch_shapes=[pltpu.VMEM((tt + HALO, CONV_CB), f32), pltpu.VMEM((HALO + tt, CONV_CB), f32)],
        compiler_params=_params(("parallel", "arbitrary")),
    )(*ins)


GH = SSD_HEADS // SSD_GROUPS
GW = GH * SSD_P


def _softplus(x):
    return jnp.maximum(x, 0.0) + jnp.log1p(jnp.exp(-jnp.abs(x)))


def f_ssd(hbase, x, z, bm, cm, dtraw, s_in, dt_bias, a_log, dskip, ng):
    q = x.shape[0]
    dt = _softplus(dtraw + dt_bias)
    da = dt * (-jnp.exp(a_log))
    cs = cumsum_rows(da)
    g_cb = dot_nt(cm, bm)
    lane = lax.broadcasted_iota(jnp.int32, (q, 128), 1)
    lane1 = lax.broadcasted_iota(jnp.int32, (1, 128), 1)
    li = lax.broadcasted_iota(jnp.int32, (q, q), 0)
    si = lax.broadcasted_iota(jnp.int32, (q, q), 1)
    causal = li >= si
    last = lax.broadcasted_iota(jnp.int32, (q, 1), 0) == q - 1
    ys, s_outs = [], []
    for r in range(GH):
        sel = lane == hbase + r
        dt_h = jnp.sum(jnp.where(sel, dt, 0.0), axis=1, keepdims=True)
        cs_h = jnp.sum(jnp.where(sel, cs, 0.0), axis=1, keepdims=True)
        cs_last = jnp.sum(jnp.where(last, cs_h, 0.0), axis=0, keepdims=True)
        d_h = jnp.sum(jnp.where(lane1 == hbase + r, dskip, 0.0), axis=1, keepdims=True)
        x_h = x[:, r * SSD_P:(r + 1) * SSD_P]
        s_h = s_in[r * SSD_P:(r + 1) * SSD_P, :]
        xdt = x_h * dt_h
        m1 = jnp.broadcast_to(cs_h, (q, q))
        decay = jnp.where(causal, jnp.exp(jnp.where(causal, m1 - m1.T, 0.0)), 0.0)
        y_diag = dot_nn(g_cb * decay, xdt)
        s_c = dot_tn(xdt * jnp.exp(cs_last - cs_h), bm)
        y_off = dot_nt(cm, s_h) * jnp.exp(cs_h)
        ys.append(y_diag + y_off + x_h * d_h)
        s_outs.append(s_h * jnp.exp(cs_last) + s_c)
    y = jnp.concatenate(ys, axis=1) * (z * jax.nn.sigmoid(z))
    y = y * lax.rsqrt(jnp.mean(y * y, axis=-1, keepdims=True) + EPS) * ng
    return y, jnp.concatenate(s_outs, axis=0)


def _ssd_in_specs(cmap):
    return [
        pl.BlockSpec((CHUNK, GW), lambda g, c: (cmap(c), g)),
        pl.BlockSpec((CHUNK, GW), lambda g, c: (cmap(c), C_Z // GW + g)),
        pl.BlockSpec((CHUNK, SSD_N), lambda g, c: (cmap(c), SSD_INNER // SSD_N + g)),
        pl.BlockSpec((CHUNK, SSD_N), lambda g, c: (cmap(c), SSD_INNER // SSD_N + SSD_GROUPS + g)),
        pl.BlockSpec((CHUNK, 128), lambda g, c: (cmap(c), C_DT // 128)),
    ]


_SSD_PARAM_SPECS = [pl.BlockSpec((1, 128), lambda g, c: (0, 0))] * 3 + [pl.BlockSpec((1, GW), lambda g, c: (0, g))]


def ssd_fwd(name, xact, proj, dt_bias, a_log, dskip, ng):
    t_len = xact.shape[0]
    nc = t_len // CHUNK

    def body(x_ref, z_ref, b_ref, c_ref, dt_ref, p1, p2, p3, p4, y_ref, st_ref, s_scr):
        g, c = pl.program_id(0), pl.program_id(1)

        @pl.when(c == 0)
        def _():
            s_scr[...] = jnp.zeros(s_scr.shape, f32)

        s_in = s_scr[...]
        st_ref[0, 0] = s_in
        y, s_out = f_ssd(g * GH, x_ref[...], z_ref[...], b_ref[...], c_ref[...], dt_ref[...], s_in,
                         p1[...], p2[...], p3[...], p4[...])
        y_ref[...] = y.astype(y_ref.dtype)
        s_scr[...] = s_out

    return pl.pallas_call(
        body, name=name, grid=(SSD_GROUPS, nc),
        in_specs=_ssd_in_specs(lambda c: c) + _SSD_PARAM_SPECS,
        out_specs=[pl.BlockSpec((CHUNK, GW), lambda g, c: (c, g)), pl.BlockSpec((1, 1, GW, SSD_N), lambda g, c: (g, c, 0, 0))],
        out_shape=[jax.ShapeDtypeStruct((t_len, SSD_INNER), bf16), jax.ShapeDtypeStruct((SSD_GROUPS, nc, GW, SSD_N), f32)],
        scratch_shapes=[pltpu.VMEM((GW, SSD_N), f32)],
        compiler_params=_params(("parallel", "arbitrary")),
    )(xact, proj, xact, xact, proj, dt_bias, a_log, dskip, ng)


def ssd_bwd(name, xact, proj, states, dy, dt_bias, a_log, dskip, ng):
    t_len = xact.shape[0]
    nc = t_len // CHUNK
    rev = lambda c: nc - 1 - c

    def body(x_ref, z_ref, b_ref, c_ref, dt_ref, st_ref, dy_ref, p1, p2, p3, p4,
             dx_ref, db_ref, dc_ref, dz_ref, ddt_ref, d1, d2, d3, d4, ds_scr):
        g, c = pl.program_id(0), pl.program_id(1)

        @pl.when(c == 0)
        def _():
            ds_scr[...] = jnp.zeros(ds_scr.shape, f32)
            d4[...] = jnp.zeros(d4.shape, f32)

        @pl.when((c == 0) & (g == 0))
        def _():
            for r in (d1, d2, d3):
                r[...] = jnp.zeros(r.shape, f32)

        fn = functools.partial(f_ssd, g * GH)
        _, vjp = jax.vjp(fn, x_ref[...], z_ref[...], b_ref[...], c_ref[...], dt_ref[...], st_ref[0, 0],
                         p1[...], p2[...], p3[...], p4[...])
        dx, dz, db, dc, ddt, ds_in, e1, e2, e3, e4 = vjp((dy_ref[...].astype(f32), ds_scr[...]))
        dx_ref[...] = dx
        dz_ref[...] = dz.astype(dz_ref.dtype)
        db_ref[...] = db
        dc_ref[...] = dc
        ddt_ref[0] = ddt
        ds_scr[...] = ds_in
        d1[...] += e1
        d2[...] += e2
        d3[...] += e3
        d4[...] += e4

    res = pl.pallas_call(
        body, name=name, grid=(SSD_GROUPS, nc),
        in_specs=_ssd_in_specs(rev) + [
            pl.BlockSpec((1, 1, GW, SSD_N), lambda g, c: (g, rev(c), 0, 0)),
            pl.BlockSpec((CHUNK, GW), lambda g, c: (rev(c), g)),
        ] + _SSD_PARAM_SPECS,
        out_specs=[
            pl.BlockSpec((CHUNK, GW), lambda g, c: (rev(c), g)),
            pl.BlockSpec((CHUNK, SSD_N), lambda g, c: (rev(c), g)),
            pl.BlockSpec((CHUNK, SSD_N), lambda g, c: (rev(c), g)),
            pl.BlockSpec((CHUNK, GW), lambda g, c: (rev(c), g)),
            pl.BlockSpec((1, CHUNK, 128), lambda g, c: (g, rev(c), 0)),
        ] + _SSD_PARAM_SPECS,
        out_shape=[
            jax.ShapeDtypeStruct((t_len, SSD_INNER), f32),
            jax.ShapeDtypeStruct((t_len, SSD_GROUPS * SSD_N), f32),
            jax.ShapeDtypeStruct((t_len, SSD_GROUPS * SSD_N), f32),
            jax.ShapeDtypeStruct((t_len, SSD_INNER), bf16),
            jax.ShapeDtypeStruct((SSD_GROUPS, t_len, 128), f32),
            jax.ShapeDtypeStruct((1, 128), f32), jax.ShapeDtypeStruct((1, 128), f32), jax.ShapeDtypeStruct((1, 128), f32),
            jax.ShapeDtypeStruct((1, SSD_INNER), f32),
        ],
        scratch_shapes=[pltpu.VMEM((GW, SSD_N), f32)],
        compiler_params=_params(("arbitrary", "arbitrary")),
    )(xact, proj, xact, xact, proj, states, dy, dt_bias, a_log, dskip, ng)
    return res


def _head_norm(t, g):
    return t * lax.rsqrt(jnp.mean(t * t, axis=-1, keepdims=True) + EPS) * g


def f_swa(has_prev, q, kp, kc, vp, vc, qg, kg, sinks, *bias):
    qi = lax.broadcasted_iota(jnp.int32, (BLK, BLK), 0)
    ki = lax.broadcasted_iota(jnp.int32, (BLK, BLK), 1)
    mask_p = (ki > qi) & has_prev
    mask_c = ki <= qi
    lane1 = lax.broadcasted_iota(jnp.int32, (1, 128), 1)
    scale = ATTN_D ** -0.5
    rep = ATTN_HEADS // ATTN_KV
    outs = []
    for g in range(ATTN_KV):
        sl = slice(g * ATTN_D, (g + 1) * ATTN_D)
        kpn, kcn = _head_norm(kp[:, sl], kg), _head_norm(kc[:, sl], kg)
        vpg, vcg = vp[:, sl], vc[:, sl]
        for r in range(rep):
            h = g * rep + r
            qh = _head_norm(q[:, h * ATTN_D:(h + 1) * ATTN_D], qg)
            lp = jnp.where(mask_p, dot_nt(qh, kpn) * scale + bias[h][:, :BLK], NEG_INF)
            lc = jnp.where(mask_c, dot_nt(qh, kcn) * scale + bias[h][:, BLK:], NEG_INF)
            sink = jnp.sum(jnp.where(lane1 == h, sinks, 0.0), axis=1, keepdims=True)
            m = jnp.maximum(jnp.maximum(jnp.max(lp, axis=-1, keepdims=True), jnp.max(lc, axis=-1, keepdims=True)), sink)
            pp, pc = jnp.exp(lp - m), jnp.exp(lc - m)
            den = jnp.sum(pp, axis=-1, keepdims=True) + jnp.sum(pc, axis=-1, keepdims=True) + jnp.exp(sink - m)
            outs.append((dot_nn(pp, vpg) + dot_nn(pc, vcg)) / den)
    return jnp.concatenate(outs, axis=1)


def _swa_in_specs():
    prev = lambda i: jnp.maximum(i - 1, 0)
    kw = ATTN_KV * ATTN_D
    return [
        pl.BlockSpec((BLK, D), lambda i: (i, C_Q // D)),
        pl.BlockSpec((BLK, kw), lambda i: (prev(i), C_K // kw)),
        pl.BlockSpec((BLK, kw), lambda i: (i, C_K // kw)),
        pl.BlockSpec((BLK, kw), lambda i: (prev(i), C_V // kw)),
        pl.BlockSpec((BLK, kw), lambda i: (i, C_V // kw)),
        pl.BlockSpec((1, ATTN_D), lambda i: (0, 0)),
        pl.BlockSpec((1, ATTN_D), lambda i: (0, 0)),
        pl.BlockSpec((1, 128), lambda i: (0, 0)),
        pl.BlockSpec((ATTN_HEADS, BLK, 2 * BLK), lambda i: (0, 0, 0)),
    ]


def swa_fwd(name, proj, qg, kg, sinks, bias):
    t_len = proj.shape[0]

    def body(q_ref, kp_ref, kc_ref, vp_ref, vc_ref, qg_ref, kg_ref, s_ref, b_ref, o_ref):
        i = pl.program_id(0)
        o = f_swa(i > 0, q_ref[...], kp_ref[...], kc_ref[...], vp_ref[...], vc_ref[...], qg_ref[...], kg_ref[...],
                  s_ref[...], *[b_ref[h] for h in range(ATTN_HEADS)])
        o_ref[...] = o.astype(o_ref.dtype)

    return pl.pallas_call(
        body, name=name, grid=(t_len // BLK,), in_specs=_swa_in_specs(),
        out_specs=pl.BlockSpec((BLK, D), lambda i: (i, 0)), out_shape=jax.ShapeDtypeStruct((t_len, D), bf16),
        compiler_params=_params(("parallel",)),
    )(proj, proj, proj, proj, proj, qg, kg, sinks, bias)


def swa_bwd(name, proj, do, qg, kg, sinks, bias):
    t_len = proj.shape[0]
    kw = ATTN_KV * ATTN_D

    def body(q_ref, kp_ref, kc_ref, vp_ref, vc_ref, qg_ref, kg_ref, s_ref, b_ref, do_ref,
             dq_ref, dkp_ref, dkc_ref, dvp_ref, dvc_ref, dqg_ref, dkg_ref, ds_ref, db_ref):
        i = pl.program_id(0)

        @pl.when(i == 0)
        def _():
            for r in (dqg_ref, dkg_ref, ds_ref, db_ref):
                r[...] = jnp.zeros(r.shape, f32)

        fn = functools.partial(f_swa, i > 0)
        _, vjp = jax.vjp(fn, q_ref[...], kp_ref[...], kc_ref[...], vp_ref[...], vc_ref[...], qg_ref[...], kg_ref[...],
                         s_ref[...], *[b_ref[h] for h in range(ATTN_HEADS)])
        ds = vjp(do_ref[...].astype(f32))
        dq_ref[...] = ds[0].astype(dq_ref.dtype)
        dkp_ref[...] = ds[1]
        dkc_ref[...] = ds[2]
        dvp_ref[...] = ds[3]
        dvc_ref[...] = ds[4]
        dqg_ref[...] += ds[5]
        dkg_ref[...] += ds[6]
        ds_ref[...] += ds[7]
        for h in range(ATTN_HEADS):
            db_ref[h] += ds[8 + h]

    row = lambda w: pl.BlockSpec((BLK, w), lambda i: (i, 0))
    return pl.pallas_call(
        body, name=name, grid=(t_len // BLK,), in_specs=_swa_in_specs() + [row(D)],
        out_specs=[row(D), row(kw), row(kw), row(kw), row(kw),
                   pl.BlockSpec((1, ATTN_D), lambda i: (0, 0)), pl.BlockSpec((1, ATTN_D), lambda i: (0, 0)),
                   pl.BlockSpec((1, 128), lambda i: (0, 0)), pl.BlockSpec((ATTN_HEADS, BLK, 2 * BLK), lambda i: (0, 0, 0))],
        out_shape=[jax.ShapeDtypeStruct((t_len, D), bf16)] + [jax.ShapeDtypeStruct((t_len, kw), f32)] * 4
        + [jax.ShapeDtypeStruct((1, ATTN_D), f32)] * 2 + [jax.ShapeDtypeStruct((1, 128), f32),
                                                          jax.ShapeDtypeStruct((ATTN_HEADS, BLK, 2 * BLK), f32)],
        compiler_params=_params(("arbitrary",)),
    )(proj, proj, proj, proj, proj, qg, kg, sinks, bias, do)


def _bucket_table():
    qi = np.arange(BLK)[:, None] + BLK
    kj = np.arange(2 * BLK)[None, :]
    dist = qi - kj
    max_exact = REL_BUCKETS // 2
    d = np.maximum(dist, 1).astype(np.float32)
    large = max_exact + (np.log(d / max_exact) / math.log(128 / max_exact) * (REL_BUCKETS - max_exact)).astype(np.int32)
    large = np.minimum(large, REL_BUCKETS - 1)
    return np.where(dist < max_exact, np.maximum(dist, 0), large).astype(np.int32)


def relbias_fwd(table, bucket):
    def body(t_ref, bk_ref, o_ref):
        bk = bk_ref[...]
        for h in range(ATTN_HEADS):
            acc = jnp.zeros((BLK, 2 * BLK), f32)
            for b in range(REL_BUCKETS):
                acc = jnp.where(bk == b, t_ref[b, h], acc)
            o_ref[h] = acc

    return pl.pallas_call(
        body, name="relbias_fwd", out_shape=jax.ShapeDtypeStruct((ATTN_HEADS, BLK, 2 * BLK), f32),
        in_specs=[pl.BlockSpec(memory_space=pltpu.SMEM), pl.BlockSpec(memory_space=pltpu.VMEM)],
        out_specs=pl.BlockSpec(memory_space=pltpu.VMEM),
    )(table, bucket)


def relbias_bwd(dbias, bucket):
    n = len(dbias)

    def body(*refs):
        bk = refs[n][...]
        o_ref = refs[n + 1]
        row = lax.broadcasted_iota(jnp.int32, (REL_BUCKETS, 128), 0)
        lane = lax.broadcasted_iota(jnp.int32, (REL_BUCKETS, 128), 1)
        res = jnp.zeros((REL_BUCKETS, 128), f32)
        for h in range(ATTN_HEADS):
            tot = refs[0][h]
            for k in range(1, n):
                tot = tot + refs[k][h]
            for b in range(REL_BUCKETS):
                part = jnp.sum(jnp.sum(jnp.where(bk == b, tot, 0.0), axis=1, keepdims=True), axis=0, keepdims=True)
                res = jnp.where((row == b) & (lane == h), part, res)
        o_ref[...] = res

    return pl.pallas_call(
        body, name="relbias_bwd", out_shape=jax.ShapeDtypeStruct((REL_BUCKETS, 128), f32),
        in_specs=[pl.BlockSpec(memory_space=pltpu.VMEM)] * (n + 1),
        out_specs=pl.BlockSpec(memory_space=pltpu.VMEM),
    )(*dbias, bucket)[:, :ATTN_HEADS]


XA_TILE = 512


def f_xattn(q, k, v, qg, kg):
    qn, kn = _head_norm(q, qg), _head_norm(k, kg)
    logits = dot_nt(qn, kn) * (XD ** -0.5)
    p = jnp.exp(logits - jnp.max(logits, axis=-1, keepdims=True))
    return dot_nn(p / jnp.sum(p, axis=-1, keepdims=True), v)


def _xa_in_specs(tq):
    return [
        pl.BlockSpec((tq, XD), lambda h, i: (i, h)),
        pl.BlockSpec((MEM_LEN, XD), lambda h, i: (0, h)),
        pl.BlockSpec((MEM_LEN, XD), lambda h, i: (0, XH + h)),
        pl.BlockSpec((1, XD), lambda h, i: (0, 0)),
        pl.BlockSpec((1, XD), lambda h, i: (0, 0)),
    ]


def xa_fwd(name, xq, kv, qg, kg):
    t_len = xq.shape[0]
    tq = min(XA_TILE, t_len)

    def body(q_ref, k_ref, v_ref, qg_ref, kg_ref, o_ref):
        o_ref[...] = f_xattn(q_ref[...], k_ref[...], v_ref[...], qg_ref[...], kg_ref[...]).astype(o_ref.dtype)

    return pl.pallas_call(
        body, name=name, grid=(XH, t_len // tq), in_specs=_xa_in_specs(tq),
        out_specs=pl.BlockSpec((tq, XD), lambda h, i: (i, h)), out_shape=jax.ShapeDtypeStruct((t_len, D), bf16),
        compiler_params=_params(("parallel", "parallel")),
    )(xq, kv, kv, qg, kg)


def xa_bwd(name, xq, kv, do, qg, kg):
    t_len = xq.shape[0]
    tq = min(XA_TILE, t_len)

    def body(q_ref, k_ref, v_ref, qg_ref, kg_ref, do_ref, dq_ref, dk_ref, dv_ref, dqg_ref, dkg_ref):
        h, i = pl.program_id(0), pl.program_id(1)

        @pl.when(i == 0)
        def _():
            dk_ref[...] = jnp.zeros(dk_ref.shape, f32)
            dv_ref[...] = jnp.zeros(dv_ref.shape, f32)

        @pl.when((i == 0) & (h == 0))
        def _():
            dqg_ref[...] = jnp.zeros(dqg_ref.shape, f32)
            dkg_ref[...] = jnp.zeros(dkg_ref.shape, f32)

        _, vjp = jax.vjp(f_xattn, q_ref[...], k_ref[...], v_ref[...], qg_ref[...], kg_ref[...])
        dq, dk, dv, dqg, dkg = vjp(do_ref[...].astype(f32))
        dq_ref[...] = dq.astype(dq_ref.dtype)
        dk_ref[...] += dk
        dv_ref[...] += dv
        dqg_ref[...] += dqg
        dkg_ref[...] += dkg

    return pl.pallas_call(
        body, name=name, grid=(XH, t_len // tq),
        in_specs=_xa_in_specs(tq) + [pl.BlockSpec((tq, XD), lambda h, i: (i, h))],
        out_specs=[pl.BlockSpec((tq, XD), lambda h, i: (i, h)),
                   pl.BlockSpec((MEM_LEN, XD), lambda h, i: (0, h)), pl.BlockSpec((MEM_LEN, XD), lambda h, i: (0, h)),
                   pl.BlockSpec((1, XD), lambda h, i: (0, 0)), pl.BlockSpec((1, XD), lambda h, i: (0, 0))],
        out_shape=[jax.ShapeDtypeStruct((t_len, D), bf16), jax.ShapeDtypeStruct((MEM_LEN, D), f32),
                   jax.ShapeDtypeStruct((MEM_LEN, D), f32), jax.ShapeDtypeStruct((1, XD), f32), jax.ShapeDtypeStruct((1, XD), f32)],
        compiler_params=_params(("arbitrary", "arbitrary")),
    )(xq, kv, kv, qg, kg, do)


def loss_head(y, target, tile):
    t_len = y.shape[0]

    def body(y_ref, t_ref, dy_ref, l_ref):
        i = pl.program_id(0)

        @pl.when(i == 0)
        def _():
            l_ref[...] = jnp.zeros(l_ref.shape, f32)

        err = y_ref[...] - t_ref[...]
        dy_ref[...] = err * (1.0 / D)
        l_ref[...] += 0.5 * jnp.sum(jnp.sum(err * err, axis=1, keepdims=True), axis=0, keepdims=True) * (1.0 / D)

    row = pl.BlockSpec((tile, D), lambda i: (i, 0))
    return pl.pallas_call(
        body, name="loss_head", grid=(t_len // tile,), in_specs=[row, row],
        out_specs=[row, pl.BlockSpec((8, 128), lambda i: (0, 0))],
        out_shape=[jax.ShapeDtypeStruct((t_len, D), f32), jax.ShapeDtypeStruct((8, 128), f32)],
        compiler_params=_params(("arbitrary",)),
    )(y, target)


ADAM_BLOCK_ELEMS = 512 * 1024


def adamw(name, w, g, m, v):
    rows, cols = w.shape
    tile = rows
    if rows * cols > ADAM_BLOCK_ELEMS:
        tile = _pick(rows, [t for t in (512, 256, 128, 64, 32, 16, 8) if t * cols <= ADAM_BLOCK_ELEMS])

    def body(w_ref, g_ref, m_ref, v_ref, d_ref, nm_ref, nv_ref):
        gg = g_ref[...]
        nm = ADAM_B1 * m_ref[...] + (1.0 - ADAM_B1) * gg
        nv = ADAM_B2 * v_ref[...] + (1.0 - ADAM_B2) * jnp.square(gg)
        m_hat = nm / (1.0 - ADAM_B1 ** ADAM_STEP)
        v_hat = nv / (1.0 - ADAM_B2 ** ADAM_STEP)
        d_ref[...] = -ADAM_LR * (m_hat / (jnp.sqrt(v_hat) + ADAM_EPS) + ADAM_WD * w_ref[...])
        nm_ref[...] = nm
        nv_ref[...] = nv

    spec = pl.BlockSpec((tile, cols), lambda i: (i, 0))
    return pl.pallas_call(
        body, name=name, grid=(rows // tile,), in_specs=[spec] * 4, out_specs=[spec] * 3,
        out_shape=[jax.ShapeDtypeStruct((rows, cols), f32)] * 3, compiler_params=_params(("parallel",)),
    )(w, g, m, v)


MESH = pl.DeviceIdType.MESH
HBM_SPEC = pl.BlockSpec(memory_space=pltpu.HBM)


def _place():
    x, y, c = lax.axis_index("x"), lax.axis_index("y"), lax.axis_index("c")
    chips = [(1 - x, y), (x, 1 - y), (1 - x, 1 - y)]
    return x, y, c, chips


def _rcopy(src, dst, send_sems, recv_sems, k, to):
    return pltpu.make_async_remote_copy(src_ref=src, dst_ref=dst, send_sem=send_sems.at[k], recv_sem=recv_sems.at[k],
                                        device_id=to, device_id_type=MESH)


def gather_weights(name, src):
    rows = src.shape[0]
    hh = rows // 2

    def body(src_ref, out_ref, send_sems, recv_sems, loc_sem):
        x, y, c, chips = _place()
        me = 2 * x + y
        sibling = (x, y, 1 - c)
        mine = pl.ds(pl.multiple_of(c * hh, 16), hh)
        theirs = pl.ds(pl.multiple_of((1 - c) * hh, 16), hh)
        own = pltpu.make_async_copy(src_ref, out_ref.at[me], loc_sem)
        own.start()
        first = [_rcopy(src_ref.at[mine], out_ref.at[me, mine], send_sems, recv_sems, j, (*chip, c))
                 for j, chip in enumerate(chips)]
        for cp in first:
            cp.start()
        passed = []
        for j, (cx, cy) in enumerate(chips):
            slot = out_ref.at[2 * cx + cy, mine]
            _rcopy(slot, slot, send_sems, recv_sems, j, sibling).wait_recv()
            cp = _rcopy(slot, slot, send_sems, recv_sems, 3 + j, sibling)
            cp.start()
            passed.append(cp)
        for j, (cx, cy) in enumerate(chips):
            slot = out_ref.at[2 * cx + cy, theirs]
            _rcopy(slot, slot, send_sems, recv_sems, 3 + j, sibling).wait_recv()
        for cp in first + passed:
            cp.wait_send()
        own.wait()

    return pl.pallas_call(
        body, name=name, out_shape=jax.ShapeDtypeStruct((N_CHIPS, rows, src.shape[1]), src.dtype),
        in_specs=[HBM_SPEC], out_specs=HBM_SPEC,
        scratch_shapes=[pltpu.SemaphoreType.DMA((6,)), pltpu.SemaphoreType.DMA((6,)), pltpu.SemaphoreType.DMA],
        compiler_params=pltpu.CompilerParams(has_side_effects=True),
    )(src)


def swap_halves(name, g):
    def body(g_ref, out_ref, send_sems, recv_sems):
        x, y, c, _ = _place()
        sibling = (x, y, 1 - c)
        cps = [_rcopy(g_ref.at[s, 1 - c], out_ref.at[s], send_sems, recv_sems, s, sibling) for s in range(N_CHIPS)]
        for cp in cps:
            cp.start()
        for cp in cps:
            cp.wait()

    return pl.pallas_call(
        body, name=name, out_shape=jax.ShapeDtypeStruct((N_CHIPS,) + g.shape[2:], g.dtype),
        in_specs=[HBM_SPEC], out_specs=HBM_SPEC,
        scratch_shapes=[pltpu.SemaphoreType.DMA((N_CHIPS,)), pltpu.SemaphoreType.DMA((N_CHIPS,))],
        compiler_params=pltpu.CompilerParams(has_side_effects=True),
    )(g)


def add_halves(name, g, got, c_arr):
    def body(c_ref, g_ref, r_ref, o_ref):
        o_ref[...] = g_ref[0] + r_ref[...]

    return pl.pallas_call(
        body, name=name,
        grid_spec=pltpu.PrefetchScalarGridSpec(
            num_scalar_prefetch=1, grid=(N_CHIPS, HALF_L // ADD_TILE),
            in_specs=[pl.BlockSpec((1, 1, ADD_TILE, D), lambda s, i, c_ref: (s, c_ref[0], i, 0)),
                      pl.BlockSpec((1, ADD_TILE, D), lambda s, i, c_ref: (s, i, 0))],
            out_specs=pl.BlockSpec((1, ADD_TILE, D), lambda s, i, c_ref: (s, i, 0))),
        out_shape=jax.ShapeDtypeStruct(got.shape, f32), compiler_params=_params(("parallel", "parallel")),
    )(c_arr, g, got)


def scatter_partials(name, p):
    def body(p_ref, out_ref, send_sems, recv_sems):
        x, y, c, chips = _place()
        cps = [_rcopy(p_ref.at[2 * cx + cy], out_ref.at[j], send_sems, recv_sems, j, (cx, cy, c))
               for j, (cx, cy) in enumerate(chips)]
        for cp in cps:
            cp.start()
        for cp in cps:
            cp.wait()

    return pl.pallas_call(
        body, name=name, out_shape=jax.ShapeDtypeStruct((3,) + p.shape[1:], p.dtype),
        in_specs=[HBM_SPEC], out_specs=HBM_SPEC,
        scratch_shapes=[pltpu.SemaphoreType.DMA((3,)), pltpu.SemaphoreType.DMA((3,))],
        compiler_params=pltpu.CompilerParams(has_side_effects=True),
    )(p)


def add_partials(name, p, got, me_arr):
    def body(me_ref, p_ref, r_ref, o_ref):
        o_ref[...] = ((p_ref[0] + r_ref[0]) + r_ref[1]) + r_ref[2]

    return pl.pallas_call(
        body, name=name,
        grid_spec=pltpu.PrefetchScalarGridSpec(
            num_scalar_prefetch=1, grid=(HALF_L // ADD_TILE,),
            in_specs=[pl.BlockSpec((1, ADD_TILE, D), lambda i, me_ref: (me_ref[0], i, 0)),
                      pl.BlockSpec((3, ADD_TILE, D), lambda i, me_ref: (0, i, 0))],
            out_specs=pl.BlockSpec((ADD_TILE, D), lambda i, me_ref: (i, 0))),
        out_shape=jax.ShapeDtypeStruct(p.shape[1:], f32), compiler_params=_params(("parallel",)),
    )(me_arr, p, got)


def share_half(name, r):
    def body(r_ref, out_ref, send_sems, recv_sems, loc_sem):
        x, y, c, _ = _place()
        own = pltpu.make_async_copy(r_ref, out_ref.at[c], loc_sem)
        own.start()
        cp = _rcopy(r_ref, out_ref.at[c], send_sems, recv_sems, 0, (x, y, 1 - c))
        cp.start()
        _rcopy(r_ref, out_ref.at[1 - c], send_sems, recv_sems, 0, (x, y, 1 - c)).wait_recv()
        cp.wait_send()
        own.wait()

    return pl.pallas_call(
        body, name=name, out_shape=jax.ShapeDtypeStruct((2,) + r.shape, r.dtype),
        in_specs=[HBM_SPEC], out_specs=HBM_SPEC,
        scratch_shapes=[pltpu.SemaphoreType.DMA((1,)), pltpu.SemaphoreType.DMA((1,)), pltpu.SemaphoreType.DMA],
        compiler_params=pltpu.CompilerParams(has_side_effects=True),
    )(r)


N_DEV = 8


def allreduce_small(name, buf):
    m_per = buf.shape[0]

    def body(x_ref, out_ref, all_ref, send_sems, recv_sems, local_sem):
        x, y, c, chips = _place()
        me, sibling = (x, y, c), (x, y, 1 - c)

        def rows(px, py, pc):
            return all_ref.at[pl.ds(pl.multiple_of((4 * px + 2 * py + pc) * m_per, 8), m_per), :]

        def copy(k, block, to, src=None):
            return _rcopy(rows(*block) if src is None else src, rows(*block), send_sems, recv_sems, k, to)

        mine = pltpu.make_async_copy(x_ref, rows(*me), local_sem)
        mine.start()
        first = [copy(0, me, sibling, src=x_ref)]
        first += [copy(1 + j, me, (*chip, c), src=x_ref) for j, chip in enumerate(chips)]
        for cp in first:
            cp.start()
        passed = [copy(4 + j, (*chip, c), sibling) for j, chip in enumerate(chips)]
        for j, chip in enumerate(chips):
            copy(1 + j, (*chip, c), me).wait_recv()
            passed[j].start()
        copy(0, sibling, me).wait_recv()
        for j, chip in enumerate(chips):
            copy(4 + j, (*chip, 1 - c), me).wait_recv()
        for cp in first + passed:
            cp.wait_send()
        mine.wait()
        tot = all_ref[pl.ds(0, m_per), :]
        for d in range(1, N_DEV):
            tot = tot + all_ref[pl.ds(d * m_per, m_per), :]
        out_ref[...] = tot

    return pl.pallas_call(
        body, name=name, out_shape=jax.ShapeDtypeStruct((m_per, 128), f32),
        in_specs=[pl.BlockSpec(memory_space=pltpu.VMEM)], out_specs=pl.BlockSpec(memory_space=pltpu.VMEM),
        scratch_shapes=[pltpu.VMEM((N_DEV * m_per, 128), f32), pltpu.SemaphoreType.DMA((7,)), pltpu.SemaphoreType.DMA((7,)),
                        pltpu.SemaphoreType.DMA],
        compiler_params=pltpu.CompilerParams(has_side_effects=True, vmem_limit_bytes=VMEM_LIMIT_BYTES),
    )(buf)


def pack_shards(ws, layer):
    parts = [ws[n][layer].astype(bf16).reshape(BIG_ROWS[n], D) for n in BIG]
    pad = ROWS_L - sum(BIG_ROWS.values())
    return jnp.concatenate(parts + [jnp.zeros((pad, D), bf16)], axis=0)


def unpack_gathered(full):
    out, r0 = {}, 0
    for n in BIG:
        r = BIG_ROWS[n]
        piece = full[:, r0:r0 + r, :]
        r0 += r
        if n in COL_SHARDED:
            cols = COL_SHARDED[n] // N_CHIPS
            piece = piece.reshape(N_CHIPS, -1, cols)
            out[n] = jnp.transpose(piece, (1, 0, 2)).reshape(piece.shape[1], N_CHIPS * cols)
        else:
            out[n] = piece.reshape(N_CHIPS * r, D)
    return out


def pack_grads(gs):
    parts = []
    for n in BIG:
        g = gs[n]
        if n in COL_SHARDED:
            cols = COL_SHARDED[n] // N_CHIPS
            g = jnp.transpose(g.reshape(g.shape[0], N_CHIPS, cols), (1, 0, 2))
        parts.append(g.reshape(N_CHIPS, BIG_ROWS[n], D))
    pad = ROWS_L - sum(BIG_ROWS.values())
    flat = jnp.concatenate(parts + [jnp.zeros((N_CHIPS, pad, D), f32)], axis=1)
    return flat.reshape(N_CHIPS, 2, HALF_L, D)


def unpack_shard(red, shapes):
    out, r0 = {}, 0
    for n in BIG:
        r = BIG_ROWS[n]
        out[n] = red[r0:r0 + r].reshape(shapes[n])
        r0 += r
    return out


def arrange_w_in(w):
    z = jnp.zeros((w.shape[0], PCOLS - C_DT - 32), w.dtype)
    return jnp.concatenate([w[:, :OFF_DT], w[:, OFF_Q:OFF_K], w[:, OFF_GATE:], w[:, OFF_K:OFF_GATE], w[:, OFF_DT:OFF_Q], z], axis=1)


def restore_w_in(g):
    return jnp.concatenate([g[:, :C_Q], g[:, C_DT:C_DT + 32], g[:, C_Q:C_G], g[:, C_K:C_DT], g[:, C_G:C_K]], axis=1)


ROW_TILE = 256


def _pad128(v):
    return jnp.pad(v.reshape(1, -1), ((0, 0), (0, 128 - v.shape[-1])))


def layer_forward(i, h, mem, w, p, bias):
    tile = min(ROW_TILE, h.shape[0])
    s = dict(h=h)
    row = lambda a: a.reshape(1, -1)
    s["u"], = tile_fwd(f"rms_mix{i}", f_rms, [(h, D, 0)], [row(p["norm_mix"])], [(D, bf16)], tile)
    s["proj"] = proj = matmul(s["u"], w["w_in"], "nn", f"mm_in{i}")
    s["c1"], = conv_fwd(f"conv_a{i}", proj, C_CONV, C_CONV + D, p["conv_dw_w"], row(p["conv_dw_b"]), True, False)
    s["ca"], = tile_fwd(f"ln_silu{i}", f_ln_silu, [(s["c1"], D, 0)], [row(p["conv_ln_g"]), row(p["conv_ln_b"])], [(D, bf16)], tile)
    s["ya"] = matmul(s["ca"], w["w_conv_out"], "nn", f"mm_conv_out{i}")
    s["xpre"], s["xact"] = conv_fwd(f"conv_b{i}", proj, C_XBC, C_XBC, p["ssd_conv_w"], row(p["ssd_conv_b"]), False, True)
    ssd_p = (_pad128(p["ssd_dt_bias"]), _pad128(p["ssd_A_log"]), _pad128(p["ssd_D"]), row(p["ssd_norm_g"]))
    s["yB"], s["states"] = ssd_fwd(f"ssd_fwd{i}", s["xact"], proj, *ssd_p)
    s["yb"] = matmul(s["yB"], w["w_ssd_out"], "nn", f"mm_ssd_out{i}")
    swa_p = (row(p["attn_q_norm"]), row(p["attn_k_norm"]), _pad128(p["attn_sinks"]))
    s["oc"] = swa_fwd(f"swa_fwd{i}", proj, *swa_p, bias)
    s["yc"] = matmul(s["oc"], w["w_attn_out"], "nn", f"mm_attn_out{i}")
    gate_cols = [(proj, D, C_G // D + k) for k in range(3)]
    s["merged"], = tile_fwd(f"merge{i}", f_merge, gate_cols + [(s["ya"], D, 0), (s["yb"], D, 0), (s["yc"], D, 0)],
                            [row(p["gate_bias"])], [(D, bf16)], tile)
    s["h1"] = h1 = matmul(s["merged"], w["w_mix_out"], "nn", f"mm_mix_out{i}", residual=h)
    s["hx"], = tile_fwd(f"rms_x{i}", f_rms, [(h1, D, 0)], [row(p["norm_xattn"])], [(D, bf16)], tile)
    s["memh"], = tile_fwd(f"rms_mem{i}", f_rms, [(mem, D, 0)], [row(p["norm_mem"])], [(D, bf16)], MEM_LEN)
    s["xq"] = matmul(s["hx"], w["w_xq"], "nn", f"mm_xq{i}")
    s["kv"] = matmul(s["memh"], w["w_xkv"], "nn", f"mm_xkv{i}")
    s["xo"] = xa_fwd(f"xa_fwd{i}", s["xq"], s["kv"], row(p["xattn_q_norm"]), row(p["xattn_k_norm"]))
    s["h2"] = h2 = matmul(s["xo"], w["w_xo"], "nn", f"mm_xo{i}", residual=h1)
    s["um"], = tile_fwd(f"rms_mlp{i}", f_rms, [(h2, D, 0)], [row(p["norm_mlp"])], [(D, bf16)], tile)
    s["up"] = matmul(s["um"], w["w_mlp_up"], "nn", f"mm_up{i}")
    s["act"], = tile_fwd(f"relu2{i}", f_relu2, [(s["up"], MLP, 0)], [], [(MLP, bf16)], tile)
    h3 = matmul(s["act"], w["w_mlp_down"], "nn", f"mm_down{i}", residual=h2)
    return h3, s


def layer_backward(i, dh3, mem, w, p, bias, s):
    t_len = dh3.shape[0]
    tile = min(ROW_TILE, t_len)
    row = lambda a: a.reshape(1, -1)
    gw, gp = {}, {}
    dact = matmul(dh3, w["w_mlp_down"], "nt", f"mm_down_dx{i}")
    gw["w_mlp_down"] = matmul(s["act"], dh3, "tn", f"mm_down_dw{i}")
    (dup,), _ = tile_bwd(f"relu2_bwd{i}", f_relu2, [(s["up"], MLP, 0)], [], [dact], [bf16], tile)
    gw["w_mlp_up"] = matmul(s["um"], dup, "tn", f"mm_up_dw{i}")
    dum = matmul(dup, w["w_mlp_up"], "nt", f"mm_up_dx{i}")
    (dh2,), (g,) = tile_bwd(f"rms_mlp_bwd{i}", f_rms, [(s["h2"], D, 0)], [row(p["norm_mlp"])], [dum], [f32], tile, addend=dh3)
    gp["norm_mlp"] = g
    dxo = matmul(dh2, w["w_xo"], "nt", f"mm_xo_dx{i}")
    gw["w_xo"] = matmul(s["xo"], dh2, "tn", f"mm_xo_dw{i}")
    dxq, dk, dv, gp["xattn_q_norm"], gp["xattn_k_norm"] = xa_bwd(f"xa_bwd{i}", s["xq"], s["kv"], dxo, row(p["xattn_q_norm"]),
                                                                 row(p["xattn_k_norm"]))
    dkv = jnp.concatenate([dk, dv], axis=1)
    gw["w_xq"] = matmul(s["hx"], dxq, "tn", f"mm_xq_dw{i}")
    dhx = matmul(dxq, w["w_xq"], "nt", f"mm_xq_dx{i}")
    gw["w_xkv"] = matmul(s["memh"], dkv, "tn", f"mm_xkv_dw{i}")
    dmemh = matmul(dkv, w["w_xkv"], "nt", f"mm_xkv_dx{i}")
    _, (g,) = tile_bwd(f"rms_mem_bwd{i}", f_rms, [(mem, D, 0)], [row(p["norm_mem"])], [dmemh], [f32], MEM_LEN)
    gp["norm_mem"] = g
    (dh1,), (g,) = tile_bwd(f"rms_x_bwd{i}", f_rms, [(s["h1"], D, 0)], [row(p["norm_xattn"])], [dhx], [f32], tile, addend=dh2)
    gp["norm_xattn"] = g
    proj = s["proj"]
    dmerged = matmul(dh1, w["w_mix_out"], "nt", f"mm_mix_out_dx{i}")
    gw["w_mix_out"] = matmul(s["merged"], dh1, "tn", f"mm_mix_out_dw{i}")
    gate_cols = [(proj, D, C_G // D + k) for k in range(3)]
    (dpa, dpb, dpc, dya, dyb, dyc), (g,) = tile_bwd(
        f"merge_bwd{i}", f_merge, gate_cols + [(s["ya"], D, 0), (s["yb"], D, 0), (s["yc"], D, 0)], [row(p["gate_bias"])],
        [dmerged], [bf16] * 6, tile)
    gp["gate_bias"] = g
    dca = matmul(dya, w["w_conv_out"], "nt", f"mm_conv_out_dx{i}")
    gw["w_conv_out"] = matmul(s["ca"], dya, "tn", f"mm_conv_out_dw{i}")
    (dc1,), (gp["conv_ln_g"], gp["conv_ln_b"]) = tile_bwd(
        f"ln_silu_bwd{i}", f_ln_silu, [(s["c1"], D, 0)], [row(p["conv_ln_g"]), row(p["conv_ln_b"])], [dca], [f32], tile)
    da, dgate, gp["conv_dw_w"], gp["conv_dw_b"] = conv_bwd(f"conv_a_bwd{i}", dc1, None, proj, C_CONV, C_CONV + D,
                                                           p["conv_dw_w"], True, False, bf16)
    dyB = matmul(dyb, w["w_ssd_out"], "nt", f"mm_ssd_out_dx{i}")
    gw["w_ssd_out"] = matmul(s["yB"], dyb, "tn", f"mm_ssd_out_dw{i}")
    ssd_p = (_pad128(p["ssd_dt_bias"]), _pad128(p["ssd_A_log"]), _pad128(p["ssd_D"]), row(p["ssd_norm_g"]))
    dxs, dbm, dcm, dz, ddt4, g1, g2, g3, g4 = ssd_bwd(f"ssd_bwd{i}", s["xact"], proj, s["states"], dyB, *ssd_p)
    gp["ssd_dt_bias"], gp["ssd_A_log"], gp["ssd_D"], gp["ssd_norm_g"] = g1[:, :SSD_HEADS], g2[:, :SSD_HEADS], g3[:, :SSD_HEADS], g4
    ddt = (ddt4[0] + ddt4[1] + ddt4[2] + ddt4[3]).astype(bf16)
    dxact = jnp.concatenate([dxs, dbm, dcm], axis=1)
    dxbc, gp["ssd_conv_w"], gp["ssd_conv_b"] = conv_bwd(f"conv_b_bwd{i}", dxact, s["xpre"], proj, C_XBC, C_XBC, p["ssd_conv_w"],
                                                        False, True, bf16)
    doc = matmul(dyc, w["w_attn_out"], "nt", f"mm_attn_out_dx{i}")
    gw["w_attn_out"] = matmul(s["oc"], dyc, "tn", f"mm_attn_out_dw{i}")
    swa_p = (row(p["attn_q_norm"]), row(p["attn_k_norm"]), _pad128(p["attn_sinks"]))
    dq, dkp, dkc, dvp, dvc, gp["attn_q_norm"], gp["attn_k_norm"], g, dbias = swa_bwd(f"swa_bwd{i}", proj, doc, *swa_p, bias)
    gp["attn_sinks"] = g[:, :ATTN_HEADS]
    shift = lambda a: jnp.concatenate([a[BLK:], jnp.zeros((BLK, a.shape[1]), a.dtype)], axis=0)
    dk_ = (dkc + shift(dkp)).astype(bf16)
    dv_ = (dvc + shift(dvp)).astype(bf16)
    dproj = jnp.concatenate([da, dgate, dz, dxbc, dq, dpa, dpb, dpc, dk_, dv_, ddt, jnp.zeros((t_len, PCOLS - C_DT - 128), bf16)], axis=1)
    gw["w_in"] = restore_w_in(matmul(s["u"], dproj, "tn", f"mm_in_dw{i}"))
    du = matmul(dproj, w["w_in"], "nt", f"mm_in_dx{i}")
    (dh,), (g,) = tile_bwd(f"rms_mix_bwd{i}", f_rms, [(s["h"], D, 0)], [row(p["norm_mix"])], [du], [f32], tile, addend=dh1)
    gp["norm_mix"] = g
    return dh, gw, gp, dbias


SMALL = ["rel_table", "norm_mix", "gate_bias", "conv_dw_w", "conv_dw_b", "conv_ln_g", "conv_ln_b", "ssd_conv_w", "ssd_conv_b",
         "ssd_dt_bias", "ssd_A_log", "ssd_D", "ssd_norm_g", "attn_q_norm", "attn_k_norm", "attn_sinks", "norm_xattn", "norm_mem",
         "xattn_q_norm", "xattn_k_norm", "norm_mlp"]
SMALL_SHARDED = dict(gate_bias=D, conv_dw_w=D, ssd_conv_w=SSD_XBC)
ORDER = ["rel_table", "norm_mix", "w_in", "gate_bias", "conv_dw_w", "conv_dw_b", "conv_ln_g", "conv_ln_b", "w_conv_out",
         "ssd_conv_w", "ssd_conv_b", "ssd_dt_bias", "ssd_A_log", "ssd_D", "ssd_norm_g", "w_ssd_out", "attn_q_norm", "attn_k_norm",
         "attn_sinks", "w_attn_out", "w_mix_out", "norm_xattn", "norm_mem", "w_xq", "w_xkv", "xattn_q_norm", "xattn_k_norm",
         "w_xo", "norm_mlp", "w_mlp_up", "w_mlp_down"]


def local_step(x, mem, target, rel_table, small, full_w):
    depth = len(full_w)
    bucket = jnp.asarray(_bucket_table())
    bias = relbias_fwd(rel_table, bucket)
    h = x
    saved, ws = [], []
    for i in range(depth):
        w = dict(full_w[i])
        w["w_in"] = arrange_w_in(w["w_in"])
        p = {n: small[n][i] for n in small}
        h, s = layer_forward(i, h, mem, w, p, bias)
        saved.append(s)
        ws.append((w, p))
    dh, loss_tile = loss_head(h, target, min(ROW_TILE, h.shape[0]))
    big_grads, small_grads, dbiases = [None] * depth, [None] * depth, []
    for i in reversed(range(depth)):
        w, p = ws[i]
        dh, gw, gp, dbias = layer_backward(i, dh, mem, w, p, bias, saved[i])
        big_grads[i], small_grads[i] = gw, gp
        dbiases.append(dbias)
    sg = {n: jnp.stack([small_grads[i][n].reshape(small[n].shape[1:]) for i in range(depth)]) for n in small}
    sg["rel_table"] = relbias_bwd(dbiases, bucket)
    return loss_tile, dh, big_grads, sg


def kernel(x, mem, rel_table, norm_mix, w_in, gate_bias, conv_dw_w, conv_dw_b, conv_ln_g, conv_ln_b, w_conv_out, ssd_conv_w, ssd_conv_b, ssd_dt_bias, ssd_A_log, ssd_D, ssd_norm_g, w_ssd_out, attn_q_norm, attn_k_norm, attn_sinks, w_attn_out, w_mix_out, norm_xattn, norm_mem, w_xq, w_xkv, xattn_q_norm, xattn_k_norm, w_xo, norm_mlp, w_mlp_up, w_mlp_down, loss_target, m_rel_table, m_norm_mix, m_w_in, m_gate_bias, m_conv_dw_w, m_conv_dw_b, m_conv_ln_g, m_conv_ln_b, m_w_conv_out, m_ssd_conv_w, m_ssd_conv_b, m_ssd_dt_bias, m_ssd_A_log, m_ssd_D, m_ssd_norm_g, m_w_ssd_out, m_attn_q_norm, m_attn_k_norm, m_attn_sinks, m_w_attn_out, m_w_mix_out, m_norm_xattn, m_norm_mem, m_w_xq, m_w_xkv, m_xattn_q_norm, m_xattn_k_norm, m_w_xo, m_norm_mlp, m_w_mlp_up, m_w_mlp_down, v_rel_table, v_norm_mix, v_w_in, v_gate_bias, v_conv_dw_w, v_conv_dw_b, v_conv_ln_g, v_conv_ln_b, v_w_conv_out, v_ssd_conv_w, v_ssd_conv_b, v_ssd_dt_bias, v_ssd_A_log, v_ssd_D, v_ssd_norm_g, v_w_ssd_out, v_attn_q_norm, v_attn_k_norm, v_attn_sinks, v_w_attn_out, v_w_mix_out, v_norm_xattn, v_norm_mem, v_w_xq, v_w_xkv, v_xattn_q_norm, v_xattn_k_norm, v_w_xo, v_norm_mlp, v_w_mlp_up, v_w_mlp_down):
    a = dict(locals())
    wts = {n: a[n] for n in ORDER}
    ms = {n: a["m_" + n] for n in ORDER}
    vs = {n: a["v_" + n] for n in ORDER}
    depth = norm_mix.shape[0]
    ax, ay, ac = lax.axis_index("x"), lax.axis_index("y"), lax.axis_index("c")
    chip = 2 * ax + ay

    small = {n: wts[n] for n in SMALL if n != "rel_table" and n not in SMALL_SHARDED}
    gather_buf, spans, r0 = [], {}, 0
    for n, width in SMALL_SHARDED.items():
        blk = wts[n]
        q = width // N_CHIPS
        whole = lax.dynamic_update_slice(jnp.zeros(blk.shape[:-1] + (width,), f32), blk, (0, 0, chip * q))
        flat = whole.reshape(-1, 128)
        spans[n] = (r0, flat.shape[0], whole.shape)
        r0 += flat.shape[0]
        gather_buf.append(flat)
    summed = allreduce_small("gather_small_params", jnp.concatenate(gather_buf, axis=0))
    for n, (s0, nr, shp) in spans.items():
        small[n] = summed[s0:s0 + nr].reshape(shp) * 0.5

    full_w = []
    for i in range(depth):
        full_w.append(unpack_gathered(gather_weights(f"gather_weights{i}", pack_shards(wts, i))))

    loss_tile, grad_x, big_grads, sg = local_step(x[0], mem[0], loss_target[0], rel_table, small, full_w)
    loss = lax.psum(loss_tile[0, 0], ("x", "y", "c"))

    c_arr = jnp.reshape(ac, (1,)).astype(jnp.int32)
    me_arr = jnp.reshape(chip, (1,)).astype(jnp.int32)
    shard_shapes = {n: wts[n].shape[1:] for n in BIG}
    red = {n: [] for n in BIG}
    for i in range(depth):
        g = pack_grads(big_grads[i])
        got = swap_halves(f"swap_halves{i}", g)
        part = add_halves(f"add_halves{i}", g, got, c_arr)
        got3 = scatter_partials(f"scatter_partials{i}", part)
        mine = add_partials(f"add_partials{i}", part, got3, me_arr)
        both = share_half(f"share_half{i}", mine)
        for n, blk in unpack_shard(both.reshape(ROWS_L, D), shard_shapes).items():
            red[n].append(blk)
    grads = {n: jnp.stack(red[n]) for n in BIG}

    parts, spans, r0 = [], {}, 0
    for n in SMALL:
        flat = sg[n].reshape(-1)
        nr = -(-flat.shape[0] // 128)
        nr = -(-nr // 8) * 8
        flat = jnp.pad(flat, (0, nr * 128 - flat.shape[0])).reshape(nr, 128)
        spans[n] = (r0, nr, sg[n].shape)
        r0 += nr
        parts.append(flat)
    summed = allreduce_small("allreduce_small_grads", jnp.concatenate(parts, axis=0))
    for n, (s0, nr, shp) in spans.items():
        size = int(np.prod(shp))
        g = summed[s0:s0 + nr].reshape(-1)[:size].reshape(shp)
        if n in SMALL_SHARDED:
            q = SMALL_SHARDED[n] // N_CHIPS
            g = lax.dynamic_slice_in_dim(g, chip * q, q, axis=g.ndim - 1)
        grads[n] = g

    deltas, new_m, new_v = {}, {}, {}
    for n in ORDER:
        shp = wts[n].shape
        two = (lambda t: t.reshape(-1, shp[-1]))
        d, nm, nv = adamw(f"adamw_{n}", two(wts[n]), two(grads[n]), two(ms[n]), two(vs[n]))
        deltas[n], new_m[n], new_v[n] = d.reshape(shp), nm.reshape(shp), nv.reshape(shp)

    return (loss, grad_x[None], *[grads[n] for n in ORDER], *[deltas[n] for n in ORDER],
            *[new_m[n] for n in ORDER], *[new_v[n] for n in ORDER])
```

```python
import functools
import math

import numpy as np
import jax
import jax.numpy as jnp
from jax import lax
from jax.experimental import pallas as pl
from jax.experimental.pallas import tpu as pltpu

f32, bf16 = jnp.float32, jnp.bfloat16

D = 1024
EPS = 1e-6
NEG_INF = -1e30
CONV_K = 31
SSD_INNER = 2048
SSD_HEADS = 32
SSD_P = 64
SSD_GROUPS = 4
SSD_N = 128
SSD_K = 4
CHUNK = 128
SSD_XBC = 3072
ATTN_HEADS = 16
ATTN_KV = 4
ATTN_D = 64
BLK = 128
REL_BUCKETS = 32
XH = 4
XD = 256
MLP = 4096
MEM_LEN = 256

OFF_Z, OFF_XBC, OFF_DT, OFF_Q, OFF_K, OFF_V, OFF_GATE, IN_COLS = 2048, 4096, 7168, 7200, 8224, 8480, 8736, 11808
C_CONV, C_Z, C_XBC, C_Q, C_G, C_K, C_V, C_DT, PCOLS = 0, 2048, 4096, 7168, 8192, 11264, 11520, 11776, 12288

ADAM_LR, ADAM_B1, ADAM_B2, ADAM_EPS, ADAM_WD, ADAM_STEP = 0.001, 0.9, 0.999, 1e-08, 0.01, 10

VMEM_LIMIT_BYTES = 56 * 1024 * 1024
N_CHIPS = 4
BIG = ["w_in", "w_conv_out", "w_ssd_out", "w_attn_out", "w_mix_out", "w_xq", "w_xkv", "w_xo", "w_mlp_up", "w_mlp_down"]
WIDE = ["w_conv_out", "w_ssd_out", "w_attn_out", "w_mix_out", "w_xq", "w_xo", "w_mlp_up", "w_mlp_down"]
WIDE_ROWS = dict(w_conv_out=256, w_ssd_out=512, w_attn_out=256, w_mix_out=256, w_xq=256, w_xo=256, w_mlp_up=1024, w_mlp_down=1024)
WIDE_OFF = {n: sum(WIDE_ROWS[m] for m in WIDE[:k]) for k, n in enumerate(WIDE)}
WIDE_TOTAL = sum(WIDE_ROWS.values())
IN_SHARD = IN_COLS // N_CHIPS
ORIG_SEGS = [(0, OFF_DT, 0), (OFF_DT, OFF_Q, C_DT), (OFF_Q, OFF_K, C_Q), (OFF_K, OFF_GATE, C_K), (OFF_GATE, IN_COLS, C_G)]


def _params(sem=None):
    return pltpu.CompilerParams(dimension_semantics=sem, vmem_limit_bytes=VMEM_LIMIT_BYTES)


NN, NT, TN = ((1,), (0,)), ((1,), (1,)), ((0,), (0,))


def _dg(a, b, dims):
    return lax.dot_general(a.astype(bf16), b.astype(bf16), (dims, ((), ())), preferred_element_type=f32)


@jax.custom_vjp
def dot_nn(a, b):
    return _dg(a, b, NN)


dot_nn.defvjp(lambda a, b: (_dg(a, b, NN), (a, b)), lambda r, g: (_dg(g, r[1], NT), _dg(r[0], g, TN)))


@jax.custom_vjp
def dot_nt(a, b):
    return _dg(a, b, NT)


dot_nt.defvjp(lambda a, b: (_dg(a, b, NT), (a, b)), lambda r, g: (_dg(g, r[1], NN), _dg(g, r[0], TN)))


@jax.custom_vjp
def dot_tn(a, b):
    return _dg(a, b, TN)


dot_tn.defvjp(lambda a, b: (_dg(a, b, TN), (a, b)), lambda r, g: (_dg(r[1], g, NT), _dg(r[0], g, NN)))


def _tri(n, upper):
    r = lax.broadcasted_iota(jnp.int32, (n, n), 0)
    c = lax.broadcasted_iota(jnp.int32, (n, n), 1)
    return jnp.where((c >= r) if upper else (r >= c), 1.0, 0.0).astype(f32)


def _hdot(a, b):
    return lax.dot_general(a, b, (NN, ((), ())), preferred_element_type=f32, precision=lax.Precision.HIGHEST)


@jax.custom_vjp
def cumsum_rows(x):
    return _hdot(_tri(x.shape[0], False), x)


cumsum_rows.defvjp(lambda x: (_hdot(_tri(x.shape[0], False), x), None),
                   lambda _, g: (_hdot(_tri(g.shape[0], True), g),))


def _pick(n, prefs):
    for p in prefs:
        if n % p == 0:
            return p
    return n


def matmul(a, b, mode, name, out_dtype=f32, residual=None):
    if mode == "nn":
        (m, k), n = a.shape, b.shape[1]
    elif mode == "nt":
        (m, k), n = a.shape, b.shape[0]
    else:
        (k, m), n = a.shape, b.shape[1]
    tm = _pick(m, (1024, 512, 256))
    tn = _pick(n, (1024, 512, 256, 128))
    tk = _pick(k, (1024, 512))
    nk = k // tk
    dims = dict(nn=NN, nt=NT, tn=TN)[mode]
    a_spec = pl.BlockSpec((tk, tm), lambda i, j, l: (l, i)) if mode == "tn" else pl.BlockSpec((tm, tk), lambda i, j, l: (i, l))
    b_spec = pl.BlockSpec((tn, tk), lambda i, j, l: (j, l)) if mode == "nt" else pl.BlockSpec((tk, tn), lambda i, j, l: (l, j))
    o_spec = pl.BlockSpec((tm, tn), lambda i, j, l: (i, j))
    has_res = residual is not None

    def body(*refs):
        a_ref, b_ref = refs[0], refs[1]
        r_ref = refs[2] if has_res else None
        o_ref = refs[2 + has_res]
        acc = refs[3 + has_res]
        l = pl.program_id(2)

        @pl.when(l == 0)
        def _():
            acc[...] = jnp.zeros(acc.shape, f32)

        acc[...] += _dg(a_ref[...], b_ref[...], dims)

        @pl.when(l == nk - 1)
        def _():
            r = acc[...]
            if has_res:
                r = r + r_ref[...]
            o_ref[...] = r.astype(o_ref.dtype)

    ins = [a, b] + ([residual] if has_res else [])
    return pl.pallas_call(
        body, name=name, grid=(m // tm, n // tn, nk),
        in_specs=[a_spec, b_spec] + ([o_spec] if has_res else []),
        out_specs=o_spec, out_shape=jax.ShapeDtypeStruct((m, n), out_dtype),
        scratch_shapes=[pltpu.VMEM((tm, tn), f32)],
        compiler_params=_params(("parallel", "parallel", "arbitrary")),
    )(*ins)


def _tile_in_specs(tiled, params, tile):
    specs = [pl.BlockSpec((tile, cols), functools.partial(lambda i, cb: (i, cb), cb=cb)) for (_, cols, cb) in tiled]
    specs += [pl.BlockSpec(p.shape, lambda i: (0, 0)) for p in params]
    return specs


def tile_fwd(name, f, tiled, params, outs, tile):
    rows = tiled[0][0].shape[0]
    nt, npar = len(tiled), len(params)

    def body(*refs):
        xs = [r[...].astype(f32) for r in refs[:nt + npar]]
        res = f(*xs)
        for o_ref, o in zip(refs[nt + npar:], res):
            o_ref[...] = o.astype(o_ref.dtype)

    return pl.pallas_call(
        body, name=name, grid=(rows // tile,),
        in_specs=_tile_in_specs(tiled, params, tile),
        out_specs=[pl.BlockSpec((tile, c), lambda i: (i, 0)) for c, _ in outs],
        out_shape=[jax.ShapeDtypeStruct((rows, c), dt) for c, dt in outs],
        compiler_params=_params(("parallel",)),
    )(*[t[0] for t in tiled], *params)


def tile_bwd(name, f, tiled, params, cots, d_dtypes, tile, addend=None):
    rows = tiled[0][0].shape[0]
    nt, npar, nc = len(tiled), len(params), len(cots)
    has_add = addend is not None

    def body(*refs):
        i = pl.program_id(0)
        xs = [r[...].astype(f32) for r in refs[:nt + npar]]
        gs = tuple(r[...].astype(f32) for r in refs[nt + npar:nt + npar + nc])
        pos = nt + npar + nc
        add_ref = refs[pos] if has_add else None
        pos += has_add
        dt_refs, dp_refs = refs[pos:pos + nt], refs[pos + nt:pos + nt + npar]
        _, vjp = jax.vjp(f, *xs)
        ds = vjp(gs)
        for k in range(nt):
            d = ds[k]
            if has_add and k == 0:
                d = d + add_ref[...]
            dt_refs[k][...] = d.astype(dt_refs[k].dtype)

        @pl.when(i == 0)
        def _():
            for r in dp_refs:
                r[...] = jnp.zeros(r.shape, f32)

        for k in range(npar):
            dp_refs[k][...] += ds[nt + k]

    in_specs = _tile_in_specs(tiled, params, tile)
    in_specs += [pl.BlockSpec((tile, c.shape[1]), lambda i: (i, 0)) for c in cots]
    ins = [t[0] for t in tiled] + list(params) + list(cots)
    if has_add:
        in_specs.append(pl.BlockSpec((tile, addend.shape[1]), lambda i: (i, 0)))
        ins.append(addend)
    out_specs = [pl.BlockSpec((tile, cols), lambda i: (i, 0)) for (_, cols, _) in tiled]
    out_specs += [pl.BlockSpec(p.shape, lambda i: (0, 0)) for p in params]
    out_shape = [jax.ShapeDtypeStruct((rows, cols), dt) for (_, cols, _), dt in zip(tiled, d_dtypes)]
    out_shape += [jax.ShapeDtypeStruct(p.shape, f32) for p in params]
    res = pl.pallas_call(
        body, name=name, grid=(rows // tile,), in_specs=in_specs, out_specs=out_specs, out_shape=out_shape,
        compiler_params=_params(("arbitrary",)),
    )(*ins)
    return res[:nt], res[nt:]


def f_rms(h, g):
    return (h * lax.rsqrt(jnp.mean(h * h, axis=-1, keepdims=True) + EPS) * g,)


def f_ln_silu(x, g, b):
    mu = jnp.mean(x, axis=-1, keepdims=True)
    xc = x - mu
    y = xc * lax.rsqrt(jnp.mean(xc * xc, axis=-1, keepdims=True) + EPS) * g + b
    return (y * jax.nn.sigmoid(y),)


def f_merge(pa, pb, pc, ya, yb, yc, gb):
    ga = jax.nn.sigmoid(pa + gb[:, 0:D])
    gb_ = jax.nn.sigmoid(pb + gb[:, D:2 * D])
    gc = jax.nn.sigmoid(pc + gb[:, 2 * D:3 * D])
    return (ga * ya + gb_ * yb + gc * yc,)


def f_relu2(u):
    return (jnp.square(jnp.maximum(u, 0.0)),)


HALO = 32
CONV_CB = 256
SUB = 128


def _silu_grad(p):
    s = jax.nn.sigmoid(p)
    return s * (1.0 + p * (1.0 - s))


def conv_fwd(name, src, col0, gate_col0, w, b, glu, silu):
    t_len = src.shape[0]
    k_taps, c = w.shape
    tt = min(512, t_len)
    nt_ = t_len // tt
    cb0, gb0 = col0 // CONV_CB, gate_col0 // CONV_CB

    def body(*refs):
        pos = 0
        x_cur, x_prev = refs[0], refs[1]
        pos = 2
        if glu:
            g_cur, g_prev = refs[2], refs[3]
            pos = 4
        w_ref, b_ref = refs[pos], refs[pos + 1]
        pre_ref = refs[pos + 2]
        act_ref = refs[pos + 3] if silu else None
        xp = refs[-1]
        t = pl.program_id(1)
        cur = x_cur[...]
        tail = x_prev[pl.ds(tt - HALO, HALO), :]
        if glu:
            cur = cur * jax.nn.sigmoid(g_cur[...])
            tail = tail * jax.nn.sigmoid(g_prev[pl.ds(tt - HALO, HALO), :])
        xp[pl.ds(0, HALO), :] = jnp.where(t > 0, tail, 0.0)
        xp[pl.ds(HALO, tt), :] = cur
        for s in range(tt // SUB):
            acc = jnp.broadcast_to(b_ref[...], (SUB, CONV_CB))
            for j in range(k_taps):
                acc = acc + xp[pl.ds(s * SUB + HALO - (k_taps - 1) + j, SUB), :] * w_ref[pl.ds(j, 1), :]
            pre_ref[pl.ds(s * SUB, SUB), :] = acc
            if silu:
                act_ref[pl.ds(s * SUB, SUB), :] = acc * jax.nn.sigmoid(acc)

    blk = lambda off: pl.BlockSpec((tt, CONV_CB), functools.partial(lambda j, t, off: (t, off + j), off=off))
    prev = lambda off: pl.BlockSpec((tt, CONV_CB), functools.partial(lambda j, t, off: (jnp.maximum(t - 1, 0), off + j), off=off))
    in_specs, ins = [blk(cb0), prev(cb0)], [src, src]
    if glu:
        in_specs += [blk(gb0), prev(gb0)]
        ins += [src, src]
    in_specs += [pl.BlockSpec((k_taps, CONV_CB), lambda j, t: (0, j)), pl.BlockSpec((1, CONV_CB), lambda j, t: (0, j))]
    ins += [w, b]
    o_spec = pl.BlockSpec((tt, CONV_CB), lambda j, t: (t, j))
    n_out = 2 if silu else 1
    res = pl.pallas_call(
        body, name=name, grid=(c // CONV_CB, nt_), in_specs=in_specs, out_specs=[o_spec] * n_out,
        out_shape=[jax.ShapeDtypeStruct((t_len, c), f32)] * n_out,
        scratch_shapes=[pltpu.VMEM((HALO + tt, CONV_CB), f32)],
        compiler_params=_params(("parallel", "arbitrary")),
    )(*ins)
    return res


def conv_bwd(name, dy, pre, src, col0, gate_col0, w, glu, silu, out_dtype):
    t_len = src.shape[0]
    k_taps, c = w.shape
    tt = min(512, t_len)
    nt_ = t_len // tt
    cb0, gb0 = col0 // CONV_CB, gate_col0 // CONV_CB

    def body(*refs):
        dy_cur, dy_next = refs[0], refs[1]
        pos = 2
        if silu:
            p_cur, p_next = refs[2], refs[3]
            pos = 4
        x_cur, x_prev = refs[pos], refs[pos + 1]
        pos += 2
        if glu:
            g_cur, g_prev = refs[pos], refs[pos + 1]
            pos += 2
        w_ref = refs[pos]
        pos += 1
        n_dx = 2 if glu else 1
        dx_refs = refs[pos:pos + n_dx]
        dw_ref, db_ref = refs[pos + n_dx], refs[pos + n_dx + 1]
        dp, xp = refs[-2], refs[-1]
        t = pl.program_id(1)

        dcur = dy_cur[...]
        dhead = dy_next[pl.ds(0, HALO), :]
        if silu:
            dcur = dcur * _silu_grad(p_cur[...])
            dhead = dhead * _silu_grad(p_next[pl.ds(0, HALO), :])
        dp[pl.ds(0, tt), :] = dcur
        dp[pl.ds(tt, HALO), :] = jnp.where(t < nt_ - 1, dhead, 0.0)
        cur = x_cur[...]
        tail = x_prev[pl.ds(tt - HALO, HALO), :]
        if glu:
            cur = cur * jax.nn.sigmoid(g_cur[...])
            tail = tail * jax.nn.sigmoid(g_prev[pl.ds(tt - HALO, HALO), :])
        xp[pl.ds(0, HALO), :] = jnp.where(t > 0, tail, 0.0)
        xp[pl.ds(HALO, tt), :] = cur

        @pl.when(t == 0)
        def _():
            dw_ref[...] = jnp.zeros(dw_ref.shape, f32)
            db_ref[...] = jnp.zeros(db_ref.shape, f32)

        for s in range(tt // SUB):
            d_sub = dp[pl.ds(s * SUB, SUB), :]
            acc = jnp.zeros((SUB, CONV_CB), f32)
            for j in range(k_taps):
                acc = acc + dp[pl.ds(s * SUB + (k_taps - 1) - j, SUB), :] * w_ref[pl.ds(j, 1), :]
                xs = xp[pl.ds(s * SUB + HALO - (k_taps - 1) + j, SUB), :]
                dw_ref[pl.ds(j, 1), :] += jnp.sum(d_sub * xs, axis=0, keepdims=True)
            db_ref[...] += jnp.sum(d_sub, axis=0, keepdims=True)
            if glu:
                a = x_cur[pl.ds(s * SUB, SUB), :]
                sg = jax.nn.sigmoid(g_cur[pl.ds(s * SUB, SUB), :])
                dx_refs[0][pl.ds(s * SUB, SUB), :] = (acc * sg).astype(out_dtype)
                dx_refs[1][pl.ds(s * SUB, SUB), :] = (acc * a * sg * (1.0 - sg)).astype(out_dtype)
            else:
                dx_refs[0][pl.ds(s * SUB, SUB), :] = acc.astype(out_dtype)

    blk = lambda off: pl.BlockSpec((tt, CONV_CB), functools.partial(lambda j, t, off: (t, off + j), off=off))
    prev = lambda off: pl.BlockSpec((tt, CONV_CB), functools.partial(lambda j, t, off: (jnp.maximum(t - 1, 0), off + j), off=off))
    nxt = pl.BlockSpec((tt, CONV_CB), lambda j, t: (jnp.minimum(t + 1, nt_ - 1), j))
    in_specs, ins = [blk(0), nxt], [dy, dy]
    if silu:
        in_specs += [blk(0), nxt]
        ins += [pre, pre]
    in_specs += [blk(cb0), prev(cb0)]
    ins += [src, src]
    if glu:
        in_specs += [blk(gb0), prev(gb0)]
        ins += [src, src]
    in_specs.append(pl.BlockSpec((k_taps, CONV_CB), lambda j, t: (0, j)))
    ins.append(w)
    n_dx = 2 if glu else 1
    o_spec = pl.BlockSpec((tt, CONV_CB), lambda j, t: (t, j))
    out_specs = [o_spec] * n_dx + [pl.BlockSpec((k_taps, CONV_CB), lambda j, t: (0, j)), pl.BlockSpec((1, CONV_CB), lambda j, t: (0, j))]
    out_shape = [jax.ShapeDtypeStruct((t_len, c), out_dtype)] * n_dx + [jax.ShapeDtypeStruct((k_taps, c), f32), jax.ShapeDtypeStruct((1, c), f32)]
    return pl.pallas_call(
        body, name=name, grid=(c // CONV_CB, nt_), in_specs=in_specs, out_specs=out_specs, out_shape=out_shape,
        scratch_shapes=[pltpu.VMEM((tt + HALO, CONV_CB), f32), pltpu.VMEM((HALO + tt, CONV_CB), f32)],
        compiler_params=_params(("parallel", "arbitrary")),
    )(*ins)


GH = SSD_HEADS // SSD_GROUPS
GW = GH * SSD_P


def _softplus(x):
    return jnp.maximum(x, 0.0) + jnp.log1p(jnp.exp(-jnp.abs(x)))


def f_ssd(hbase, x, z, bm, cm, dtraw, s_in, dt_bias, a_log, dskip, ng):
    q = x.shape[0]
    dt = _softplus(dtraw + dt_bias)
    da = dt * (-jnp.exp(a_log))
    cs = cumsum_rows(da)
    g_cb = dot_nt(cm, bm)
    lane = lax.broadcasted_iota(jnp.int32, (q, 128), 1)
    lane1 = lax.broadcasted_iota(jnp.int32, (1, 128), 1)
    li = lax.broadcasted_iota(jnp.int32, (q, q), 0)
    si = lax.broadcasted_iota(jnp.int32, (q, q), 1)
    causal = li >= si
    last = lax.broadcasted_iota(jnp.int32, (q, 1), 0) == q - 1
    ys, s_outs = [], []
    for r in range(GH):
        sel = lane == hbase + r
        dt_h = jnp.sum(jnp.where(sel, dt, 0.0), axis=1, keepdims=True)
        cs_h = jnp.sum(jnp.where(sel, cs, 0.0), axis=1, keepdims=True)
        cs_last = jnp.sum(jnp.where(last, cs_h, 0.0), axis=0, keepdims=True)
        d_h = jnp.sum(jnp.where(lane1 == hbase + r, dskip, 0.0), axis=1, keepdims=True)
        x_h = x[:, r * SSD_P:(r + 1) * SSD_P]
        s_h = s_in[r * SSD_P:(r + 1) * SSD_P, :]
        xdt = x_h * dt_h
        m1 = jnp.broadcast_to(cs_h, (q, q))
        decay = jnp.where(causal, jnp.exp(jnp.where(causal, m1 - m1.T, 0.0)), 0.0)
        y_diag = dot_nn(g_cb * decay, xdt)
        s_c = dot_tn(xdt * jnp.exp(cs_last - cs_h), bm)
        y_off = dot_nt(cm, s_h) * jnp.exp(cs_h)
        ys.append(y_diag + y_off + x_h * d_h)
        s_outs.append(s_h * jnp.exp(cs_last) + s_c)
    y = jnp.concatenate(ys, axis=1) * (z * jax.nn.sigmoid(z))
    y = y * lax.rsqrt(jnp.mean(y * y, axis=-1, keepdims=True) + EPS) * ng
    return y, jnp.concatenate(s_outs, axis=0)


def _ssd_in_specs(cmap):
    return [
        pl.BlockSpec((CHUNK, GW), lambda g, c: (cmap(c), g)),
        pl.BlockSpec((CHUNK, GW), lambda g, c: (cmap(c), C_Z // GW + g)),
        pl.BlockSpec((CHUNK, SSD_N), lambda g, c: (cmap(c), SSD_INNER // SSD_N + g)),
        pl.BlockSpec((CHUNK, SSD_N), lambda g, c: (cmap(c), SSD_INNER // SSD_N + SSD_GROUPS + g)),
        pl.BlockSpec((CHUNK, 128), lambda g, c: (cmap(c), C_DT // 128)),
    ]


_SSD_PARAM_SPECS = [pl.BlockSpec((1, 128), lambda g, c: (0, 0))] * 3 + [pl.BlockSpec((1, GW), lambda g, c: (0, g))]


def ssd_fwd(name, xact, proj, dt_bias, a_log, dskip, ng, ex=None):
    t_len = xact.shape[0]
    nc = t_len // CHUNK
    h = _hosted(ex)

    def body(*refs):
        x_ref, z_ref, b_ref, c_ref, dt_ref, p1, p2, p3, p4 = refs[:9]
        y_ref, st_ref = refs[9 + h.n_in:11 + h.n_in]
        s_scr = refs[11 + h.n_in + h.n_out]
        g, c = pl.program_id(0), pl.program_id(1)
        h.at_start(refs, 9, 11 + h.n_in, (g == 0) & (c == 0))

        @pl.when(c == 0)
        def _():
            s_scr[...] = jnp.zeros(s_scr.shape, f32)

        s_in = s_scr[...]
        st_ref[0, 0] = s_in
        y, s_out = f_ssd(g * GH, x_ref[...], z_ref[...], b_ref[...], c_ref[...], dt_ref[...], s_in,
                         p1[...], p2[...], p3[...], p4[...])
        y_ref[...] = y.astype(y_ref.dtype)
        s_scr[...] = s_out
        h.at_end(refs, 9, 11 + h.n_in, (g == SSD_GROUPS - 1) & (c == nc - 1))

    res = pl.pallas_call(
        body, name=name, grid=(SSD_GROUPS, nc),
        in_specs=_ssd_in_specs(lambda c: c) + _SSD_PARAM_SPECS + h.in_specs,
        out_specs=[pl.BlockSpec((CHUNK, GW), lambda g, c: (c, g)),
                   pl.BlockSpec((1, 1, GW, SSD_N), lambda g, c: (g, c, 0, 0))] + h.out_specs,
        out_shape=[jax.ShapeDtypeStruct((t_len, SSD_INNER), bf16),
                   jax.ShapeDtypeStruct((SSD_GROUPS, nc, GW, SSD_N), f32)] + h.out_shape,
        scratch_shapes=[pltpu.VMEM((GW, SSD_N), f32)] + h.scratch,
        input_output_aliases=h.aliases(9, 2),
        compiler_params=_params(("arbitrary", "arbitrary")),
    )(xact, proj, xact, xact, proj, dt_bias, a_log, dskip, ng, *h.ins)
    return res[0], res[1], list(res[2:])


def ssd_bwd(name, xact, proj, states, dy, dt_bias, a_log, dskip, ng, ex=None):
    t_len = xact.shape[0]
    nc = t_len // CHUNK
    rev = lambda c: nc - 1 - c
    h = _hosted(ex)

    def body(*refs):
        x_ref, z_ref, b_ref, c_ref, dt_ref, st_ref, dy_ref, p1, p2, p3, p4 = refs[:11]
        dx_ref, db_ref, dc_ref, dz_ref, ddt_ref, d1, d2, d3, d4 = refs[11 + h.n_in:20 + h.n_in]
        ds_scr = refs[20 + h.n_in + h.n_out]
        g, c = pl.program_id(0), pl.program_id(1)
        h.at_start(refs, 11, 20 + h.n_in, (g == 0) & (c == 0))

        @pl.when(c == 0)
        def _():
            ds_scr[...] = jnp.zeros(ds_scr.shape, f32)
            d4[...] = jnp.zeros(d4.shape, f32)

        @pl.when((c == 0) & (g == 0))
        def _():
            for r in (d1, d2, d3):
                r[...] = jnp.zeros(r.shape, f32)

        fn = functools.partial(f_ssd, g * GH)
        _, vjp = jax.vjp(fn, x_ref[...], z_ref[...], b_ref[...], c_ref[...], dt_ref[...], st_ref[0, 0],
                         p1[...], p2[...], p3[...], p4[...])
        dx, dz, db, dc, ddt, ds_in, e1, e2, e3, e4 = vjp((dy_ref[...].astype(f32), ds_scr[...]))
        dx_ref[...] = dx
        dz_ref[...] = dz.astype(dz_ref.dtype)
        db_ref[...] = db
        dc_ref[...] = dc
        ddt_ref[0] = ddt
        ds_scr[...] = ds_in
        d1[...] += e1
        d2[...] += e2
        d3[...] += e3
        d4[...] += e4
        h.at_end(refs, 11, 20 + h.n_in, (g == SSD_GROUPS - 1) & (c == nc - 1))

    res = pl.pallas_call(
        body, name=name, grid=(SSD_GROUPS, nc),
        in_specs=_ssd_in_specs(rev) + [
            pl.BlockSpec((1, 1, GW, SSD_N), lambda g, c: (g, rev(c), 0, 0)),
            pl.BlockSpec((CHUNK, GW), lambda g, c: (rev(c), g)),
        ] + _SSD_PARAM_SPECS + h.in_specs,
        out_specs=[
            pl.BlockSpec((CHUNK, GW), lambda g, c: (rev(c), g)),
            pl.BlockSpec((CHUNK, SSD_N), lambda g, c: (rev(c), g)),
            pl.BlockSpec((CHUNK, SSD_N), lambda g, c: (rev(c), g)),
            pl.BlockSpec((CHUNK, GW), lambda g, c: (rev(c), g)),
            pl.BlockSpec((1, CHUNK, 128), lambda g, c: (g, rev(c), 0)),
        ] + _SSD_PARAM_SPECS + h.out_specs,
        out_shape=[
            jax.ShapeDtypeStruct((t_len, SSD_INNER), f32),
            jax.ShapeDtypeStruct((t_len, SSD_GROUPS * SSD_N), f32),
            jax.ShapeDtypeStruct((t_len, SSD_GROUPS * SSD_N), f32),
            jax.ShapeDtypeStruct((t_len, SSD_INNER), bf16),
            jax.ShapeDtypeStruct((SSD_GROUPS, t_len, 128), f32),
            jax.ShapeDtypeStruct((1, 128), f32), jax.ShapeDtypeStruct((1, 128), f32), jax.ShapeDtypeStruct((1, 128), f32),
            jax.ShapeDtypeStruct((1, SSD_INNER), f32),
        ] + h.out_shape,
        scratch_shapes=[pltpu.VMEM((GW, SSD_N), f32)] + h.scratch,
        input_output_aliases=h.aliases(11, 9),
        compiler_params=_params(("arbitrary", "arbitrary")),
    )(xact, proj, xact, xact, proj, states, dy, dt_bias, a_log, dskip, ng, *h.ins)
    return list(res[:9]), list(res[9:])


def _head_norm(t, g):
    return t * lax.rsqrt(jnp.mean(t * t, axis=-1, keepdims=True) + EPS) * g


def f_swa(has_prev, q, kp, kc, vp, vc, qg, kg, sinks, *bias):
    qi = lax.broadcasted_iota(jnp.int32, (BLK, BLK), 0)
    ki = lax.broadcasted_iota(jnp.int32, (BLK, BLK), 1)
    mask_p = (ki > qi) & has_prev
    mask_c = ki <= qi
    lane1 = lax.broadcasted_iota(jnp.int32, (1, 128), 1)
    scale = ATTN_D ** -0.5
    rep = ATTN_HEADS // ATTN_KV
    outs = []
    for g in range(ATTN_KV):
        sl = slice(g * ATTN_D, (g + 1) * ATTN_D)
        kpn, kcn = _head_norm(kp[:, sl], kg), _head_norm(kc[:, sl], kg)
        vpg, vcg = vp[:, sl], vc[:, sl]
        for r in range(rep):
            h = g * rep + r
            qh = _head_norm(q[:, h * ATTN_D:(h + 1) * ATTN_D], qg)
            lp = jnp.where(mask_p, dot_nt(qh, kpn) * scale + bias[h][:, :BLK], NEG_INF)
            lc = jnp.where(mask_c, dot_nt(qh, kcn) * scale + bias[h][:, BLK:], NEG_INF)
            sink = jnp.sum(jnp.where(lane1 == h, sinks, 0.0), axis=1, keepdims=True)
            m = jnp.maximum(jnp.maximum(jnp.max(lp, axis=-1, keepdims=True), jnp.max(lc, axis=-1, keepdims=True)), sink)
            pp, pc = jnp.exp(lp - m), jnp.exp(lc - m)
            den = jnp.sum(pp, axis=-1, keepdims=True) + jnp.sum(pc, axis=-1, keepdims=True) + jnp.exp(sink - m)
            outs.append((dot_nn(pp, vpg) + dot_nn(pc, vcg)) / den)
    return jnp.concatenate(outs, axis=1)


def _swa_in_specs():
    prev = lambda i: jnp.maximum(i - 1, 0)
    kw = ATTN_KV * ATTN_D
    return [
        pl.BlockSpec((BLK, D), lambda i: (i, C_Q // D)),
        pl.BlockSpec((BLK, kw), lambda i: (prev(i), C_K // kw)),
        pl.BlockSpec((BLK, kw), lambda i: (i, C_K // kw)),
        pl.BlockSpec((BLK, kw), lambda i: (prev(i), C_V // kw)),
        pl.BlockSpec((BLK, kw), lambda i: (i, C_V // kw)),
        pl.BlockSpec((1, ATTN_D), lambda i: (0, 0)),
        pl.BlockSpec((1, ATTN_D), lambda i: (0, 0)),
        pl.BlockSpec((1, 128), lambda i: (0, 0)),
        pl.BlockSpec((ATTN_HEADS, BLK, 2 * BLK), lambda i: (0, 0, 0)),
    ]


def swa_fwd(name, proj, qg, kg, sinks, bias, ex=None):
    t_len = proj.shape[0]
    nb = t_len // BLK
    h = _hosted(ex)

    def body(*refs):
        q_ref, kp_ref, kc_ref, vp_ref, vc_ref, qg_ref, kg_ref, s_ref, b_ref = refs[:9]
        o_ref = refs[9 + h.n_in]
        i = pl.program_id(0)
        h.at_start(refs, 9, 10 + h.n_in, i == 0)
        o = f_swa(i > 0, q_ref[...], kp_ref[...], kc_ref[...], vp_ref[...], vc_ref[...], qg_ref[...], kg_ref[...],
                  s_ref[...], *[b_ref[hd] for hd in range(ATTN_HEADS)])
        o_ref[...] = o.astype(o_ref.dtype)
        h.at_end(refs, 9, 10 + h.n_in, i == nb - 1)

    res = pl.pallas_call(
        body, name=name, grid=(nb,), in_specs=_swa_in_specs() + h.in_specs,
        out_specs=[pl.BlockSpec((BLK, D), lambda i: (i, 0))] + h.out_specs,
        out_shape=[jax.ShapeDtypeStruct((t_len, D), bf16)] + h.out_shape,
        scratch_shapes=h.scratch, input_output_aliases=h.aliases(9, 1),
        compiler_params=_params(("arbitrary",)),
    )(proj, proj, proj, proj, proj, qg, kg, sinks, bias, *h.ins)
    return res[0], list(res[1:])


def swa_bwd(name, proj, do, qg, kg, sinks, bias, ex=None):
    t_len = proj.shape[0]
    nb = t_len // BLK
    kw = ATTN_KV * ATTN_D
    h = _hosted(ex)

    def body(*refs):
        q_ref, kp_ref, kc_ref, vp_ref, vc_ref, qg_ref, kg_ref, s_ref, b_ref, do_ref = refs[:10]
        dq_ref, dkp_ref, dkc_ref, dvp_ref, dvc_ref, dqg_ref, dkg_ref, ds_ref, db_ref = refs[10 + h.n_in:19 + h.n_in]
        i = pl.program_id(0)
        h.at_start(refs, 10, 19 + h.n_in, i == 0)

        @pl.when(i == 0)
        def _():
            for r in (dqg_ref, dkg_ref, ds_ref, db_ref):
                r[...] = jnp.zeros(r.shape, f32)

        fn = functools.partial(f_swa, i > 0)
        _, vjp = jax.vjp(fn, q_ref[...], kp_ref[...], kc_ref[...], vp_ref[...], vc_ref[...], qg_ref[...], kg_ref[...],
                         s_ref[...], *[b_ref[hd] for hd in range(ATTN_HEADS)])
        ds = vjp(do_ref[...].astype(f32))
        dq_ref[...] = ds[0].astype(dq_ref.dtype)
        dkp_ref[...] = ds[1]
        dkc_ref[...] = ds[2]
        dvp_ref[...] = ds[3]
        dvc_ref[...] = ds[4]
        dqg_ref[...] += ds[5]
        dkg_ref[...] += ds[6]
        ds_ref[...] += ds[7]
        for hd in range(ATTN_HEADS):
            db_ref[hd] += ds[8 + hd]
        h.at_end(refs, 10, 19 + h.n_in, i == nb - 1)

    row = lambda w: pl.BlockSpec((BLK, w), lambda i: (i, 0))
    res = pl.pallas_call(
        body, name=name, grid=(nb,), in_specs=_swa_in_specs() + [row(D)] + h.in_specs,
        out_specs=[row(D), row(kw), row(kw), row(kw), row(kw),
                   pl.BlockSpec((1, ATTN_D), lambda i: (0, 0)), pl.BlockSpec((1, ATTN_D), lambda i: (0, 0)),
                   pl.BlockSpec((1, 128), lambda i: (0, 0)),
                   pl.BlockSpec((ATTN_HEADS, BLK, 2 * BLK), lambda i: (0, 0, 0))] + h.out_specs,
        out_shape=[jax.ShapeDtypeStruct((t_len, D), bf16)] + [jax.ShapeDtypeStruct((t_len, kw), f32)] * 4
        + [jax.ShapeDtypeStruct((1, ATTN_D), f32)] * 2 + [jax.ShapeDtypeStruct((1, 128), f32),
                                                          jax.ShapeDtypeStruct((ATTN_HEADS, BLK, 2 * BLK), f32)] + h.out_shape,
        scratch_shapes=h.scratch, input_output_aliases=h.aliases(10, 9),
        compiler_params=_params(("arbitrary",)),
    )(proj, proj, proj, proj, proj, qg, kg, sinks, bias, do, *h.ins)
    return list(res[:9]), list(res[9:])


def _bucket_table():
    qi = np.arange(BLK)[:, None] + BLK
    kj = np.arange(2 * BLK)[None, :]
    dist = qi - kj
    max_exact = REL_BUCKETS // 2
    d = np.maximum(dist, 1).astype(np.float32)
    large = max_exact + (np.log(d / max_exact) / math.log(128 / max_exact) * (REL_BUCKETS - max_exact)).astype(np.int32)
    large = np.minimum(large, REL_BUCKETS - 1)
    return np.where(dist < max_exact, np.maximum(dist, 0), large).astype(np.int32)


def relbias_fwd(table, bucket):
    def body(t_ref, bk_ref, o_ref):
        bk = bk_ref[...]
        for h in range(ATTN_HEADS):
            acc = jnp.zeros((BLK, 2 * BLK), f32)
            for b in range(REL_BUCKETS):
                acc = jnp.where(bk == b, t_ref[b, h], acc)
            o_ref[h] = acc

    return pl.pallas_call(
        body, name="relbias_fwd", out_shape=jax.ShapeDtypeStruct((ATTN_HEADS, BLK, 2 * BLK), f32),
        in_specs=[pl.BlockSpec(memory_space=pltpu.SMEM), pl.BlockSpec(memory_space=pltpu.VMEM)],
        out_specs=pl.BlockSpec(memory_space=pltpu.VMEM),
    )(table, bucket)


def relbias_bwd(dbias, bucket):
    n = len(dbias)

    def body(*refs):
        bk = refs[n][...]
        o_ref = refs[n + 1]
        row = lax.broadcasted_iota(jnp.int32, (REL_BUCKETS, 128), 0)
        lane = lax.broadcasted_iota(jnp.int32, (REL_BUCKETS, 128), 1)
        res = jnp.zeros((REL_BUCKETS, 128), f32)
        for h in range(ATTN_HEADS):
            tot = refs[0][h]
            for k in range(1, n):
                tot = tot + refs[k][h]
            for b in range(REL_BUCKETS):
                part = jnp.sum(jnp.sum(jnp.where(bk == b, tot, 0.0), axis=1, keepdims=True), axis=0, keepdims=True)
                res = jnp.where((row == b) & (lane == h), part, res)
        o_ref[...] = res

    return pl.pallas_call(
        body, name="relbias_bwd", out_shape=jax.ShapeDtypeStruct((REL_BUCKETS, 128), f32),
        in_specs=[pl.BlockSpec(memory_space=pltpu.VMEM)] * (n + 1),
        out_specs=pl.BlockSpec(memory_space=pltpu.VMEM),
    )(*dbias, bucket)[:, :ATTN_HEADS]


XA_TILE = 512


def f_xattn(q, k, v, qg, kg):
    qn, kn = _head_norm(q, qg), _head_norm(k, kg)
    logits = dot_nt(qn, kn) * (XD ** -0.5)
    p = jnp.exp(logits - jnp.max(logits, axis=-1, keepdims=True))
    return dot_nn(p / jnp.sum(p, axis=-1, keepdims=True), v)


def _xa_in_specs(tq):
    return [
        pl.BlockSpec((tq, XD), lambda h, i: (i, h)),
        pl.BlockSpec((MEM_LEN, XD), lambda h, i: (0, h)),
        pl.BlockSpec((MEM_LEN, XD), lambda h, i: (0, XH + h)),
        pl.BlockSpec((1, XD), lambda h, i: (0, 0)),
        pl.BlockSpec((1, XD), lambda h, i: (0, 0)),
    ]


def xa_fwd(name, xq, kv, qg, kg):
    t_len = xq.shape[0]
    tq = min(XA_TILE, t_len)

    def body(q_ref, k_ref, v_ref, qg_ref, kg_ref, o_ref):
        o_ref[...] = f_xattn(q_ref[...], k_ref[...], v_ref[...], qg_ref[...], kg_ref[...]).astype(o_ref.dtype)

    return pl.pallas_call(
        body, name=name, grid=(XH, t_len // tq), in_specs=_xa_in_specs(tq),
        out_specs=pl.BlockSpec((tq, XD), lambda h, i: (i, h)), out_shape=jax.ShapeDtypeStruct((t_len, D), bf16),
        compiler_params=_params(("parallel", "parallel")),
    )(xq, kv, kv, qg, kg)


def xa_bwd(name, xq, kv, do, qg, kg):
    t_len = xq.shape[0]
    tq = min(XA_TILE, t_len)

    def body(q_ref, k_ref, v_ref, qg_ref, kg_ref, do_ref, dq_ref, dk_ref, dv_ref, dqg_ref, dkg_ref):
        h, i = pl.program_id(0), pl.program_id(1)

        @pl.when(i == 0)
        def _():
            dk_ref[...] = jnp.zeros(dk_ref.shape, f32)
            dv_ref[...] = jnp.zeros(dv_ref.shape, f32)

        @pl.when((i == 0) & (h == 0))
        def _():
            dqg_ref[...] = jnp.zeros(dqg_ref.shape, f32)
            dkg_ref[...] = jnp.zeros(dkg_ref.shape, f32)

        _, vjp = jax.vjp(f_xattn, q_ref[...], k_ref[...], v_ref[...], qg_ref[...], kg_ref[...])
        dq, dk, dv, dqg, dkg = vjp(do_ref[...].astype(f32))
        dq_ref[...] = dq.astype(dq_ref.dtype)
        dk_ref[...] += dk
        dv_ref[...] += dv
        dqg_ref[...] += dqg
        dkg_ref[...] += dkg

    return pl.pallas_call(
        body, name=name, grid=(XH, t_len // tq),
        in_specs=_xa_in_specs(tq) + [pl.BlockSpec((tq, XD), lambda h, i: (i, h))],
        out_specs=[pl.BlockSpec((tq, XD), lambda h, i: (i, h)),
                   pl.BlockSpec((MEM_LEN, XD), lambda h, i: (0, h)), pl.BlockSpec((MEM_LEN, XD), lambda h, i: (0, h)),
                   pl.BlockSpec((1, XD), lambda h, i: (0, 0)), pl.BlockSpec((1, XD), lambda h, i: (0, 0))],
        out_shape=[jax.ShapeDtypeStruct((t_len, D), bf16), jax.ShapeDtypeStruct((MEM_LEN, D), f32),
                   jax.ShapeDtypeStruct((MEM_LEN, D), f32), jax.ShapeDtypeStruct((1, XD), f32), jax.ShapeDtypeStruct((1, XD), f32)],
        compiler_params=_params(("arbitrary", "arbitrary")),
    )(xq, kv, kv, qg, kg, do)


def loss_head(y, target, tile):
    t_len = y.shape[0]

    def body(y_ref, t_ref, dy_ref, l_ref):
        i = pl.program_id(0)

        @pl.when(i == 0)
        def _():
            l_ref[...] = jnp.zeros(l_ref.shape, f32)

        err = y_ref[...] - t_ref[...]
        dy_ref[...] = err * (1.0 / D)
        l_ref[...] += 0.5 * jnp.sum(jnp.sum(err * err, axis=1, keepdims=True), axis=0, keepdims=True) * (1.0 / D)

    row = pl.BlockSpec((tile, D), lambda i: (i, 0))
    return pl.pallas_call(
        body, name="loss_head", grid=(t_len // tile,), in_specs=[row, row],
        out_specs=[row, pl.BlockSpec((8, 128), lambda i: (0, 0))],
        out_shape=[jax.ShapeDtypeStruct((t_len, D), f32), jax.ShapeDtypeStruct((8, 128), f32)],
        compiler_params=_params(("arbitrary",)),
    )(y, target)


ADAM_BLOCK_ELEMS = 512 * 1024


def adamw(name, w, g, m, v):
    rows, cols = w.shape
    tile = rows
    if rows * cols > ADAM_BLOCK_ELEMS:
        tile = _pick(rows, [t for t in (512, 256, 128, 64, 32, 16, 8) if t * cols <= ADAM_BLOCK_ELEMS])

    def body(w_ref, g_ref, m_ref, v_ref, d_ref, nm_ref, nv_ref):
        gg = g_ref[...]
        nm = ADAM_B1 * m_ref[...] + (1.0 - ADAM_B1) * gg
        nv = ADAM_B2 * v_ref[...] + (1.0 - ADAM_B2) * jnp.square(gg)
        m_hat = nm / (1.0 - ADAM_B1 ** ADAM_STEP)
        v_hat = nv / (1.0 - ADAM_B2 ** ADAM_STEP)
        d_ref[...] = -ADAM_LR * (m_hat / (jnp.sqrt(v_hat) + ADAM_EPS) + ADAM_WD * w_ref[...])
        nm_ref[...] = nm
        nv_ref[...] = nv

    spec = pl.BlockSpec((tile, cols), lambda i: (i, 0))
    return pl.pallas_call(
        body, name=name, grid=(rows // tile,), in_specs=[spec] * 4, out_specs=[spec] * 3,
        out_shape=[jax.ShapeDtypeStruct((rows, cols), f32)] * 3, compiler_params=_params(("parallel",)),
    )(w, g, m, v)


def adamw_layers(name, w, m, v, gsrcs, r0):
    depth, rows, cols = w.shape
    tile = _pick(rows, [t for t in (512, 256, 128, 64, 32, 16, 8) if t * cols <= ADAM_BLOCK_ELEMS and r0 % t == 0])

    def body(*refs):
        w_ref, m_ref, v_ref = refs[:3]
        g_refs = refs[3:3 + depth]
        go_ref, d_ref, nm_ref, nv_ref = refs[3 + depth:]
        layer = pl.program_id(0)
        gg = g_refs[0][...]
        for k in range(1, depth):
            gg = jnp.where(layer == k, g_refs[k][...], gg)
        nm = ADAM_B1 * m_ref[0] + (1.0 - ADAM_B1) * gg
        nv = ADAM_B2 * v_ref[0] + (1.0 - ADAM_B2) * jnp.square(gg)
        m_hat = nm / (1.0 - ADAM_B1 ** ADAM_STEP)
        v_hat = nv / (1.0 - ADAM_B2 ** ADAM_STEP)
        go_ref[0] = gg
        d_ref[0] = -ADAM_LR * (m_hat / (jnp.sqrt(v_hat) + ADAM_EPS) + ADAM_WD * w_ref[0])
        nm_ref[0] = nm
        nv_ref[0] = nv

    spec = pl.BlockSpec((1, tile, cols), lambda l, i: (l, i, 0))
    g_specs = [pl.BlockSpec((tile, cols), functools.partial(lambda l, i, k: (jnp.where(l == k, r0 // tile + i, 0), 0), k=k))
               for k in range(depth)]
    return pl.pallas_call(
        body, name=name, grid=(depth, rows // tile), in_specs=[spec] * 3 + g_specs, out_specs=[spec] * 4,
        out_shape=[jax.ShapeDtypeStruct(w.shape, f32)] * 4, compiler_params=_params(("parallel", "parallel")),
    )(w, m, v, *gsrcs)


MESH = pl.DeviceIdType.MESH
HBM_SPEC = pl.BlockSpec(memory_space=pltpu.HBM)


def _place():
    x, y, c = lax.axis_index("x"), lax.axis_index("y"), lax.axis_index("c")
    chips = [(1 - x, y), (x, 1 - y), (1 - x, 1 - y)]
    return x, y, c, chips


def _rcopy(src, dst, send_sems, recv_sems, k, to):
    return pltpu.make_async_remote_copy(src_ref=src, dst_ref=dst, send_sem=send_sems.at[k], recv_sem=recv_sems.at[k],
                                        device_id=to, device_id_type=MESH)


class Exchange:
    def __init__(self, ins, outs, n_sems, n_local, start, wait, aliases=None):
        self.ins, self.outs, self.n_sems, self.n_local = list(ins), list(outs), n_sems, n_local
        self.start, self.wait, self.aliases = start, wait, dict(aliases or {})


class _hosted:
    def __init__(self, ex):
        self.ex = ex
        self.ins = list(ex.ins) if ex else []
        self.out_shape = list(ex.outs) if ex else []
        self.n_in, self.n_out = len(self.ins), len(self.out_shape)
        self.in_specs, self.out_specs = [HBM_SPEC] * self.n_in, [HBM_SPEC] * self.n_out
        self.scratch = [pltpu.SemaphoreType.DMA((ex.n_sems,)), pltpu.SemaphoreType.DMA((ex.n_sems,)),
                        pltpu.SemaphoreType.DMA((max(ex.n_local, 1),))] if ex else []

    def aliases(self, first_in, first_out):
        return {first_in + a: first_out + b for a, b in self.ex.aliases.items()} if self.ex else {}

    def _args(self, refs, i0, o0):
        return refs[i0:i0 + self.n_in], refs[o0:o0 + self.n_out], refs[-3], refs[-2], refs[-1]

    def at_start(self, refs, i0, o0, pred):
        if self.ex is not None:
            pl.when(pred)(lambda: self.ex.start(*self._args(refs, i0, o0)))

    def at_end(self, refs, i0, o0, pred):
        if self.ex is not None:
            pl.when(pred)(lambda: self.ex.wait(*self._args(refs, i0, o0)))


def run_exchange(name, ex):
    h = _hosted(ex)

    def body(*refs):
        args = h._args(refs, 0, h.n_in)
        ex.start(*args)
        ex.wait(*args)

    res = pl.pallas_call(
        body, name=name, out_shape=h.out_shape, in_specs=h.in_specs, out_specs=h.out_specs, scratch_shapes=h.scratch,
        input_output_aliases=h.aliases(0, 0), compiler_params=pltpu.CompilerParams(has_side_effects=True),
    )(*h.ins)
    return list(res)


def _halves(ref, axis, c):
    hh = ref.shape[axis] // 2
    return pl.ds(pl.multiple_of(c * hh, 16), hh), pl.ds(pl.multiple_of((1 - c) * hh, 16), hh)


def ex_gather_ici(shards):
    n = len(shards)

    def copies(ins, outs, ssem, rsem, landing):
        x, y, c, chips = _place()
        me = 2 * x + y
        res = []
        for a, (s, o) in enumerate(zip(ins, outs)):
            mine, _ = _halves(s, 0, c)
            for j, (cx, cy) in enumerate(chips):
                slot = 2 * cx + cy if landing else me
                res.append(_rcopy(s.at[mine], o.at[slot, mine], ssem, rsem, 3 * a + j, (cx, cy, c)))
        return me, res

    def start(ins, outs, ssem, rsem, lsem):
        me, send = copies(ins, outs, ssem, rsem, False)
        for a, (s, o) in enumerate(zip(ins, outs)):
            pltpu.make_async_copy(s, o.at[me], lsem.at[a]).start()
        for cp in send:
            cp.start()

    def wait(ins, outs, ssem, rsem, lsem):
        for cp in copies(ins, outs, ssem, rsem, True)[1]:
            cp.wait_recv()
        me, send = copies(ins, outs, ssem, rsem, False)
        for cp in send:
            cp.wait_send()
        for a, (s, o) in enumerate(zip(ins, outs)):
            pltpu.make_async_copy(s, o.at[me], lsem.at[a]).wait()

    outs = [jax.ShapeDtypeStruct((N_CHIPS,) + s.shape, s.dtype) for s in shards]
    return Exchange(shards, outs, 3 * n, n, start, wait)


def ex_gather_d2d(gathered):
    n = len(gathered)

    def copies(outs, ssem, rsem, landing):
        x, y, c, chips = _place()
        res = []
        for a, o in enumerate(outs):
            mine, theirs = _halves(o, 1, c)
            for j, (cx, cy) in enumerate(chips):
                rows = o.at[2 * cx + cy, theirs if landing else mine]
                res.append(_rcopy(rows, rows, ssem, rsem, 3 * a + j, (x, y, 1 - c)))
        return res

    def start(ins, outs, ssem, rsem, lsem):
        for cp in copies(outs, ssem, rsem, False):
            cp.start()

    def wait(ins, outs, ssem, rsem, lsem):
        for cp in copies(outs, ssem, rsem, True):
            cp.wait_recv()
        for cp in copies(outs, ssem, rsem, False):
            cp.wait_send()

    outs = [jax.ShapeDtypeStruct(g.shape, g.dtype) for g in gathered]
    return Exchange(gathered, outs, 3 * n, 0, start, wait, aliases={a: a for a in range(n)})


def ex_swap_halves(gs):
    n = len(gs)

    def copies(ins, outs, ssem, rsem):
        x, y, c, _ = _place()
        return [_rcopy(g.at[s, 1 - c], o.at[s], ssem, rsem, N_CHIPS * a + s, (x, y, 1 - c))
                for a, (g, o) in enumerate(zip(ins, outs)) for s in range(N_CHIPS)]

    def start(ins, outs, ssem, rsem, lsem):
        for cp in copies(ins, outs, ssem, rsem):
            cp.start()

    def wait(ins, outs, ssem, rsem, lsem):
        for cp in copies(ins, outs, ssem, rsem):
            cp.wait()

    outs = [jax.ShapeDtypeStruct((N_CHIPS,) + g.shape[2:], g.dtype) for g in gs]
    return Exchange(gs, outs, N_CHIPS * n, 0, start, wait)


def ex_scatter(ps):
    n = len(ps)

    def copies(ins, outs, ssem, rsem):
        x, y, c, chips = _place()
        return [_rcopy(p.at[2 * cx + cy], o.at[j], ssem, rsem, 3 * a + j, (cx, cy, c))
                for a, (p, o) in enumerate(zip(ins, outs)) for j, (cx, cy) in enumerate(chips)]

    def start(ins, outs, ssem, rsem, lsem):
        for cp in copies(ins, outs, ssem, rsem):
            cp.start()

    def wait(ins, outs, ssem, rsem, lsem):
        for cp in copies(ins, outs, ssem, rsem):
            cp.wait()

    outs = [jax.ShapeDtypeStruct((3,) + p.shape[1:], p.dtype) for p in ps]
    return Exchange(ps, outs, 3 * n, 0, start, wait)


def ex_share(rs):
    n = len(rs)

    def start(ins, outs, ssem, rsem, lsem):
        x, y, c, _ = _place()
        for a, (r, o) in enumerate(zip(ins, outs)):
            pltpu.make_async_copy(r, o.at[c], lsem.at[a]).start()
            _rcopy(r, o.at[c], ssem, rsem, a, (x, y, 1 - c)).start()

    def wait(ins, outs, ssem, rsem, lsem):
        x, y, c, _ = _place()
        for a, (r, o) in enumerate(zip(ins, outs)):
            _rcopy(r, o.at[1 - c], ssem, rsem, a, (x, y, 1 - c)).wait_recv()
            _rcopy(r, o.at[c], ssem, rsem, a, (x, y, 1 - c)).wait_send()
            pltpu.make_async_copy(r, o.at[c], lsem.at[a]).wait()

    outs = [jax.ShapeDtypeStruct((2,) + r.shape, r.dtype) for r in rs]
    return Exchange(rs, outs, n, n, start, wait)


ADD_BLOCK_ELEMS = 384 * 1024


def _add_tile(rows, cols):
    return _pick(rows, [t for t in (1920, 960, 512, 384, 256, 128, 64, 32, 16) if t * cols <= ADD_BLOCK_ELEMS])


def add_halves(name, g, got, c_arr):
    _, _, rows, cols = g.shape
    tile = _add_tile(rows, cols)

    def body(c_ref, g_ref, r_ref, o_ref):
        o_ref[...] = (g_ref[0] + r_ref[...]).astype(o_ref.dtype)

    return pl.pallas_call(
        body, name=name,
        grid_spec=pltpu.PrefetchScalarGridSpec(
            num_scalar_prefetch=1, grid=(N_CHIPS, rows // tile),
            in_specs=[pl.BlockSpec((1, 1, tile, cols), lambda s, i, c_ref: (s, c_ref[0], i, 0)),
                      pl.BlockSpec((1, tile, cols), lambda s, i, c_ref: (s, i, 0))],
            out_specs=pl.BlockSpec((1, tile, cols), lambda s, i, c_ref: (s, i, 0))),
        out_shape=jax.ShapeDtypeStruct(got.shape, bf16), compiler_params=_params(("parallel", "parallel")),
    )(c_arr, g, got)


def add_partials(name, p, got, me_arr):
    _, rows, cols = p.shape
    tile = _add_tile(rows, cols)

    def body(me_ref, p_ref, r_ref, o_ref):
        o_ref[...] = ((p_ref[0].astype(f32) + r_ref[0].astype(f32)) + r_ref[1].astype(f32)) + r_ref[2].astype(f32)

    return pl.pallas_call(
        body, name=name,
        grid_spec=pltpu.PrefetchScalarGridSpec(
            num_scalar_prefetch=1, grid=(rows // tile,),
            in_specs=[pl.BlockSpec((1, tile, cols), lambda i, me_ref: (me_ref[0], i, 0)),
                      pl.BlockSpec((3, tile, cols), lambda i, me_ref: (0, i, 0))],
            out_specs=pl.BlockSpec((tile, cols), lambda i, me_ref: (i, 0))),
        out_shape=jax.ShapeDtypeStruct((rows, cols), f32), compiler_params=_params(("parallel",)),
    )(me_arr, p, got)


N_DEV = 8


def allreduce_small(name, buf):
    m_per = buf.shape[0]

    def body(x_ref, out_ref, all_ref, send_sems, recv_sems, local_sem):
        x, y, c, chips = _place()
        me, sibling = (x, y, c), (x, y, 1 - c)

        def rows(px, py, pc):
            return all_ref.at[pl.ds(pl.multiple_of((4 * px + 2 * py + pc) * m_per, 8), m_per), :]

        def copy(k, block, to, src=None):
            return _rcopy(rows(*block) if src is None else src, rows(*block), send_sems, recv_sems, k, to)

        mine = pltpu.make_async_copy(x_ref, rows(*me), local_sem)
        mine.start()
        first = [copy(0, me, sibling, src=x_ref)]
        first += [copy(1 + j, me, (*chip, c), src=x_ref) for j, chip in enumerate(chips)]
        for cp in first:
            cp.start()
        passed = [copy(4 + j, (*chip, c), sibling) for j, chip in enumerate(chips)]
        for j, chip in enumerate(chips):
            copy(1 + j, (*chip, c), me).wait_recv()
            passed[j].start()
        copy(0, sibling, me).wait_recv()
        for j, chip in enumerate(chips):
            copy(4 + j, (*chip, 1 - c), me).wait_recv()
        for cp in first + passed:
            cp.wait_send()
        mine.wait()
        tot = all_ref[pl.ds(0, m_per), :]
        for d in range(1, N_DEV):
            tot = tot + all_ref[pl.ds(d * m_per, m_per), :]
        out_ref[...] = tot

    return pl.pallas_call(
        body, name=name, out_shape=jax.ShapeDtypeStruct((m_per, 128), f32),
        in_specs=[pl.BlockSpec(memory_space=pltpu.VMEM)], out_specs=pl.BlockSpec(memory_space=pltpu.VMEM),
        scratch_shapes=[pltpu.VMEM((N_DEV * m_per, 128), f32), pltpu.SemaphoreType.DMA((7,)), pltpu.SemaphoreType.DMA((7,)),
                        pltpu.SemaphoreType.DMA],
        compiler_params=pltpu.CompilerParams(has_side_effects=True, vmem_limit_bytes=VMEM_LIMIT_BYTES),
    )(buf)


def pack_shards(ws, layer):
    wide = jnp.concatenate([ws[n][layer].astype(bf16) for n in WIDE], axis=0)
    return [ws["w_in"][layer].astype(bf16), wide, ws["w_xkv"][layer].astype(bf16)]


def arrange_w_in(shards):
    parts = []
    for a, b, _ in sorted(ORIG_SEGS, key=lambda seg: seg[2]):
        for s in range(N_CHIPS):
            lo, hi = max(a, s * IN_SHARD), min(b, (s + 1) * IN_SHARD)
            if lo < hi:
                parts.append(shards[s][:, lo - s * IN_SHARD:hi - s * IN_SHARD])
    parts.append(jnp.zeros((shards.shape[1], PCOLS - C_DT - (OFF_Q - OFF_DT)), shards.dtype))
    return jnp.concatenate(parts, axis=1)


def shard_w_in_grad(g):
    out = []
    for s in range(N_CHIPS):
        parts = []
        for a, b, first in ORIG_SEGS:
            lo, hi = max(a, s * IN_SHARD), min(b, (s + 1) * IN_SHARD)
            if lo < hi:
                parts.append(g[:, first + lo - a:first + hi - a])
        out.append(jnp.concatenate(parts, axis=1))
    return jnp.stack(out)


def unpack_gathered(gathered):
    g_in, g_wide, g_xkv = gathered
    out = dict(w_in=arrange_w_in(g_in), w_xkv=jnp.concatenate([g_xkv[s] for s in range(N_CHIPS)], axis=1))
    for n in WIDE:
        piece = g_wide[:, WIDE_OFF[n]:WIDE_OFF[n] + WIDE_ROWS[n], :]
        if n == "w_mlp_up":
            out[n] = jnp.concatenate([piece[s] for s in range(N_CHIPS)], axis=1)
        else:
            out[n] = piece.reshape(N_CHIPS * WIDE_ROWS[n], D)
    return out


def pack_grads(gs):
    cols = lambda g, q: jnp.stack([g[:, s * q:(s + 1) * q] for s in range(N_CHIPS)])
    wide = [cols(gs[n], D) if n == "w_mlp_up" else gs[n].reshape(N_CHIPS, WIDE_ROWS[n], D) for n in WIDE]
    arrs = [shard_w_in_grad(gs["w_in"]), jnp.concatenate(wide, axis=1), cols(gs["w_xkv"], 2048 // N_CHIPS)]
    return [a.reshape(N_CHIPS, 2, a.shape[1] // 2, a.shape[2]) for a in arrs]


ROW_TILE = 256


def _pad128(v):
    return jnp.pad(v.reshape(1, -1), ((0, 0), (0, 128 - v.shape[-1])))


def layer_forward(i, h, mem, w, p, bias, next_shards=None):
    tile = min(ROW_TILE, h.shape[0])
    s = dict(h=h)
    row = lambda a: a.reshape(1, -1)
    s["u"], = tile_fwd(f"rms_mix{i}", f_rms, [(h, D, 0)], [row(p["norm_mix"])], [(D, bf16)], tile)
    s["proj"] = proj = matmul(s["u"], w["w_in"], "nn", f"mm_in{i}")
    s["c1"], = conv_fwd(f"conv_a{i}", proj, C_CONV, C_CONV + D, p["conv_dw_w"], row(p["conv_dw_b"]), True, False)
    s["ca"], = tile_fwd(f"ln_silu{i}", f_ln_silu, [(s["c1"], D, 0)], [row(p["conv_ln_g"]), row(p["conv_ln_b"])], [(D, bf16)], tile)
    s["ya"] = matmul(s["ca"], w["w_conv_out"], "nn", f"mm_conv_out{i}")
    s["xpre"], s["xact"] = conv_fwd(f"conv_b{i}", proj, C_XBC, C_XBC, p["ssd_conv_w"], row(p["ssd_conv_b"]), False, True)
    ssd_p = (_pad128(p["ssd_dt_bias"]), _pad128(p["ssd_A_log"]), _pad128(p["ssd_D"]), row(p["ssd_norm_g"]))
    s["yB"], s["states"], landed = ssd_fwd(f"ssd_fwd{i}", s["xact"], proj, *ssd_p,
                                           ex=ex_gather_ici(next_shards) if next_shards else None)
    s["yb"] = matmul(s["yB"], w["w_ssd_out"], "nn", f"mm_ssd_out{i}")
    swa_p = (row(p["attn_q_norm"]), row(p["attn_k_norm"]), _pad128(p["attn_sinks"]))
    s["oc"], gathered = swa_fwd(f"swa_fwd{i}", proj, *swa_p, bias, ex=ex_gather_d2d(landed) if next_shards else None)
    s["yc"] = matmul(s["oc"], w["w_attn_out"], "nn", f"mm_attn_out{i}")
    gate_cols = [(proj, D, C_G // D + k) for k in range(3)]
    s["merged"], = tile_fwd(f"merge{i}", f_merge, gate_cols + [(s["ya"], D, 0), (s["yb"], D, 0), (s["yc"], D, 0)],
                            [row(p["gate_bias"])], [(D, bf16)], tile)
    s["h1"] = h1 = matmul(s["merged"], w["w_mix_out"], "nn", f"mm_mix_out{i}", residual=h)
    s["hx"], = tile_fwd(f"rms_x{i}", f_rms, [(h1, D, 0)], [row(p["norm_xattn"])], [(D, bf16)], tile)
    s["memh"], = tile_fwd(f"rms_mem{i}", f_rms, [(mem, D, 0)], [row(p["norm_mem"])], [(D, bf16)], MEM_LEN)
    s["xq"] = matmul(s["hx"], w["w_xq"], "nn", f"mm_xq{i}")
    s["kv"] = matmul(s["memh"], w["w_xkv"], "nn", f"mm_xkv{i}")
    s["xo"] = xa_fwd(f"xa_fwd{i}", s["xq"], s["kv"], row(p["xattn_q_norm"]), row(p["xattn_k_norm"]))
    s["h2"] = h2 = matmul(s["xo"], w["w_xo"], "nn", f"mm_xo{i}", residual=h1)
    s["um"], = tile_fwd(f"rms_mlp{i}", f_rms, [(h2, D, 0)], [row(p["norm_mlp"])], [(D, bf16)], tile)
    s["up"] = matmul(s["um"], w["w_mlp_up"], "nn", f"mm_up{i}")
    s["act"], = tile_fwd(f"relu2{i}", f_relu2, [(s["up"], MLP, 0)], [], [(MLP, bf16)], tile)
    h3 = matmul(s["act"], w["w_mlp_down"], "nn", f"mm_down{i}", residual=h2)
    return h3, s, gathered


def layer_backward(i, dh3, mem, w, p, bias, s, pending=None, c_arr=None):
    t_len = dh3.shape[0]
    tile = min(ROW_TILE, t_len)
    row = lambda a: a.reshape(1, -1)
    gw, gp = {}, {}
    dact = matmul(dh3, w["w_mlp_down"], "nt", f"mm_down_dx{i}")
    gw["w_mlp_down"] = matmul(s["act"], dh3, "tn", f"mm_down_dw{i}")
    (dup,), _ = tile_bwd(f"relu2_bwd{i}", f_relu2, [(s["up"], MLP, 0)], [], [dact], [bf16], tile)
    gw["w_mlp_up"] = matmul(s["um"], dup, "tn", f"mm_up_dw{i}")
    dum = matmul(dup, w["w_mlp_up"], "nt", f"mm_up_dx{i}")
    (dh2,), (g,) = tile_bwd(f"rms_mlp_bwd{i}", f_rms, [(s["h2"], D, 0)], [row(p["norm_mlp"])], [dum], [f32], tile, addend=dh3)
    gp["norm_mlp"] = g
    dxo = matmul(dh2, w["w_xo"], "nt", f"mm_xo_dx{i}")
    gw["w_xo"] = matmul(s["xo"], dh2, "tn", f"mm_xo_dw{i}")
    dxq, dk, dv, gp["xattn_q_norm"], gp["xattn_k_norm"] = xa_bwd(f"xa_bwd{i}", s["xq"], s["kv"], dxo, row(p["xattn_q_norm"]),
                                                                 row(p["xattn_k_norm"]))
    dkv = jnp.concatenate([dk, dv], axis=1)
    gw["w_xq"] = matmul(s["hx"], dxq, "tn", f"mm_xq_dw{i}")
    dhx = matmul(dxq, w["w_xq"], "nt", f"mm_xq_dx{i}")
    gw["w_xkv"] = matmul(s["memh"], dkv, "tn", f"mm_xkv_dw{i}")
    dmemh = matmul(dkv, w["w_xkv"], "nt", f"mm_xkv_dx{i}")
    _, (g,) = tile_bwd(f"rms_mem_bwd{i}", f_rms, [(mem, D, 0)], [row(p["norm_mem"])], [dmemh], [f32], MEM_LEN)
    gp["norm_mem"] = g
    (dh1,), (g,) = tile_bwd(f"rms_x_bwd{i}", f_rms, [(s["h1"], D, 0)], [row(p["norm_xattn"])], [dhx], [f32], tile, addend=dh2)
    gp["norm_xattn"] = g
    proj = s["proj"]
    dmerged = matmul(dh1, w["w_mix_out"], "nt", f"mm_mix_out_dx{i}")
    gw["w_mix_out"] = matmul(s["merged"], dh1, "tn", f"mm_mix_out_dw{i}")
    gate_cols = [(proj, D, C_G // D + k) for k in range(3)]
    (dpa, dpb, dpc, dya, dyb, dyc), (g,) = tile_bwd(
        f"merge_bwd{i}", f_merge, gate_cols + [(s["ya"], D, 0), (s["yb"], D, 0), (s["yc"], D, 0)], [row(p["gate_bias"])],
        [dmerged], [bf16] * 6, tile)
    gp["gate_bias"] = g
    dca = matmul(dya, w["w_conv_out"], "nt", f"mm_conv_out_dx{i}")
    gw["w_conv_out"] = matmul(s["ca"], dya, "tn", f"mm_conv_out_dw{i}")
    (dc1,), (gp["conv_ln_g"], gp["conv_ln_b"]) = tile_bwd(
        f"ln_silu_bwd{i}", f_ln_silu, [(s["c1"], D, 0)], [row(p["conv_ln_g"]), row(p["conv_ln_b"])], [dca], [f32], tile)
    da, dgate, gp["conv_dw_w"], gp["conv_dw_b"] = conv_bwd(f"conv_a_bwd{i}", dc1, None, proj, C_CONV, C_CONV + D,
                                                           p["conv_dw_w"], True, False, bf16)
    doc = matmul(dyc, w["w_attn_out"], "nt", f"mm_attn_out_dx{i}")
    gw["w_attn_out"] = matmul(s["oc"], dyc, "tn", f"mm_attn_out_dw{i}")
    swa_p = (row(p["attn_q_norm"]), row(p["attn_k_norm"]), _pad128(p["attn_sinks"]))
    (dq, dkp, dkc, dvp, dvc, gp["attn_q_norm"], gp["attn_k_norm"], g, dbias), swapped = swa_bwd(
        f"swa_bwd{i}", proj, doc, *swa_p, bias, ex=ex_swap_halves(pending) if pending else None)
    gp["attn_sinks"] = g[:, :ATTN_HEADS]
    shift = lambda a: jnp.concatenate([a[BLK:], jnp.zeros((BLK, a.shape[1]), a.dtype)], axis=0)
    dk_ = (dkc + shift(dkp)).astype(bf16)
    dv_ = (dvc + shift(dvp)).astype(bf16)
    partials = [add_halves(f"add_halves{i + 1}_{k}", g_, got, c_arr) for k, (g_, got) in enumerate(zip(pending, swapped))] if pending else None
    dyB = matmul(dyb, w["w_ssd_out"], "nt", f"mm_ssd_out_dx{i}")
    gw["w_ssd_out"] = matmul(s["yB"], dyb, "tn", f"mm_ssd_out_dw{i}")
    ssd_p = (_pad128(p["ssd_dt_bias"]), _pad128(p["ssd_A_log"]), _pad128(p["ssd_D"]), row(p["ssd_norm_g"]))
    (dxs, dbm, dcm, dz, ddt4, g1, g2, g3, g4), arrived = ssd_bwd(f"ssd_bwd{i}", s["xact"], proj, s["states"], dyB, *ssd_p,
                                                                  ex=ex_scatter(partials) if pending else None)
    gp["ssd_dt_bias"], gp["ssd_A_log"], gp["ssd_D"], gp["ssd_norm_g"] = g1[:, :SSD_HEADS], g2[:, :SSD_HEADS], g3[:, :SSD_HEADS], g4
    ddt = (ddt4[0] + ddt4[1] + ddt4[2] + ddt4[3]).astype(bf16)
    dxact = jnp.concatenate([dxs, dbm, dcm], axis=1)
    dxbc, gp["ssd_conv_w"], gp["ssd_conv_b"] = conv_bwd(f"conv_b_bwd{i}", dxact, s["xpre"], proj, C_XBC, C_XBC, p["ssd_conv_w"],
                                                        False, True, bf16)
    dproj = jnp.concatenate([da, dgate, dz, dxbc, dq, dpa, dpb, dpc, dk_, dv_, ddt, jnp.zeros((t_len, PCOLS - C_DT - 128), bf16)], axis=1)
    gw["w_in"] = matmul(s["u"], dproj, "tn", f"mm_in_dw{i}")
    du = matmul(dproj, w["w_in"], "nt", f"mm_in_dx{i}")
    (dh,), (g,) = tile_bwd(f"rms_mix_bwd{i}", f_rms, [(s["h"], D, 0)], [row(p["norm_mix"])], [du], [f32], tile, addend=dh1)
    gp["norm_mix"] = g
    return dh, gw, gp, dbias, ((partials, arrived) if pending else None)


SMALL = ["rel_table", "norm_mix", "gate_bias", "conv_dw_w", "conv_dw_b", "conv_ln_g", "conv_ln_b", "ssd_conv_w", "ssd_conv_b",
         "ssd_dt_bias", "ssd_A_log", "ssd_D", "ssd_norm_g", "attn_q_norm", "attn_k_norm", "attn_sinks", "norm_xattn", "norm_mem",
         "xattn_q_norm", "xattn_k_norm", "norm_mlp"]
SMALL_SHARDED = dict(gate_bias=D, conv_dw_w=D, ssd_conv_w=SSD_XBC)
ORDER = ["rel_table", "norm_mix", "w_in", "gate_bias", "conv_dw_w", "conv_dw_b", "conv_ln_g", "conv_ln_b", "w_conv_out",
         "ssd_conv_w", "ssd_conv_b", "ssd_dt_bias", "ssd_A_log", "ssd_D", "ssd_norm_g", "w_ssd_out", "attn_q_norm", "attn_k_norm",
         "attn_sinks", "w_attn_out", "w_mix_out", "norm_xattn", "norm_mem", "w_xq", "w_xkv", "xattn_q_norm", "xattn_k_norm",
         "w_xo", "norm_mlp", "w_mlp_up", "w_mlp_down"]


def finish_reduction(i, partials, arrived, me_arr):
    mine = [add_partials(f"add_partials{i}_{k}", p, got, me_arr) for k, (p, got) in enumerate(zip(partials, arrived))]
    both = run_exchange(f"share_halves{i}", ex_share(mine))
    return [b.reshape(2 * b.shape[1], b.shape[2]) for b in both]


def kernel(x, mem, rel_table, norm_mix, w_in, gate_bias, conv_dw_w, conv_dw_b, conv_ln_g, conv_ln_b, w_conv_out, ssd_conv_w, ssd_conv_b, ssd_dt_bias, ssd_A_log, ssd_D, ssd_norm_g, w_ssd_out, attn_q_norm, attn_k_norm, attn_sinks, w_attn_out, w_mix_out, norm_xattn, norm_mem, w_xq, w_xkv, xattn_q_norm, xattn_k_norm, w_xo, norm_mlp, w_mlp_up, w_mlp_down, loss_target, m_rel_table, m_norm_mix, m_w_in, m_gate_bias, m_conv_dw_w, m_conv_dw_b, m_conv_ln_g, m_conv_ln_b, m_w_conv_out, m_ssd_conv_w, m_ssd_conv_b, m_ssd_dt_bias, m_ssd_A_log, m_ssd_D, m_ssd_norm_g, m_w_ssd_out, m_attn_q_norm, m_attn_k_norm, m_attn_sinks, m_w_attn_out, m_w_mix_out, m_norm_xattn, m_norm_mem, m_w_xq, m_w_xkv, m_xattn_q_norm, m_xattn_k_norm, m_w_xo, m_norm_mlp, m_w_mlp_up, m_w_mlp_down, v_rel_table, v_norm_mix, v_w_in, v_gate_bias, v_conv_dw_w, v_conv_dw_b, v_conv_ln_g, v_conv_ln_b, v_w_conv_out, v_ssd_conv_w, v_ssd_conv_b, v_ssd_dt_bias, v_ssd_A_log, v_ssd_D, v_ssd_norm_g, v_w_ssd_out, v_attn_q_norm, v_attn_k_norm, v_attn_sinks, v_w_attn_out, v_w_mix_out, v_norm_xattn, v_norm_mem, v_w_xq, v_w_xkv, v_xattn_q_norm, v_xattn_k_norm, v_w_xo, v_norm_mlp, v_w_mlp_up, v_w_mlp_down):
    a = dict(locals())
    wts = {n: a[n] for n in ORDER}
    ms = {n: a["m_" + n] for n in ORDER}
    vs = {n: a["v_" + n] for n in ORDER}
    depth = norm_mix.shape[0]
    ax, ay, ac = lax.axis_index("x"), lax.axis_index("y"), lax.axis_index("c")
    chip = 2 * ax + ay

    small = {n: wts[n] for n in SMALL if n != "rel_table" and n not in SMALL_SHARDED}
    gather_buf, spans, r0 = [], {}, 0
    for n, width in SMALL_SHARDED.items():
        blk = wts[n]
        q = width // N_CHIPS
        whole = lax.dynamic_update_slice(jnp.zeros(blk.shape[:-1] + (width,), f32), blk, (0, 0, chip * q))
        flat = whole.reshape(-1, 128)
        spans[n] = (r0, flat.shape[0], whole.shape)
        r0 += flat.shape[0]
        gather_buf.append(flat)
    summed = allreduce_small("gather_small_params", jnp.concatenate(gather_buf, axis=0))
    for n, (s0, nr, shp) in spans.items():
        small[n] = summed[s0:s0 + nr].reshape(shp) * 0.5

    c_arr = jnp.reshape(ac, (1,)).astype(jnp.int32)
    me_arr = jnp.reshape(chip, (1,)).astype(jnp.int32)
    x0, mem0 = x[0], mem[0]
    bucket = jnp.asarray(_bucket_table())
    bias = relbias_fwd(rel_table, bucket)

    gathered = run_exchange("gather_d2d_first", ex_gather_d2d(run_exchange("gather_ici_first", ex_gather_ici(pack_shards(wts, 0)))))
    h, saved, ws = x0, [], []
    for i in range(depth):
        w = unpack_gathered(gathered)
        p = {n: small[n][i] for n in small}
        h, s, gathered = layer_forward(i, h, mem0, w, p, bias, pack_shards(wts, i + 1) if i + 1 < depth else None)
        saved.append(s)
        ws.append((w, p))
    grad_x, loss_tile = loss_head(h, loss_target[0], min(ROW_TILE, h.shape[0]))
    loss = lax.psum(loss_tile[0, 0], ("x", "y", "c"))

    reduced, small_grads, dbiases, pending = [None] * depth, [None] * depth, [], None
    for i in reversed(range(depth)):
        w, p = ws[i]
        grad_x, gw, gp, dbias, travelled = layer_backward(i, grad_x, mem0, w, p, bias, saved[i], pending, c_arr)
        if travelled is not None:
            reduced[i + 1] = finish_reduction(i + 1, *travelled, me_arr)
        small_grads[i] = gp
        dbiases.append(dbias)
        pending = pack_grads(gw)
    swapped = run_exchange("swap_halves_last", ex_swap_halves(pending))
    partials = [add_halves(f"add_halves0_{k}", g_, got, c_arr) for k, (g_, got) in enumerate(zip(pending, swapped))]
    reduced[0] = finish_reduction(0, partials, run_exchange("scatter_last", ex_scatter(partials)), me_arr)
    sg = {n: jnp.stack([small_grads[i][n].reshape(small[n].shape[1:]) for i in range(depth)]) for n in small}
    sg["rel_table"] = relbias_bwd(dbiases, bucket)

    grads, deltas, new_m, new_v = {}, {}, {}, {}
    sources = dict(w_in=(0, 0), w_xkv=(2, 0), **{n: (1, WIDE_OFF[n]) for n in WIDE})
    for n, (k, r0) in sources.items():
        grads[n], deltas[n], new_m[n], new_v[n] = adamw_layers(f"adamw_{n}", wts[n], ms[n], vs[n], [reduced[i][k] for i in range(depth)], r0)

    parts, spans, r0 = [], {}, 0
    for n in SMALL:
        flat = sg[n].reshape(-1)
        nr = -(-flat.shape[0] // 128)
        nr = -(-nr // 8) * 8
        flat = jnp.pad(flat, (0, nr * 128 - flat.shape[0])).reshape(nr, 128)
        spans[n] = (r0, nr, sg[n].shape)
        r0 += nr
        parts.append(flat)
    summed = allreduce_small("allreduce_small_grads", jnp.concatenate(parts, axis=0))
    for n, (s0, nr, shp) in spans.items():
        size = int(np.prod(shp))
        g = summed[s0:s0 + nr].reshape(-1)[:size].reshape(shp)
        if n in SMALL_SHARDED:
            q = SMALL_SHARDED[n] // N_CHIPS
            g = lax.dynamic_slice_in_dim(g, chip * q, q, axis=g.ndim - 1)
        grads[n] = g

    for n in SMALL:
        shp = wts[n].shape
        two = (lambda t: t.reshape(-1, shp[-1]))
        d, nm, nv = adamw(f"adamw_{n}", two(wts[n]), two(grads[n]), two(ms[n]), two(vs[n]))
        deltas[n], new_m[n], new_v[n] = d.reshape(shp), nm.reshape(shp), nv.reshape(shp)

    return (loss, grad_x[None], *[grads[n] for n in ORDER], *[deltas[n] for n in ORDER],
            *[new_m[n] for n in ORDER], *[new_v[n] for n in ORDER])
```

```python
import functools
import math

import numpy as np
import jax
import jax.numpy as jnp
from jax import lax
from jax.experimental import pallas as pl
from jax.experimental.pallas import tpu as pltpu

f32, bf16 = jnp.float32, jnp.bfloat16

D = 1024
EPS = 1e-6
NEG_INF = -1e30
CONV_K = 31
SSD_INNER = 2048
SSD_HEADS = 32
SSD_P = 64
SSD_GROUPS = 4
SSD_N = 128
SSD_K = 4
CHUNK = 128
SSD_XBC = 3072
ATTN_HEADS = 16
ATTN_KV = 4
ATTN_D = 64
BLK = 128
REL_BUCKETS = 32
XH = 4
XD = 256
MLP = 4096
MEM_LEN = 256

OFF_Z, OFF_XBC, OFF_DT, OFF_Q, OFF_K, OFF_V, OFF_GATE, IN_COLS = 2048, 4096, 7168, 7200, 8224, 8480, 8736, 11808
C_CONV, C_Z, C_XBC, C_Q, C_G, C_K, C_V, C_DT, PCOLS = 0, 2048, 4096, 7168, 8192, 11264, 11520, 11776, 12288

ADAM_LR, ADAM_B1, ADAM_B2, ADAM_EPS, ADAM_WD, ADAM_STEP = 0.001, 0.9, 0.999, 1e-08, 0.01, 10

VMEM_LIMIT_BYTES = 56 * 1024 * 1024
N_CHIPS = 4
BIG = ["w_in", "w_conv_out", "w_ssd_out", "w_attn_out", "w_mix_out", "w_xq", "w_xkv", "w_xo", "w_mlp_up", "w_mlp_down"]
WIDE = ["w_conv_out", "w_ssd_out", "w_attn_out", "w_mix_out", "w_xq", "w_xo", "w_mlp_up", "w_mlp_down"]
WIDE_ROWS = dict(w_conv_out=256, w_ssd_out=512, w_attn_out=256, w_mix_out=256, w_xq=256, w_xo=256, w_mlp_up=1024, w_mlp_down=1024)
WIDE_OFF = {n: sum(WIDE_ROWS[m] for m in WIDE[:k]) for k, n in enumerate(WIDE)}
WIDE_TOTAL = sum(WIDE_ROWS.values())
IN_SHARD = IN_COLS // N_CHIPS
ORIG_SEGS = [(0, OFF_DT, 0), (OFF_DT, OFF_Q, C_DT), (OFF_Q, OFF_K, C_Q), (OFF_K, OFF_GATE, C_K), (OFF_GATE, IN_COLS, C_G)]


def _params(sem=None):
    return pltpu.CompilerParams(dimension_semantics=sem, vmem_limit_bytes=VMEM_LIMIT_BYTES)


NN, NT, TN = ((1,), (0,)), ((1,), (1,)), ((0,), (0,))


def _dg(a, b, dims):
    return lax.dot_general(a.astype(bf16), b.astype(bf16), (dims, ((), ())), preferred_element_type=f32)


@jax.custom_vjp
def dot_nn(a, b):
    return _dg(a, b, NN)


dot_nn.defvjp(lambda a, b: (_dg(a, b, NN), (a, b)), lambda r, g: (_dg(g, r[1], NT), _dg(r[0], g, TN)))


@jax.custom_vjp
def dot_nt(a, b):
    return _dg(a, b, NT)


dot_nt.defvjp(lambda a, b: (_dg(a, b, NT), (a, b)), lambda r, g: (_dg(g, r[1], NN), _dg(g, r[0], TN)))


@jax.custom_vjp
def dot_tn(a, b):
    return _dg(a, b, TN)


dot_tn.defvjp(lambda a, b: (_dg(a, b, TN), (a, b)), lambda r, g: (_dg(r[1], g, NT), _dg(r[0], g, NN)))


def _tri(n, upper):
    r = lax.broadcasted_iota(jnp.int32, (n, n), 0)
    c = lax.broadcasted_iota(jnp.int32, (n, n), 1)
    return jnp.where((c >= r) if upper else (r >= c), 1.0, 0.0).astype(f32)


def _hdot(a, b):
    return lax.dot_general(a, b, (NN, ((), ())), preferred_element_type=f32, precision=lax.Precision.HIGHEST)


@jax.custom_vjp
def cumsum_rows(x):
    return _hdot(_tri(x.shape[0], False), x)


cumsum_rows.defvjp(lambda x: (_hdot(_tri(x.shape[0], False), x), None),
                   lambda _, g: (_hdot(_tri(g.shape[0], True), g),))


def _pick(n, prefs):
    for p in prefs:
        if n % p == 0:
            return p
    return n


def matmul(a, b, mode, name, out_dtype=f32, residual=None):
    if mode == "nn":
        (m, k), n = a.shape, b.shape[1]
    elif mode == "nt":
        (m, k), n = a.shape, b.shape[0]
    else:
        (k, m), n = a.shape, b.shape[1]
    tm = _pick(m, (1024, 512, 256))
    tn = _pick(n, (1024, 512, 256, 128))
    tk = _pick(k, (1024, 512))
    nk = k // tk
    dims = dict(nn=NN, nt=NT, tn=TN)[mode]
    a_spec = pl.BlockSpec((tk, tm), lambda i, j, l: (l, i)) if mode == "tn" else pl.BlockSpec((tm, tk), lambda i, j, l: (i, l))
    b_spec = pl.BlockSpec((tn, tk), lambda i, j, l: (j, l)) if mode == "nt" else pl.BlockSpec((tk, tn), lambda i, j, l: (l, j))
    o_spec = pl.BlockSpec((tm, tn), lambda i, j, l: (i, j))
    has_res = residual is not None

    def body(*refs):
        a_ref, b_ref = refs[0], refs[1]
        r_ref = refs[2] if has_res else None
        o_ref = refs[2 + has_res]
        acc = refs[3 + has_res]
        l = pl.program_id(2)

        @pl.when(l == 0)
        def _():
            acc[...] = jnp.zeros(acc.shape, f32)

        acc[...] += _dg(a_ref[...], b_ref[...], dims)

        @pl.when(l == nk - 1)
        def _():
            r = acc[...]
            if has_res:
                r = r + r_ref[...]
            o_ref[...] = r.astype(o_ref.dtype)

    ins = [a, b] + ([residual] if has_res else [])
    return pl.pallas_call(
        body, name=name, grid=(m // tm, n // tn, nk),
        in_specs=[a_spec, b_spec] + ([o_spec] if has_res else []),
        out_specs=o_spec, out_shape=jax.ShapeDtypeStruct((m, n), out_dtype),
        scratch_shapes=[pltpu.VMEM((tm, tn), f32)],
        compiler_params=_params(("parallel", "parallel", "arbitrary")),
    )(*ins)


def _tile_in_specs(tiled, params, tile):
    specs = [pl.BlockSpec((tile, cols), functools.partial(lambda i, cb: (i, cb), cb=cb)) for (_, cols, cb) in tiled]
    specs += [pl.BlockSpec(p.shape, lambda i: (0, 0)) for p in params]
    return specs


def tile_fwd(name, f, tiled, params, outs, tile):
    rows = tiled[0][0].shape[0]
    nt, npar = len(tiled), len(params)

    def body(*refs):
        xs = [r[...].astype(f32) for r in refs[:nt + npar]]
        res = f(*xs)
        for o_ref, o in zip(refs[nt + npar:], res):
            o_ref[...] = o.astype(o_ref.dtype)

    return pl.pallas_call(
        body, name=name, grid=(rows // tile,),
        in_specs=_tile_in_specs(tiled, params, tile),
        out_specs=[pl.BlockSpec((tile, c), lambda i: (i, 0)) for c, _ in outs],
        out_shape=[jax.ShapeDtypeStruct((rows, c), dt) for c, dt in outs],
        compiler_params=_params(("parallel",)),
    )(*[t[0] for t in tiled], *params)


def tile_bwd(name, f, tiled, params, cots, d_dtypes, tile, addend=None):
    rows = tiled[0][0].shape[0]
    nt, npar, nc = len(tiled), len(params), len(cots)
    has_add = addend is not None

    def body(*refs):
        i = pl.program_id(0)
        xs = [r[...].astype(f32) for r in refs[:nt + npar]]
        gs = tuple(r[...].astype(f32) for r in refs[nt + npar:nt + npar + nc])
        pos = nt + npar + nc
        add_ref = refs[pos] if has_add else None
        pos += has_add
        dt_refs, dp_refs = refs[pos:pos + nt], refs[pos + nt:pos + nt + npar]
        _, vjp = jax.vjp(f, *xs)
        ds = vjp(gs)
        for k in range(nt):
            d = ds[k]
            if has_add and k == 0:
                d = d + add_ref[...]
            dt_refs[k][...] = d.astype(dt_refs[k].dtype)

        @pl.when(i == 0)
        def _():
            for r in dp_refs:
                r[...] = jnp.zeros(r.shape, f32)

        for k in range(npar):
            dp_refs[k][...] += ds[nt + k]

    in_specs = _tile_in_specs(tiled, params, tile)
    in_specs += [pl.BlockSpec((tile, c.shape[1]), lambda i: (i, 0)) for c in cots]
    ins = [t[0] for t in tiled] + list(params) + list(cots)
    if has_add:
        in_specs.append(pl.BlockSpec((tile, addend.shape[1]), lambda i: (i, 0)))
        ins.append(addend)
    out_specs = [pl.BlockSpec((tile, cols), lambda i: (i, 0)) for (_, cols, _) in tiled]
    out_specs += [pl.BlockSpec(p.shape, lambda i: (0, 0)) for p in params]
    out_shape = [jax.ShapeDtypeStruct((rows, cols), dt) for (_, cols, _), dt in zip(tiled, d_dtypes)]
    out_shape += [jax.ShapeDtypeStruct(p.shape, f32) for p in params]
    res = pl.pallas_call(
        body, name=name, grid=(rows // tile,), in_specs=in_specs, out_specs=out_specs, out_shape=out_shape,
        compiler_params=_params(("arbitrary",)),
    )(*ins)
    return res[:nt], res[nt:]


def f_rms(h, g):
    return (h * lax.rsqrt(jnp.mean(h * h, axis=-1, keepdims=True) + EPS) * g,)


def f_ln_silu(x, g, b):
    mu = jnp.mean(x, axis=-1, keepdims=True)
    xc = x - mu
    y = xc * lax.rsqrt(jnp.mean(xc * xc, axis=-1, keepdims=True) + EPS) * g + b
    return (y * jax.nn.sigmoid(y),)


def f_merge(pa, pb, pc, ya, yb, yc, gb):
    ga = jax.nn.sigmoid(pa + gb[:, 0:D])
    gb_ = jax.nn.sigmoid(pb + gb[:, D:2 * D])
    gc = jax.nn.sigmoid(pc + gb[:, 2 * D:3 * D])
    return (ga * ya + gb_ * yb + gc * yc,)


def f_relu2(u):
    return (jnp.square(jnp.maximum(u, 0.0)),)


HALO = 32
CONV_CB = 256
SUB = 128


def _silu_grad(p):
    s = jax.nn.sigmoid(p)
    return s * (1.0 + p * (1.0 - s))


def conv_fwd(name, src, col0, gate_col0, w, b, glu, silu):
    t_len = src.shape[0]
    k_taps, c = w.shape
    tt = min(512, t_len)
    nt_ = t_len // tt
    cb0, gb0 = col0 // CONV_CB, gate_col0 // CONV_CB

    def body(*refs):
        pos = 0
        x_cur, x_prev = refs[0], refs[1]
        pos = 2
        if glu:
            g_cur, g_prev = refs[2], refs[3]
            pos = 4
        w_ref, b_ref = refs[pos], refs[pos + 1]
        pre_ref = refs[pos + 2]
        act_ref = refs[pos + 3] if silu else None
        xp = refs[-1]
        t = pl.program_id(1)
        cur = x_cur[...]
        tail = x_prev[...]
        if glu:
            cur = cur * jax.nn.sigmoid(g_cur[...])
            tail = tail * jax.nn.sigmoid(g_prev[...])
        xp[pl.ds(0, HALO), :] = jnp.where(t > 0, tail, 0.0)
        xp[pl.ds(HALO, tt), :] = cur
        for s in range(tt // SUB):
            acc = jnp.broadcast_to(b_ref[...], (SUB, CONV_CB))
            for j in range(k_taps):
                acc = acc + xp[pl.ds(s * SUB + HALO - (k_taps - 1) + j, SUB), :] * w_ref[pl.ds(j, 1), :]
            pre_ref[pl.ds(s * SUB, SUB), :] = acc
            if silu:
                act_ref[pl.ds(s * SUB, SUB), :] = acc * jax.nn.sigmoid(acc)

    blk = lambda off: pl.BlockSpec((tt, CONV_CB), functools.partial(lambda j, t, off: (t, off + j), off=off))
    prev = lambda off: pl.BlockSpec((HALO, CONV_CB), functools.partial(lambda j, t, off: (jnp.maximum(t * (tt // HALO) - 1, 0), off + j), off=off))
    in_specs, ins = [blk(cb0), prev(cb0)], [src, src]
    if glu:
        in_specs += [blk(gb0), prev(gb0)]
        ins += [src, src]
    in_specs += [pl.BlockSpec((k_taps, CONV_CB), lambda j, t: (0, j)), pl.BlockSpec((1, CONV_CB), lambda j, t: (0, j))]
    ins += [w, b]
    o_spec = pl.BlockSpec((tt, CONV_CB), lambda j, t: (t, j))
    n_out = 2 if silu else 1
    res = pl.pallas_call(
        body, name=name, grid=(c // CONV_CB, nt_), in_specs=in_specs, out_specs=[o_spec] * n_out,
        out_shape=[jax.ShapeDtypeStruct((t_len, c), f32)] * n_out,
        scratch_shapes=[pltpu.VMEM((HALO + tt, CONV_CB), f32)],
        compiler_params=_params(("parallel", "arbitrary")),
    )(*ins)
    return res


def conv_bwd(name, dy, pre, src, col0, gate_col0, w, glu, silu, out_dtype):
    t_len = src.shape[0]
    k_taps, c = w.shape
    tt = min(512, t_len)
    nt_ = t_len // tt
    cb0, gb0 = col0 // CONV_CB, gate_col0 // CONV_CB

    def body(*refs):
        dy_cur, dy_next = refs[0], refs[1]
        pos = 2
        if silu:
            p_cur, p_next = refs[2], refs[3]
            pos = 4
        x_cur, x_prev = refs[pos], refs[pos + 1]
        pos += 2
        if glu:
            g_cur, g_prev = refs[pos], refs[pos + 1]
            pos += 2
        w_ref = refs[pos]
        pos += 1
        n_dx = 2 if glu else 1
        dx_refs = refs[pos:pos + n_dx]
        dw_ref, db_ref = refs[pos + n_dx], refs[pos + n_dx + 1]
        dp, xp = refs[-2], refs[-1]
        t = pl.program_id(1)

        dcur = dy_cur[...]
        dhead = dy_next[...]
        if silu:
            dcur = dcur * _silu_grad(p_cur[...])
            dhead = dhead * _silu_grad(p_next[...])
        dp[pl.ds(0, tt), :] = dcur
        dp[pl.ds(tt, HALO), :] = jnp.where(t < nt_ - 1, dhead, 0.0)
        cur = x_cur[...]
        tail = x_prev[...]
        if glu:
            cur = cur * jax.nn.sigmoid(g_cur[...])
            tail = tail * jax.nn.sigmoid(g_prev[...])
        xp[pl.ds(0, HALO), :] = jnp.where(t > 0, tail, 0.0)
        xp[pl.ds(HALO, tt), :] = cur

        @pl.when(t == 0)
        def _():
            dw_ref[...] = jnp.zeros(dw_ref.shape, f32)
            db_ref[...] = jnp.zeros(db_ref.shape, f32)

        for s in range(tt // SUB):
            d_sub = dp[pl.ds(s * SUB, SUB), :]
            acc = jnp.zeros((SUB, CONV_CB), f32)
            for j in range(k_taps):
                acc = acc + dp[pl.ds(s * SUB + (k_taps - 1) - j, SUB), :] * w_ref[pl.ds(j, 1), :]
                xs = xp[pl.ds(s * SUB + HALO - (k_taps - 1) + j, SUB), :]
                dw_ref[pl.ds(j, 1), :] += jnp.sum(d_sub * xs, axis=0, keepdims=True)
            db_ref[...] += jnp.sum(d_sub, axis=0, keepdims=True)
            if glu:
                a = x_cur[pl.ds(s * SUB, SUB), :]
                sg = jax.nn.sigmoid(g_cur[pl.ds(s * SUB, SUB), :])
                dx_refs[0][pl.ds(s * SUB, SUB), :] = (acc * sg).astype(out_dtype)
                dx_refs[1][pl.ds(s * SUB, SUB), :] = (acc * a * sg * (1.0 - sg)).astype(out_dtype)
            else:
                dx_refs[0][pl.ds(s * SUB, SUB), :] = acc.astype(out_dtype)

    blk = lambda off: pl.BlockSpec((tt, CONV_CB), functools.partial(lambda j, t, off: (t, off + j), off=off))
    prev = lambda off: pl.BlockSpec((HALO, CONV_CB), functools.partial(lambda j, t, off: (jnp.maximum(t * (tt // HALO) - 1, 0), off + j), off=off))
    nxt = pl.BlockSpec((HALO, CONV_CB), lambda j, t: (jnp.minimum((t + 1) * (tt // HALO), t_len // HALO - 1), j))
    in_specs, ins = [blk(0), nxt], [dy, dy]
    if silu:
        in_specs += [blk(0), nxt]
        ins += [pre, pre]
    in_specs += [blk(cb0), prev(cb0)]
    ins += [src, src]
    if glu:
        in_specs += [blk(gb0), prev(gb0)]
        ins += [src, src]
    in_specs.append(pl.BlockSpec((k_taps, CONV_CB), lambda j, t: (0, j)))
    ins.append(w)
    n_dx = 2 if glu else 1
    o_spec = pl.BlockSpec((tt, CONV_CB), lambda j, t: (t, j))
    out_specs = [o_spec] * n_dx + [pl.BlockSpec((k_taps, CONV_CB), lambda j, t: (0, j)), pl.BlockSpec((1, CONV_CB), lambda j, t: (0, j))]
    out_shape = [jax.ShapeDtypeStruct((t_len, c), out_dtype)] * n_dx + [jax.ShapeDtypeStruct((k_taps, c), f32), jax.ShapeDtypeStruct((1, c), f32)]
    return pl.pallas_call(
        body, name=name, grid=(c // CONV_CB, nt_), in_specs=in_specs, out_specs=out_specs, out_shape=out_shape,
        scratch_shapes=[pltpu.VMEM((tt + HALO, CONV_CB), f32), pltpu.VMEM((HALO + tt, CONV_CB), f32)],
        compiler_params=_params(("parallel", "arbitrary")),
    )(*ins)


GH = SSD_HEADS // SSD_GROUPS
GW = GH * SSD_P


def _softplus(x):
    return jnp.maximum(x, 0.0) + jnp.log1p(jnp.exp(-jnp.abs(x)))


def f_ssd(hbase, x, z, bm, cm, dtraw, s_in, dt_bias, a_log, dskip, ng):
    q = x.shape[0]
    dt = _softplus(dtraw + dt_bias)
    da = dt * (-jnp.exp(a_log))
    cs = cumsum_rows(da)
    g_cb = dot_nt(cm, bm)
    lane = lax.broadcasted_iota(jnp.int32, (q, 128), 1)
    lane1 = lax.broadcasted_iota(jnp.int32, (1, 128), 1)
    causal = lax.broadcasted_iota(jnp.int32, (q, q), 0) >= lax.broadcasted_iota(jnp.int32, (q, q), 1)
    last = lax.broadcasted_iota(jnp.int32, (q, GW), 0) == q - 1
    dt_cols, cs_cols, d_cols = [], [], []
    for r in range(GH):
        sel = lane == hbase + r
        dt_cols.append(jnp.sum(jnp.where(sel, dt, 0.0), axis=1, keepdims=True))
        cs_cols.append(jnp.sum(jnp.where(sel, cs, 0.0), axis=1, keepdims=True))
        d_cols.append(jnp.sum(jnp.where(lane1 == hbase + r, dskip, 0.0), axis=1, keepdims=True))
    spread = lambda cols: jnp.concatenate([jnp.broadcast_to(c, (c.shape[0], SSD_P)) for c in cols], axis=1)
    dt_x, cs_x, d_x = spread(dt_cols), spread(cs_cols), spread(d_cols)
    cs_last = jnp.sum(jnp.where(last, cs_x, 0.0), axis=0, keepdims=True)
    xdt = x * dt_x
    y_diag = []
    for r in range(GH):
        m1 = jnp.broadcast_to(cs_cols[r], (q, q))
        decay = jnp.where(causal, jnp.exp(jnp.where(causal, m1 - m1.T, 0.0)), 0.0)
        y_diag.append(dot_nn(g_cb * decay, xdt[:, r * SSD_P:(r + 1) * SSD_P]))
    s_c = dot_tn(bm, xdt * jnp.exp(cs_last - cs_x))
    y_off = dot_nn(cm, s_in) * jnp.exp(cs_x)
    y = jnp.concatenate(y_diag, axis=1) + y_off + x * d_x
    s_out = s_in * jnp.exp(cs_last) + s_c
    y = y * (z * jax.nn.sigmoid(z))
    y = y * lax.rsqrt(jnp.mean(y * y, axis=-1, keepdims=True) + EPS) * ng
    return y, s_out


def _ssd_in_specs(cmap):
    return [
        pl.BlockSpec((CHUNK, GW), lambda g, c: (cmap(c), g)),
        pl.BlockSpec((CHUNK, GW), lambda g, c: (cmap(c), C_Z // GW + g)),
        pl.BlockSpec((CHUNK, SSD_N), lambda g, c: (cmap(c), SSD_INNER // SSD_N + g)),
        pl.BlockSpec((CHUNK, SSD_N), lambda g, c: (cmap(c), SSD_INNER // SSD_N + SSD_GROUPS + g)),
        pl.BlockSpec((CHUNK, 128), lambda g, c: (cmap(c), C_DT // 128)),
    ]


_SSD_PARAM_SPECS = [pl.BlockSpec((1, 128), lambda g, c: (0, 0))] * 3 + [pl.BlockSpec((1, GW), lambda g, c: (0, g))]


def ssd_fwd(name, xact, proj, dt_bias, a_log, dskip, ng, ex=None):
    t_len = xact.shape[0]
    nc = t_len // CHUNK
    h = _hosted(ex)

    def body(*refs):
        x_ref, z_ref, b_ref, c_ref, dt_ref, p1, p2, p3, p4 = refs[:9]
        y_ref, st_ref = refs[9 + h.n_in:11 + h.n_in]
        s_scr = refs[11 + h.n_in + h.n_out]
        g, c = pl.program_id(0), pl.program_id(1)
        h.at_start(refs, 9, 11 + h.n_in, (g == 0) & (c == 0))

        @pl.when(c == 0)
        def _():
            s_scr[...] = jnp.zeros(s_scr.shape, f32)

        s_in = s_scr[...]
        st_ref[0, 0] = s_in
        y, s_out = f_ssd(g * GH, x_ref[...], z_ref[...], b_ref[...], c_ref[...], dt_ref[...], s_in,
                         p1[...], p2[...], p3[...], p4[...])
        y_ref[...] = y.astype(y_ref.dtype)
        s_scr[...] = s_out
        h.at_end(refs, 9, 11 + h.n_in, (g == SSD_GROUPS - 1) & (c == nc - 1))

    res = pl.pallas_call(
        body, name=name, grid=(SSD_GROUPS, nc),
        in_specs=_ssd_in_specs(lambda c: c) + _SSD_PARAM_SPECS + h.in_specs,
        out_specs=[pl.BlockSpec((CHUNK, GW), lambda g, c: (c, g)),
                   pl.BlockSpec((1, 1, SSD_N, GW), lambda g, c: (g, c, 0, 0))] + h.out_specs,
        out_shape=[jax.ShapeDtypeStruct((t_len, SSD_INNER), bf16),
                   jax.ShapeDtypeStruct((SSD_GROUPS, nc, SSD_N, GW), f32)] + h.out_shape,
        scratch_shapes=[pltpu.VMEM((SSD_N, GW), f32)] + h.scratch,
        input_output_aliases=h.aliases(9, 2),
        compiler_params=_params(("arbitrary", "arbitrary")),
    )(xact, proj, xact, xact, proj, dt_bias, a_log, dskip, ng, *h.ins)
    return res[0], res[1], list(res[2:])


def ssd_bwd(name, xact, proj, states, dy, dt_bias, a_log, dskip, ng, ex=None):
    t_len = xact.shape[0]
    nc = t_len // CHUNK
    rev = lambda c: nc - 1 - c
    h = _hosted(ex)

    def body(*refs):
        x_ref, z_ref, b_ref, c_ref, dt_ref, st_ref, dy_ref, p1, p2, p3, p4 = refs[:11]
        dx_ref, db_ref, dc_ref, dz_ref, ddt_ref, d1, d2, d3, d4 = refs[11 + h.n_in:20 + h.n_in]
        ds_scr = refs[20 + h.n_in + h.n_out]
        g, c = pl.program_id(0), pl.program_id(1)
        h.at_start(refs, 11, 20 + h.n_in, (g == 0) & (c == 0))

        @pl.when(c == 0)
        def _():
            ds_scr[...] = jnp.zeros(ds_scr.shape, f32)
            d4[...] = jnp.zeros(d4.shape, f32)

        @pl.when((c == 0) & (g == 0))
        def _():
            for r in (d1, d2, d3):
                r[...] = jnp.zeros(r.shape, f32)

        fn = functools.partial(f_ssd, g * GH)
        _, vjp = jax.vjp(fn, x_ref[...], z_ref[...], b_ref[...], c_ref[...], dt_ref[...], st_ref[0, 0],
                         p1[...], p2[...], p3[...], p4[...])
        dx, dz, db, dc, ddt, ds_in, e1, e2, e3, e4 = vjp((dy_ref[...].astype(f32), ds_scr[...]))
        dx_ref[...] = dx
        dz_ref[...] = dz.astype(dz_ref.dtype)
        db_ref[...] = db
        dc_ref[...] = dc
        ddt_ref[0] = ddt
        ds_scr[...] = ds_in
        d1[...] += e1
        d2[...] += e2
        d3[...] += e3
        d4[...] += e4
        h.at_end(refs, 11, 20 + h.n_in, (g == SSD_GROUPS - 1) & (c == nc - 1))

    res = pl.pallas_call(
        body, name=name, grid=(SSD_GROUPS, nc),
        in_specs=_ssd_in_specs(rev) + [
            pl.BlockSpec((1, 1, SSD_N, GW), lambda g, c: (g, rev(c), 0, 0)),
            pl.BlockSpec((CHUNK, GW), lambda g, c: (rev(c), g)),
        ] + _SSD_PARAM_SPECS + h.in_specs,
        out_specs=[
            pl.BlockSpec((CHUNK, GW), lambda g, c: (rev(c), g)),
            pl.BlockSpec((CHUNK, SSD_N), lambda g, c: (rev(c), g)),
            pl.BlockSpec((CHUNK, SSD_N), lambda g, c: (rev(c), g)),
            pl.BlockSpec((CHUNK, GW), lambda g, c: (rev(c), g)),
            pl.BlockSpec((1, CHUNK, 128), lambda g, c: (g, rev(c), 0)),
        ] + _SSD_PARAM_SPECS + h.out_specs,
        out_shape=[
            jax.ShapeDtypeStruct((t_len, SSD_INNER), f32),
            jax.ShapeDtypeStruct((t_len, SSD_GROUPS * SSD_N), f32),
            jax.ShapeDtypeStruct((t_len, SSD_GROUPS * SSD_N), f32),
            jax.ShapeDtypeStruct((t_len, SSD_INNER), bf16),
            jax.ShapeDtypeStruct((SSD_GROUPS, t_len, 128), f32),
            jax.ShapeDtypeStruct((1, 128), f32), jax.ShapeDtypeStruct((1, 128), f32), jax.ShapeDtypeStruct((1, 128), f32),
            jax.ShapeDtypeStruct((1, SSD_INNER), f32),
        ] + h.out_shape,
        scratch_shapes=[pltpu.VMEM((SSD_N, GW), f32)] + h.scratch,
        input_output_aliases=h.aliases(11, 9),
        compiler_params=_params(("arbitrary", "arbitrary")),
    )(xact, proj, xact, xact, proj, states, dy, dt_bias, a_log, dskip, ng, *h.ins)
    return list(res[:9]), list(res[9:])


def _head_norm(t, g):
    return t * lax.rsqrt(jnp.mean(t * t, axis=-1, keepdims=True) + EPS) * g


def f_swa(has_prev, q, kp, kc, vp, vc, qg, kg, sinks, *bias):
    rep = ATTN_HEADS // ATTN_KV
    qi = lax.broadcasted_iota(jnp.int32, (rep * BLK, BLK), 0) & (BLK - 1)
    ki = lax.broadcasted_iota(jnp.int32, (rep * BLK, BLK), 1)
    mask_p = (ki > qi) & has_prev
    mask_c = ki <= qi
    lane1 = lax.broadcasted_iota(jnp.int32, (1, 128), 1)
    scale = ATTN_D ** -0.5
    outs = []
    for g in range(ATTN_KV):
        sl = slice(g * ATTN_D, (g + 1) * ATTN_D)
        kpn, kcn = _head_norm(kp[:, sl], kg), _head_norm(kc[:, sl], kg)
        heads = range(g * rep, (g + 1) * rep)
        qn = _head_norm(jnp.concatenate([q[:, h * ATTN_D:(h + 1) * ATTN_D] for h in heads], axis=0), qg)
        lp = jnp.where(mask_p, dot_nt(qn, kpn) * scale + bias[g][:, :BLK], NEG_INF)
        lc = jnp.where(mask_c, dot_nt(qn, kcn) * scale + bias[g][:, BLK:], NEG_INF)
        sink = jnp.concatenate([jnp.broadcast_to(jnp.sum(jnp.where(lane1 == h, sinks, 0.0), axis=1, keepdims=True), (BLK, 1))
                                for h in heads], axis=0)
        m = lax.stop_gradient(jnp.maximum(jnp.maximum(jnp.max(lp, axis=-1, keepdims=True), jnp.max(lc, axis=-1, keepdims=True)), sink))
        pp, pc = jnp.exp(lp - m), jnp.exp(lc - m)
        den = jnp.sum(pp, axis=-1, keepdims=True) + jnp.sum(pc, axis=-1, keepdims=True) + jnp.exp(sink - m)
        o = (dot_nn(pp, vp[:, sl]) + dot_nn(pc, vc[:, sl])) * (1.0 / den)
        outs += [o[r * BLK:(r + 1) * BLK] for r in range(rep)]
    return jnp.concatenate(outs, axis=1)


_SWA_BIAS_SHAPE = (ATTN_KV, ATTN_HEADS // ATTN_KV * BLK, 2 * BLK)
_SWA_BIAS_SPEC = pl.BlockSpec(_SWA_BIAS_SHAPE, lambda i: (0, 0, 0))


def _swa_in_specs():
    prev = lambda i: jnp.maximum(i - 1, 0)
    kw = ATTN_KV * ATTN_D
    return [
        pl.BlockSpec((BLK, D), lambda i: (i, C_Q // D)),
        pl.BlockSpec((BLK, kw), lambda i: (prev(i), C_K // kw)),
        pl.BlockSpec((BLK, kw), lambda i: (i, C_K // kw)),
        pl.BlockSpec((BLK, kw), lambda i: (prev(i), C_V // kw)),
        pl.BlockSpec((BLK, kw), lambda i: (i, C_V // kw)),
        pl.BlockSpec((1, ATTN_D), lambda i: (0, 0)),
        pl.BlockSpec((1, ATTN_D), lambda i: (0, 0)),
        pl.BlockSpec((1, 128), lambda i: (0, 0)),
        _SWA_BIAS_SPEC,
    ]


def swa_fwd(name, proj, qg, kg, sinks, bias, ex=None):
    t_len = proj.shape[0]
    nb = t_len // BLK
    h = _hosted(ex)

    def body(*refs):
        q_ref, kp_ref, kc_ref, vp_ref, vc_ref, qg_ref, kg_ref, s_ref, b_ref = refs[:9]
        o_ref = refs[9 + h.n_in]
        i = pl.program_id(0)
        h.at_start(refs, 9, 10 + h.n_in, i == 0)
        o = f_swa(i > 0, q_ref[...], kp_ref[...], kc_ref[...], vp_ref[...], vc_ref[...], qg_ref[...], kg_ref[...],
                  s_ref[...], *[b_ref[kv] for kv in range(ATTN_KV)])
        o_ref[...] = o.astype(o_ref.dtype)
        h.at_end(refs, 9, 10 + h.n_in, i == nb - 1)

    res = pl.pallas_call(
        body, name=name, grid=(nb,), in_specs=_swa_in_specs() + h.in_specs,
        out_specs=[pl.BlockSpec((BLK, D), lambda i: (i, 0))] + h.out_specs,
        out_shape=[jax.ShapeDtypeStruct((t_len, D), bf16)] + h.out_shape,
        scratch_shapes=h.scratch, input_output_aliases=h.aliases(9, 1),
        compiler_params=_params(("arbitrary",)),
    )(proj, proj, proj, proj, proj, qg, kg, sinks, bias.reshape(_SWA_BIAS_SHAPE), *h.ins)
    return res[0], list(res[1:])


def swa_bwd(name, proj, do, qg, kg, sinks, bias, ex=None):
    t_len = proj.shape[0]
    nb = t_len // BLK
    kw = ATTN_KV * ATTN_D
    h = _hosted(ex)

    def body(*refs):
        q_ref, kp_ref, kc_ref, vp_ref, vc_ref, qg_ref, kg_ref, s_ref, b_ref, do_ref = refs[:10]
        dq_ref, dkp_ref, dkc_ref, dvp_ref, dvc_ref, dqg_ref, dkg_ref, ds_ref, db_ref = refs[10 + h.n_in:19 + h.n_in]
        i = pl.program_id(0)
        h.at_start(refs, 10, 19 + h.n_in, i == 0)

        @pl.when(i == 0)
        def _():
            for r in (dqg_ref, dkg_ref, ds_ref, db_ref):
                r[...] = jnp.zeros(r.shape, f32)

        fn = functools.partial(f_swa, i > 0)
        _, vjp = jax.vjp(fn, q_ref[...], kp_ref[...], kc_ref[...], vp_ref[...], vc_ref[...], qg_ref[...], kg_ref[...],
                         s_ref[...], *[b_ref[kv] for kv in range(ATTN_KV)])
        ds = vjp(do_ref[...].astype(f32))
        dq_ref[...] = ds[0].astype(dq_ref.dtype)
        dkp_ref[...] = ds[1]
        dkc_ref[...] = ds[2]
        dvp_ref[...] = ds[3]
        dvc_ref[...] = ds[4]
        dqg_ref[...] += ds[5]
        dkg_ref[...] += ds[6]
        ds_ref[...] += ds[7]
        for kv in range(ATTN_KV):
            db_ref[kv] += ds[8 + kv]
        h.at_end(refs, 10, 19 + h.n_in, i == nb - 1)

    row = lambda w: pl.BlockSpec((BLK, w), lambda i: (i, 0))
    res = pl.pallas_call(
        body, name=name, grid=(nb,), in_specs=_swa_in_specs() + [row(D)] + h.in_specs,
        out_specs=[row(D), row(kw), row(kw), row(kw), row(kw),
                   pl.BlockSpec((1, ATTN_D), lambda i: (0, 0)), pl.BlockSpec((1, ATTN_D), lambda i: (0, 0)),
                   pl.BlockSpec((1, 128), lambda i: (0, 0)), _SWA_BIAS_SPEC] + h.out_specs,
        out_shape=[jax.ShapeDtypeStruct((t_len, D), bf16)] + [jax.ShapeDtypeStruct((t_len, kw), f32)] * 4
        + [jax.ShapeDtypeStruct((1, ATTN_D), f32)] * 2 + [jax.ShapeDtypeStruct((1, 128), f32),
                                                          jax.ShapeDtypeStruct(_SWA_BIAS_SHAPE, f32)] + h.out_shape,
        scratch_shapes=h.scratch, input_output_aliases=h.aliases(10, 9),
        compiler_params=_params(("arbitrary",)),
    )(proj, proj, proj, proj, proj, qg, kg, sinks, bias.reshape(_SWA_BIAS_SHAPE), do, *h.ins)
    return list(res[:8]) + [res[8].reshape(ATTN_HEADS, BLK, 2 * BLK)], list(res[9:])


def _bucket_table():
    qi = np.arange(BLK)[:, None] + BLK
    kj = np.arange(2 * BLK)[None, :]
    dist = qi - kj
    max_exact = REL_BUCKETS // 2
    d = np.maximum(dist, 1).astype(np.float32)
    large = max_exact + (np.log(d / max_exact) / math.log(128 / max_exact) * (REL_BUCKETS - max_exact)).astype(np.int32)
    large = np.minimum(large, REL_BUCKETS - 1)
    return np.where(dist < max_exact, np.maximum(dist, 0), large).astype(np.int32)


def relbias_fwd(table, bucket):
    def body(t_ref, bk_ref, o_ref):
        bk = bk_ref[...]
        for h in range(ATTN_HEADS):
            acc = jnp.zeros((BLK, 2 * BLK), f32)
            for b in range(REL_BUCKETS):
                acc = jnp.where(bk == b, t_ref[b, h], acc)
            o_ref[h] = acc

    return pl.pallas_call(
        body, name="relbias_fwd", out_shape=jax.ShapeDtypeStruct((ATTN_HEADS, BLK, 2 * BLK), f32),
        in_specs=[pl.BlockSpec(memory_space=pltpu.SMEM), pl.BlockSpec(memory_space=pltpu.VMEM)],
        out_specs=pl.BlockSpec(memory_space=pltpu.VMEM),
    )(table, bucket)


def relbias_bwd(dbias, bucket):
    n = len(dbias)

    def body(*refs):
        bk = refs[n][...]
        o_ref = refs[n + 1]
        row = lax.broadcasted_iota(jnp.int32, (REL_BUCKETS, 128), 0)
        lane = lax.broadcasted_iota(jnp.int32, (REL_BUCKETS, 128), 1)
        res = jnp.zeros((REL_BUCKETS, 128), f32)
        for h in range(ATTN_HEADS):
            tot = refs[0][h]
            for k in range(1, n):
                tot = tot + refs[k][h]
            for b in range(REL_BUCKETS):
                part = jnp.sum(jnp.sum(jnp.where(bk == b, tot, 0.0), axis=1, keepdims=True), axis=0, keepdims=True)
                res = jnp.where((row == b) & (lane == h), part, res)
        o_ref[...] = res

    return pl.pallas_call(
        body, name="relbias_bwd", out_shape=jax.ShapeDtypeStruct((REL_BUCKETS, 128), f32),
        in_specs=[pl.BlockSpec(memory_space=pltpu.VMEM)] * (n + 1),
        out_specs=pl.BlockSpec(memory_space=pltpu.VMEM),
    )(*dbias, bucket)[:, :ATTN_HEADS]


XA_TILE = 512


def f_xattn(q, k, v, qg, kg):
    qn, kn = _head_norm(q, qg), _head_norm(k, kg)
    logits = dot_nt(qn, kn) * (XD ** -0.5)
    p = jnp.exp(logits - lax.stop_gradient(jnp.max(logits, axis=-1, keepdims=True)))
    return dot_nn(p * (1.0 / jnp.sum(p, axis=-1, keepdims=True)), v)


def _xa_in_specs(tq):
    return [
        pl.BlockSpec((tq, XD), lambda h, i: (i, h)),
        pl.BlockSpec((MEM_LEN, XD), lambda h, i: (0, h)),
        pl.BlockSpec((MEM_LEN, XD), lambda h, i: (0, XH + h)),
        pl.BlockSpec((1, XD), lambda h, i: (0, 0)),
        pl.BlockSpec((1, XD), lambda h, i: (0, 0)),
    ]


def xa_fwd(name, xq, kv, qg, kg):
    t_len = xq.shape[0]
    tq = min(XA_TILE, t_len)

    def body(q_ref, k_ref, v_ref, qg_ref, kg_ref, o_ref):
        o_ref[...] = f_xattn(q_ref[...], k_ref[...], v_ref[...], qg_ref[...], kg_ref[...]).astype(o_ref.dtype)

    return pl.pallas_call(
        body, name=name, grid=(XH, t_len // tq), in_specs=_xa_in_specs(tq),
        out_specs=pl.BlockSpec((tq, XD), lambda h, i: (i, h)), out_shape=jax.ShapeDtypeStruct((t_len, D), bf16),
        compiler_params=_params(("parallel", "parallel")),
    )(xq, kv, kv, qg, kg)


def xa_bwd(name, xq, kv, do, qg, kg):
    t_len = xq.shape[0]
    tq = min(XA_TILE, t_len)

    def body(q_ref, k_ref, v_ref, qg_ref, kg_ref, do_ref, dq_ref, dk_ref, dv_ref, dqg_ref, dkg_ref):
        h, i = pl.program_id(0), pl.program_id(1)

        @pl.when(i == 0)
        def _():
            dk_ref[...] = jnp.zeros(dk_ref.shape, f32)
            dv_ref[...] = jnp.zeros(dv_ref.shape, f32)

        @pl.when((i == 0) & (h == 0))
        def _():
            dqg_ref[...] = jnp.zeros(dqg_ref.shape, f32)
            dkg_ref[...] = jnp.zeros(dkg_ref.shape, f32)

        _, vjp = jax.vjp(f_xattn, q_ref[...], k_ref[...], v_ref[...], qg_ref[...], kg_ref[...])
        dq, dk, dv, dqg, dkg = vjp(do_ref[...].astype(f32))
        dq_ref[...] = dq.astype(dq_ref.dtype)
        dk_ref[...] += dk
        dv_ref[...] += dv
        dqg_ref[...] += dqg
        dkg_ref[...] += dkg

    return pl.pallas_call(
        body, name=name, grid=(XH, t_len // tq),
        in_specs=_xa_in_specs(tq) + [pl.BlockSpec((tq, XD), lambda h, i: (i, h))],
        out_specs=[pl.BlockSpec((tq, XD), lambda h, i: (i, h)),
                   pl.BlockSpec((MEM_LEN, XD), lambda h, i: (0, h)), pl.BlockSpec((MEM_LEN, XD), lambda h, i: (0, h)),
                   pl.BlockSpec((1, XD), lambda h, i: (0, 0)), pl.BlockSpec((1, XD), lambda h, i: (0, 0))],
        out_shape=[jax.ShapeDtypeStruct((t_len, D), bf16), jax.ShapeDtypeStruct((MEM_LEN, D), f32),
                   jax.ShapeDtypeStruct((MEM_LEN, D), f32), jax.ShapeDtypeStruct((1, XD), f32), jax.ShapeDtypeStruct((1, XD), f32)],
        compiler_params=_params(("arbitrary", "arbitrary")),
    )(xq, kv, kv, qg, kg, do)


def loss_head(y, target, tile):
    t_len = y.shape[0]

    def body(y_ref, t_ref, dy_ref, l_ref):
        i = pl.program_id(0)

        @pl.when(i == 0)
        def _():
            l_ref[...] = jnp.zeros(l_ref.shape, f32)

        err = y_ref[...] - t_ref[...]
        dy_ref[...] = err * (1.0 / D)
        l_ref[...] += 0.5 * jnp.sum(jnp.sum(err * err, axis=1, keepdims=True), axis=0, keepdims=True) * (1.0 / D)

    row = pl.BlockSpec((tile, D), lambda i: (i, 0))
    return pl.pallas_call(
        body, name="loss_head", grid=(t_len // tile,), in_specs=[row, row],
        out_specs=[row, pl.BlockSpec((8, 128), lambda i: (0, 0))],
        out_shape=[jax.ShapeDtypeStruct((t_len, D), f32), jax.ShapeDtypeStruct((8, 128), f32)],
        compiler_params=_params(("arbitrary",)),
    )(y, target)


ADAM_BLOCK_ELEMS = 512 * 1024


def adamw(name, w, g, m, v):
    rows, cols = w.shape
    tile = rows
    if rows * cols > ADAM_BLOCK_ELEMS:
        tile = _pick(rows, [t for t in (512, 256, 128, 64, 32, 16, 8) if t * cols <= ADAM_BLOCK_ELEMS])

    def body(w_ref, g_ref, m_ref, v_ref, d_ref, nm_ref, nv_ref):
        gg = g_ref[...]
        nm = ADAM_B1 * m_ref[...] + (1.0 - ADAM_B1) * gg
        nv = ADAM_B2 * v_ref[...] + (1.0 - ADAM_B2) * jnp.square(gg)
        m_hat = nm / (1.0 - ADAM_B1 ** ADAM_STEP)
        v_hat = nv / (1.0 - ADAM_B2 ** ADAM_STEP)
        d_ref[...] = -ADAM_LR * (m_hat / (jnp.sqrt(v_hat) + ADAM_EPS) + ADAM_WD * w_ref[...])
        nm_ref[...] = nm
        nv_ref[...] = nv

    spec = pl.BlockSpec((tile, cols), lambda i: (i, 0))
    return pl.pallas_call(
        body, name=name, grid=(rows // tile,), in_specs=[spec] * 4, out_specs=[spec] * 3,
        out_shape=[jax.ShapeDtypeStruct((rows, cols), f32)] * 3, compiler_params=_params(("parallel",)),
    )(w, g, m, v)


def adamw_layers(name, w, m, v, gsrcs, r0):
    depth, rows, cols = w.shape
    tile = _pick(rows, [t for t in (512, 256, 128, 64, 32, 16, 8) if t * cols <= ADAM_BLOCK_ELEMS and r0 % t == 0])

    def body(*refs):
        w_ref, m_ref, v_ref = refs[:3]
        g_refs = refs[3:3 + depth]
        go_ref, d_ref, nm_ref, nv_ref = refs[3 + depth:]
        layer = pl.program_id(0)
        gg = g_refs[0][...]
        for k in range(1, depth):
            gg = jnp.where(layer == k, g_refs[k][...], gg)
        nm = ADAM_B1 * m_ref[0] + (1.0 - ADAM_B1) * gg
        nv = ADAM_B2 * v_ref[0] + (1.0 - ADAM_B2) * jnp.square(gg)
        m_hat = nm / (1.0 - ADAM_B1 ** ADAM_STEP)
        v_hat = nv / (1.0 - ADAM_B2 ** ADAM_STEP)
        go_ref[0] = gg
        d_ref[0] = -ADAM_LR * (m_hat / (jnp.sqrt(v_hat) + ADAM_EPS) + ADAM_WD * w_ref[0])
        nm_ref[0] = nm
        nv_ref[0] = nv

    spec = pl.BlockSpec((1, tile, cols), lambda l, i: (l, i, 0))
    g_specs = [pl.BlockSpec((tile, cols), functools.partial(lambda l, i, k: (jnp.where(l == k, r0 // tile + i, 0), 0), k=k))
               for k in range(depth)]
    return pl.pallas_call(
        body, name=name, grid=(depth, rows // tile), in_specs=[spec] * 3 + g_specs, out_specs=[spec] * 4,
        out_shape=[jax.ShapeDtypeStruct(w.shape, f32)] * 4, compiler_params=_params(("parallel", "parallel")),
    )(w, m, v, *gsrcs)


MESH = pl.DeviceIdType.MESH
HBM_SPEC = pl.BlockSpec(memory_space=pltpu.HBM)


def _place():
    x, y, c = lax.axis_index("x"), lax.axis_index("y"), lax.axis_index("c")
    chips = [(1 - x, y), (x, 1 - y), (1 - x, 1 - y)]
    return x, y, c, chips


def _rcopy(src, dst, send_sems, recv_sems, k, to):
    return pltpu.make_async_remote_copy(src_ref=src, dst_ref=dst, send_sem=send_sems.at[k], recv_sem=recv_sems.at[k],
                                        device_id=to, device_id_type=MESH)


class Exchange:
    def __init__(self, ins, outs, n_sems, n_local, start, wait, aliases=None):
        self.ins, self.outs, self.n_sems, self.n_local = list(ins), list(outs), n_sems, n_local
        self.start, self.wait, self.aliases = start, wait, dict(aliases or {})


class _hosted:
    def __init__(self, ex):
        self.ex = ex
        self.ins = list(ex.ins) if ex else []
        self.out_shape = list(ex.outs) if ex else []
        self.n_in, self.n_out = len(self.ins), len(self.out_shape)
        self.in_specs, self.out_specs = [HBM_SPEC] * self.n_in, [HBM_SPEC] * self.n_out
        self.scratch = [pltpu.SemaphoreType.DMA((ex.n_sems,)), pltpu.SemaphoreType.DMA((ex.n_sems,)),
                        pltpu.SemaphoreType.DMA((max(ex.n_local, 1),))] if ex else []

    def aliases(self, first_in, first_out):
        return {first_in + a: first_out + b for a, b in self.ex.aliases.items()} if self.ex else {}

    def _args(self, refs, i0, o0):
        return refs[i0:i0 + self.n_in], refs[o0:o0 + self.n_out], refs[-3], refs[-2], refs[-1]

    def at_start(self, refs, i0, o0, pred):
        if self.ex is not None:
            pl.when(pred)(lambda: self.ex.start(*self._args(refs, i0, o0)))

    def at_end(self, refs, i0, o0, pred):
        if self.ex is not None:
            pl.when(pred)(lambda: self.ex.wait(*self._args(refs, i0, o0)))


def run_exchange(name, ex):
    h = _hosted(ex)

    def body(*refs):
        args = h._args(refs, 0, h.n_in)
        ex.start(*args)
        ex.wait(*args)

    res = pl.pallas_call(
        body, name=name, out_shape=h.out_shape, in_specs=h.in_specs, out_specs=h.out_specs, scratch_shapes=h.scratch,
        input_output_aliases=h.aliases(0, 0), compiler_params=pltpu.CompilerParams(has_side_effects=True),
    )(*h.ins)
    return list(res)


def _halves(ref, axis, c):
    hh = ref.shape[axis] // 2
    return pl.ds(pl.multiple_of(c * hh, 16), hh), pl.ds(pl.multiple_of((1 - c) * hh, 16), hh)


def ex_gather_ici(shards):
    n = len(shards)

    def copies(ins, outs, ssem, rsem, landing):
        x, y, c, chips = _place()
        me = 2 * x + y
        res = []
        for a, (s, o) in enumerate(zip(ins, outs)):
            mine, _ = _halves(s, 0, c)
            for j, (cx, cy) in enumerate(chips):
                slot = 2 * cx + cy if landing else me
                res.append(_rcopy(s.at[mine], o.at[slot, mine], ssem, rsem, 4 * a + j, (cx, cy, c)))
            res.append(_rcopy(s, o.at[me], ssem, rsem, 4 * a + 3, (x, y, 1 - c)))
        return res

    def start(ins, outs, ssem, rsem, lsem):
        for cp in copies(ins, outs, ssem, rsem, False):
            cp.start()

    def wait(ins, outs, ssem, rsem, lsem):
        for cp in copies(ins, outs, ssem, rsem, True):
            cp.wait_recv()
        for cp in copies(ins, outs, ssem, rsem, False):
            cp.wait_send()

    outs = [jax.ShapeDtypeStruct((N_CHIPS,) + s.shape, s.dtype) for s in shards]
    return Exchange(shards, outs, 4 * n, 0, start, wait)


def ex_gather_d2d(gathered):
    n = len(gathered)

    def copies(outs, ssem, rsem, landing):
        x, y, c, chips = _place()
        res = []
        for a, o in enumerate(outs):
            mine, theirs = _halves(o, 1, c)
            for j, (cx, cy) in enumerate(chips):
                rows = o.at[2 * cx + cy, theirs if landing else mine]
                res.append(_rcopy(rows, rows, ssem, rsem, 3 * a + j, (x, y, 1 - c)))
        return res

    def start(ins, outs, ssem, rsem, lsem):
        for cp in copies(outs, ssem, rsem, False):
            cp.start()

    def wait(ins, outs, ssem, rsem, lsem):
        for cp in copies(outs, ssem, rsem, True):
            cp.wait_recv()
        for cp in copies(outs, ssem, rsem, False):
            cp.wait_send()

    outs = [jax.ShapeDtypeStruct(g.shape, g.dtype) for g in gathered]
    return Exchange(gathered, outs, 3 * n, 0, start, wait, aliases={a: a for a in range(n)})


def ex_swap_halves(gs):
    n = len(gs)

    def copies(ins, outs, ssem, rsem):
        x, y, c, _ = _place()
        return [_rcopy(g.at[s, 1 - c], o.at[s], ssem, rsem, N_CHIPS * a + s, (x, y, 1 - c))
                for a, (g, o) in enumerate(zip(ins, outs)) for s in range(N_CHIPS)]

    def start(ins, outs, ssem, rsem, lsem):
        for cp in copies(ins, outs, ssem, rsem):
            cp.start()

    def wait(ins, outs, ssem, rsem, lsem):
        for cp in copies(ins, outs, ssem, rsem):
            cp.wait()

    outs = [jax.ShapeDtypeStruct((N_CHIPS,) + g.shape[2:], g.dtype) for g in gs]
    return Exchange(gs, outs, N_CHIPS * n, 0, start, wait)


def ex_scatter(ps):
    n = len(ps)

    def copies(ins, outs, ssem, rsem):
        x, y, c, chips = _place()
        return [_rcopy(p.at[2 * cx + cy], o.at[j], ssem, rsem, 3 * a + j, (cx, cy, c))
                for a, (p, o) in enumerate(zip(ins, outs)) for j, (cx, cy) in enumerate(chips)]

    def start(ins, outs, ssem, rsem, lsem):
        for cp in copies(ins, outs, ssem, rsem):
            cp.start()

    def wait(ins, outs, ssem, rsem, lsem):
        for cp in copies(ins, outs, ssem, rsem):
            cp.wait()

    outs = [jax.ShapeDtypeStruct((3,) + p.shape[1:], p.dtype) for p in ps]
    return Exchange(ps, outs, 3 * n, 0, start, wait)


def ex_share(rs):
    n = len(rs)

    def copies(ins, outs, ssem, rsem):
        x, y, c, _ = _place()
        return [_rcopy(r, o, ssem, rsem, a, (x, y, 1 - c)) for a, (r, o) in enumerate(zip(ins, outs))]

    def start(ins, outs, ssem, rsem, lsem):
        for cp in copies(ins, outs, ssem, rsem):
            cp.start()

    def wait(ins, outs, ssem, rsem, lsem):
        for cp in copies(ins, outs, ssem, rsem):
            cp.wait()

    outs = [jax.ShapeDtypeStruct(r.shape, r.dtype) for r in rs]
    return Exchange(rs, outs, n, 0, start, wait)


ADD_BLOCK_ELEMS = 384 * 1024


def _add_tile(rows, cols):
    return _pick(rows, [t for t in (1920, 960, 512, 384, 256, 128, 64, 32, 16) if t * cols <= ADD_BLOCK_ELEMS])


def add_halves(name, g, got, c_arr):
    _, _, rows, cols = g.shape
    tile = _add_tile(rows, cols)

    def body(c_ref, g_ref, r_ref, o_ref):
        o_ref[...] = (g_ref[0] + r_ref[...]).astype(o_ref.dtype)

    return pl.pallas_call(
        body, name=name,
        grid_spec=pltpu.PrefetchScalarGridSpec(
            num_scalar_prefetch=1, grid=(N_CHIPS, rows // tile),
            in_specs=[pl.BlockSpec((1, 1, tile, cols), lambda s, i, c_ref: (s, c_ref[0], i, 0)),
                      pl.BlockSpec((1, tile, cols), lambda s, i, c_ref: (s, i, 0))],
            out_specs=pl.BlockSpec((1, tile, cols), lambda s, i, c_ref: (s, i, 0))),
        out_shape=jax.ShapeDtypeStruct(got.shape, bf16), compiler_params=_params(("parallel", "parallel")),
    )(c_arr, g, got)


def add_partials(name, p, got, me_arr):
    _, rows, cols = p.shape
    tile = _add_tile(rows, cols)

    def body(me_ref, p_ref, r_ref, o_ref):
        o_ref[...] = ((p_ref[0].astype(f32) + r_ref[0].astype(f32)) + r_ref[1].astype(f32)) + r_ref[2].astype(f32)

    return pl.pallas_call(
        body, name=name,
        grid_spec=pltpu.PrefetchScalarGridSpec(
            num_scalar_prefetch=1, grid=(rows // tile,),
            in_specs=[pl.BlockSpec((1, tile, cols), lambda i, me_ref: (me_ref[0], i, 0)),
                      pl.BlockSpec((3, tile, cols), lambda i, me_ref: (0, i, 0))],
            out_specs=pl.BlockSpec((tile, cols), lambda i, me_ref: (i, 0))),
        out_shape=jax.ShapeDtypeStruct((rows, cols), f32), compiler_params=_params(("parallel",)),
    )(me_arr, p, got)


def join_halves(name, mine, theirs):
    rows, cols = mine.shape
    tile = _add_tile(rows, cols)

    def body(m_ref, t_ref, o_ref):
        south = lax.axis_index("c") == 0
        o_ref[0] = jnp.where(south, m_ref[...], t_ref[...])
        o_ref[1] = jnp.where(south, t_ref[...], m_ref[...])

    spec = pl.BlockSpec((tile, cols), lambda i: (i, 0))
    return pl.pallas_call(
        body, name=name, grid=(rows // tile,), in_specs=[spec, spec],
        out_specs=pl.BlockSpec((2, tile, cols), lambda i: (0, i, 0)),
        out_shape=jax.ShapeDtypeStruct((2, rows, cols), f32), compiler_params=_params(("parallel",)),
    )(mine, theirs)


N_DEV = 8


def allreduce_small(name, buf):
    m_per = buf.shape[0]

    def body(x_ref, out_ref, all_ref, send_sems, recv_sems, local_sem):
        x, y, c, chips = _place()
        me, sibling = (x, y, c), (x, y, 1 - c)

        def rows(px, py, pc):
            return all_ref.at[pl.ds(pl.multiple_of((4 * px + 2 * py + pc) * m_per, 8), m_per), :]

        def copy(k, block, to, src=None):
            return _rcopy(rows(*block) if src is None else src, rows(*block), send_sems, recv_sems, k, to)

        mine = pltpu.make_async_copy(x_ref, rows(*me), local_sem)
        mine.start()
        first = [copy(0, me, sibling, src=x_ref)]
        first += [copy(1 + j, me, (*chip, c), src=x_ref) for j, chip in enumerate(chips)]
        for cp in first:
            cp.start()
        passed = [copy(4 + j, (*chip, c), sibling) for j, chip in enumerate(chips)]
        for j, chip in enumerate(chips):
            copy(1 + j, (*chip, c), me).wait_recv()
            passed[j].start()
        copy(0, sibling, me).wait_recv()
        for j, chip in enumerate(chips):
            copy(4 + j, (*chip, 1 - c), me).wait_recv()
        for cp in first + passed:
            cp.wait_send()
        mine.wait()
        tot = all_ref[pl.ds(0, m_per), :]
        for d in range(1, N_DEV):
            tot = tot + all_ref[pl.ds(d * m_per, m_per), :]
        out_ref[...] = tot

    return pl.pallas_call(
        body, name=name, out_shape=jax.ShapeDtypeStruct((m_per, 128), f32),
        in_specs=[pl.BlockSpec(memory_space=pltpu.VMEM)], out_specs=pl.BlockSpec(memory_space=pltpu.VMEM),
        scratch_shapes=[pltpu.VMEM((N_DEV * m_per, 128), f32), pltpu.SemaphoreType.DMA((7,)), pltpu.SemaphoreType.DMA((7,)),
                        pltpu.SemaphoreType.DMA],
        compiler_params=pltpu.CompilerParams(has_side_effects=True, vmem_limit_bytes=VMEM_LIMIT_BYTES),
    )(buf)


def pack_shards(ws, layer):
    wide = jnp.concatenate([ws[n][layer].astype(bf16) for n in WIDE], axis=0)
    return [ws["w_in"][layer].astype(bf16), wide, ws["w_xkv"][layer].astype(bf16)]


def arrange_w_in(shards):
    parts = []
    for a, b, _ in sorted(ORIG_SEGS, key=lambda seg: seg[2]):
        for s in range(N_CHIPS):
            lo, hi = max(a, s * IN_SHARD), min(b, (s + 1) * IN_SHARD)
            if lo < hi:
                parts.append(shards[s][:, lo - s * IN_SHARD:hi - s * IN_SHARD])
    parts.append(jnp.zeros((shards.shape[1], PCOLS - C_DT - (OFF_Q - OFF_DT)), shards.dtype))
    return jnp.concatenate(parts, axis=1)


def shard_w_in_grad(g):
    out = []
    for s in range(N_CHIPS):
        parts = []
        for a, b, first in ORIG_SEGS:
            lo, hi = max(a, s * IN_SHARD), min(b, (s + 1) * IN_SHARD)
            if lo < hi:
                parts.append(g[:, first + lo - a:first + hi - a])
        out.append(jnp.concatenate(parts, axis=1))
    return jnp.stack(out)


def unpack_gathered(gathered):
    g_in, g_wide, g_xkv = gathered
    out = dict(w_in=arrange_w_in(g_in), w_xkv=jnp.concatenate([g_xkv[s] for s in range(N_CHIPS)], axis=1))
    for n in WIDE:
        piece = g_wide[:, WIDE_OFF[n]:WIDE_OFF[n] + WIDE_ROWS[n], :]
        if n == "w_mlp_up":
            out[n] = jnp.concatenate([piece[s] for s in range(N_CHIPS)], axis=1)
        else:
            out[n] = piece.reshape(N_CHIPS * WIDE_ROWS[n], D)
    return out


def pack_grads(gs):
    cols = lambda g, q: jnp.stack([g[:, s * q:(s + 1) * q] for s in range(N_CHIPS)])
    wide = [cols(gs[n], D) if n == "w_mlp_up" else gs[n].reshape(N_CHIPS, WIDE_ROWS[n], D) for n in WIDE]
    arrs = [shard_w_in_grad(gs["w_in"]), jnp.concatenate(wide, axis=1), cols(gs["w_xkv"], 2048 // N_CHIPS)]
    return [a.reshape(N_CHIPS, 2, a.shape[1] // 2, a.shape[2]) for a in arrs]


ROW_TILE = 256


def _pad128(v):
    return jnp.pad(v.reshape(1, -1), ((0, 0), (0, 128 - v.shape[-1])))


def layer_forward(i, h, mem, w, p, bias, next_shards=None):
    tile = min(ROW_TILE, h.shape[0])
    s = dict(h=h)
    row = lambda a: a.reshape(1, -1)
    s["u"], = tile_fwd(f"rms_mix{i}", f_rms, [(h, D, 0)], [row(p["norm_mix"])], [(D, bf16)], tile)
    s["proj"] = proj = matmul(s["u"], w["w_in"], "nn", f"mm_in{i}")
    s["c1"], = conv_fwd(f"conv_a{i}", proj, C_CONV, C_CONV + D, p["conv_dw_w"], row(p["conv_dw_b"]), True, False)
    s["ca"], = tile_fwd(f"ln_silu{i}", f_ln_silu, [(s["c1"], D, 0)], [row(p["conv_ln_g"]), row(p["conv_ln_b"])], [(D, bf16)], tile)
    s["ya"] = matmul(s["ca"], w["w_conv_out"], "nn", f"mm_conv_out{i}")
    s["xpre"], s["xact"] = conv_fwd(f"conv_b{i}", proj, C_XBC, C_XBC, p["ssd_conv_w"], row(p["ssd_conv_b"]), False, True)
    ssd_p = (_pad128(p["ssd_dt_bias"]), _pad128(p["ssd_A_log"]), _pad128(p["ssd_D"]), row(p["ssd_norm_g"]))
    s["yB"], s["states"], landed = ssd_fwd(f"ssd_fwd{i}", s["xact"], proj, *ssd_p,
                                           ex=ex_gather_ici(next_shards) if next_shards else None)
    s["yb"] = matmul(s["yB"], w["w_ssd_out"], "nn", f"mm_ssd_out{i}")
    swa_p = (row(p["attn_q_norm"]), row(p["attn_k_norm"]), _pad128(p["attn_sinks"]))
    s["oc"], gathered = swa_fwd(f"swa_fwd{i}", proj, *swa_p, bias, ex=ex_gather_d2d(landed) if next_shards else None)
    s["yc"] = matmul(s["oc"], w["w_attn_out"], "nn", f"mm_attn_out{i}")
    gate_cols = [(proj, D, C_G // D + k) for k in range(3)]
    s["merged"], = tile_fwd(f"merge{i}", f_merge, gate_cols + [(s["ya"], D, 0), (s["yb"], D, 0), (s["yc"], D, 0)],
                            [row(p["gate_bias"])], [(D, bf16)], tile)
    s["h1"] = h1 = matmul(s["merged"], w["w_mix_out"], "nn", f"mm_mix_out{i}", residual=h)
    s["hx"], = tile_fwd(f"rms_x{i}", f_rms, [(h1, D, 0)], [row(p["norm_xattn"])], [(D, bf16)], tile)
    s["memh"], = tile_fwd(f"rms_mem{i}", f_rms, [(mem, D, 0)], [row(p["norm_mem"])], [(D, bf16)], MEM_LEN)
    s["xq"] = matmul(s["hx"], w["w_xq"], "nn", f"mm_xq{i}")
    s["kv"] = matmul(s["memh"], w["w_xkv"], "nn", f"mm_xkv{i}")
    s["xo"] = xa_fwd(f"xa_fwd{i}", s["xq"], s["kv"], row(p["xattn_q_norm"]), row(p["xattn_k_norm"]))
    s["h2"] = h2 = matmul(s["xo"], w["w_xo"], "nn", f"mm_xo{i}", residual=h1)
    s["um"], = tile_fwd(f"rms_mlp{i}", f_rms, [(h2, D, 0)], [row(p["norm_mlp"])], [(D, bf16)], tile)
    s["up"] = matmul(s["um"], w["w_mlp_up"], "nn", f"mm_up{i}")
    s["act"], = tile_fwd(f"relu2{i}", f_relu2, [(s["up"], MLP, 0)], [], [(MLP, bf16)], tile)
    h3 = matmul(s["act"], w["w_mlp_down"], "nn", f"mm_down{i}", residual=h2)
    return h3, s, gathered


def layer_backward(i, dh3, mem, w, p, bias, s, pending=None, c_arr=None):
    t_len = dh3.shape[0]
    tile = min(ROW_TILE, t_len)
    row = lambda a: a.reshape(1, -1)
    gw, gp = {}, {}
    dact = matmul(dh3, w["w_mlp_down"], "nt", f"mm_down_dx{i}")
    gw["w_mlp_down"] = matmul(s["act"], dh3, "tn", f"mm_down_dw{i}")
    (dup,), _ = tile_bwd(f"relu2_bwd{i}", f_relu2, [(s["up"], MLP, 0)], [], [dact], [bf16], tile)
    gw["w_mlp_up"] = matmul(s["um"], dup, "tn", f"mm_up_dw{i}")
    dum = matmul(dup, w["w_mlp_up"], "nt", f"mm_up_dx{i}")
    (dh2,), (g,) = tile_bwd(f"rms_mlp_bwd{i}", f_rms, [(s["h2"], D, 0)], [row(p["norm_mlp"])], [dum], [f32], tile, addend=dh3)
    gp["norm_mlp"] = g
    dxo = matmul(dh2, w["w_xo"], "nt", f"mm_xo_dx{i}")
    gw["w_xo"] = matmul(s["xo"], dh2, "tn", f"mm_xo_dw{i}")
    dxq, dk, dv, gp["xattn_q_norm"], gp["xattn_k_norm"] = xa_bwd(f"xa_bwd{i}", s["xq"], s["kv"], dxo, row(p["xattn_q_norm"]),
                                                                 row(p["xattn_k_norm"]))
    dkv = jnp.concatenate([dk, dv], axis=1)
    gw["w_xq"] = matmul(s["hx"], dxq, "tn", f"mm_xq_dw{i}")
    dhx = matmul(dxq, w["w_xq"], "nt", f"mm_xq_dx{i}")
    gw["w_xkv"] = matmul(s["memh"], dkv, "tn", f"mm_xkv_dw{i}")
    dmemh = matmul(dkv, w["w_xkv"], "nt", f"mm_xkv_dx{i}")
    _, (g,) = tile_bwd(f"rms_mem_bwd{i}", f_rms, [(mem, D, 0)], [row(p["norm_mem"])], [dmemh], [f32], MEM_LEN)
    gp["norm_mem"] = g
    (dh1,), (g,) = tile_bwd(f"rms_x_bwd{i}", f_rms, [(s["h1"], D, 0)], [row(p["norm_xattn"])], [dhx], [f32], tile, addend=dh2)
    gp["norm_xattn"] = g
    proj = s["proj"]
    dmerged = matmul(dh1, w["w_mix_out"], "nt", f"mm_mix_out_dx{i}")
    gw["w_mix_out"] = matmul(s["merged"], dh1, "tn", f"mm_mix_out_dw{i}")
    gate_cols = [(proj, D, C_G // D + k) for k in range(3)]
    (dpa, dpb, dpc, dya, dyb, dyc), (g,) = tile_bwd(
        f"merge_bwd{i}", f_merge, gate_cols + [(s["ya"], D, 0), (s["yb"], D, 0), (s["yc"], D, 0)], [row(p["gate_bias"])],
        [dmerged], [bf16] * 6, tile)
    gp["gate_bias"] = g
    dca = matmul(dya, w["w_conv_out"], "nt", f"mm_conv_out_dx{i}")
    gw["w_conv_out"] = matmul(s["ca"], dya, "tn", f"mm_conv_out_dw{i}")
    (dc1,), (gp["conv_ln_g"], gp["conv_ln_b"]) = tile_bwd(
        f"ln_silu_bwd{i}", f_ln_silu, [(s["c1"], D, 0)], [row(p["conv_ln_g"]), row(p["conv_ln_b"])], [dca], [f32], tile)
    da, dgate, gp["conv_dw_w"], gp["conv_dw_b"] = conv_bwd(f"conv_a_bwd{i}", dc1, None, proj, C_CONV, C_CONV + D,
                                                           p["conv_dw_w"], True, False, bf16)
    doc = matmul(dyc, w["w_attn_out"], "nt", f"mm_attn_out_dx{i}")
    gw["w_attn_out"] = matmul(s["oc"], dyc, "tn", f"mm_attn_out_dw{i}")
    swa_p = (row(p["attn_q_norm"]), row(p["attn_k_norm"]), _pad128(p["attn_sinks"]))
    (dq, dkp, dkc, dvp, dvc, gp["attn_q_norm"], gp["attn_k_norm"], g, dbias), swapped = swa_bwd(
        f"swa_bwd{i}", proj, doc, *swa_p, bias, ex=ex_swap_halves(pending) if pending else None)
    gp["attn_sinks"] = g[:, :ATTN_HEADS]
    shift = lambda a: jnp.concatenate([a[BLK:], jnp.zeros((BLK, a.shape[1]), a.dtype)], axis=0)
    dk_ = (dkc + shift(dkp)).astype(bf16)
    dv_ = (dvc + shift(dvp)).astype(bf16)
    partials = [add_halves(f"add_halves{i + 1}_{k}", g_, got, c_arr) for k, (g_, got) in enumerate(zip(pending, swapped))] if pending else None
    dyB = matmul(dyb, w["w_ssd_out"], "nt", f"mm_ssd_out_dx{i}")
    gw["w_ssd_out"] = matmul(s["yB"], dyb, "tn", f"mm_ssd_out_dw{i}")
    ssd_p = (_pad128(p["ssd_dt_bias"]), _pad128(p["ssd_A_log"]), _pad128(p["ssd_D"]), row(p["ssd_norm_g"]))
    (dxs, dbm, dcm, dz, ddt4, g1, g2, g3, g4), arrived = ssd_bwd(f"ssd_bwd{i}", s["xact"], proj, s["states"], dyB, *ssd_p,
                                                                  ex=ex_scatter(partials) if pending else None)
    gp["ssd_dt_bias"], gp["ssd_A_log"], gp["ssd_D"], gp["ssd_norm_g"] = g1[:, :SSD_HEADS], g2[:, :SSD_HEADS], g3[:, :SSD_HEADS], g4
    ddt = (ddt4[0] + ddt4[1] + ddt4[2] + ddt4[3]).astype(bf16)
    dxact = jnp.concatenate([dxs, dbm, dcm], axis=1)
    dxbc, gp["ssd_conv_w"], gp["ssd_conv_b"] = conv_bwd(f"conv_b_bwd{i}", dxact, s["xpre"], proj, C_XBC, C_XBC, p["ssd_conv_w"],
                                                        False, True, bf16)
    dproj = jnp.concatenate([da, dgate, dz, dxbc, dq, dpa, dpb, dpc, dk_, dv_, ddt, jnp.zeros((t_len, PCOLS - C_DT - 128), bf16)], axis=1)
    gw["w_in"] = matmul(s["u"], dproj, "tn", f"mm_in_dw{i}")
    du = matmul(dproj, w["w_in"], "nt", f"mm_in_dx{i}")
    (dh,), (g,) = tile_bwd(f"rms_mix_bwd{i}", f_rms, [(s["h"], D, 0)], [row(p["norm_mix"])], [du], [f32], tile, addend=dh1)
    gp["norm_mix"] = g
    return dh, gw, gp, dbias, ((partials, arrived) if pending else None)


SMALL = ["rel_table", "norm_mix", "gate_bias", "conv_dw_w", "conv_dw_b", "conv_ln_g", "conv_ln_b", "ssd_conv_w", "ssd_conv_b",
         "ssd_dt_bias", "ssd_A_log", "ssd_D", "ssd_norm_g", "attn_q_norm", "attn_k_norm", "attn_sinks", "norm_xattn", "norm_mem",
         "xattn_q_norm", "xattn_k_norm", "norm_mlp"]
SMALL_SHARDED = dict(gate_bias=D, conv_dw_w=D, ssd_conv_w=SSD_XBC)
ORDER = ["rel_table", "norm_mix", "w_in", "gate_bias", "conv_dw_w", "conv_dw_b", "conv_ln_g", "conv_ln_b", "w_conv_out",
         "ssd_conv_w", "ssd_conv_b", "ssd_dt_bias", "ssd_A_log", "ssd_D", "ssd_norm_g", "w_ssd_out", "attn_q_norm", "attn_k_norm",
         "attn_sinks", "w_attn_out", "w_mix_out", "norm_xattn", "norm_mem", "w_xq", "w_xkv", "xattn_q_norm", "xattn_k_norm",
         "w_xo", "norm_mlp", "w_mlp_up", "w_mlp_down"]


def finish_reduction(i, partials, arrived, me_arr):
    mine = [add_partials(f"add_partials{i}_{k}", p, got, me_arr) for k, (p, got) in enumerate(zip(partials, arrived))]
    theirs = run_exchange(f"share_halves{i}", ex_share(mine))
    both = [join_halves(f"join_halves{i}_{k}", m, t) for k, (m, t) in enumerate(zip(mine, theirs))]
    return [b.reshape(2 * b.shape[1], b.shape[2]) for b in both]


def kernel(x, mem, rel_table, norm_mix, w_in, gate_bias, conv_dw_w, conv_dw_b, conv_ln_g, conv_ln_b, w_conv_out, ssd_conv_w, ssd_conv_b, ssd_dt_bias, ssd_A_log, ssd_D, ssd_norm_g, w_ssd_out, attn_q_norm, attn_k_norm, attn_sinks, w_attn_out, w_mix_out, norm_xattn, norm_mem, w_xq, w_xkv, xattn_q_norm, xattn_k_norm, w_xo, norm_mlp, w_mlp_up, w_mlp_down, loss_target, m_rel_table, m_norm_mix, m_w_in, m_gate_bias, m_conv_dw_w, m_conv_dw_b, m_conv_ln_g, m_conv_ln_b, m_w_conv_out, m_ssd_conv_w, m_ssd_conv_b, m_ssd_dt_bias, m_ssd_A_log, m_ssd_D, m_ssd_norm_g, m_w_ssd_out, m_attn_q_norm, m_attn_k_norm, m_attn_sinks, m_w_attn_out, m_w_mix_out, m_norm_xattn, m_norm_mem, m_w_xq, m_w_xkv, m_xattn_q_norm, m_xattn_k_norm, m_w_xo, m_norm_mlp, m_w_mlp_up, m_w_mlp_down, v_rel_table, v_norm_mix, v_w_in, v_gate_bias, v_conv_dw_w, v_conv_dw_b, v_conv_ln_g, v_conv_ln_b, v_w_conv_out, v_ssd_conv_w, v_ssd_conv_b, v_ssd_dt_bias, v_ssd_A_log, v_ssd_D, v_ssd_norm_g, v_w_ssd_out, v_attn_q_norm, v_attn_k_norm, v_attn_sinks, v_w_attn_out, v_w_mix_out, v_norm_xattn, v_norm_mem, v_w_xq, v_w_xkv, v_xattn_q_norm, v_xattn_k_norm, v_w_xo, v_norm_mlp, v_w_mlp_up, v_w_mlp_down):
    a = dict(locals())
    wts = {n: a[n] for n in ORDER}
    ms = {n: a["m_" + n] for n in ORDER}
    vs = {n: a["v_" + n] for n in ORDER}
    depth = norm_mix.shape[0]
    ax, ay, ac = lax.axis_index("x"), lax.axis_index("y"), lax.axis_index("c")
    chip = 2 * ax + ay

    small = {n: wts[n] for n in SMALL if n != "rel_table" and n not in SMALL_SHARDED}
    gather_buf, spans, r0 = [], {}, 0
    for n, width in SMALL_SHARDED.items():
        blk = wts[n]
        q = width // N_CHIPS
        whole = lax.dynamic_update_slice(jnp.zeros(blk.shape[:-1] + (width,), f32), blk, (0, 0, chip * q))
        flat = whole.reshape(-1, 128)
        spans[n] = (r0, flat.shape[0], whole.shape)
        r0 += flat.shape[0]
        gather_buf.append(flat)
    summed = allreduce_small("gather_small_params", jnp.concatenate(gather_buf, axis=0))
    for n, (s0, nr, shp) in spans.items():
        small[n] = summed[s0:s0 + nr].reshape(shp) * 0.5

    c_arr = jnp.reshape(ac, (1,)).astype(jnp.int32)
    me_arr = jnp.reshape(chip, (1,)).astype(jnp.int32)
    x0, mem0 = x[0], mem[0]
    bucket = jnp.asarray(_bucket_table())
    bias = relbias_fwd(rel_table, bucket)

    gathered = run_exchange("gather_d2d_first", ex_gather_d2d(run_exchange("gather_ici_first", ex_gather_ici(pack_shards(wts, 0)))))
    h, saved, ws = x0, [], []
    for i in range(depth):
        w = unpack_gathered(gathered)
        p = {n: small[n][i] for n in small}
        h, s, gathered = layer_forward(i, h, mem0, w, p, bias, pack_shards(wts, i + 1) if i + 1 < depth else None)
        saved.append(s)
        ws.append((w, p))
    grad_x, loss_tile = loss_head(h, loss_target[0], min(ROW_TILE, h.shape[0]))
    loss = lax.psum(loss_tile[0, 0], ("x", "y", "c"))

    reduced, small_grads, dbiases, pending = [None] * depth, [None] * depth, [], None
    for i in reversed(range(depth)):
        w, p = ws[i]
        grad_x, gw, gp, dbias, travelled = layer_backward(i, grad_x, mem0, w, p, bias, saved[i], pending, c_arr)
        if travelled is not None:
            reduced[i + 1] = finish_reduction(i + 1, *travelled, me_arr)
        small_grads[i] = gp
        dbiases.append(dbias)
        pending = pack_grads(gw)
    swapped = run_exchange("swap_halves_last", ex_swap_halves(pending))
    partials = [add_halves(f"add_halves0_{k}", g_, got, c_arr) for k, (g_, got) in enumerate(zip(pending, swapped))]
    reduced[0] = finish_reduction(0, partials, run_exchange("scatter_last", ex_scatter(partials)), me_arr)
    sg = {n: jnp.stack([small_grads[i][n].reshape(small[n].shape[1:]) for i in range(depth)]) for n in small}
    sg["rel_table"] = relbias_bwd(dbiases, bucket)

    grads, deltas, new_m, new_v = {}, {}, {}, {}
    sources = dict(w_in=(0, 0), w_xkv=(2, 0), **{n: (1, WIDE_OFF[n]) for n in WIDE})
    for n, (k, r0) in sources.items():
        grads[n], deltas[n], new_m[n], new_v[n] = adamw_layers(f"adamw_{n}", wts[n], ms[n], vs[n], [reduced[i][k] for i in range(depth)], r0)

    parts, spans, r0 = [], {}, 0
    for n in SMALL:
        flat = sg[n].reshape(-1)
        nr = -(-flat.shape[0] // 128)
        nr = -(-nr // 8) * 8
        flat = jnp.pad(flat, (0, nr * 128 - flat.shape[0])).reshape(nr, 128)
        spans[n] = (r0, nr, sg[n].shape)
        r0 += nr
        parts.append(flat)
    summed = allreduce_small("allreduce_small_grads", jnp.concatenate(parts, axis=0))
    for n, (s0, nr, shp) in spans.items():
        size = int(np.prod(shp))
        g = summed[s0:s0 + nr].reshape(-1)[:size].reshape(shp)
        if n in SMALL_SHARDED:
            q = SMALL_SHARDED[n] // N_CHIPS
            g = lax.dynamic_slice_in_dim(g, chip * q, q, axis=g.ndim - 1)
        grads[n] = g

    for n in SMALL:
        shp = wts[n].shape
        two = (lambda t: t.reshape(-1, shp[-1]))
        d, nm, nv = adamw(f"adamw_{n}", two(wts[n]), two(grads[n]), two(ms[n]), two(vs[n]))
        deltas[n], new_m[n], new_v[n] = d.reshape(shp), nm.reshape(shp), nv.reshape(shp)

    return (loss, grad_x[None], *[grads[n] for n in ORDER], *[deltas[n] for n in ORDER],
            *[new_m[n] for n in ORDER], *[new_v[n] for n in ORDER])
```

```python
import functools
import math

import numpy as np
import jax
import jax.numpy as jnp
from jax import lax
from jax.experimental import pallas as pl
from jax.experimental.pallas import tpu as pltpu

f32, bf16 = jnp.float32, jnp.bfloat16

D = 1024
EPS = 1e-6
NEG_INF = -1e30
CONV_K = 31
SSD_INNER = 2048
SSD_HEADS = 32
SSD_P = 64
SSD_GROUPS = 4
SSD_N = 128
SSD_K = 4
CHUNK = 128
SSD_XBC = 3072
ATTN_HEADS = 16
ATTN_KV = 4
ATTN_D = 64
BLK = 128
REL_BUCKETS = 32
XH = 4
XD = 256
MLP = 4096
MEM_LEN = 256

OFF_Z, OFF_XBC, OFF_DT, OFF_Q, OFF_K, OFF_V, OFF_GATE, IN_COLS = 2048, 4096, 7168, 7200, 8224, 8480, 8736, 11808
C_CONV, C_Z, C_XBC, C_Q, C_G, C_K, C_V, C_DT, PCOLS = 0, 2048, 4096, 7168, 8192, 11264, 11520, 11776, 12288

ADAM_LR, ADAM_B1, ADAM_B2, ADAM_EPS, ADAM_WD, ADAM_STEP = 0.001, 0.9, 0.999, 1e-08, 0.01, 10

VMEM_LIMIT_BYTES = 56 * 1024 * 1024
N_CHIPS = 4
BIG = ["w_in", "w_conv_out", "w_ssd_out", "w_attn_out", "w_mix_out", "w_xq", "w_xkv", "w_xo", "w_mlp_up", "w_mlp_down"]
WIDE = ["w_conv_out", "w_ssd_out", "w_attn_out", "w_mix_out", "w_xq", "w_xo", "w_mlp_up", "w_mlp_down"]
WIDE_ROWS = dict(w_conv_out=256, w_ssd_out=512, w_attn_out=256, w_mix_out=256, w_xq=256, w_xo=256, w_mlp_up=1024, w_mlp_down=1024)
WIDE_OFF = {n: sum(WIDE_ROWS[m] for m in WIDE[:k]) for k, n in enumerate(WIDE)}
WIDE_TOTAL = sum(WIDE_ROWS.values())
IN_SHARD = IN_COLS // N_CHIPS
ORIG_SEGS = [(0, OFF_DT, 0), (OFF_DT, OFF_Q, C_DT), (OFF_Q, OFF_K, C_Q), (OFF_K, OFF_GATE, C_K), (OFF_GATE, IN_COLS, C_G)]


def _params(sem=None):
    return pltpu.CompilerParams(dimension_semantics=sem, vmem_limit_bytes=VMEM_LIMIT_BYTES)


NN, NT, TN = ((1,), (0,)), ((1,), (1,)), ((0,), (0,))


def _dg(a, b, dims):
    return lax.dot_general(a.astype(bf16), b.astype(bf16), (dims, ((), ())), preferred_element_type=f32)


@jax.custom_vjp
def dot_nn(a, b):
    return _dg(a, b, NN)


dot_nn.defvjp(lambda a, b: (_dg(a, b, NN), (a, b)), lambda r, g: (_dg(g, r[1], NT), _dg(r[0], g, TN)))


@jax.custom_vjp
def dot_nt(a, b):
    return _dg(a, b, NT)


dot_nt.defvjp(lambda a, b: (_dg(a, b, NT), (a, b)), lambda r, g: (_dg(g, r[1], NN), _dg(g, r[0], TN)))


@jax.custom_vjp
def dot_tn(a, b):
    return _dg(a, b, TN)


dot_tn.defvjp(lambda a, b: (_dg(a, b, TN), (a, b)), lambda r, g: (_dg(r[1], g, NT), _dg(r[0], g, NN)))


def _tri(n, upper):
    r = lax.broadcasted_iota(jnp.int32, (n, n), 0)
    c = lax.broadcasted_iota(jnp.int32, (n, n), 1)
    return jnp.where((c >= r) if upper else (r >= c), 1.0, 0.0).astype(f32)


def _hdot(a, b):
    return lax.dot_general(a, b, (NN, ((), ())), preferred_element_type=f32, precision=lax.Precision.HIGHEST)


@jax.custom_vjp
def cumsum_rows(x):
    return _hdot(_tri(x.shape[0], False), x)


cumsum_rows.defvjp(lambda x: (_hdot(_tri(x.shape[0], False), x), None),
                   lambda _, g: (_hdot(_tri(g.shape[0], True), g),))


def _pick(n, prefs):
    for p in prefs:
        if n % p == 0:
            return p
    return n


def matmul(a, b, mode, name, out_dtype=f32, residual=None, relu2=None):
    if mode == "nn":
        (m, k), n = a.shape, b.shape[1]
    elif mode == "nt":
        (m, k), n = a.shape, b.shape[0]
    else:
        (k, m), n = a.shape, b.shape[1]
    tm = _pick(m, (1024, 512, 256))
    tn = _pick(n, (1024, 512, 256, 128))
    tk = _pick(k, (1024, 512))
    nk = k // tk
    dims = dict(nn=NN, nt=NT, tn=TN)[mode]
    a_spec = pl.BlockSpec((tk, tm), lambda i, j, l: (l, i)) if mode == "tn" else pl.BlockSpec((tm, tk), lambda i, j, l: (i, l))
    b_spec = pl.BlockSpec((tn, tk), lambda i, j, l: (j, l)) if mode == "nt" else pl.BlockSpec((tk, tn), lambda i, j, l: (l, j))
    o_spec = pl.BlockSpec((tm, tn), lambda i, j, l: (i, j))
    two_outs = isinstance(relu2, str)
    extra = residual if residual is not None else (None if two_outs or relu2 is None else relu2)
    has_extra = extra is not None
    n_out = 2 if two_outs else 1

    def body(*refs):
        a_ref, b_ref = refs[0], refs[1]
        e_ref = refs[2] if has_extra else None
        o_refs = refs[2 + has_extra:2 + has_extra + n_out]
        acc = refs[2 + has_extra + n_out]
        l = pl.program_id(2)

        @pl.when(l == 0)
        def _():
            acc[...] = jnp.zeros(acc.shape, f32)

        acc[...] += _dg(a_ref[...], b_ref[...], dims)

        @pl.when(l == nk - 1)
        def _():
            r = acc[...]
            if residual is not None:
                r = r + e_ref[...]
            elif has_extra:
                r = r * (2.0 * jnp.maximum(e_ref[...], 0.0))
            o_refs[0][...] = r.astype(o_refs[0].dtype)
            if two_outs:
                o_refs[1][...] = jnp.square(jnp.maximum(r, 0.0)).astype(o_refs[1].dtype)

    ins = [a, b] + ([extra] if has_extra else [])
    shapes = [jax.ShapeDtypeStruct((m, n), out_dtype)] + ([jax.ShapeDtypeStruct((m, n), bf16)] if two_outs else [])
    res = pl.pallas_call(
        body, name=name, grid=(m // tm, n // tn, nk),
        in_specs=[a_spec, b_spec] + ([o_spec] if has_extra else []),
        out_specs=[o_spec] * n_out, out_shape=shapes,
        scratch_shapes=[pltpu.VMEM((tm, tn), f32)],
        compiler_params=_params(("parallel", "parallel", "arbitrary")),
    )(*ins)
    return res if two_outs else res[0]


def _tile_in_specs(tiled, params, tile):
    specs = [pl.BlockSpec((tile, cols), functools.partial(lambda i, cb: (i, cb), cb=cb)) for (_, cols, cb) in tiled]
    specs += [pl.BlockSpec(p.shape, lambda i: (0, 0)) for p in params]
    return specs


def tile_fwd(name, f, tiled, params, outs, tile):
    rows = tiled[0][0].shape[0]
    nt, npar = len(tiled), len(params)

    def body(*refs):
        xs = [r[...].astype(f32) for r in refs[:nt + npar]]
        res = f(*xs)
        for o_ref, o in zip(refs[nt + npar:], res):
            o_ref[...] = o.astype(o_ref.dtype)

    return pl.pallas_call(
        body, name=name, grid=(rows // tile,),
        in_specs=_tile_in_specs(tiled, params, tile),
        out_specs=[pl.BlockSpec((tile, c), lambda i: (i, 0)) for c, _ in outs],
        out_shape=[jax.ShapeDtypeStruct((rows, c), dt) for c, dt in outs],
        compiler_params=_params(("parallel",)),
    )(*[t[0] for t in tiled], *params)


def tile_bwd(name, f, tiled, params, cots, d_dtypes, tile, addend=None):
    rows = tiled[0][0].shape[0]
    nt, npar, nc = len(tiled), len(params), len(cots)
    has_add = addend is not None

    def body(*refs):
        i = pl.program_id(0)
        xs = [r[...].astype(f32) for r in refs[:nt + npar]]
        gs = tuple(r[...].astype(f32) for r in refs[nt + npar:nt + npar + nc])
        pos = nt + npar + nc
        add_ref = refs[pos] if has_add else None
        pos += has_add
        dt_refs, dp_refs = refs[pos:pos + nt], refs[pos + nt:pos + nt + npar]
        _, vjp = jax.vjp(f, *xs)
        ds = vjp(gs)
        for k in range(nt):
            d = ds[k]
            if has_add and k == 0:
                d = d + add_ref[...]
            dt_refs[k][...] = d.astype(dt_refs[k].dtype)

        @pl.when(i == 0)
        def _():
            for r in dp_refs:
                r[...] = jnp.zeros(r.shape, f32)

        for k in range(npar):
            dp_refs[k][...] += ds[nt + k]

    in_specs = _tile_in_specs(tiled, params, tile)
    in_specs += [pl.BlockSpec((tile, c.shape[1]), lambda i: (i, 0)) for c in cots]
    ins = [t[0] for t in tiled] + list(params) + list(cots)
    if has_add:
        in_specs.append(pl.BlockSpec((tile, addend.shape[1]), lambda i: (i, 0)))
        ins.append(addend)
    out_specs = [pl.BlockSpec((tile, cols), lambda i: (i, 0)) for (_, cols, _) in tiled]
    out_specs += [pl.BlockSpec(p.shape, lambda i: (0, 0)) for p in params]
    out_shape = [jax.ShapeDtypeStruct((rows, cols), dt) for (_, cols, _), dt in zip(tiled, d_dtypes)]
    out_shape += [jax.ShapeDtypeStruct(p.shape, f32) for p in params]
    res = pl.pallas_call(
        body, name=name, grid=(rows // tile,), in_specs=in_specs, out_specs=out_specs, out_shape=out_shape,
        compiler_params=_params(("arbitrary",)),
    )(*ins)
    return res[:nt], res[nt:]


def f_rms(h, g):
    return (h * lax.rsqrt(jnp.mean(h * h, axis=-1, keepdims=True) + EPS) * g,)


def f_ln_silu(x, g, b):
    mu = jnp.mean(x, axis=-1, keepdims=True)
    xc = x - mu
    y = xc * lax.rsqrt(jnp.mean(xc * xc, axis=-1, keepdims=True) + EPS) * g + b
    return (y * jax.nn.sigmoid(y),)


def f_merge(pa, pb, pc, ya, yb, yc, gb):
    ga = jax.nn.sigmoid(pa + gb[:, 0:D])
    gb_ = jax.nn.sigmoid(pb + gb[:, D:2 * D])
    gc = jax.nn.sigmoid(pc + gb[:, 2 * D:3 * D])
    return (ga * ya + gb_ * yb + gc * yc,)


HALO = 32
CONV_CB = 256
SUB = 128


def _silu_grad(p):
    s = jax.nn.sigmoid(p)
    return s * (1.0 + p * (1.0 - s))


def _fill_shifted(buf, copies, length):
    for k in range(1, 8):
        for r0 in range(0, length - 8, SUB):
            n = min(SUB, length - 8 - r0)
            copies[k - 1, pl.ds(r0, n), :] = buf[pl.ds(r0 + k, n), :]


def _rows(buf, copies, off, n):
    if copies is None or off % 8 == 0:
        return buf[pl.ds(off, n), :]
    return copies[off % 8 - 1, pl.ds(off - off % 8, n), :]


def conv_fwd(name, src, col0, gate_col0, w, b, glu, silu):
    t_len = src.shape[0]
    k_taps, c = w.shape
    tt = min(512, t_len)
    nt_ = t_len // tt
    cb0, gb0 = col0 // CONV_CB, gate_col0 // CONV_CB
    many_taps = k_taps > 8

    def body(*refs):
        pos = 0
        x_cur, x_prev = refs[0], refs[1]
        pos = 2
        if glu:
            g_cur, g_prev = refs[2], refs[3]
            pos = 4
        w_ref, b_ref = refs[pos], refs[pos + 1]
        pre_ref = refs[pos + 2]
        act_ref = refs[pos + 3] if silu else None
        xp = refs[-2] if many_taps else refs[-1]
        xs = refs[-1] if many_taps else None
        t = pl.program_id(1)
        cur = x_cur[...]
        tail = x_prev[...]
        if glu:
            cur = cur * jax.nn.sigmoid(g_cur[...])
            tail = tail * jax.nn.sigmoid(g_prev[...])
        xp[pl.ds(0, HALO), :] = jnp.where(t > 0, tail, 0.0)
        xp[pl.ds(HALO, tt), :] = cur
        if many_taps:
            _fill_shifted(xp, xs, HALO + tt)
        for s in range(tt // SUB):
            acc = jnp.broadcast_to(b_ref[...], (SUB, CONV_CB))
            for j in range(k_taps):
                acc = acc + _rows(xp, xs, s * SUB + HALO - (k_taps - 1) + j, SUB) * w_ref[pl.ds(j, 1), :]
            pre_ref[pl.ds(s * SUB, SUB), :] = acc
            if silu:
                act_ref[pl.ds(s * SUB, SUB), :] = acc * jax.nn.sigmoid(acc)

    blk = lambda off: pl.BlockSpec((tt, CONV_CB), functools.partial(lambda j, t, off: (t, off + j), off=off))
    prev = lambda off: pl.BlockSpec((HALO, CONV_CB), functools.partial(lambda j, t, off: (jnp.maximum(t * (tt // HALO) - 1, 0), off + j), off=off))
    in_specs, ins = [blk(cb0), prev(cb0)], [src, src]
    if glu:
        in_specs += [blk(gb0), prev(gb0)]
        ins += [src, src]
    in_specs += [pl.BlockSpec((k_taps, CONV_CB), lambda j, t: (0, j)), pl.BlockSpec((1, CONV_CB), lambda j, t: (0, j))]
    ins += [w, b]
    o_spec = pl.BlockSpec((tt, CONV_CB), lambda j, t: (t, j))
    n_out = 2 if silu else 1
    res = pl.pallas_call(
        body, name=name, grid=(c // CONV_CB, nt_), in_specs=in_specs, out_specs=[o_spec] * n_out,
        out_shape=[jax.ShapeDtypeStruct((t_len, c), f32)] * n_out,
        scratch_shapes=[pltpu.VMEM((HALO + tt, CONV_CB), f32)] + ([pltpu.VMEM((7, HALO + tt, CONV_CB), f32)] if many_taps else []),
        compiler_params=_params(("parallel", "arbitrary")),
    )(*ins)
    return res


def conv_bwd(name, dy, pre, src, col0, gate_col0, w, glu, silu, out_dtype):
    t_len = src.shape[0]
    k_taps, c = w.shape
    tt = min(512, t_len)
    nt_ = t_len // tt
    cb0, gb0 = col0 // CONV_CB, gate_col0 // CONV_CB
    many_taps = k_taps > 8

    def body(*refs):
        dy_cur, dy_next = refs[0], refs[1]
        pos = 2
        if silu:
            p_cur, p_next = refs[2], refs[3]
            pos = 4
        x_cur, x_prev = refs[pos], refs[pos + 1]
        pos += 2
        if glu:
            g_cur, g_prev = refs[pos], refs[pos + 1]
            pos += 2
        w_ref = refs[pos]
        pos += 1
        n_dx = 2 if glu else 1
        dx_refs = refs[pos:pos + n_dx]
        dw_ref, db_ref = refs[pos + n_dx], refs[pos + n_dx + 1]
        dp, xp = (refs[-4], refs[-3]) if many_taps else (refs[-2], refs[-1])
        ds_, xs = (refs[-2], refs[-1]) if many_taps else (None, None)
        t = pl.program_id(1)

        dcur = dy_cur[...]
        dhead = dy_next[...]
        if silu:
            dcur = dcur * _silu_grad(p_cur[...])
            dhead = dhead * _silu_grad(p_next[...])
        dp[pl.ds(0, tt), :] = dcur
        dp[pl.ds(tt, HALO), :] = jnp.where(t < nt_ - 1, dhead, 0.0)
        cur = x_cur[...]
        tail = x_prev[...]
        if glu:
            cur = cur * jax.nn.sigmoid(g_cur[...])
            tail = tail * jax.nn.sigmoid(g_prev[...])
        xp[pl.ds(0, HALO), :] = jnp.where(t > 0, tail, 0.0)
        xp[pl.ds(HALO, tt), :] = cur

        if many_taps:
            _fill_shifted(dp, ds_, tt + HALO)
            _fill_shifted(xp, xs, HALO + tt)

        @pl.when(t == 0)
        def _():
            dw_ref[...] = jnp.zeros(dw_ref.shape, f32)
            db_ref[...] = jnp.zeros(db_ref.shape, f32)

        for s in range(tt // SUB):
            d_sub = dp[pl.ds(s * SUB, SUB), :]
            acc = jnp.zeros((SUB, CONV_CB), f32)
            for j in range(k_taps):
                acc = acc + _rows(dp, ds_, s * SUB + (k_taps - 1) - j, SUB) * w_ref[pl.ds(j, 1), :]
                x_sub = _rows(xp, xs, s * SUB + HALO - (k_taps - 1) + j, SUB)
                dw_ref[pl.ds(j, 1), :] += jnp.sum(d_sub * x_sub, axis=0, keepdims=True)
            db_ref[...] += jnp.sum(d_sub, axis=0, keepdims=True)
            if glu:
                a = x_cur[pl.ds(s * SUB, SUB), :]
                sg = jax.nn.sigmoid(g_cur[pl.ds(s * SUB, SUB), :])
                dx_refs[0][pl.ds(s * SUB, SUB), :] = (acc * sg).astype(out_dtype)
                dx_refs[1][pl.ds(s * SUB, SUB), :] = (acc * a * sg * (1.0 - sg)).astype(out_dtype)
            else:
                dx_refs[0][pl.ds(s * SUB, SUB), :] = acc.astype(out_dtype)

    blk = lambda off: pl.BlockSpec((tt, CONV_CB), functools.partial(lambda j, t, off: (t, off + j), off=off))
    prev = lambda off: pl.BlockSpec((HALO, CONV_CB), functools.partial(lambda j, t, off: (jnp.maximum(t * (tt // HALO) - 1, 0), off + j), off=off))
    nxt = pl.BlockSpec((HALO, CONV_CB), lambda j, t: (jnp.minimum((t + 1) * (tt // HALO), t_len // HALO - 1), j))
    in_specs, ins = [blk(0), nxt], [dy, dy]
    if silu:
        in_specs += [blk(0), nxt]
        ins += [pre, pre]
    in_specs += [blk(cb0), prev(cb0)]
    ins += [src, src]
    if glu:
        in_specs += [blk(gb0), prev(gb0)]
        ins += [src, src]
    in_specs.append(pl.BlockSpec((k_taps, CONV_CB), lambda j, t: (0, j)))
    ins.append(w)
    n_dx = 2 if glu else 1
    o_spec = pl.BlockSpec((tt, CONV_CB), lambda j, t: (t, j))
    out_specs = [o_spec] * n_dx + [pl.BlockSpec((k_taps, CONV_CB), lambda j, t: (0, j)), pl.BlockSpec((1, CONV_CB), lambda j, t: (0, j))]
    out_shape = [jax.ShapeDtypeStruct((t_len, c), out_dtype)] * n_dx + [jax.ShapeDtypeStruct((k_taps, c), f32), jax.ShapeDtypeStruct((1, c), f32)]
    return pl.pallas_call(
        body, name=name, grid=(c // CONV_CB, nt_), in_specs=in_specs, out_specs=out_specs, out_shape=out_shape,
        scratch_shapes=[pltpu.VMEM((tt + HALO, CONV_CB), f32)] * 2 + ([pltpu.VMEM((7, tt + HALO, CONV_CB), f32)] * 2 if many_taps else []),
        compiler_params=_params(("parallel", "arbitrary")),
    )(*ins)


GH = SSD_HEADS // SSD_GROUPS
GW = GH * SSD_P


def _softplus(x):
    return jnp.maximum(x, 0.0) + jnp.log1p(jnp.exp(-jnp.abs(x)))


def f_ssd(hbase, x, z, bm, cm, dtraw, s_in, dt_bias, a_log, dskip, ng):
    q = x.shape[0]
    dt = _softplus(dtraw + dt_bias)
    da = dt * (-jnp.exp(a_log))
    cs = cumsum_rows(da)
    g_cb = dot_nt(cm, bm)
    lane = lax.broadcasted_iota(jnp.int32, (q, 128), 1)
    lane1 = lax.broadcasted_iota(jnp.int32, (1, 128), 1)
    causal = lax.broadcasted_iota(jnp.int32, (q, q), 0) >= lax.broadcasted_iota(jnp.int32, (q, q), 1)
    last = lax.broadcasted_iota(jnp.int32, (q, GW), 0) == q - 1
    dt_cols, cs_cols, d_cols = [], [], []
    for r in range(GH):
        sel = lane == hbase + r
        dt_cols.append(jnp.sum(jnp.where(sel, dt, 0.0), axis=1, keepdims=True))
        cs_cols.append(jnp.sum(jnp.where(sel, cs, 0.0), axis=1, keepdims=True))
        d_cols.append(jnp.sum(jnp.where(lane1 == hbase + r, dskip, 0.0), axis=1, keepdims=True))
    spread = lambda cols: jnp.concatenate([jnp.broadcast_to(c, (c.shape[0], SSD_P)) for c in cols], axis=1)
    dt_x, cs_x, d_x = spread(dt_cols), spread(cs_cols), spread(d_cols)
    cs_last = jnp.sum(jnp.where(last, cs_x, 0.0), axis=0, keepdims=True)
    xdt = x * dt_x
    y_diag = []
    for r in range(GH):
        m1 = jnp.broadcast_to(cs_cols[r], (q, q))
        decay = jnp.where(causal, jnp.exp(jnp.where(causal, m1 - m1.T, 0.0)), 0.0)
        y_diag.append(dot_nn(g_cb * decay, xdt[:, r * SSD_P:(r + 1) * SSD_P]))
    s_c = dot_tn(bm, xdt * jnp.exp(cs_last - cs_x))
    y_off = dot_nn(cm, s_in) * jnp.exp(cs_x)
    y = jnp.concatenate(y_diag, axis=1) + y_off + x * d_x
    s_out = s_in * jnp.exp(cs_last) + s_c
    y = y * (z * jax.nn.sigmoid(z))
    y = y * lax.rsqrt(jnp.mean(y * y, axis=-1, keepdims=True) + EPS) * ng
    return y, s_out


GPS = 2
PW = GPS * GW
PN = GPS * SSD_N
NPAIR = SSD_GROUPS // GPS


def _ssd_in_specs(cmap):
    return [
        pl.BlockSpec((CHUNK, PW), lambda g, c: (cmap(c), g)),
        pl.BlockSpec((CHUNK, PW), lambda g, c: (cmap(c), C_Z // PW + g)),
        pl.BlockSpec((CHUNK, PN), lambda g, c: (cmap(c), SSD_INNER // PN + g)),
        pl.BlockSpec((CHUNK, PN), lambda g, c: (cmap(c), (SSD_INNER + SSD_GROUPS * SSD_N) // PN + g)),
        pl.BlockSpec((CHUNK, 128), lambda g, c: (cmap(c), C_DT // 128)),
    ]


_SSD_PARAM_SPECS = [pl.BlockSpec((1, 128), lambda g, c: (0, 0))] * 3 + [pl.BlockSpec((1, PW), lambda g, c: (0, g))]


def _cols(ref, k, width):
    return ref[:, pl.ds(k * width, width)]


def ssd_fwd(name, xact, proj, dt_bias, a_log, dskip, ng, ex=None):
    t_len = xact.shape[0]
    nc = t_len // CHUNK
    h = _hosted(ex)

    def body(*refs):
        x_ref, z_ref, b_ref, c_ref, dt_ref, p1, p2, p3, p4 = refs[:9]
        y_ref, st_ref = refs[9 + h.n_in:11 + h.n_in]
        s_scr = refs[11 + h.n_in + h.n_out]
        g, c = pl.program_id(0), pl.program_id(1)
        h.at_start(refs, 9, 11 + h.n_in, (g == 0) & (c == 0))

        @pl.when(c == 0)
        def _():
            s_scr[...] = jnp.zeros(s_scr.shape, f32)

        for k in range(GPS):
            s_in = s_scr[k]
            st_ref[k, 0] = s_in
            y, s_out = f_ssd((g * GPS + k) * GH, _cols(x_ref, k, GW), _cols(z_ref, k, GW), _cols(b_ref, k, SSD_N),
                             _cols(c_ref, k, SSD_N), dt_ref[...], s_in, p1[...], p2[...], p3[...], _cols(p4, k, GW))
            y_ref[:, pl.ds(k * GW, GW)] = y.astype(y_ref.dtype)
            s_scr[k] = s_out
        h.at_end(refs, 9, 11 + h.n_in, (g == NPAIR - 1) & (c == nc - 1))

    res = pl.pallas_call(
        body, name=name, grid=(NPAIR, nc),
        in_specs=_ssd_in_specs(lambda c: c) + _SSD_PARAM_SPECS + h.in_specs,
        out_specs=[pl.BlockSpec((CHUNK, PW), lambda g, c: (c, g)),
                   pl.BlockSpec((GPS, 1, SSD_N, GW), lambda g, c: (g, c, 0, 0))] + h.out_specs,
        out_shape=[jax.ShapeDtypeStruct((t_len, SSD_INNER), bf16),
                   jax.ShapeDtypeStruct((SSD_GROUPS, nc, SSD_N, GW), f32)] + h.out_shape,
        scratch_shapes=[pltpu.VMEM((GPS, SSD_N, GW), f32)] + h.scratch,
        input_output_aliases=h.aliases(9, 2),
        compiler_params=_params(("arbitrary", "arbitrary")),
    )(xact, proj, xact, xact, proj, dt_bias, a_log, dskip, ng, *h.ins)
    return res[0], res[1], list(res[2:])


def ssd_bwd(name, xact, proj, states, dy, dt_bias, a_log, dskip, ng, ex=None):
    t_len = xact.shape[0]
    nc = t_len // CHUNK
    rev = lambda c: nc - 1 - c
    h = _hosted(ex)

    def body(*refs):
        x_ref, z_ref, b_ref, c_ref, dt_ref, st_ref, dy_ref, p1, p2, p3, p4 = refs[:11]
        dx_ref, db_ref, dc_ref, dz_ref, ddt_ref, d1, d2, d3, d4 = refs[11 + h.n_in:20 + h.n_in]
        ds_scr = refs[20 + h.n_in + h.n_out]
        g, c = pl.program_id(0), pl.program_id(1)
        h.at_start(refs, 11, 20 + h.n_in, (g == 0) & (c == 0))

        @pl.when(c == 0)
        def _():
            ds_scr[...] = jnp.zeros(ds_scr.shape, f32)
            d4[...] = jnp.zeros(d4.shape, f32)

        @pl.when((c == 0) & (g == 0))
        def _():
            for r in (d1, d2, d3):
                r[...] = jnp.zeros(r.shape, f32)

        for k in range(GPS):
            fn = functools.partial(f_ssd, (g * GPS + k) * GH)
            _, vjp = jax.vjp(fn, _cols(x_ref, k, GW), _cols(z_ref, k, GW), _cols(b_ref, k, SSD_N), _cols(c_ref, k, SSD_N),
                             dt_ref[...], st_ref[k, 0], p1[...], p2[...], p3[...], _cols(p4, k, GW))
            dx, dz, db, dc, ddt, ds_in, e1, e2, e3, e4 = vjp((_cols(dy_ref, k, GW).astype(f32), ds_scr[k]))
            dx_ref[:, pl.ds(k * GW, GW)] = dx
            dz_ref[:, pl.ds(k * GW, GW)] = dz.astype(dz_ref.dtype)
            db_ref[:, pl.ds(k * SSD_N, SSD_N)] = db
            dc_ref[:, pl.ds(k * SSD_N, SSD_N)] = dc
            ddt_ref[k] = ddt
            ds_scr[k] = ds_in
            d1[...] += e1
            d2[...] += e2
            d3[...] += e3
            d4[:, pl.ds(k * GW, GW)] += e4
        h.at_end(refs, 11, 20 + h.n_in, (g == NPAIR - 1) & (c == nc - 1))

    res = pl.pallas_call(
        body, name=name, grid=(NPAIR, nc),
        in_specs=_ssd_in_specs(rev) + [
            pl.BlockSpec((GPS, 1, SSD_N, GW), lambda g, c: (g, rev(c), 0, 0)),
            pl.BlockSpec((CHUNK, PW), lambda g, c: (rev(c), g)),
        ] + _SSD_PARAM_SPECS + h.in_specs,
        out_specs=[
            pl.BlockSpec((CHUNK, PW), lambda g, c: (rev(c), g)),
            pl.BlockSpec((CHUNK, PN), lambda g, c: (rev(c), g)),
            pl.BlockSpec((CHUNK, PN), lambda g, c: (rev(c), g)),
            pl.BlockSpec((CHUNK, PW), lambda g, c: (rev(c), g)),
            pl.BlockSpec((GPS, CHUNK, 128), lambda g, c: (g, rev(c), 0)),
        ] + _SSD_PARAM_SPECS + h.out_specs,
        out_shape=[
            jax.ShapeDtypeStruct((t_len, SSD_INNER), f32),
            jax.ShapeDtypeStruct((t_len, SSD_GROUPS * SSD_N), f32),
            jax.ShapeDtypeStruct((t_len, SSD_GROUPS * SSD_N), f32),
            jax.ShapeDtypeStruct((t_len, SSD_INNER), bf16),
            jax.ShapeDtypeStruct((SSD_GROUPS, t_len, 128), f32),
            jax.ShapeDtypeStruct((1, 128), f32), jax.ShapeDtypeStruct((1, 128), f32), jax.ShapeDtypeStruct((1, 128), f32),
            jax.ShapeDtypeStruct((1, SSD_INNER), f32),
        ] + h.out_shape,
        scratch_shapes=[pltpu.VMEM((GPS, SSD_N, GW), f32)] + h.scratch,
        input_output_aliases=h.aliases(11, 9),
        compiler_params=_params(("arbitrary", "arbitrary")),
    )(xact, proj, xact, xact, proj, states, dy, dt_bias, a_log, dskip, ng, *h.ins)
    return list(res[:9]), list(res[9:])


def _head_norm(t, g):
    return t * lax.rsqrt(jnp.mean(t * t, axis=-1, keepdims=True) + EPS) * g


def f_swa(has_prev, q, kp, kc, vp, vc, qg, kg, sinks, *bias):
    rep = ATTN_HEADS // ATTN_KV
    qi = lax.broadcasted_iota(jnp.int32, (rep * BLK, BLK), 0) & (BLK - 1)
    ki = lax.broadcasted_iota(jnp.int32, (rep * BLK, BLK), 1)
    mask_p = (ki > qi) & has_prev
    mask_c = ki <= qi
    lane1 = lax.broadcasted_iota(jnp.int32, (1, 128), 1)
    scale = ATTN_D ** -0.5
    outs = []
    for g in range(ATTN_KV):
        sl = slice(g * ATTN_D, (g + 1) * ATTN_D)
        kpn, kcn = _head_norm(kp[:, sl], kg), _head_norm(kc[:, sl], kg)
        heads = range(g * rep, (g + 1) * rep)
        qn = _head_norm(jnp.concatenate([q[:, h * ATTN_D:(h + 1) * ATTN_D] for h in heads], axis=0), qg)
        lp = jnp.where(mask_p, dot_nt(qn, kpn) * scale + bias[g][:, :BLK], NEG_INF)
        lc = jnp.where(mask_c, dot_nt(qn, kcn) * scale + bias[g][:, BLK:], NEG_INF)
        sink = jnp.concatenate([jnp.broadcast_to(jnp.sum(jnp.where(lane1 == h, sinks, 0.0), axis=1, keepdims=True), (BLK, 1))
                                for h in heads], axis=0)
        m = lax.stop_gradient(jnp.maximum(jnp.maximum(jnp.max(lp, axis=-1, keepdims=True), jnp.max(lc, axis=-1, keepdims=True)), sink))
        pp, pc = jnp.exp(lp - m), jnp.exp(lc - m)
        den = jnp.sum(pp, axis=-1, keepdims=True) + jnp.sum(pc, axis=-1, keepdims=True) + jnp.exp(sink - m)
        o = (dot_nn(pp, vp[:, sl]) + dot_nn(pc, vc[:, sl])) * (1.0 / den)
        outs += [o[r * BLK:(r + 1) * BLK] for r in range(rep)]
    return jnp.concatenate(outs, axis=1)


_SWA_BIAS_SHAPE = (ATTN_KV, ATTN_HEADS // ATTN_KV * BLK, 2 * BLK)
_SWA_BIAS_SPEC = pl.BlockSpec(_SWA_BIAS_SHAPE, lambda i: (0, 0, 0))


def _swa_in_specs():
    prev = lambda i: jnp.maximum(i - 1, 0)
    kw = ATTN_KV * ATTN_D
    return [
        pl.BlockSpec((BLK, D), lambda i: (i, C_Q // D)),
        pl.BlockSpec((BLK, kw), lambda i: (prev(i), C_K // kw)),
        pl.BlockSpec((BLK, kw), lambda i: (i, C_K // kw)),
        pl.BlockSpec((BLK, kw), lambda i: (prev(i), C_V // kw)),
        pl.BlockSpec((BLK, kw), lambda i: (i, C_V // kw)),
        pl.BlockSpec((1, ATTN_D), lambda i: (0, 0)),
        pl.BlockSpec((1, ATTN_D), lambda i: (0, 0)),
        pl.BlockSpec((1, 128), lambda i: (0, 0)),
        _SWA_BIAS_SPEC,
    ]


def swa_fwd(name, proj, qg, kg, sinks, bias, ex=None):
    t_len = proj.shape[0]
    nb = t_len // BLK
    h = _hosted(ex)

    def body(*refs):
        q_ref, kp_ref, kc_ref, vp_ref, vc_ref, qg_ref, kg_ref, s_ref, b_ref = refs[:9]
        o_ref = refs[9 + h.n_in]
        i = pl.program_id(0)
        h.at_start(refs, 9, 10 + h.n_in, i == 0)
        o = f_swa(i > 0, q_ref[...], kp_ref[...], kc_ref[...], vp_ref[...], vc_ref[...], qg_ref[...], kg_ref[...],
                  s_ref[...], *[b_ref[kv] for kv in range(ATTN_KV)])
        o_ref[...] = o.astype(o_ref.dtype)
        h.at_end(refs, 9, 10 + h.n_in, i == nb - 1)

    res = pl.pallas_call(
        body, name=name, grid=(nb,), in_specs=_swa_in_specs() + h.in_specs,
        out_specs=[pl.BlockSpec((BLK, D), lambda i: (i, 0))] + h.out_specs,
        out_shape=[jax.ShapeDtypeStruct((t_len, D), bf16)] + h.out_shape,
        scratch_shapes=h.scratch, input_output_aliases=h.aliases(9, 1),
        compiler_params=_params(("arbitrary",)),
    )(proj, proj, proj, proj, proj, qg, kg, sinks, bias.reshape(_SWA_BIAS_SHAPE), *h.ins)
    return res[0], list(res[1:])


def swa_bwd(name, proj, do, qg, kg, sinks, bias, ex=None):
    t_len = proj.shape[0]
    nb = t_len // BLK
    kw = ATTN_KV * ATTN_D
    h = _hosted(ex)

    def body(*refs):
        q_ref, kp_ref, kc_ref, vp_ref, vc_ref, qg_ref, kg_ref, s_ref, b_ref, do_ref = refs[:10]
        dq_ref, dkp_ref, dkc_ref, dvp_ref, dvc_ref, dqg_ref, dkg_ref, ds_ref, db_ref = refs[10 + h.n_in:19 + h.n_in]
        i = pl.program_id(0)
        h.at_start(refs, 10, 19 + h.n_in, i == 0)

        @pl.when(i == 0)
        def _():
            for r in (dqg_ref, dkg_ref, ds_ref, db_ref):
                r[...] = jnp.zeros(r.shape, f32)

        fn = functools.partial(f_swa, i > 0)
        _, vjp = jax.vjp(fn, q_ref[...], kp_ref[...], kc_ref[...], vp_ref[...], vc_ref[...], qg_ref[...], kg_ref[...],
                         s_ref[...], *[b_ref[kv] for kv in range(ATTN_KV)])
        ds = vjp(do_ref[...].astype(f32))
        dq_ref[...] = ds[0].astype(dq_ref.dtype)
        dkp_ref[...] = ds[1]
        dkc_ref[...] = ds[2]
        dvp_ref[...] = ds[3]
        dvc_ref[...] = ds[4]
        dqg_ref[...] += ds[5]
        dkg_ref[...] += ds[6]
        ds_ref[...] += ds[7]
        for kv in range(ATTN_KV):
            db_ref[kv] += ds[8 + kv]
        h.at_end(refs, 10, 19 + h.n_in, i == nb - 1)

    row = lambda w: pl.BlockSpec((BLK, w), lambda i: (i, 0))
    res = pl.pallas_call(
        body, name=name, grid=(nb,), in_specs=_swa_in_specs() + [row(D)] + h.in_specs,
        out_specs=[row(D), row(kw), row(kw), row(kw), row(kw),
                   pl.BlockSpec((1, ATTN_D), lambda i: (0, 0)), pl.BlockSpec((1, ATTN_D), lambda i: (0, 0)),
                   pl.BlockSpec((1, 128), lambda i: (0, 0)), _SWA_BIAS_SPEC] + h.out_specs,
        out_shape=[jax.ShapeDtypeStruct((t_len, D), bf16)] + [jax.ShapeDtypeStruct((t_len, kw), f32)] * 4
        + [jax.ShapeDtypeStruct((1, ATTN_D), f32)] * 2 + [jax.ShapeDtypeStruct((1, 128), f32),
                                                          jax.ShapeDtypeStruct(_SWA_BIAS_SHAPE, f32)] + h.out_shape,
        scratch_shapes=h.scratch, input_output_aliases=h.aliases(10, 9),
        compiler_params=_params(("arbitrary",)),
    )(proj, proj, proj, proj, proj, qg, kg, sinks, bias.reshape(_SWA_BIAS_SHAPE), do, *h.ins)
    return list(res[:8]) + [res[8].reshape(ATTN_HEADS, BLK, 2 * BLK)], list(res[9:])


def _bucket_table():
    qi = np.arange(BLK)[:, None] + BLK
    kj = np.arange(2 * BLK)[None, :]
    dist = qi - kj
    max_exact = REL_BUCKETS // 2
    d = np.maximum(dist, 1).astype(np.float32)
    large = max_exact + (np.log(d / max_exact) / math.log(128 / max_exact) * (REL_BUCKETS - max_exact)).astype(np.int32)
    large = np.minimum(large, REL_BUCKETS - 1)
    return np.where(dist < max_exact, np.maximum(dist, 0), large).astype(np.int32)


def relbias_fwd(table, bucket):
    def body(t_ref, bk_ref, o_ref):
        bk = bk_ref[...]
        for h in range(ATTN_HEADS):
            acc = jnp.zeros((BLK, 2 * BLK), f32)
            for b in range(REL_BUCKETS):
                acc = jnp.where(bk == b, t_ref[b, h], acc)
            o_ref[h] = acc

    return pl.pallas_call(
        body, name="relbias_fwd", out_shape=jax.ShapeDtypeStruct((ATTN_HEADS, BLK, 2 * BLK), f32),
        in_specs=[pl.BlockSpec(memory_space=pltpu.SMEM), pl.BlockSpec(memory_space=pltpu.VMEM)],
        out_specs=pl.BlockSpec(memory_space=pltpu.VMEM),
    )(table, bucket)


def relbias_bwd(dbias, bucket):
    n = len(dbias)

    def body(*refs):
        bk = refs[n][...]
        o_ref = refs[n + 1]
        row = lax.broadcasted_iota(jnp.int32, (REL_BUCKETS, 128), 0)
        lane = lax.broadcasted_iota(jnp.int32, (REL_BUCKETS, 128), 1)
        res = jnp.zeros((REL_BUCKETS, 128), f32)
        for h in range(ATTN_HEADS):
            tot = refs[0][h]
            for k in range(1, n):
                tot = tot + refs[k][h]
            for b in range(REL_BUCKETS):
                part = jnp.sum(jnp.sum(jnp.where(bk == b, tot, 0.0), axis=1, keepdims=True), axis=0, keepdims=True)
                res = jnp.where((row == b) & (lane == h), part, res)
        o_ref[...] = res

    return pl.pallas_call(
        body, name="relbias_bwd", out_shape=jax.ShapeDtypeStruct((REL_BUCKETS, 128), f32),
        in_specs=[pl.BlockSpec(memory_space=pltpu.VMEM)] * (n + 1),
        out_specs=pl.BlockSpec(memory_space=pltpu.VMEM),
    )(*dbias, bucket)[:, :ATTN_HEADS]


XA_TILE = 512


def f_xattn(q, k, v, qg, kg):
    qn, kn = _head_norm(q, qg), _head_norm(k, kg)
    logits = dot_nt(qn, kn) * (XD ** -0.5)
    p = jnp.exp(logits - lax.stop_gradient(jnp.max(logits, axis=-1, keepdims=True)))
    return dot_nn(p * (1.0 / jnp.sum(p, axis=-1, keepdims=True)), v)


def _xa_in_specs(tq):
    return [
        pl.BlockSpec((tq, XD), lambda h, i: (i, h)),
        pl.BlockSpec((MEM_LEN, XD), lambda h, i: (0, h)),
        pl.BlockSpec((MEM_LEN, XD), lambda h, i: (0, XH + h)),
        pl.BlockSpec((1, XD), lambda h, i: (0, 0)),
        pl.BlockSpec((1, XD), lambda h, i: (0, 0)),
    ]


def xa_fwd(name, xq, kv, qg, kg):
    t_len = xq.shape[0]
    tq = min(XA_TILE, t_len)

    def body(q_ref, k_ref, v_ref, qg_ref, kg_ref, o_ref):
        o_ref[...] = f_xattn(q_ref[...], k_ref[...], v_ref[...], qg_ref[...], kg_ref[...]).astype(o_ref.dtype)

    return pl.pallas_call(
        body, name=name, grid=(XH, t_len // tq), in_specs=_xa_in_specs(tq),
        out_specs=pl.BlockSpec((tq, XD), lambda h, i: (i, h)), out_shape=jax.ShapeDtypeStruct((t_len, D), bf16),
        compiler_params=_params(("parallel", "parallel")),
    )(xq, kv, kv, qg, kg)


def xa_bwd(name, xq, kv, do, qg, kg):
    t_len = xq.shape[0]
    tq = min(XA_TILE, t_len)

    def body(q_ref, k_ref, v_ref, qg_ref, kg_ref, do_ref, dq_ref, dk_ref, dv_ref, dqg_ref, dkg_ref):
        h, i = pl.program_id(0), pl.program_id(1)

        @pl.when(i == 0)
        def _():
            dk_ref[...] = jnp.zeros(dk_ref.shape, f32)
            dv_ref[...] = jnp.zeros(dv_ref.shape, f32)

        @pl.when((i == 0) & (h == 0))
        def _():
            dqg_ref[...] = jnp.zeros(dqg_ref.shape, f32)
            dkg_ref[...] = jnp.zeros(dkg_ref.shape, f32)

        _, vjp = jax.vjp(f_xattn, q_ref[...], k_ref[...], v_ref[...], qg_ref[...], kg_ref[...])
        dq, dk, dv, dqg, dkg = vjp(do_ref[...].astype(f32))
        dq_ref[...] = dq.astype(dq_ref.dtype)
        dk_ref[...] += dk
        dv_ref[...] += dv
        dqg_ref[...] += dqg
        dkg_ref[...] += dkg

    return pl.pallas_call(
        body, name=name, grid=(XH, t_len // tq),
        in_specs=_xa_in_specs(tq) + [pl.BlockSpec((tq, XD), lambda h, i: (i, h))],
        out_specs=[pl.BlockSpec((tq, XD), lambda h, i: (i, h)),
                   pl.BlockSpec((MEM_LEN, XD), lambda h, i: (0, h)), pl.BlockSpec((MEM_LEN, XD), lambda h, i: (0, h)),
                   pl.BlockSpec((1, XD), lambda h, i: (0, 0)), pl.BlockSpec((1, XD), lambda h, i: (0, 0))],
        out_shape=[jax.ShapeDtypeStruct((t_len, D), bf16), jax.ShapeDtypeStruct((MEM_LEN, D), f32),
                   jax.ShapeDtypeStruct((MEM_LEN, D), f32), jax.ShapeDtypeStruct((1, XD), f32), jax.ShapeDtypeStruct((1, XD), f32)],
        compiler_params=_params(("arbitrary", "arbitrary")),
    )(xq, kv, kv, qg, kg, do)


def loss_head(y, target, tile):
    t_len = y.shape[0]

    def body(y_ref, t_ref, dy_ref, l_ref):
        i = pl.program_id(0)

        @pl.when(i == 0)
        def _():
            l_ref[...] = jnp.zeros(l_ref.shape, f32)

        err = y_ref[...] - t_ref[...]
        dy_ref[...] = err * (1.0 / D)
        l_ref[...] += 0.5 * jnp.sum(jnp.sum(err * err, axis=1, keepdims=True), axis=0, keepdims=True) * (1.0 / D)

    row = pl.BlockSpec((tile, D), lambda i: (i, 0))
    return pl.pallas_call(
        body, name="loss_head", grid=(t_len // tile,), in_specs=[row, row],
        out_specs=[row, pl.BlockSpec((8, 128), lambda i: (0, 0))],
        out_shape=[jax.ShapeDtypeStruct((t_len, D), f32), jax.ShapeDtypeStruct((8, 128), f32)],
        compiler_params=_params(("arbitrary",)),
    )(y, target)


ADAM_BLOCK_ELEMS = 512 * 1024


def adamw(name, w, g, m, v):
    rows, cols = w.shape
    tile = rows
    if rows * cols > ADAM_BLOCK_ELEMS:
        tile = _pick(rows, [t for t in (512, 256, 128, 64, 32, 16, 8) if t * cols <= ADAM_BLOCK_ELEMS])

    def body(w_ref, g_ref, m_ref, v_ref, d_ref, nm_ref, nv_ref):
        gg = g_ref[...]
        nm = ADAM_B1 * m_ref[...] + (1.0 - ADAM_B1) * gg
        nv = ADAM_B2 * v_ref[...] + (1.0 - ADAM_B2) * jnp.square(gg)
        m_hat = nm / (1.0 - ADAM_B1 ** ADAM_STEP)
        v_hat = nv / (1.0 - ADAM_B2 ** ADAM_STEP)
        d_ref[...] = -ADAM_LR * (m_hat / (jnp.sqrt(v_hat) + ADAM_EPS) + ADAM_WD * w_ref[...])
        nm_ref[...] = nm
        nv_ref[...] = nv

    spec = pl.BlockSpec((tile, cols), lambda i: (i, 0))
    return pl.pallas_call(
        body, name=name, grid=(rows // tile,), in_specs=[spec] * 4, out_specs=[spec] * 3,
        out_shape=[jax.ShapeDtypeStruct((rows, cols), f32)] * 3, compiler_params=_params(("parallel",)),
    )(w, g, m, v)


def adamw_layers(name, w, m, v, gsrcs, r0):
    depth, rows, cols = w.shape
    tile = _pick(rows, [t for t in (512, 256, 128, 64, 32, 16, 8) if t * cols <= ADAM_BLOCK_ELEMS and r0 % t == 0])

    def body(*refs):
        w_ref, m_ref, v_ref = refs[:3]
        g_refs = refs[3:3 + depth]
        go_ref, d_ref, nm_ref, nv_ref = refs[3 + depth:]
        layer = pl.program_id(0)
        gg = g_refs[0][...]
        for k in range(1, depth):
            gg = jnp.where(layer == k, g_refs[k][...], gg)
        nm = ADAM_B1 * m_ref[0] + (1.0 - ADAM_B1) * gg
        nv = ADAM_B2 * v_ref[0] + (1.0 - ADAM_B2) * jnp.square(gg)
        m_hat = nm / (1.0 - ADAM_B1 ** ADAM_STEP)
        v_hat = nv / (1.0 - ADAM_B2 ** ADAM_STEP)
        go_ref[0] = gg
        d_ref[0] = -ADAM_LR * (m_hat / (jnp.sqrt(v_hat) + ADAM_EPS) + ADAM_WD * w_ref[0])
        nm_ref[0] = nm
        nv_ref[0] = nv

    spec = pl.BlockSpec((1, tile, cols), lambda l, i: (l, i, 0))
    g_specs = [pl.BlockSpec((tile, cols), functools.partial(lambda l, i, k: (jnp.where(l == k, r0 // tile + i, 0), 0), k=k))
               for k in range(depth)]
    return pl.pallas_call(
        body, name=name, grid=(depth, rows // tile), in_specs=[spec] * 3 + g_specs, out_specs=[spec] * 4,
        out_shape=[jax.ShapeDtypeStruct(w.shape, f32)] * 4, compiler_params=_params(("parallel", "parallel")),
    )(w, m, v, *gsrcs)


MESH = pl.DeviceIdType.MESH
HBM_SPEC = pl.BlockSpec(memory_space=pltpu.HBM)


def _place():
    x, y, c = lax.axis_index("x"), lax.axis_index("y"), lax.axis_index("c")
    chips = [(1 - x, y), (x, 1 - y), (1 - x, 1 - y)]
    return x, y, c, chips


def _rcopy(src, dst, send_sems, recv_sems, k, to):
    return pltpu.make_async_remote_copy(src_ref=src, dst_ref=dst, send_sem=send_sems.at[k], recv_sem=recv_sems.at[k],
                                        device_id=to, device_id_type=MESH)


class Exchange:
    def __init__(self, ins, outs, n_sems, n_local, start, wait, aliases=None):
        self.ins, self.outs, self.n_sems, self.n_local = list(ins), list(outs), n_sems, n_local
        self.start, self.wait, self.aliases = start, wait, dict(aliases or {})


class _hosted:
    def __init__(self, ex):
        self.ex = ex
        self.ins = list(ex.ins) if ex else []
        self.out_shape = list(ex.outs) if ex else []
        self.n_in, self.n_out = len(self.ins), len(self.out_shape)
        self.in_specs, self.out_specs = [HBM_SPEC] * self.n_in, [HBM_SPEC] * self.n_out
        self.scratch = [pltpu.SemaphoreType.DMA((ex.n_sems,)), pltpu.SemaphoreType.DMA((ex.n_sems,)),
                        pltpu.SemaphoreType.DMA((max(ex.n_local, 1),))] if ex else []

    def aliases(self, first_in, first_out):
        return {first_in + a: first_out + b for a, b in self.ex.aliases.items()} if self.ex else {}

    def _args(self, refs, i0, o0):
        return refs[i0:i0 + self.n_in], refs[o0:o0 + self.n_out], refs[-3], refs[-2], refs[-1]

    def at_start(self, refs, i0, o0, pred):
        if self.ex is not None:
            pl.when(pred)(lambda: self.ex.start(*self._args(refs, i0, o0)))

    def at_end(self, refs, i0, o0, pred):
        if self.ex is not None:
            pl.when(pred)(lambda: self.ex.wait(*self._args(refs, i0, o0)))


def run_exchange(name, ex):
    h = _hosted(ex)

    def body(*refs):
        args = h._args(refs, 0, h.n_in)
        ex.start(*args)
        ex.wait(*args)

    res = pl.pallas_call(
        body, name=name, out_shape=h.out_shape, in_specs=h.in_specs, out_specs=h.out_specs, scratch_shapes=h.scratch,
        input_output_aliases=h.aliases(0, 0), compiler_params=pltpu.CompilerParams(has_side_effects=True),
    )(*h.ins)
    return list(res)


def _halves(ref, axis, c):
    hh = ref.shape[axis] // 2
    return pl.ds(pl.multiple_of(c * hh, 16), hh), pl.ds(pl.multiple_of((1 - c) * hh, 16), hh)


def ex_gather_ici(shards):
    n = len(shards)

    def copies(ins, outs, ssem, rsem, landing):
        x, y, c, chips = _place()
        me = 2 * x + y
        res = []
        for a, (s, o) in enumerate(zip(ins, outs)):
            mine, _ = _halves(s, 0, c)
            for j, (cx, cy) in enumerate(chips):
                slot = 2 * cx + cy if landing else me
                res.append(_rcopy(s.at[mine], o.at[slot, mine], ssem, rsem, 4 * a + j, (cx, cy, c)))
            res.append(_rcopy(s, o.at[me], ssem, rsem, 4 * a + 3, (x, y, 1 - c)))
        return res

    def start(ins, outs, ssem, rsem, lsem):
        for cp in copies(ins, outs, ssem, rsem, False):
            cp.start()

    def wait(ins, outs, ssem, rsem, lsem):
        for cp in copies(ins, outs, ssem, rsem, True):
            cp.wait_recv()
        for cp in copies(ins, outs, ssem, rsem, False):
            cp.wait_send()

    outs = [jax.ShapeDtypeStruct((N_CHIPS,) + s.shape, s.dtype) for s in shards]
    return Exchange(shards, outs, 4 * n, 0, start, wait)


def ex_gather_d2d(gathered):
    n = len(gathered)

    def copies(outs, ssem, rsem, landing):
        x, y, c, chips = _place()
        res = []
        for a, o in enumerate(outs):
            mine, theirs = _halves(o, 1, c)
            for j, (cx, cy) in enumerate(chips):
                rows = o.at[2 * cx + cy, theirs if landing else mine]
                res.append(_rcopy(rows, rows, ssem, rsem, 3 * a + j, (x, y, 1 - c)))
        return res

    def start(ins, outs, ssem, rsem, lsem):
        for cp in copies(outs, ssem, rsem, False):
            cp.start()

    def wait(ins, outs, ssem, rsem, lsem):
        for cp in copies(outs, ssem, rsem, True):
            cp.wait_recv()
        for cp in copies(outs, ssem, rsem, False):
            cp.wait_send()

    outs = [jax.ShapeDtypeStruct(g.shape, g.dtype) for g in gathered]
    return Exchange(gathered, outs, 3 * n, 0, start, wait, aliases={a: a for a in range(n)})


def ex_swap_halves(gs):
    n = len(gs)

    def copies(ins, outs, ssem, rsem):
        x, y, c, _ = _place()
        return [_rcopy(g.at[s, 1 - c], o.at[s], ssem, rsem, N_CHIPS * a + s, (x, y, 1 - c))
                for a, (g, o) in enumerate(zip(ins, outs)) for s in range(N_CHIPS)]

    def start(ins, outs, ssem, rsem, lsem):
        for cp in copies(ins, outs, ssem, rsem):
            cp.start()

    def wait(ins, outs, ssem, rsem, lsem):
        for cp in copies(ins, outs, ssem, rsem):
            cp.wait()

    outs = [jax.ShapeDtypeStruct((N_CHIPS,) + g.shape[2:], g.dtype) for g in gs]
    return Exchange(gs, outs, N_CHIPS * n, 0, start, wait)


def ex_scatter(ps):
    n = len(ps)

    def copies(ins, outs, ssem, rsem):
        x, y, c, chips = _place()
        return [_rcopy(p.at[2 * cx + cy], o.at[j], ssem, rsem, 3 * a + j, (cx, cy, c))
                for a, (p, o) in enumerate(zip(ins, outs)) for j, (cx, cy) in enumerate(chips)]

    def start(ins, outs, ssem, rsem, lsem):
        for cp in copies(ins, outs, ssem, rsem):
            cp.start()

    def wait(ins, outs, ssem, rsem, lsem):
        for cp in copies(ins, outs, ssem, rsem):
            cp.wait()

    outs = [jax.ShapeDtypeStruct((3,) + p.shape[1:], p.dtype) for p in ps]
    return Exchange(ps, outs, 3 * n, 0, start, wait)


def ex_share(rs):
    n = len(rs)

    def copies(ins, outs, ssem, rsem):
        x, y, c, _ = _place()
        return [_rcopy(r, o, ssem, rsem, a, (x, y, 1 - c)) for a, (r, o) in enumerate(zip(ins, outs))]

    def start(ins, outs, ssem, rsem, lsem):
        for cp in copies(ins, outs, ssem, rsem):
            cp.start()

    def wait(ins, outs, ssem, rsem, lsem):
        for cp in copies(ins, outs, ssem, rsem):
            cp.wait()

    outs = [jax.ShapeDtypeStruct(r.shape, r.dtype) for r in rs]
    return Exchange(rs, outs, n, 0, start, wait)


ADD_BLOCK_ELEMS = 384 * 1024


def _add_tile(rows, cols):
    return _pick(rows, [t for t in (1920, 960, 512, 384, 256, 128, 64, 32, 16) if t * cols <= ADD_BLOCK_ELEMS])


def add_halves(name, g, got, c_arr):
    _, _, rows, cols = g.shape
    tile = _add_tile(rows, cols)

    def body(c_ref, g_ref, r_ref, o_ref):
        o_ref[...] = (g_ref[0] + r_ref[...]).astype(o_ref.dtype)

    return pl.pallas_call(
        body, name=name,
        grid_spec=pltpu.PrefetchScalarGridSpec(
            num_scalar_prefetch=1, grid=(N_CHIPS, rows // tile),
            in_specs=[pl.BlockSpec((1, 1, tile, cols), lambda s, i, c_ref: (s, c_ref[0], i, 0)),
                      pl.BlockSpec((1, tile, cols), lambda s, i, c_ref: (s, i, 0))],
            out_specs=pl.BlockSpec((1, tile, cols), lambda s, i, c_ref: (s, i, 0))),
        out_shape=jax.ShapeDtypeStruct(got.shape, bf16), compiler_params=_params(("parallel", "parallel")),
    )(c_arr, g, got)


def add_partials(name, p, got, me_arr):
    _, rows, cols = p.shape
    tile = _add_tile(rows, cols)

    def body(me_ref, p_ref, r_ref, o_ref):
        o_ref[...] = ((p_ref[0].astype(f32) + r_ref[0].astype(f32)) + r_ref[1].astype(f32)) + r_ref[2].astype(f32)

    return pl.pallas_call(
        body, name=name,
        grid_spec=pltpu.PrefetchScalarGridSpec(
            num_scalar_prefetch=1, grid=(rows // tile,),
            in_specs=[pl.BlockSpec((1, tile, cols), lambda i, me_ref: (me_ref[0], i, 0)),
                      pl.BlockSpec((3, tile, cols), lambda i, me_ref: (0, i, 0))],
            out_specs=pl.BlockSpec((tile, cols), lambda i, me_ref: (i, 0))),
        out_shape=jax.ShapeDtypeStruct((rows, cols), f32), compiler_params=_params(("parallel",)),
    )(me_arr, p, got)


def join_halves(name, mine, theirs):
    rows, cols = mine.shape
    tile = _add_tile(rows, cols)

    def body(m_ref, t_ref, o_ref):
        south = lax.axis_index("c") == 0
        o_ref[0] = jnp.where(south, m_ref[...], t_ref[...])
        o_ref[1] = jnp.where(south, t_ref[...], m_ref[...])

    spec = pl.BlockSpec((tile, cols), lambda i: (i, 0))
    return pl.pallas_call(
        body, name=name, grid=(rows // tile,), in_specs=[spec, spec],
        out_specs=pl.BlockSpec((2, tile, cols), lambda i: (0, i, 0)),
        out_shape=jax.ShapeDtypeStruct((2, rows, cols), f32), compiler_params=_params(("parallel",)),
    )(mine, theirs)


N_DEV = 8


def allreduce_small(name, buf):
    m_per = buf.shape[0]

    def body(x_ref, out_ref, all_ref, send_sems, recv_sems, local_sem):
        x, y, c, chips = _place()
        me, sibling = (x, y, c), (x, y, 1 - c)

        def rows(px, py, pc):
            return all_ref.at[pl.ds(pl.multiple_of((4 * px + 2 * py + pc) * m_per, 8), m_per), :]

        def copy(k, block, to, src=None):
            return _rcopy(rows(*block) if src is None else src, rows(*block), send_sems, recv_sems, k, to)

        mine = pltpu.make_async_copy(x_ref, rows(*me), local_sem)
        mine.start()
        first = [copy(0, me, sibling, src=x_ref)]
        first += [copy(1 + j, me, (*chip, c), src=x_ref) for j, chip in enumerate(chips)]
        for cp in first:
            cp.start()
        passed = [copy(4 + j, (*chip, c), sibling) for j, chip in enumerate(chips)]
        for j, chip in enumerate(chips):
            copy(1 + j, (*chip, c), me).wait_recv()
            passed[j].start()
        copy(0, sibling, me).wait_recv()
        for j, chip in enumerate(chips):
            copy(4 + j, (*chip, 1 - c), me).wait_recv()
        for cp in first + passed:
            cp.wait_send()
        mine.wait()
        tot = all_ref[pl.ds(0, m_per), :]
        for d in range(1, N_DEV):
            tot = tot + all_ref[pl.ds(d * m_per, m_per), :]
        out_ref[...] = tot

    return pl.pallas_call(
        body, name=name, out_shape=jax.ShapeDtypeStruct((m_per, 128), f32),
        in_specs=[pl.BlockSpec(memory_space=pltpu.VMEM)], out_specs=pl.BlockSpec(memory_space=pltpu.VMEM),
        scratch_shapes=[pltpu.VMEM((N_DEV * m_per, 128), f32), pltpu.SemaphoreType.DMA((7,)), pltpu.SemaphoreType.DMA((7,)),
                        pltpu.SemaphoreType.DMA],
        compiler_params=pltpu.CompilerParams(has_side_effects=True, vmem_limit_bytes=VMEM_LIMIT_BYTES),
    )(buf)


def pack_shards(ws, layer):
    wide = jnp.concatenate([ws[n][layer].astype(bf16) for n in WIDE], axis=0)
    return [ws["w_in"][layer].astype(bf16), wide, ws["w_xkv"][layer].astype(bf16)]


def arrange_w_in(shards):
    parts = []
    for a, b, _ in sorted(ORIG_SEGS, key=lambda seg: seg[2]):
        for s in range(N_CHIPS):
            lo, hi = max(a, s * IN_SHARD), min(b, (s + 1) * IN_SHARD)
            if lo < hi:
                parts.append(shards[s][:, lo - s * IN_SHARD:hi - s * IN_SHARD])
    parts.append(jnp.zeros((shards.shape[1], PCOLS - C_DT - (OFF_Q - OFF_DT)), shards.dtype))
    return jnp.concatenate(parts, axis=1)


def shard_w_in_grad(g):
    out = []
    for s in range(N_CHIPS):
        parts = []
        for a, b, first in ORIG_SEGS:
            lo, hi = max(a, s * IN_SHARD), min(b, (s + 1) * IN_SHARD)
            if lo < hi:
                parts.append(g[:, first + lo - a:first + hi - a])
        out.append(jnp.concatenate(parts, axis=1))
    return jnp.stack(out)


def unpack_gathered(gathered):
    g_in, g_wide, g_xkv = gathered
    out = dict(w_in=arrange_w_in(g_in), w_xkv=jnp.concatenate([g_xkv[s] for s in range(N_CHIPS)], axis=1))
    for n in WIDE:
        piece = g_wide[:, WIDE_OFF[n]:WIDE_OFF[n] + WIDE_ROWS[n], :]
        if n == "w_mlp_up":
            out[n] = jnp.concatenate([piece[s] for s in range(N_CHIPS)], axis=1)
        else:
            out[n] = piece.reshape(N_CHIPS * WIDE_ROWS[n], D)
    return out


def _halved(a):
    return a.reshape(N_CHIPS, 2, a.shape[1] // 2, a.shape[2])


def pack_w_in_grad(g):
    return [_halved(shard_w_in_grad(g))]


def pack_other_grads(gs):
    cols = lambda g, q: jnp.stack([g[:, s * q:(s + 1) * q] for s in range(N_CHIPS)])
    wide = [cols(gs[n], D) if n == "w_mlp_up" else gs[n].reshape(N_CHIPS, WIDE_ROWS[n], D) for n in WIDE]
    return [_halved(jnp.concatenate(wide, axis=1)), _halved(cols(gs["w_xkv"], 2048 // N_CHIPS))]


ROW_TILE = 256


def _pad128(v):
    return jnp.pad(v.reshape(1, -1), ((0, 0), (0, 128 - v.shape[-1])))


def layer_forward(i, h, mem, w, p, bias, next_shards=None):
    tile = min(ROW_TILE, h.shape[0])
    s = dict(h=h)
    row = lambda a: a.reshape(1, -1)
    s["u"], = tile_fwd(f"rms_mix{i}", f_rms, [(h, D, 0)], [row(p["norm_mix"])], [(D, bf16)], tile)
    s["proj"] = proj = matmul(s["u"], w["w_in"], "nn", f"mm_in{i}")
    s["c1"], = conv_fwd(f"conv_a{i}", proj, C_CONV, C_CONV + D, p["conv_dw_w"], row(p["conv_dw_b"]), True, False)
    s["ca"], = tile_fwd(f"ln_silu{i}", f_ln_silu, [(s["c1"], D, 0)], [row(p["conv_ln_g"]), row(p["conv_ln_b"])], [(D, bf16)], tile)
    s["ya"] = matmul(s["ca"], w["w_conv_out"], "nn", f"mm_conv_out{i}")
    s["xpre"], s["xact"] = conv_fwd(f"conv_b{i}", proj, C_XBC, C_XBC, p["ssd_conv_w"], row(p["ssd_conv_b"]), False, True)
    ssd_p = (_pad128(p["ssd_dt_bias"]), _pad128(p["ssd_A_log"]), _pad128(p["ssd_D"]), row(p["ssd_norm_g"]))
    s["yB"], s["states"], landed = ssd_fwd(f"ssd_fwd{i}", s["xact"], proj, *ssd_p,
                                           ex=ex_gather_ici(next_shards) if next_shards else None)
    s["yb"] = matmul(s["yB"], w["w_ssd_out"], "nn", f"mm_ssd_out{i}")
    swa_p = (row(p["attn_q_norm"]), row(p["attn_k_norm"]), _pad128(p["attn_sinks"]))
    s["oc"], gathered = swa_fwd(f"swa_fwd{i}", proj, *swa_p, bias, ex=ex_gather_d2d(landed) if next_shards else None)
    s["yc"] = matmul(s["oc"], w["w_attn_out"], "nn", f"mm_attn_out{i}")
    gate_cols = [(proj, D, C_G // D + k) for k in range(3)]
    s["merged"], = tile_fwd(f"merge{i}", f_merge, gate_cols + [(s["ya"], D, 0), (s["yb"], D, 0), (s["yc"], D, 0)],
                            [row(p["gate_bias"])], [(D, bf16)], tile)
    s["h1"] = h1 = matmul(s["merged"], w["w_mix_out"], "nn", f"mm_mix_out{i}", residual=h)
    s["hx"], = tile_fwd(f"rms_x{i}", f_rms, [(h1, D, 0)], [row(p["norm_xattn"])], [(D, bf16)], tile)
    s["memh"], = tile_fwd(f"rms_mem{i}", f_rms, [(mem, D, 0)], [row(p["norm_mem"])], [(D, bf16)], MEM_LEN)
    s["xq"] = matmul(s["hx"], w["w_xq"], "nn", f"mm_xq{i}")
    s["kv"] = matmul(s["memh"], w["w_xkv"], "nn", f"mm_xkv{i}")
    s["xo"] = xa_fwd(f"xa_fwd{i}", s["xq"], s["kv"], row(p["xattn_q_norm"]), row(p["xattn_k_norm"]))
    s["h2"] = h2 = matmul(s["xo"], w["w_xo"], "nn", f"mm_xo{i}", residual=h1)
    s["um"], = tile_fwd(f"rms_mlp{i}", f_rms, [(h2, D, 0)], [row(p["norm_mlp"])], [(D, bf16)], tile)
    s["up"], s["act"] = matmul(s["um"], w["w_mlp_up"], "nn", f"mm_up{i}", relu2="fwd")
    h3 = matmul(s["act"], w["w_mlp_down"], "nn", f"mm_down{i}", residual=h2)
    return h3, s, gathered


def layer_backward(i, dh3, mem, w, p, bias, s, pending=None, c_arr=None):
    t_len = dh3.shape[0]
    tile = min(ROW_TILE, t_len)
    row = lambda a: a.reshape(1, -1)
    gw, gp = {}, {}
    dup = matmul(dh3, w["w_mlp_down"], "nt", f"mm_down_dx{i}", out_dtype=bf16, relu2=s["up"])
    gw["w_mlp_down"] = matmul(s["act"], dh3, "tn", f"mm_down_dw{i}")
    gw["w_mlp_up"] = matmul(s["um"], dup, "tn", f"mm_up_dw{i}")
    dum = matmul(dup, w["w_mlp_up"], "nt", f"mm_up_dx{i}")
    (dh2,), (g,) = tile_bwd(f"rms_mlp_bwd{i}", f_rms, [(s["h2"], D, 0)], [row(p["norm_mlp"])], [dum], [f32], tile, addend=dh3)
    gp["norm_mlp"] = g
    dxo = matmul(dh2, w["w_xo"], "nt", f"mm_xo_dx{i}")
    gw["w_xo"] = matmul(s["xo"], dh2, "tn", f"mm_xo_dw{i}")
    dxq, dk, dv, gp["xattn_q_norm"], gp["xattn_k_norm"] = xa_bwd(f"xa_bwd{i}", s["xq"], s["kv"], dxo, row(p["xattn_q_norm"]),
                                                                 row(p["xattn_k_norm"]))
    dkv = jnp.concatenate([dk, dv], axis=1)
    gw["w_xq"] = matmul(s["hx"], dxq, "tn", f"mm_xq_dw{i}")
    dhx = matmul(dxq, w["w_xq"], "nt", f"mm_xq_dx{i}")
    gw["w_xkv"] = matmul(s["memh"], dkv, "tn", f"mm_xkv_dw{i}")
    dmemh = matmul(dkv, w["w_xkv"], "nt", f"mm_xkv_dx{i}")
    _, (g,) = tile_bwd(f"rms_mem_bwd{i}", f_rms, [(mem, D, 0)], [row(p["norm_mem"])], [dmemh], [f32], MEM_LEN)
    gp["norm_mem"] = g
    (dh1,), (g,) = tile_bwd(f"rms_x_bwd{i}", f_rms, [(s["h1"], D, 0)], [row(p["norm_xattn"])], [dhx], [f32], tile, addend=dh2)
    gp["norm_xattn"] = g
    proj = s["proj"]
    dmerged = matmul(dh1, w["w_mix_out"], "nt", f"mm_mix_out_dx{i}")
    gw["w_mix_out"] = matmul(s["merged"], dh1, "tn", f"mm_mix_out_dw{i}")
    gate_cols = [(proj, D, C_G // D + k) for k in range(3)]
    (dpa, dpb, dpc, dya, dyb, dyc), (g,) = tile_bwd(
        f"merge_bwd{i}", f_merge, gate_cols + [(s["ya"], D, 0), (s["yb"], D, 0), (s["yc"], D, 0)], [row(p["gate_bias"])],
        [dmerged], [bf16] * 6, tile)
    gp["gate_bias"] = g
    dca = matmul(dya, w["w_conv_out"], "nt", f"mm_conv_out_dx{i}")
    gw["w_conv_out"] = matmul(s["ca"], dya, "tn", f"mm_conv_out_dw{i}")
    (dc1,), (gp["conv_ln_g"], gp["conv_ln_b"]) = tile_bwd(
        f"ln_silu_bwd{i}", f_ln_silu, [(s["c1"], D, 0)], [row(p["conv_ln_g"]), row(p["conv_ln_b"])], [dca], [f32], tile)
    da, dgate, gp["conv_dw_w"], gp["conv_dw_b"] = conv_bwd(f"conv_a_bwd{i}", dc1, None, proj, C_CONV, C_CONV + D,
                                                           p["conv_dw_w"], True, False, bf16)
    dyB = matmul(dyb, w["w_ssd_out"], "nt", f"mm_ssd_out_dx{i}")
    gw["w_ssd_out"] = matmul(s["yB"], dyb, "tn", f"mm_ssd_out_dw{i}")
    doc = matmul(dyc, w["w_attn_out"], "nt", f"mm_attn_out_dx{i}")
    gw["w_attn_out"] = matmul(s["oc"], dyc, "tn", f"mm_attn_out_dw{i}")
    moving = list(pending or []) + pack_other_grads(gw)
    swa_p = (row(p["attn_q_norm"]), row(p["attn_k_norm"]), _pad128(p["attn_sinks"]))
    (dq, dkp, dkc, dvp, dvc, gp["attn_q_norm"], gp["attn_k_norm"], g, dbias), swapped = swa_bwd(
        f"swa_bwd{i}", proj, doc, *swa_p, bias, ex=ex_swap_halves(moving))
    gp["attn_sinks"] = g[:, :ATTN_HEADS]
    shift = lambda a: jnp.concatenate([a[BLK:], jnp.zeros((BLK, a.shape[1]), a.dtype)], axis=0)
    dk_ = (dkc + shift(dkp)).astype(bf16)
    dv_ = (dvc + shift(dvp)).astype(bf16)
    partials = [add_halves(f"add_halves{i}_{k}", g_, got, c_arr) for k, (g_, got) in enumerate(zip(moving, swapped))]
    ssd_p = (_pad128(p["ssd_dt_bias"]), _pad128(p["ssd_A_log"]), _pad128(p["ssd_D"]), row(p["ssd_norm_g"]))
    (dxs, dbm, dcm, dz, ddt4, g1, g2, g3, g4), arrived = ssd_bwd(f"ssd_bwd{i}", s["xact"], proj, s["states"], dyB, *ssd_p,
                                                                  ex=ex_scatter(partials))
    gp["ssd_dt_bias"], gp["ssd_A_log"], gp["ssd_D"], gp["ssd_norm_g"] = g1[:, :SSD_HEADS], g2[:, :SSD_HEADS], g3[:, :SSD_HEADS], g4
    ddt = (ddt4[0] + ddt4[1] + ddt4[2] + ddt4[3]).astype(bf16)
    dxact = jnp.concatenate([dxs, dbm, dcm], axis=1)
    dxbc, gp["ssd_conv_w"], gp["ssd_conv_b"] = conv_bwd(f"conv_b_bwd{i}", dxact, s["xpre"], proj, C_XBC, C_XBC, p["ssd_conv_w"],
                                                        False, True, bf16)
    dproj = jnp.concatenate([da, dgate, dz, dxbc, dq, dpa, dpb, dpc, dk_, dv_, ddt, jnp.zeros((t_len, PCOLS - C_DT - 128), bf16)], axis=1)
    gw["w_in"] = matmul(s["u"], dproj, "tn", f"mm_in_dw{i}")
    du = matmul(dproj, w["w_in"], "nt", f"mm_in_dx{i}")
    (dh,), (g,) = tile_bwd(f"rms_mix_bwd{i}", f_rms, [(s["h"], D, 0)], [row(p["norm_mix"])], [du], [f32], tile, addend=dh1)
    gp["norm_mix"] = g
    return dh, gw, gp, dbias, (partials, arrived)


SMALL = ["rel_table", "norm_mix", "gate_bias", "conv_dw_w", "conv_dw_b", "conv_ln_g", "conv_ln_b", "ssd_conv_w", "ssd_conv_b",
         "ssd_dt_bias", "ssd_A_log", "ssd_D", "ssd_norm_g", "attn_q_norm", "attn_k_norm", "attn_sinks", "norm_xattn", "norm_mem",
         "xattn_q_norm", "xattn_k_norm", "norm_mlp"]
SMALL_SHARDED = dict(gate_bias=D, conv_dw_w=D, ssd_conv_w=SSD_XBC)
ORDER = ["rel_table", "norm_mix", "w_in", "gate_bias", "conv_dw_w", "conv_dw_b", "conv_ln_g", "conv_ln_b", "w_conv_out",
         "ssd_conv_w", "ssd_conv_b", "ssd_dt_bias", "ssd_A_log", "ssd_D", "ssd_norm_g", "w_ssd_out", "attn_q_norm", "attn_k_norm",
         "attn_sinks", "w_attn_out", "w_mix_out", "norm_xattn", "norm_mem", "w_xq", "w_xkv", "xattn_q_norm", "xattn_k_norm",
         "w_xo", "norm_mlp", "w_mlp_up", "w_mlp_down"]


def finish_reduction(i, partials, arrived, me_arr):
    mine = [add_partials(f"add_partials{i}_{k}", p, got, me_arr) for k, (p, got) in enumerate(zip(partials, arrived))]
    theirs = run_exchange(f"share_halves{i}", ex_share(mine))
    both = [join_halves(f"join_halves{i}_{k}", m, t) for k, (m, t) in enumerate(zip(mine, theirs))]
    return [b.reshape(2 * b.shape[1], b.shape[2]) for b in both]


def kernel(x, mem, rel_table, norm_mix, w_in, gate_bias, conv_dw_w, conv_dw_b, conv_ln_g, conv_ln_b, w_conv_out, ssd_conv_w, ssd_conv_b, ssd_dt_bias, ssd_A_log, ssd_D, ssd_norm_g, w_ssd_out, attn_q_norm, attn_k_norm, attn_sinks, w_attn_out, w_mix_out, norm_xattn, norm_mem, w_xq, w_xkv, xattn_q_norm, xattn_k_norm, w_xo, norm_mlp, w_mlp_up, w_mlp_down, loss_target, m_rel_table, m_norm_mix, m_w_in, m_gate_bias, m_conv_dw_w, m_conv_dw_b, m_conv_ln_g, m_conv_ln_b, m_w_conv_out, m_ssd_conv_w, m_ssd_conv_b, m_ssd_dt_bias, m_ssd_A_log, m_ssd_D, m_ssd_norm_g, m_w_ssd_out, m_attn_q_norm, m_attn_k_norm, m_attn_sinks, m_w_attn_out, m_w_mix_out, m_norm_xattn, m_norm_mem, m_w_xq, m_w_xkv, m_xattn_q_norm, m_xattn_k_norm, m_w_xo, m_norm_mlp, m_w_mlp_up, m_w_mlp_down, v_rel_table, v_norm_mix, v_w_in, v_gate_bias, v_conv_dw_w, v_conv_dw_b, v_conv_ln_g, v_conv_ln_b, v_w_conv_out, v_ssd_conv_w, v_ssd_conv_b, v_ssd_dt_bias, v_ssd_A_log, v_ssd_D, v_ssd_norm_g, v_w_ssd_out, v_attn_q_norm, v_attn_k_norm, v_attn_sinks, v_w_attn_out, v_w_mix_out, v_norm_xattn, v_norm_mem, v_w_xq, v_w_xkv, v_xattn_q_norm, v_xattn_k_norm, v_w_xo, v_norm_mlp, v_w_mlp_up, v_w_mlp_down):
    a = dict(locals())
    wts = {n: a[n] for n in ORDER}
    ms = {n: a["m_" + n] for n in ORDER}
    vs = {n: a["v_" + n] for n in ORDER}
    depth = norm_mix.shape[0]
    ax, ay, ac = lax.axis_index("x"), lax.axis_index("y"), lax.axis_index("c")
    chip = 2 * ax + ay

    small = {n: wts[n] for n in SMALL if n != "rel_table" and n not in SMALL_SHARDED}
    gather_buf, spans, r0 = [], {}, 0
    for n, width in SMALL_SHARDED.items():
        blk = wts[n]
        q = width // N_CHIPS
        whole = lax.dynamic_update_slice(jnp.zeros(blk.shape[:-1] + (width,), f32), blk, (0, 0, chip * q))
        flat = whole.reshape(-1, 128)
        spans[n] = (r0, flat.shape[0], whole.shape)
        r0 += flat.shape[0]
        gather_buf.append(flat)
    summed = allreduce_small("gather_small_params", jnp.concatenate(gather_buf, axis=0))
    for n, (s0, nr, shp) in spans.items():
        small[n] = summed[s0:s0 + nr].reshape(shp) * 0.5

    c_arr = jnp.reshape(ac, (1,)).astype(jnp.int32)
    me_arr = jnp.reshape(chip, (1,)).astype(jnp.int32)
    x0, mem0 = x[0], mem[0]
    bucket = jnp.asarray(_bucket_table())
    bias = relbias_fwd(rel_table, bucket)

    gathered = run_exchange("gather_d2d_first", ex_gather_d2d(run_exchange("gather_ici_first", ex_gather_ici(pack_shards(wts, 0)))))
    h, saved, ws = x0, [], []
    for i in range(depth):
        w = unpack_gathered(gathered)
        p = {n: small[n][i] for n in small}
        h, s, gathered = layer_forward(i, h, mem0, w, p, bias, pack_shards(wts, i + 1) if i + 1 < depth else None)
        saved.append(s)
        ws.append((w, p))
    grad_x, loss_tile = loss_head(h, loss_target[0], min(ROW_TILE, h.shape[0]))
    loss = lax.psum(loss_tile[0, 0], ("x", "y", "c"))

    reduced, small_grads, dbiases, pending = [[None] * 3 for _ in range(depth)], [None] * depth, [], None
    for i in reversed(range(depth)):
        w, p = ws[i]
        grad_x, gw, gp, dbias, travelled = layer_backward(i, grad_x, mem0, w, p, bias, saved[i], pending, c_arr)
        done = finish_reduction(i, *travelled, me_arr)
        if pending is not None:
            reduced[i + 1][0] = done.pop(0)
        reduced[i][1], reduced[i][2] = done
        small_grads[i] = gp
        dbiases.append(dbias)
        pending = pack_w_in_grad(gw["w_in"])
    swapped = run_exchange("swap_halves_last", ex_swap_halves(pending))
    partials = [add_halves("add_halves_last", pending[0], swapped[0], c_arr)]
    reduced[0][0], = finish_reduction("_last", partials, run_exchange("scatter_last", ex_scatter(partials)), me_arr)
    sg = {n: jnp.stack([small_grads[i][n].reshape(small[n].shape[1:]) for i in range(depth)]) for n in small}
    sg["rel_table"] = relbias_bwd(dbiases, bucket)

    grads, deltas, new_m, new_v = {}, {}, {}, {}
    sources = dict(w_in=(0, 0), w_xkv=(2, 0), **{n: (1, WIDE_OFF[n]) for n in WIDE})
    for n, (k, r0) in sources.items():
        grads[n], deltas[n], new_m[n], new_v[n] = adamw_layers(f"adamw_{n}", wts[n], ms[n], vs[n], [reduced[i][k] for i in range(depth)], r0)

    parts, spans, r0 = [], {}, 0
    for n in SMALL:
        flat = sg[n].reshape(-1)
        nr = -(-flat.shape[0] // 128)
        nr = -(-nr // 8) * 8
        flat = jnp.pad(flat, (0, nr * 128 - flat.shape[0])).reshape(nr, 128)
        spans[n] = (r0, nr, sg[n].shape)
        r0 += nr
        parts.append(flat)
    summed = allreduce_small("allreduce_small_grads", jnp.concatenate(parts, axis=0))
    for n, (s0, nr, shp) in spans.items():
        size = int(np.prod(shp))
        g = summed[s0:s0 + nr].reshape(-1)[:size].reshape(shp)
        if n in SMALL_SHARDED:
            q = SMALL_SHARDED[n] // N_CHIPS
            g = lax.dynamic_slice_in_dim(g, chip * q, q, axis=g.ndim - 1)
        grads[n] = g

    for n in SMALL:
        shp = wts[n].shape
        two = (lambda t: t.reshape(-1, shp[-1]))
        d, nm, nv = adamw(f"adamw_{n}", two(wts[n]), two(grads[n]), two(ms[n]), two(vs[n]))
        deltas[n], new_m[n], new_v[n] = d.reshape(shp), nm.reshape(shp), nv.reshape(shp)

    return (loss, grad_x[None], *[grads[n] for n in ORDER], *[deltas[n] for n in ORDER],
            *[new_m[n] for n in ORDER], *[new_v[n] for n in ORDER])
```

```python
import functools
import math

import numpy as np
import jax
import jax.numpy as jnp
from jax import lax
from jax.experimental import pallas as pl
from jax.experimental.pallas import tpu as pltpu

f32, bf16 = jnp.float32, jnp.bfloat16

D = 1024
EPS = 1e-6
NEG_INF = -1e30
CONV_K = 31
SSD_INNER = 2048
SSD_HEADS = 32
SSD_P = 64
SSD_GROUPS = 4
SSD_N = 128
SSD_K = 4
CHUNK = 128
SSD_XBC = 3072
ATTN_HEADS = 16
ATTN_KV = 4
ATTN_D = 64
BLK = 128
REL_BUCKETS = 32
XH = 4
XD = 256
MLP = 4096
MEM_LEN = 256

OFF_Z, OFF_XBC, OFF_DT, OFF_Q, OFF_K, OFF_V, OFF_GATE, IN_COLS = 2048, 4096, 7168, 7200, 8224, 8480, 8736, 11808
C_CONV, C_Z, C_XBC, C_Q, C_G, C_K, C_V, C_DT, PCOLS = 0, 2048, 4096, 7168, 8192, 11264, 11520, 11776, 12288

ADAM_LR, ADAM_B1, ADAM_B2, ADAM_EPS, ADAM_WD, ADAM_STEP = 0.001, 0.9, 0.999, 1e-08, 0.01, 10

VMEM_LIMIT_BYTES = 56 * 1024 * 1024
N_CHIPS = 4
BIG = ["w_in", "w_conv_out", "w_ssd_out", "w_attn_out", "w_mix_out", "w_xq", "w_xkv", "w_xo", "w_mlp_up", "w_mlp_down"]
WIDE = ["w_conv_out", "w_ssd_out", "w_attn_out", "w_mix_out", "w_xq", "w_xo", "w_mlp_up", "w_mlp_down"]
WIDE_ROWS = dict(w_conv_out=256, w_ssd_out=512, w_attn_out=256, w_mix_out=256, w_xq=256, w_xo=256, w_mlp_up=1024, w_mlp_down=1024)
WIDE_OFF = {n: sum(WIDE_ROWS[m] for m in WIDE[:k]) for k, n in enumerate(WIDE)}
WIDE_TOTAL = sum(WIDE_ROWS.values())
IN_SHARD = IN_COLS // N_CHIPS
ORIG_SEGS = [(0, OFF_DT, 0), (OFF_DT, OFF_Q, C_DT), (OFF_Q, OFF_K, C_Q), (OFF_K, OFF_GATE, C_K), (OFF_GATE, IN_COLS, C_G)]


def _params(sem=None):
    return pltpu.CompilerParams(dimension_semantics=sem, vmem_limit_bytes=VMEM_LIMIT_BYTES)


NN, NT, TN = ((1,), (0,)), ((1,), (1,)), ((0,), (0,))


def _dg(a, b, dims):
    return lax.dot_general(a.astype(bf16), b.astype(bf16), (dims, ((), ())), preferred_element_type=f32)


@jax.custom_vjp
def dot_nn(a, b):
    return _dg(a, b, NN)


dot_nn.defvjp(lambda a, b: (_dg(a, b, NN), (a, b)), lambda r, g: (_dg(g, r[1], NT), _dg(r[0], g, TN)))


@jax.custom_vjp
def dot_nt(a, b):
    return _dg(a, b, NT)


dot_nt.defvjp(lambda a, b: (_dg(a, b, NT), (a, b)), lambda r, g: (_dg(g, r[1], NN), _dg(g, r[0], TN)))


@jax.custom_vjp
def dot_tn(a, b):
    return _dg(a, b, TN)


dot_tn.defvjp(lambda a, b: (_dg(a, b, TN), (a, b)), lambda r, g: (_dg(r[1], g, NT), _dg(r[0], g, NN)))


def _tri(n, upper):
    r = lax.broadcasted_iota(jnp.int32, (n, n), 0)
    c = lax.broadcasted_iota(jnp.int32, (n, n), 1)
    return jnp.where((c >= r) if upper else (r >= c), 1.0, 0.0).astype(f32)


def _hdot(a, b):
    return lax.dot_general(a, b, (NN, ((), ())), preferred_element_type=f32, precision=lax.Precision.HIGHEST)


@jax.custom_vjp
def cumsum_rows(x):
    return _hdot(_tri(x.shape[0], False), x)


cumsum_rows.defvjp(lambda x: (_hdot(_tri(x.shape[0], False), x), None),
                   lambda _, g: (_hdot(_tri(g.shape[0], True), g),))


def _pick(n, prefs):
    for p in prefs:
        if n % p == 0:
            return p
    return n


def matmul(a, b, mode, name, out_dtype=f32, residual=None, relu2=None):
    if mode == "nn":
        (m, k), n = a.shape, b.shape[1]
    elif mode == "nt":
        (m, k), n = a.shape, b.shape[0]
    else:
        (k, m), n = a.shape, b.shape[1]
    tm = _pick(m, (1024, 512, 256))
    tn = _pick(n, (1024, 512, 256, 128))
    tk = _pick(k, (1024, 512))
    nk = k // tk
    dims = dict(nn=NN, nt=NT, tn=TN)[mode]
    a_spec = pl.BlockSpec((tk, tm), lambda i, j, l: (l, i)) if mode == "tn" else pl.BlockSpec((tm, tk), lambda i, j, l: (i, l))
    b_spec = pl.BlockSpec((tn, tk), lambda i, j, l: (j, l)) if mode == "nt" else pl.BlockSpec((tk, tn), lambda i, j, l: (l, j))
    o_spec = pl.BlockSpec((tm, tn), lambda i, j, l: (i, j))
    two_outs = isinstance(relu2, str)
    extra = residual if residual is not None else (None if two_outs or relu2 is None else relu2)
    has_extra = extra is not None
    n_out = 2 if two_outs else 1

    def body(*refs):
        a_ref, b_ref = refs[0], refs[1]
        e_ref = refs[2] if has_extra else None
        o_refs = refs[2 + has_extra:2 + has_extra + n_out]
        acc = refs[2 + has_extra + n_out]
        l = pl.program_id(2)

        @pl.when(l == 0)
        def _():
            acc[...] = jnp.zeros(acc.shape, f32)

        acc[...] += _dg(a_ref[...], b_ref[...], dims)

        @pl.when(l == nk - 1)
        def _():
            r = acc[...]
            if residual is not None:
                r = r + e_ref[...]
            elif has_extra:
                r = r * (2.0 * jnp.maximum(e_ref[...], 0.0))
            o_refs[0][...] = r.astype(o_refs[0].dtype)
            if two_outs:
                o_refs[1][...] = jnp.square(jnp.maximum(r, 0.0)).astype(o_refs[1].dtype)

    ins = [a, b] + ([extra] if has_extra else [])
    shapes = [jax.ShapeDtypeStruct((m, n), out_dtype)] + ([jax.ShapeDtypeStruct((m, n), bf16)] if two_outs else [])
    res = pl.pallas_call(
        body, name=name, grid=(m // tm, n // tn, nk),
        in_specs=[a_spec, b_spec] + ([o_spec] if has_extra else []),
        out_specs=[o_spec] * n_out, out_shape=shapes,
        scratch_shapes=[pltpu.VMEM((tm, tn), f32)],
        compiler_params=_params(("parallel", "parallel", "arbitrary")),
    )(*ins)
    return res if two_outs else res[0]


PIECE = 1024


def matmul_pieces(pieces, other, mode, name):
    t_len = pieces[0].shape[0]
    counts = [p.shape[1] // PIECE for p in pieces]
    firsts = [sum(counts[:k]) for k in range(len(pieces))]
    total = sum(counts)
    npc = len(pieces)
    if mode == "nt":
        n = other.shape[0]
        tm, tn = _pick(t_len, (512, 256)), _pick(n, (1024, 512, 256))
        grid, steps, axis = (t_len // tm, n // tn, total), total, 2
        p_specs = [pl.BlockSpec((tm, PIECE), functools.partial(lambda i, j, l, f0, cnt: (i, jnp.clip(l - f0, 0, cnt - 1)), f0=f0, cnt=cnt))
                   for f0, cnt in zip(firsts, counts)]
        o_in_spec = pl.BlockSpec((tn, PIECE), lambda i, j, l: (j, l))
        out_spec, out_shape, acc_shape = pl.BlockSpec((tm, tn), lambda i, j, l: (i, j)), (t_len, n), (tm, tn)
    else:
        m = other.shape[1]
        tm, tk = _pick(m, (1024, 512, 256)), _pick(t_len, (512, 256))
        grid, steps, axis = (m // tm, total, t_len // tk), t_len // tk, 1

        def p_map(i, j, l, f0, cnt):
            mine = (j >= f0) & (j < f0 + cnt)
            return jnp.where(mine, l, 0), jnp.clip(j - f0, 0, cnt - 1)

        p_specs = [pl.BlockSpec((tk, PIECE), functools.partial(p_map, f0=f0, cnt=cnt)) for f0, cnt in zip(firsts, counts)]
        o_in_spec = pl.BlockSpec((tk, tm), lambda i, j, l: (l, i))
        out_spec, out_shape, acc_shape = pl.BlockSpec((tm, PIECE), lambda i, j, l: (i, j)), (m, total * PIECE), (tm, PIECE)

    def body(*refs):
        p_refs, o_ref, out_ref, acc = refs[:npc], refs[npc], refs[npc + 1], refs[npc + 2]
        l = pl.program_id(2)
        which = pl.program_id(axis)

        @pl.when(l == 0)
        def _():
            acc[...] = jnp.zeros(acc.shape, f32)

        for p_ref, f0, cnt in zip(p_refs, firsts, counts):
            @pl.when((which >= f0) & (which < f0 + cnt))
            def _(p_ref=p_ref):
                if mode == "nt":
                    acc[...] += _dg(p_ref[...], o_ref[...], NT)
                else:
                    acc[...] += _dg(o_ref[...], p_ref[...], TN)

        @pl.when(l == steps - 1)
        def _():
            out_ref[...] = acc[...]

    return pl.pallas_call(
        body, name=name, grid=grid, in_specs=p_specs + [o_in_spec], out_specs=out_spec,
        out_shape=jax.ShapeDtypeStruct(out_shape, f32), scratch_shapes=[pltpu.VMEM(acc_shape, f32)],
        compiler_params=_params(("parallel", "parallel", "arbitrary")),
    )(*pieces, other)


def _tile_in_specs(tiled, params, tile):
    specs = [pl.BlockSpec((tile, cols), functools.partial(lambda i, cb: (i, cb), cb=cb)) for (_, cols, cb) in tiled]
    specs += [pl.BlockSpec(p.shape, lambda i: (0, 0)) for p in params]
    return specs


def tile_fwd(name, f, tiled, params, outs, tile):
    rows = tiled[0][0].shape[0]
    nt, npar = len(tiled), len(params)

    def body(*refs):
        xs = [r[...].astype(f32) for r in refs[:nt + npar]]
        res = f(*xs)
        for o_ref, o in zip(refs[nt + npar:], res):
            o_ref[...] = o.astype(o_ref.dtype)

    return pl.pallas_call(
        body, name=name, grid=(rows // tile,),
        in_specs=_tile_in_specs(tiled, params, tile),
        out_specs=[pl.BlockSpec((tile, c), lambda i: (i, 0)) for c, _ in outs],
        out_shape=[jax.ShapeDtypeStruct((rows, c), dt) for c, dt in outs],
        compiler_params=_params(("parallel",)),
    )(*[t[0] for t in tiled], *params)


def tile_bwd(name, f, tiled, params, cots, d_dtypes, tile, addend=None):
    rows = tiled[0][0].shape[0]
    nt, npar, nc = len(tiled), len(params), len(cots)
    has_add = addend is not None

    def body(*refs):
        i = pl.program_id(0)
        xs = [r[...].astype(f32) for r in refs[:nt + npar]]
        gs = tuple(r[...].astype(f32) for r in refs[nt + npar:nt + npar + nc])
        pos = nt + npar + nc
        add_ref = refs[pos] if has_add else None
        pos += has_add
        dt_refs, dp_refs = refs[pos:pos + nt], refs[pos + nt:pos + nt + npar]
        _, vjp = jax.vjp(f, *xs)
        ds = vjp(gs)
        for k in range(nt):
            d = ds[k]
            if has_add and k == 0:
                d = d + add_ref[...]
            dt_refs[k][...] = d.astype(dt_refs[k].dtype)

        @pl.when(i == 0)
        def _():
            for r in dp_refs:
                r[...] = jnp.zeros(r.shape, f32)

        for k in range(npar):
            dp_refs[k][...] += ds[nt + k]

    in_specs = _tile_in_specs(tiled, params, tile)
    in_specs += [pl.BlockSpec((tile, c.shape[1]), lambda i: (i, 0)) for c in cots]
    ins = [t[0] for t in tiled] + list(params) + list(cots)
    if has_add:
        in_specs.append(pl.BlockSpec((tile, addend.shape[1]), lambda i: (i, 0)))
        ins.append(addend)
    out_specs = [pl.BlockSpec((tile, cols), lambda i: (i, 0)) for (_, cols, _) in tiled]
    out_specs += [pl.BlockSpec(p.shape, lambda i: (0, 0)) for p in params]
    out_shape = [jax.ShapeDtypeStruct((rows, cols), dt) for (_, cols, _), dt in zip(tiled, d_dtypes)]
    out_shape += [jax.ShapeDtypeStruct(p.shape, f32) for p in params]
    res = pl.pallas_call(
        body, name=name, grid=(rows // tile,), in_specs=in_specs, out_specs=out_specs, out_shape=out_shape,
        compiler_params=_params(("arbitrary",)),
    )(*ins)
    return res[:nt], res[nt:]


def f_rms(h, g):
    return (h * lax.rsqrt(jnp.mean(h * h, axis=-1, keepdims=True) + EPS) * g,)


def f_ln_silu(x, g, b):
    mu = jnp.mean(x, axis=-1, keepdims=True)
    xc = x - mu
    y = xc * lax.rsqrt(jnp.mean(xc * xc, axis=-1, keepdims=True) + EPS) * g + b
    return (y * jax.nn.sigmoid(y),)


def f_merge(pa, pb, pc, ya, yb, yc, gb):
    ga = jax.nn.sigmoid(pa + gb[:, 0:D])
    gb_ = jax.nn.sigmoid(pb + gb[:, D:2 * D])
    gc = jax.nn.sigmoid(pc + gb[:, 2 * D:3 * D])
    return (ga * ya + gb_ * yb + gc * yc,)


HALO = 32
CONV_CB = 256
SUB = 128


def _silu_grad(p):
    s = jax.nn.sigmoid(p)
    return s * (1.0 + p * (1.0 - s))


def _fill_shifted(buf, copies, length):
    for k in range(1, 8):
        for r0 in range(0, length - 8, SUB):
            n = min(SUB, length - 8 - r0)
            copies[k - 1, pl.ds(r0, n), :] = buf[pl.ds(r0 + k, n), :]


def _rows(buf, copies, off, n):
    if copies is None or off % 8 == 0:
        return buf[pl.ds(off, n), :]
    return copies[off % 8 - 1, pl.ds(off - off % 8, n), :]


def conv_fwd(name, src, col0, gate_col0, w, b, glu, silu):
    t_len = src.shape[0]
    k_taps, c = w.shape
    tt = min(512, t_len)
    nt_ = t_len // tt
    cb0, gb0 = col0 // CONV_CB, gate_col0 // CONV_CB
    many_taps = k_taps > 8

    def body(*refs):
        pos = 0
        x_cur, x_prev = refs[0], refs[1]
        pos = 2
        if glu:
            g_cur, g_prev = refs[2], refs[3]
            pos = 4
        w_ref, b_ref = refs[pos], refs[pos + 1]
        pre_ref = refs[pos + 2]
        act_ref = refs[pos + 3] if silu else None
        xp = refs[-2] if many_taps else refs[-1]
        xs = refs[-1] if many_taps else None
        t = pl.program_id(1)
        cur = x_cur[...]
        tail = x_prev[...]
        if glu:
            cur = cur * jax.nn.sigmoid(g_cur[...])
            tail = tail * jax.nn.sigmoid(g_prev[...])
        xp[pl.ds(0, HALO), :] = jnp.where(t > 0, tail, 0.0)
        xp[pl.ds(HALO, tt), :] = cur
        if many_taps:
            _fill_shifted(xp, xs, HALO + tt)
        for s in range(tt // SUB):
            acc = jnp.broadcast_to(b_ref[...], (SUB, CONV_CB))
            for j in range(k_taps):
                acc = acc + _rows(xp, xs, s * SUB + HALO - (k_taps - 1) + j, SUB) * w_ref[pl.ds(j, 1), :]
            pre_ref[pl.ds(s * SUB, SUB), :] = acc
            if silu:
                act_ref[pl.ds(s * SUB, SUB), :] = acc * jax.nn.sigmoid(acc)

    blk = lambda off: pl.BlockSpec((tt, CONV_CB), functools.partial(lambda j, t, off: (t, off + j), off=off))
    prev = lambda off: pl.BlockSpec((HALO, CONV_CB), functools.partial(lambda j, t, off: (jnp.maximum(t * (tt // HALO) - 1, 0), off + j), off=off))
    in_specs, ins = [blk(cb0), prev(cb0)], [src, src]
    if glu:
        in_specs += [blk(gb0), prev(gb0)]
        ins += [src, src]
    in_specs += [pl.BlockSpec((k_taps, CONV_CB), lambda j, t: (0, j)), pl.BlockSpec((1, CONV_CB), lambda j, t: (0, j))]
    ins += [w, b]
    o_spec = pl.BlockSpec((tt, CONV_CB), lambda j, t: (t, j))
    n_out = 2 if silu else 1
    res = pl.pallas_call(
        body, name=name, grid=(c // CONV_CB, nt_), in_specs=in_specs, out_specs=[o_spec] * n_out,
        out_shape=[jax.ShapeDtypeStruct((t_len, c), f32)] * n_out,
        scratch_shapes=[pltpu.VMEM((HALO + tt, CONV_CB), f32)] + ([pltpu.VMEM((7, HALO + tt, CONV_CB), f32)] if many_taps else []),
        compiler_params=_params(("parallel", "arbitrary")),
    )(*ins)
    return res


def conv_bwd(name, dy, pre, src, col0, gate_col0, w, glu, silu, out_dtype):
    t_len = src.shape[0]
    k_taps, c = w.shape
    tt = min(512, t_len)
    nt_ = t_len // tt
    cb0, gb0 = col0 // CONV_CB, gate_col0 // CONV_CB
    many_taps = k_taps > 8

    def body(*refs):
        dy_cur, dy_next = refs[0], refs[1]
        pos = 2
        if silu:
            p_cur, p_next = refs[2], refs[3]
            pos = 4
        x_cur, x_prev = refs[pos], refs[pos + 1]
        pos += 2
        if glu:
            g_cur, g_prev = refs[pos], refs[pos + 1]
            pos += 2
        w_ref = refs[pos]
        pos += 1
        n_dx = 2 if glu else 1
        dx_refs = refs[pos:pos + n_dx]
        dw_ref, db_ref = refs[pos + n_dx], refs[pos + n_dx + 1]
        dp, xp = (refs[-4], refs[-3]) if many_taps else (refs[-2], refs[-1])
        ds_, xs = (refs[-2], refs[-1]) if many_taps else (None, None)
        t = pl.program_id(1)

        dcur = dy_cur[...]
        dhead = dy_next[...]
        if silu:
            dcur = dcur * _silu_grad(p_cur[...])
            dhead = dhead * _silu_grad(p_next[...])
        dp[pl.ds(0, tt), :] = dcur
        dp[pl.ds(tt, HALO), :] = jnp.where(t < nt_ - 1, dhead, 0.0)
        cur = x_cur[...]
        tail = x_prev[...]
        if glu:
            cur = cur * jax.nn.sigmoid(g_cur[...])
            tail = tail * jax.nn.sigmoid(g_prev[...])
        xp[pl.ds(0, HALO), :] = jnp.where(t > 0, tail, 0.0)
        xp[pl.ds(HALO, tt), :] = cur

        if many_taps:
            _fill_shifted(dp, ds_, tt + HALO)
            _fill_shifted(xp, xs, HALO + tt)

        @pl.when(t == 0)
        def _():
            dw_ref[...] = jnp.zeros(dw_ref.shape, f32)
            db_ref[...] = jnp.zeros(db_ref.shape, f32)

        for s in range(tt // SUB):
            d_sub = dp[pl.ds(s * SUB, SUB), :]
            acc = jnp.zeros((SUB, CONV_CB), f32)
            for j in range(k_taps):
                acc = acc + _rows(dp, ds_, s * SUB + (k_taps - 1) - j, SUB) * w_ref[pl.ds(j, 1), :]
                x_sub = _rows(xp, xs, s * SUB + HALO - (k_taps - 1) + j, SUB)
                dw_ref[pl.ds(j, 1), :] += jnp.sum(d_sub * x_sub, axis=0, keepdims=True)
            db_ref[...] += jnp.sum(d_sub, axis=0, keepdims=True)
            if glu:
                a = x_cur[pl.ds(s * SUB, SUB), :]
                sg = jax.nn.sigmoid(g_cur[pl.ds(s * SUB, SUB), :])
                dx_refs[0][pl.ds(s * SUB, SUB), :] = (acc * sg).astype(out_dtype)
                dx_refs[1][pl.ds(s * SUB, SUB), :] = (acc * a * sg * (1.0 - sg)).astype(out_dtype)
            else:
                dx_refs[0][pl.ds(s * SUB, SUB), :] = acc.astype(out_dtype)

    blk = lambda off: pl.BlockSpec((tt, CONV_CB), functools.partial(lambda j, t, off: (t, off + j), off=off))
    prev = lambda off: pl.BlockSpec((HALO, CONV_CB), functools.partial(lambda j, t, off: (jnp.maximum(t * (tt // HALO) - 1, 0), off + j), off=off))
    nxt = pl.BlockSpec((HALO, CONV_CB), lambda j, t: (jnp.minimum((t + 1) * (tt // HALO), t_len // HALO - 1), j))
    in_specs, ins = [blk(0), nxt], [dy, dy]
    if silu:
        in_specs += [blk(0), nxt]
        ins += [pre, pre]
    in_specs += [blk(cb0), prev(cb0)]
    ins += [src, src]
    if glu:
        in_specs += [blk(gb0), prev(gb0)]
        ins += [src, src]
    in_specs.append(pl.BlockSpec((k_taps, CONV_CB), lambda j, t: (0, j)))
    ins.append(w)
    n_dx = 2 if glu else 1
    o_spec = pl.BlockSpec((tt, CONV_CB), lambda j, t: (t, j))
    out_specs = [o_spec] * n_dx + [pl.BlockSpec((k_taps, CONV_CB), lambda j, t: (0, j)), pl.BlockSpec((1, CONV_CB), lambda j, t: (0, j))]
    out_shape = [jax.ShapeDtypeStruct((t_len, c), out_dtype)] * n_dx + [jax.ShapeDtypeStruct((k_taps, c), f32), jax.ShapeDtypeStruct((1, c), f32)]
    return pl.pallas_call(
        body, name=name, grid=(c // CONV_CB, nt_), in_specs=in_specs, out_specs=out_specs, out_shape=out_shape,
        scratch_shapes=[pltpu.VMEM((tt + HALO, CONV_CB), f32)] * 2 + ([pltpu.VMEM((7, tt + HALO, CONV_CB), f32)] * 2 if many_taps else []),
        compiler_params=_params(("parallel", "arbitrary")),
    )(*ins)


GH = SSD_HEADS // SSD_GROUPS
GW = GH * SSD_P


def _softplus(x):
    return jnp.maximum(x, 0.0) + jnp.log1p(jnp.exp(-jnp.abs(x)))


def f_ssd(hbase, x, z, bm, cm, dtraw, s_in, dt_bias, a_log, dskip, ng):
    q = x.shape[0]
    dt = _softplus(dtraw + dt_bias)
    da = dt * (-jnp.exp(a_log))
    cs = cumsum_rows(da)
    g_cb = dot_nt(cm, bm)
    lane = lax.broadcasted_iota(jnp.int32, (q, 128), 1)
    lane1 = lax.broadcasted_iota(jnp.int32, (1, 128), 1)
    causal = lax.broadcasted_iota(jnp.int32, (q, q), 0) >= lax.broadcasted_iota(jnp.int32, (q, q), 1)
    last = lax.broadcasted_iota(jnp.int32, (q, GW), 0) == q - 1
    dt_cols, cs_cols, d_cols = [], [], []
    for r in range(GH):
        sel = lane == hbase + r
        dt_cols.append(jnp.sum(jnp.where(sel, dt, 0.0), axis=1, keepdims=True))
        cs_cols.append(jnp.sum(jnp.where(sel, cs, 0.0), axis=1, keepdims=True))
        d_cols.append(jnp.sum(jnp.where(lane1 == hbase + r, dskip, 0.0), axis=1, keepdims=True))
    spread = lambda cols: jnp.concatenate([jnp.broadcast_to(c, (c.shape[0], SSD_P)) for c in cols], axis=1)
    dt_x, cs_x, d_x = spread(dt_cols), spread(cs_cols), spread(d_cols)
    cs_last = jnp.sum(jnp.where(last, cs_x, 0.0), axis=0, keepdims=True)
    xdt = x * dt_x
    y_diag = []
    for r in range(GH):
        m1 = jnp.broadcast_to(cs_cols[r], (q, q))
        decay = jnp.where(causal, jnp.exp(jnp.where(causal, m1 - m1.T, 0.0)), 0.0)
        y_diag.append(dot_nn(g_cb * decay, xdt[:, r * SSD_P:(r + 1) * SSD_P]))
    s_c = dot_tn(bm, xdt * jnp.exp(cs_last - cs_x))
    y_off = dot_nn(cm, s_in) * jnp.exp(cs_x)
    y = jnp.concatenate(y_diag, axis=1) + y_off + x * d_x
    s_out = s_in * jnp.exp(cs_last) + s_c
    y = y * (z * jax.nn.sigmoid(z))
    y = y * lax.rsqrt(jnp.mean(y * y, axis=-1, keepdims=True) + EPS) * ng
    return y, s_out


GPS_FWD, GPS_BWD = 2, 1


def _ssd_in_specs(cmap, gps):
    PW, PN = gps * GW, gps * SSD_N
    return [
        pl.BlockSpec((CHUNK, PW), lambda g, c: (cmap(c), g)),
        pl.BlockSpec((CHUNK, PW), lambda g, c: (cmap(c), C_Z // PW + g)),
        pl.BlockSpec((CHUNK, PN), lambda g, c: (cmap(c), SSD_INNER // PN + g)),
        pl.BlockSpec((CHUNK, PN), lambda g, c: (cmap(c), (SSD_INNER + SSD_GROUPS * SSD_N) // PN + g)),
        pl.BlockSpec((CHUNK, 128), lambda g, c: (cmap(c), C_DT // 128)),
    ]


def _ssd_param_specs(gps):
    return [pl.BlockSpec((1, 128), lambda g, c: (0, 0))] * 3 + [pl.BlockSpec((1, gps * GW), lambda g, c: (0, g))]


def _cols(ref, k, width):
    return ref[:, pl.ds(k * width, width)]


def ssd_fwd(name, xact, proj, dt_bias, a_log, dskip, ng, ex=None):
    t_len = xact.shape[0]
    nc = t_len // CHUNK
    h = _hosted(ex)
    GPS = GPS_FWD
    PW, PN, NPAIR = GPS * GW, GPS * SSD_N, SSD_GROUPS // GPS

    def body(*refs):
        x_ref, z_ref, b_ref, c_ref, dt_ref, p1, p2, p3, p4 = refs[:9]
        y_ref, st_ref = refs[9 + h.n_in:11 + h.n_in]
        s_scr = refs[11 + h.n_in + h.n_out]
        g, c = pl.program_id(0), pl.program_id(1)
        h.at_start(refs, 9, 11 + h.n_in, (g == 0) & (c == 0))

        @pl.when(c == 0)
        def _():
            s_scr[...] = jnp.zeros(s_scr.shape, f32)

        for k in range(GPS):
            s_in = s_scr[k]
            st_ref[k, 0] = s_in
            y, s_out = f_ssd((g * GPS + k) * GH, _cols(x_ref, k, GW), _cols(z_ref, k, GW), _cols(b_ref, k, SSD_N),
                             _cols(c_ref, k, SSD_N), dt_ref[...], s_in, p1[...], p2[...], p3[...], _cols(p4, k, GW))
            y_ref[:, pl.ds(k * GW, GW)] = y.astype(y_ref.dtype)
            s_scr[k] = s_out
        h.at_end(refs, 9, 11 + h.n_in, (g == NPAIR - 1) & (c == nc - 1))

    res = pl.pallas_call(
        body, name=name, grid=(NPAIR, nc),
        in_specs=_ssd_in_specs(lambda c: c, GPS) + _ssd_param_specs(GPS) + h.in_specs,
        out_specs=[pl.BlockSpec((CHUNK, PW), lambda g, c: (c, g)),
                   pl.BlockSpec((GPS, 1, SSD_N, GW), lambda g, c: (g, c, 0, 0))] + h.out_specs,
        out_shape=[jax.ShapeDtypeStruct((t_len, SSD_INNER), bf16),
                   jax.ShapeDtypeStruct((SSD_GROUPS, nc, SSD_N, GW), f32)] + h.out_shape,
        scratch_shapes=[pltpu.VMEM((GPS, SSD_N, GW), f32)] + h.scratch,
        input_output_aliases=h.aliases(9, 2),
        compiler_params=_params(("arbitrary", "arbitrary")),
    )(xact, proj, xact, xact, proj, dt_bias, a_log, dskip, ng, *h.ins)
    return res[0], res[1], list(res[2:])


def ssd_bwd(name, xact, proj, states, dy, dt_bias, a_log, dskip, ng, ex=None):
    t_len = xact.shape[0]
    nc = t_len // CHUNK
    rev = lambda c: nc - 1 - c
    h = _hosted(ex)
    GPS = GPS_BWD
    PW, PN, NPAIR = GPS * GW, GPS * SSD_N, SSD_GROUPS // GPS

    def body(*refs):
        x_ref, z_ref, b_ref, c_ref, dt_ref, st_ref, dy_ref, p1, p2, p3, p4 = refs[:11]
        dx_ref, db_ref, dc_ref, dz_ref, ddt_ref, d1, d2, d3, d4 = refs[11 + h.n_in:20 + h.n_in]
        ds_scr = refs[20 + h.n_in + h.n_out]
        g, c = pl.program_id(0), pl.program_id(1)
        h.at_start(refs, 11, 20 + h.n_in, (g == 0) & (c == 0))

        @pl.when(c == 0)
        def _():
            ds_scr[...] = jnp.zeros(ds_scr.shape, f32)
            d4[...] = jnp.zeros(d4.shape, f32)

        @pl.when((c == 0) & (g == 0))
        def _():
            for r in (d1, d2, d3):
                r[...] = jnp.zeros(r.shape, f32)

        for k in range(GPS):
            fn = functools.partial(f_ssd, (g * GPS + k) * GH)
            _, vjp = jax.vjp(fn, _cols(x_ref, k, GW), _cols(z_ref, k, GW), _cols(b_ref, k, SSD_N), _cols(c_ref, k, SSD_N),
                             dt_ref[...], st_ref[k, 0], p1[...], p2[...], p3[...], _cols(p4, k, GW))
            dx, dz, db, dc, ddt, ds_in, e1, e2, e3, e4 = vjp((_cols(dy_ref, k, GW).astype(f32), ds_scr[k]))
            dx_ref[:, pl.ds(k * GW, GW)] = dx
            dz_ref[:, pl.ds(k * GW, GW)] = dz.astype(dz_ref.dtype)
            db_ref[:, pl.ds(k * SSD_N, SSD_N)] = db
            dc_ref[:, pl.ds(k * SSD_N, SSD_N)] = dc
            ddt_ref[k] = ddt
            ds_scr[k] = ds_in
            d1[...] += e1
            d2[...] += e2
            d3[...] += e3
            d4[:, pl.ds(k * GW, GW)] += e4
        h.at_end(refs, 11, 20 + h.n_in, (g == NPAIR - 1) & (c == nc - 1))

    res = pl.pallas_call(
        body, name=name, grid=(NPAIR, nc),
        in_specs=_ssd_in_specs(rev, GPS) + [
            pl.BlockSpec((GPS, 1, SSD_N, GW), lambda g, c: (g, rev(c), 0, 0)),
            pl.BlockSpec((CHUNK, PW), lambda g, c: (rev(c), g)),
        ] + _ssd_param_specs(GPS) + h.in_specs,
        out_specs=[
            pl.BlockSpec((CHUNK, PW), lambda g, c: (rev(c), g)),
            pl.BlockSpec((CHUNK, PN), lambda g, c: (rev(c), g)),
            pl.BlockSpec((CHUNK, PN), lambda g, c: (rev(c), g)),
            pl.BlockSpec((CHUNK, PW), lambda g, c: (rev(c), g)),
            pl.BlockSpec((GPS, CHUNK, 128), lambda g, c: (g, rev(c), 0)),
        ] + _ssd_param_specs(GPS) + h.out_specs,
        out_shape=[
            jax.ShapeDtypeStruct((t_len, SSD_INNER), f32),
            jax.ShapeDtypeStruct((t_len, SSD_GROUPS * SSD_N), f32),
            jax.ShapeDtypeStruct((t_len, SSD_GROUPS * SSD_N), f32),
            jax.ShapeDtypeStruct((t_len, SSD_INNER), bf16),
            jax.ShapeDtypeStruct((SSD_GROUPS, t_len, 128), f32),
            jax.ShapeDtypeStruct((1, 128), f32), jax.ShapeDtypeStruct((1, 128), f32), jax.ShapeDtypeStruct((1, 128), f32),
            jax.ShapeDtypeStruct((1, SSD_INNER), f32),
        ] + h.out_shape,
        scratch_shapes=[pltpu.VMEM((GPS, SSD_N, GW), f32)] + h.scratch,
        input_output_aliases=h.aliases(11, 9),
        compiler_params=_params(("arbitrary", "arbitrary")),
    )(xact, proj, xact, xact, proj, states, dy, dt_bias, a_log, dskip, ng, *h.ins)
    return list(res[:9]), list(res[9:])


def _head_norm(t, g):
    return t * lax.rsqrt(jnp.mean(t * t, axis=-1, keepdims=True) + EPS) * g


def f_swa(has_prev, q, kp, kc, vp, vc, qg, kg, sinks, *bias):
    rep = ATTN_HEADS // ATTN_KV
    qi = lax.broadcasted_iota(jnp.int32, (rep * BLK, BLK), 0) & (BLK - 1)
    ki = lax.broadcasted_iota(jnp.int32, (rep * BLK, BLK), 1)
    mask_p = (ki > qi) & has_prev
    mask_c = ki <= qi
    lane1 = lax.broadcasted_iota(jnp.int32, (1, 128), 1)
    scale = ATTN_D ** -0.5
    outs = []
    for g in range(ATTN_KV):
        sl = slice(g * ATTN_D, (g + 1) * ATTN_D)
        kpn, kcn = _head_norm(kp[:, sl], kg), _head_norm(kc[:, sl], kg)
        heads = range(g * rep, (g + 1) * rep)
        qn = _head_norm(jnp.concatenate([q[:, h * ATTN_D:(h + 1) * ATTN_D] for h in heads], axis=0), qg)
        lp = jnp.where(mask_p, dot_nt(qn, kpn) * scale + bias[g][:, :BLK], NEG_INF)
        lc = jnp.where(mask_c, dot_nt(qn, kcn) * scale + bias[g][:, BLK:], NEG_INF)
        sink = jnp.concatenate([jnp.broadcast_to(jnp.sum(jnp.where(lane1 == h, sinks, 0.0), axis=1, keepdims=True), (BLK, 1))
                                for h in heads], axis=0)
        m = lax.stop_gradient(jnp.maximum(jnp.maximum(jnp.max(lp, axis=-1, keepdims=True), jnp.max(lc, axis=-1, keepdims=True)), sink))
        pp, pc = jnp.exp(lp - m), jnp.exp(lc - m)
        den = jnp.sum(pp, axis=-1, keepdims=True) + jnp.sum(pc, axis=-1, keepdims=True) + jnp.exp(sink - m)
        o = (dot_nn(pp, vp[:, sl]) + dot_nn(pc, vc[:, sl])) * (1.0 / den)
        outs += [o[r * BLK:(r + 1) * BLK] for r in range(rep)]
    return jnp.concatenate(outs, axis=1)


_SWA_BIAS_SHAPE = (ATTN_KV, ATTN_HEADS // ATTN_KV * BLK, 2 * BLK)
_SWA_BIAS_SPEC = pl.BlockSpec(_SWA_BIAS_SHAPE, lambda i: (0, 0, 0))


def _swa_in_specs():
    prev = lambda i: jnp.maximum(i - 1, 0)
    kw = ATTN_KV * ATTN_D
    return [
        pl.BlockSpec((BLK, D), lambda i: (i, C_Q // D)),
        pl.BlockSpec((BLK, kw), lambda i: (prev(i), C_K // kw)),
        pl.BlockSpec((BLK, kw), lambda i: (i, C_K // kw)),
        pl.BlockSpec((BLK, kw), lambda i: (prev(i), C_V // kw)),
        pl.BlockSpec((BLK, kw), lambda i: (i, C_V // kw)),
        pl.BlockSpec((1, ATTN_D), lambda i: (0, 0)),
        pl.BlockSpec((1, ATTN_D), lambda i: (0, 0)),
        pl.BlockSpec((1, 128), lambda i: (0, 0)),
        _SWA_BIAS_SPEC,
    ]


def swa_fwd(name, proj, qg, kg, sinks, bias, ex=None):
    t_len = proj.shape[0]
    nb = t_len // BLK
    h = _hosted(ex)

    def body(*refs):
        q_ref, kp_ref, kc_ref, vp_ref, vc_ref, qg_ref, kg_ref, s_ref, b_ref = refs[:9]
        o_ref = refs[9 + h.n_in]
        i = pl.program_id(0)
        h.at_start(refs, 9, 10 + h.n_in, i == 0)
        o = f_swa(i > 0, q_ref[...], kp_ref[...], kc_ref[...], vp_ref[...], vc_ref[...], qg_ref[...], kg_ref[...],
                  s_ref[...], *[b_ref[kv] for kv in range(ATTN_KV)])
        o_ref[...] = o.astype(o_ref.dtype)
        h.at_end(refs, 9, 10 + h.n_in, i == nb - 1)

    res = pl.pallas_call(
        body, name=name, grid=(nb,), in_specs=_swa_in_specs() + h.in_specs,
        out_specs=[pl.BlockSpec((BLK, D), lambda i: (i, 0))] + h.out_specs,
        out_shape=[jax.ShapeDtypeStruct((t_len, D), bf16)] + h.out_shape,
        scratch_shapes=h.scratch, input_output_aliases=h.aliases(9, 1),
        compiler_params=_params(("arbitrary",)),
    )(proj, proj, proj, proj, proj, qg, kg, sinks, bias.reshape(_SWA_BIAS_SHAPE), *h.ins)
    return res[0], list(res[1:])


def swa_bwd(name, proj, do, qg, kg, sinks, bias, ex=None):
    t_len = proj.shape[0]
    nb = t_len // BLK
    kw = ATTN_KV * ATTN_D
    h = _hosted(ex)

    def body(*refs):
        q_ref, kp_ref, kc_ref, vp_ref, vc_ref, qg_ref, kg_ref, s_ref, b_ref, do_ref = refs[:10]
        dq_ref, dkp_ref, dkc_ref, dvp_ref, dvc_ref, dqg_ref, dkg_ref, ds_ref, db_ref = refs[10 + h.n_in:19 + h.n_in]
        i = pl.program_id(0)
        h.at_start(refs, 10, 19 + h.n_in, i == 0)

        @pl.when(i == 0)
        def _():
            for r in (dqg_ref, dkg_ref, ds_ref, db_ref):
                r[...] = jnp.zeros(r.shape, f32)

        fn = functools.partial(f_swa, i > 0)
        _, vjp = jax.vjp(fn, q_ref[...], kp_ref[...], kc_ref[...], vp_ref[...], vc_ref[...], qg_ref[...], kg_ref[...],
                         s_ref[...], *[b_ref[kv] for kv in range(ATTN_KV)])
        ds = vjp(do_ref[...].astype(f32))
        dq_ref[...] = ds[0].astype(dq_ref.dtype)
        dkp_ref[...] = ds[1]
        dkc_ref[...] = ds[2]
        dvp_ref[...] = ds[3]
        dvc_ref[...] = ds[4]
        dqg_ref[...] += ds[5]
        dkg_ref[...] += ds[6]
        ds_ref[...] += ds[7]
        for kv in range(ATTN_KV):
            db_ref[kv] += ds[8 + kv]
        h.at_end(refs, 10, 19 + h.n_in, i == nb - 1)

    row = lambda w: pl.BlockSpec((BLK, w), lambda i: (i, 0))
    res = pl.pallas_call(
        body, name=name, grid=(nb,), in_specs=_swa_in_specs() + [row(D)] + h.in_specs,
        out_specs=[row(D), row(kw), row(kw), row(kw), row(kw),
                   pl.BlockSpec((1, ATTN_D), lambda i: (0, 0)), pl.BlockSpec((1, ATTN_D), lambda i: (0, 0)),
                   pl.BlockSpec((1, 128), lambda i: (0, 0)), _SWA_BIAS_SPEC] + h.out_specs,
        out_shape=[jax.ShapeDtypeStruct((t_len, D), bf16)] + [jax.ShapeDtypeStruct((t_len, kw), f32)] * 4
        + [jax.ShapeDtypeStruct((1, ATTN_D), f32)] * 2 + [jax.ShapeDtypeStruct((1, 128), f32),
                                                          jax.ShapeDtypeStruct(_SWA_BIAS_SHAPE, f32)] + h.out_shape,
        scratch_shapes=h.scratch, input_output_aliases=h.aliases(10, 9),
        compiler_params=_params(("arbitrary",)),
    )(proj, proj, proj, proj, proj, qg, kg, sinks, bias.reshape(_SWA_BIAS_SHAPE), do, *h.ins)
    return list(res[:8]) + [res[8].reshape(ATTN_HEADS, BLK, 2 * BLK)], list(res[9:])


def _bucket_table():
    qi = np.arange(BLK)[:, None] + BLK
    kj = np.arange(2 * BLK)[None, :]
    dist = qi - kj
    max_exact = REL_BUCKETS // 2
    d = np.maximum(dist, 1).astype(np.float32)
    large = max_exact + (np.log(d / max_exact) / math.log(128 / max_exact) * (REL_BUCKETS - max_exact)).astype(np.int32)
    large = np.minimum(large, REL_BUCKETS - 1)
    return np.where(dist < max_exact, np.maximum(dist, 0), large).astype(np.int32)


def relbias_fwd(table, bucket):
    def body(t_ref, bk_ref, o_ref):
        bk = bk_ref[...]
        for h in range(ATTN_HEADS):
            acc = jnp.zeros((BLK, 2 * BLK), f32)
            for b in range(REL_BUCKETS):
                acc = jnp.where(bk == b, t_ref[b, h], acc)
            o_ref[h] = acc

    return pl.pallas_call(
        body, name="relbias_fwd", out_shape=jax.ShapeDtypeStruct((ATTN_HEADS, BLK, 2 * BLK), f32),
        in_specs=[pl.BlockSpec(memory_space=pltpu.SMEM), pl.BlockSpec(memory_space=pltpu.VMEM)],
        out_specs=pl.BlockSpec(memory_space=pltpu.VMEM),
    )(table, bucket)


def relbias_bwd(dbias, bucket):
    n = len(dbias)

    def body(*refs):
        bk = refs[n][...]
        o_ref = refs[n + 1]
        row = lax.broadcasted_iota(jnp.int32, (REL_BUCKETS, 128), 0)
        lane = lax.broadcasted_iota(jnp.int32, (REL_BUCKETS, 128), 1)
        res = jnp.zeros((REL_BUCKETS, 128), f32)
        for h in range(ATTN_HEADS):
            tot = refs[0][h]
            for k in range(1, n):
                tot = tot + refs[k][h]
            for b in range(REL_BUCKETS):
                part = jnp.sum(jnp.sum(jnp.where(bk == b, tot, 0.0), axis=1, keepdims=True), axis=0, keepdims=True)
                res = jnp.where((row == b) & (lane == h), part, res)
        o_ref[...] = res

    return pl.pallas_call(
        body, name="relbias_bwd", out_shape=jax.ShapeDtypeStruct((REL_BUCKETS, 128), f32),
        in_specs=[pl.BlockSpec(memory_space=pltpu.VMEM)] * (n + 1),
        out_specs=pl.BlockSpec(memory_space=pltpu.VMEM),
    )(*dbias, bucket)[:, :ATTN_HEADS]


XA_TILE = 512


def f_xattn(q, k, v, qg, kg):
    qn, kn = _head_norm(q, qg), _head_norm(k, kg)
    logits = dot_nt(qn, kn) * (XD ** -0.5)
    p = jnp.exp(logits - lax.stop_gradient(jnp.max(logits, axis=-1, keepdims=True)))
    return dot_nn(p * (1.0 / jnp.sum(p, axis=-1, keepdims=True)), v)


def _xa_in_specs(tq):
    return [
        pl.BlockSpec((tq, XD), lambda h, i: (i, h)),
        pl.BlockSpec((MEM_LEN, XD), lambda h, i: (0, h)),
        pl.BlockSpec((MEM_LEN, XD), lambda h, i: (0, XH + h)),
        pl.BlockSpec((1, XD), lambda h, i: (0, 0)),
        pl.BlockSpec((1, XD), lambda h, i: (0, 0)),
    ]


def xa_fwd(name, xq, kv, qg, kg):
    t_len = xq.shape[0]
    tq = min(XA_TILE, t_len)

    def body(q_ref, k_ref, v_ref, qg_ref, kg_ref, o_ref):
        o_ref[...] = f_xattn(q_ref[...], k_ref[...], v_ref[...], qg_ref[...], kg_ref[...]).astype(o_ref.dtype)

    return pl.pallas_call(
        body, name=name, grid=(XH, t_len // tq), in_specs=_xa_in_specs(tq),
        out_specs=pl.BlockSpec((tq, XD), lambda h, i: (i, h)), out_shape=jax.ShapeDtypeStruct((t_len, D), bf16),
        compiler_params=_params(("parallel", "parallel")),
    )(xq, kv, kv, qg, kg)


def xa_bwd(name, xq, kv, do, qg, kg):
    t_len = xq.shape[0]
    tq = min(XA_TILE, t_len)

    def body(q_ref, k_ref, v_ref, qg_ref, kg_ref, do_ref, dq_ref, dk_ref, dv_ref, dqg_ref, dkg_ref):
        h, i = pl.program_id(0), pl.program_id(1)

        @pl.when(i == 0)
        def _():
            dk_ref[...] = jnp.zeros(dk_ref.shape, f32)
            dv_ref[...] = jnp.zeros(dv_ref.shape, f32)

        @pl.when((i == 0) & (h == 0))
        def _():
            dqg_ref[...] = jnp.zeros(dqg_ref.shape, f32)
            dkg_ref[...] = jnp.zeros(dkg_ref.shape, f32)

        _, vjp = jax.vjp(f_xattn, q_ref[...], k_ref[...], v_ref[...], qg_ref[...], kg_ref[...])
        dq, dk, dv, dqg, dkg = vjp(do_ref[...].astype(f32))
        dq_ref[...] = dq.astype(dq_ref.dtype)
        dk_ref[...] += dk
        dv_ref[...] += dv
        dqg_ref[...] += dqg
        dkg_ref[...] += dkg

    return pl.pallas_call(
        body, name=name, grid=(XH, t_len // tq),
        in_specs=_xa_in_specs(tq) + [pl.BlockSpec((tq, XD), lambda h, i: (i, h))],
        out_specs=[pl.BlockSpec((tq, XD), lambda h, i: (i, h)),
                   pl.BlockSpec((MEM_LEN, XD), lambda h, i: (0, h)), pl.BlockSpec((MEM_LEN, XD), lambda h, i: (0, h)),
                   pl.BlockSpec((1, XD), lambda h, i: (0, 0)), pl.BlockSpec((1, XD), lambda h, i: (0, 0))],
        out_shape=[jax.ShapeDtypeStruct((t_len, D), bf16), jax.ShapeDtypeStruct((MEM_LEN, D), f32),
                   jax.ShapeDtypeStruct((MEM_LEN, D), f32), jax.ShapeDtypeStruct((1, XD), f32), jax.ShapeDtypeStruct((1, XD), f32)],
        compiler_params=_params(("arbitrary", "arbitrary")),
    )(xq, kv, kv, qg, kg, do)


def loss_head(y, target, tile):
    t_len = y.shape[0]

    def body(y_ref, t_ref, dy_ref, l_ref):
        i = pl.program_id(0)

        @pl.when(i == 0)
        def _():
            l_ref[...] = jnp.zeros(l_ref.shape, f32)

        err = y_ref[...] - t_ref[...]
        dy_ref[...] = err * (1.0 / D)
        l_ref[...] += 0.5 * jnp.sum(jnp.sum(err * err, axis=1, keepdims=True), axis=0, keepdims=True) * (1.0 / D)

    row = pl.BlockSpec((tile, D), lambda i: (i, 0))
    return pl.pallas_call(
        body, name="loss_head", grid=(t_len // tile,), in_specs=[row, row],
        out_specs=[row, pl.BlockSpec((8, 128), lambda i: (0, 0))],
        out_shape=[jax.ShapeDtypeStruct((t_len, D), f32), jax.ShapeDtypeStruct((8, 128), f32)],
        compiler_params=_params(("arbitrary",)),
    )(y, target)


ADAM_BLOCK_ELEMS = 512 * 1024


def adamw(name, w, g, m, v):
    rows, cols = w.shape
    tile = rows
    if rows * cols > ADAM_BLOCK_ELEMS:
        tile = _pick(rows, [t for t in (512, 256, 128, 64, 32, 16, 8) if t * cols <= ADAM_BLOCK_ELEMS])

    def body(w_ref, g_ref, m_ref, v_ref, d_ref, nm_ref, nv_ref):
        gg = g_ref[...]
        nm = ADAM_B1 * m_ref[...] + (1.0 - ADAM_B1) * gg
        nv = ADAM_B2 * v_ref[...] + (1.0 - ADAM_B2) * jnp.square(gg)
        m_hat = nm / (1.0 - ADAM_B1 ** ADAM_STEP)
        v_hat = nv / (1.0 - ADAM_B2 ** ADAM_STEP)
        d_ref[...] = -ADAM_LR * (m_hat / (jnp.sqrt(v_hat) + ADAM_EPS) + ADAM_WD * w_ref[...])
        nm_ref[...] = nm
        nv_ref[...] = nv

    spec = pl.BlockSpec((tile, cols), lambda i: (i, 0))
    return pl.pallas_call(
        body, name=name, grid=(rows // tile,), in_specs=[spec] * 4, out_specs=[spec] * 3,
        out_shape=[jax.ShapeDtypeStruct((rows, cols), f32)] * 3, compiler_params=_params(("parallel",)),
    )(w, g, m, v)


def adamw_layers(name, w, m, v, gsrcs, r0):
    depth, rows, cols = w.shape
    tile = _pick(rows, [t for t in (512, 256, 128, 64, 32, 16, 8) if t * cols <= ADAM_BLOCK_ELEMS and r0 % t == 0])

    def body(*refs):
        w_ref, m_ref, v_ref = refs[:3]
        g_refs = refs[3:3 + depth]
        go_ref, d_ref, nm_ref, nv_ref = refs[3 + depth:]
        layer = pl.program_id(0)
        gg = g_refs[0][...]
        for k in range(1, depth):
            gg = jnp.where(layer == k, g_refs[k][...], gg)
        nm = ADAM_B1 * m_ref[0] + (1.0 - ADAM_B1) * gg
        nv = ADAM_B2 * v_ref[0] + (1.0 - ADAM_B2) * jnp.square(gg)
        m_hat = nm / (1.0 - ADAM_B1 ** ADAM_STEP)
        v_hat = nv / (1.0 - ADAM_B2 ** ADAM_STEP)
        go_ref[0] = gg
        d_ref[0] = -ADAM_LR * (m_hat / (jnp.sqrt(v_hat) + ADAM_EPS) + ADAM_WD * w_ref[0])
        nm_ref[0] = nm
        nv_ref[0] = nv

    spec = pl.BlockSpec((1, tile, cols), lambda l, i: (l, i, 0))
    g_specs = [pl.BlockSpec((tile, cols), functools.partial(lambda l, i, k: (jnp.where(l == k, r0 // tile + i, 0), 0), k=k))
               for k in range(depth)]
    return pl.pallas_call(
        body, name=name, grid=(depth, rows // tile), in_specs=[spec] * 3 + g_specs, out_specs=[spec] * 4,
        out_shape=[jax.ShapeDtypeStruct(w.shape, f32)] * 4, compiler_params=_params(("parallel", "parallel")),
    )(w, m, v, *gsrcs)


MESH = pl.DeviceIdType.MESH
HBM_SPEC = pl.BlockSpec(memory_space=pltpu.HBM)


def _place():
    x, y, c = lax.axis_index("x"), lax.axis_index("y"), lax.axis_index("c")
    chips = [(1 - x, y), (x, 1 - y), (1 - x, 1 - y)]
    return x, y, c, chips


def _rcopy(src, dst, send_sems, recv_sems, k, to):
    return pltpu.make_async_remote_copy(src_ref=src, dst_ref=dst, send_sem=send_sems.at[k], recv_sem=recv_sems.at[k],
                                        device_id=to, device_id_type=MESH)


class Exchange:
    def __init__(self, ins, outs, n_sems, n_local, start, wait, aliases=None):
        self.ins, self.outs, self.n_sems, self.n_local = list(ins), list(outs), n_sems, n_local
        self.start, self.wait, self.aliases = start, wait, dict(aliases or {})


class _hosted:
    def __init__(self, ex):
        self.ex = ex
        self.ins = list(ex.ins) if ex else []
        self.out_shape = list(ex.outs) if ex else []
        self.n_in, self.n_out = len(self.ins), len(self.out_shape)
        self.in_specs, self.out_specs = [HBM_SPEC] * self.n_in, [HBM_SPEC] * self.n_out
        self.scratch = [pltpu.SemaphoreType.DMA((ex.n_sems,)), pltpu.SemaphoreType.DMA((ex.n_sems,)),
                        pltpu.SemaphoreType.DMA((max(ex.n_local, 1),))] if ex else []

    def aliases(self, first_in, first_out):
        return {first_in + a: first_out + b for a, b in self.ex.aliases.items()} if self.ex else {}

    def _args(self, refs, i0, o0):
        return refs[i0:i0 + self.n_in], refs[o0:o0 + self.n_out], refs[-3], refs[-2], refs[-1]

    def at_start(self, refs, i0, o0, pred):
        if self.ex is not None:
            pl.when(pred)(lambda: self.ex.start(*self._args(refs, i0, o0)))

    def at_end(self, refs, i0, o0, pred):
        if self.ex is not None:
            pl.when(pred)(lambda: self.ex.wait(*self._args(refs, i0, o0)))


def run_exchange(name, ex):
    h = _hosted(ex)

    def body(*refs):
        args = h._args(refs, 0, h.n_in)
        ex.start(*args)
        ex.wait(*args)

    res = pl.pallas_call(
        body, name=name, out_shape=h.out_shape, in_specs=h.in_specs, out_specs=h.out_specs, scratch_shapes=h.scratch,
        input_output_aliases=h.aliases(0, 0), compiler_params=pltpu.CompilerParams(has_side_effects=True),
    )(*h.ins)
    return list(res)


def _halves(ref, axis, c):
    hh = ref.shape[axis] // 2
    return pl.ds(pl.multiple_of(c * hh, 16), hh), pl.ds(pl.multiple_of((1 - c) * hh, 16), hh)


def ex_gather_ici(shards):
    n = len(shards)

    def copies(ins, outs, ssem, rsem, landing):
        x, y, c, chips = _place()
        me = 2 * x + y
        res = []
        for a, (s, o) in enumerate(zip(ins, outs)):
            mine, _ = _halves(s, 0, c)
            for j, (cx, cy) in enumerate(chips):
                slot = 2 * cx + cy if landing else me
                res.append(_rcopy(s.at[mine], o.at[slot, mine], ssem, rsem, 4 * a + j, (cx, cy, c)))
            res.append(_rcopy(s, o.at[me], ssem, rsem, 4 * a + 3, (x, y, 1 - c)))
        return res

    def start(ins, outs, ssem, rsem, lsem):
        for cp in copies(ins, outs, ssem, rsem, False):
            cp.start()

    def wait(ins, outs, ssem, rsem, lsem):
        for cp in copies(ins, outs, ssem, rsem, True):
            cp.wait_recv()
        for cp in copies(ins, outs, ssem, rsem, False):
            cp.wait_send()

    outs = [jax.ShapeDtypeStruct((N_CHIPS,) + s.shape, s.dtype) for s in shards]
    return Exchange(shards, outs, 4 * n, 0, start, wait)


def ex_gather_d2d(gathered):
    n = len(gathered)

    def copies(outs, ssem, rsem, landing):
        x, y, c, chips = _place()
        res = []
        for a, o in enumerate(outs):
            mine, theirs = _halves(o, 1, c)
            for j, (cx, cy) in enumerate(chips):
                rows = o.at[2 * cx + cy, theirs if landing else mine]
                res.append(_rcopy(rows, rows, ssem, rsem, 3 * a + j, (x, y, 1 - c)))
        return res

    def start(ins, outs, ssem, rsem, lsem):
        for cp in copies(outs, ssem, rsem, False):
            cp.start()

    def wait(ins, outs, ssem, rsem, lsem):
        for cp in copies(outs, ssem, rsem, True):
            cp.wait_recv()
        for cp in copies(outs, ssem, rsem, False):
            cp.wait_send()

    outs = [jax.ShapeDtypeStruct(g.shape, g.dtype) for g in gathered]
    return Exchange(gathered, outs, 3 * n, 0, start, wait, aliases={a: a for a in range(n)})


def ex_swap_halves(gs):
    n = len(gs)

    def copies(ins, outs, ssem, rsem):
        x, y, c, _ = _place()
        return [_rcopy(g.at[s, 1 - c], o.at[s], ssem, rsem, N_CHIPS * a + s, (x, y, 1 - c))
                for a, (g, o) in enumerate(zip(ins, outs)) for s in range(N_CHIPS)]

    def start(ins, outs, ssem, rsem, lsem):
        for cp in copies(ins, outs, ssem, rsem):
            cp.start()

    def wait(ins, outs, ssem, rsem, lsem):
        for cp in copies(ins, outs, ssem, rsem):
            cp.wait()

    outs = [jax.ShapeDtypeStruct((N_CHIPS,) + g.shape[2:], g.dtype) for g in gs]
    return Exchange(gs, outs, N_CHIPS * n, 0, start, wait)


def ex_scatter(ps):
    n = len(ps)

    def copies(ins, outs, ssem, rsem):
        x, y, c, chips = _place()
        return [_rcopy(p.at[2 * cx + cy], o.at[j], ssem, rsem, 3 * a + j, (cx, cy, c))
                for a, (p, o) in enumerate(zip(ins, outs)) for j, (cx, cy) in enumerate(chips)]

    def start(ins, outs, ssem, rsem, lsem):
        for cp in copies(ins, outs, ssem, rsem):
            cp.start()

    def wait(ins, outs, ssem, rsem, lsem):
        for cp in copies(ins, outs, ssem, rsem):
            cp.wait()

    outs = [jax.ShapeDtypeStruct((3,) + p.shape[1:], p.dtype) for p in ps]
    return Exchange(ps, outs, 3 * n, 0, start, wait)


def ex_share(rs):
    n = len(rs)

    def copies(ins, outs, ssem, rsem):
        x, y, c, _ = _place()
        return [_rcopy(r, o, ssem, rsem, a, (x, y, 1 - c)) for a, (r, o) in enumerate(zip(ins, outs))]

    def start(ins, outs, ssem, rsem, lsem):
        for cp in copies(ins, outs, ssem, rsem):
            cp.start()

    def wait(ins, outs, ssem, rsem, lsem):
        for cp in copies(ins, outs, ssem, rsem):
            cp.wait()

    outs = [jax.ShapeDtypeStruct(r.shape, r.dtype) for r in rs]
    return Exchange(rs, outs, n, 0, start, wait)


ADD_BLOCK_ELEMS = 384 * 1024


def _add_tile(rows, cols):
    return _pick(rows, [t for t in (1920, 960, 512, 384, 256, 128, 64, 32, 16) if t * cols <= ADD_BLOCK_ELEMS])


def add_halves(name, g, got, c_arr):
    _, _, rows, cols = g.shape
    tile = _add_tile(rows, cols)

    def body(c_ref, g_ref, r_ref, o_ref):
        o_ref[...] = (g_ref[0] + r_ref[...]).astype(o_ref.dtype)

    return pl.pallas_call(
        body, name=name,
        grid_spec=pltpu.PrefetchScalarGridSpec(
            num_scalar_prefetch=1, grid=(N_CHIPS, rows // tile),
            in_specs=[pl.BlockSpec((1, 1, tile, cols), lambda s, i, c_ref: (s, c_ref[0], i, 0)),
                      pl.BlockSpec((1, tile, cols), lambda s, i, c_ref: (s, i, 0))],
            out_specs=pl.BlockSpec((1, tile, cols), lambda s, i, c_ref: (s, i, 0))),
        out_shape=jax.ShapeDtypeStruct(got.shape, bf16), compiler_params=_params(("parallel", "parallel")),
    )(c_arr, g, got)


def add_partials(name, p, got, me_arr):
    _, rows, cols = p.shape
    tile = _add_tile(rows, cols)

    def body(me_ref, p_ref, r_ref, o_ref):
        o_ref[...] = ((p_ref[0].astype(f32) + r_ref[0].astype(f32)) + r_ref[1].astype(f32)) + r_ref[2].astype(f32)

    return pl.pallas_call(
        body, name=name,
        grid_spec=pltpu.PrefetchScalarGridSpec(
            num_scalar_prefetch=1, grid=(rows // tile,),
            in_specs=[pl.BlockSpec((1, tile, cols), lambda i, me_ref: (me_ref[0], i, 0)),
                      pl.BlockSpec((3, tile, cols), lambda i, me_ref: (0, i, 0))],
            out_specs=pl.BlockSpec((tile, cols), lambda i, me_ref: (i, 0))),
        out_shape=jax.ShapeDtypeStruct((rows, cols), f32), compiler_params=_params(("parallel",)),
    )(me_arr, p, got)


def join_halves(name, mine, theirs):
    rows, cols = mine.shape
    tile = _add_tile(rows, cols)

    def body(m_ref, t_ref, o_ref):
        south = lax.axis_index("c") == 0
        o_ref[0] = jnp.where(south, m_ref[...], t_ref[...])
        o_ref[1] = jnp.where(south, t_ref[...], m_ref[...])

    spec = pl.BlockSpec((tile, cols), lambda i: (i, 0))
    return pl.pallas_call(
        body, name=name, grid=(rows // tile,), in_specs=[spec, spec],
        out_specs=pl.BlockSpec((2, tile, cols), lambda i: (0, i, 0)),
        out_shape=jax.ShapeDtypeStruct((2, rows, cols), f32), compiler_params=_params(("parallel",)),
    )(mine, theirs)


N_DEV = 8


def allreduce_small(name, buf):
    m_per = buf.shape[0]

    def body(x_ref, out_ref, all_ref, send_sems, recv_sems, local_sem):
        x, y, c, chips = _place()
        me, sibling = (x, y, c), (x, y, 1 - c)

        def rows(px, py, pc):
            return all_ref.at[pl.ds(pl.multiple_of((4 * px + 2 * py + pc) * m_per, 8), m_per), :]

        def copy(k, block, to, src=None):
            return _rcopy(rows(*block) if src is None else src, rows(*block), send_sems, recv_sems, k, to)

        mine = pltpu.make_async_copy(x_ref, rows(*me), local_sem)
        mine.start()
        first = [copy(0, me, sibling, src=x_ref)]
        first += [copy(1 + j, me, (*chip, c), src=x_ref) for j, chip in enumerate(chips)]
        for cp in first:
            cp.start()
        passed = [copy(4 + j, (*chip, c), sibling) for j, chip in enumerate(chips)]
        for j, chip in enumerate(chips):
            copy(1 + j, (*chip, c), me).wait_recv()
            passed[j].start()
        copy(0, sibling, me).wait_recv()
        for j, chip in enumerate(chips):
            copy(4 + j, (*chip, 1 - c), me).wait_recv()
        for cp in first + passed:
            cp.wait_send()
        mine.wait()
        tot = all_ref[pl.ds(0, m_per), :]
        for d in range(1, N_DEV):
            tot = tot + all_ref[pl.ds(d * m_per, m_per), :]
        out_ref[...] = tot

    return pl.pallas_call(
        body, name=name, out_shape=jax.ShapeDtypeStruct((m_per, 128), f32),
        in_specs=[pl.BlockSpec(memory_space=pltpu.VMEM)], out_specs=pl.BlockSpec(memory_space=pltpu.VMEM),
        scratch_shapes=[pltpu.VMEM((N_DEV * m_per, 128), f32), pltpu.SemaphoreType.DMA((7,)), pltpu.SemaphoreType.DMA((7,)),
                        pltpu.SemaphoreType.DMA],
        compiler_params=pltpu.CompilerParams(has_side_effects=True, vmem_limit_bytes=VMEM_LIMIT_BYTES),
    )(buf)


def pack_shards(ws, layer):
    wide = jnp.concatenate([ws[n][layer].astype(bf16) for n in WIDE], axis=0)
    return [ws["w_in"][layer].astype(bf16), wide, ws["w_xkv"][layer].astype(bf16)]


def arrange_w_in(shards):
    parts = []
    for a, b, _ in sorted(ORIG_SEGS, key=lambda seg: seg[2]):
        for s in range(N_CHIPS):
            lo, hi = max(a, s * IN_SHARD), min(b, (s + 1) * IN_SHARD)
            if lo < hi:
                parts.append(shards[s][:, lo - s * IN_SHARD:hi - s * IN_SHARD])
    parts.append(jnp.zeros((shards.shape[1], PCOLS - C_DT - (OFF_Q - OFF_DT)), shards.dtype))
    return jnp.concatenate(parts, axis=1)


def shard_w_in_grad(g):
    out = []
    for s in range(N_CHIPS):
        parts = []
        for a, b, first in ORIG_SEGS:
            lo, hi = max(a, s * IN_SHARD), min(b, (s + 1) * IN_SHARD)
            if lo < hi:
                parts.append(g[:, first + lo - a:first + hi - a])
        out.append(jnp.concatenate(parts, axis=1))
    return jnp.concatenate(out, axis=0).reshape(N_CHIPS, g.shape[0], IN_SHARD)


def unpack_gathered(gathered):
    g_in, g_wide, g_xkv = gathered
    out = dict(w_in=arrange_w_in(g_in), w_xkv=jnp.concatenate([g_xkv[s] for s in range(N_CHIPS)], axis=1))
    for n in WIDE:
        piece = g_wide[:, WIDE_OFF[n]:WIDE_OFF[n] + WIDE_ROWS[n], :]
        if n == "w_mlp_up":
            out[n] = jnp.concatenate([piece[s] for s in range(N_CHIPS)], axis=1)
        else:
            out[n] = piece.reshape(N_CHIPS * WIDE_ROWS[n], D)
    return out


def _halved(a):
    return a.reshape(N_CHIPS, 2, a.shape[1] // 2, a.shape[2])


def pack_w_in_grad(g):
    return [_halved(shard_w_in_grad(g))]


def pack_other_grads(gs):
    cols = lambda g, q: jnp.concatenate([g[:, s * q:(s + 1) * q] for s in range(N_CHIPS)], axis=0).reshape(N_CHIPS, g.shape[0], q)
    wide = [cols(gs[n], D) if n == "w_mlp_up" else gs[n].reshape(N_CHIPS, WIDE_ROWS[n], D) for n in WIDE]
    return [_halved(jnp.concatenate(wide, axis=1)), _halved(cols(gs["w_xkv"], 2048 // N_CHIPS))]


ROW_TILE = 256


def _pad128(v):
    return jnp.pad(v.reshape(1, -1), ((0, 0), (0, 128 - v.shape[-1])))


def layer_forward(i, h, mem, w, p, bias, next_shards=None):
    tile = min(ROW_TILE, h.shape[0])
    s = dict(h=h)
    row = lambda a: a.reshape(1, -1)
    s["u"], = tile_fwd(f"rms_mix{i}", f_rms, [(h, D, 0)], [row(p["norm_mix"])], [(D, bf16)], tile)
    s["proj"] = proj = matmul(s["u"], w["w_in"], "nn", f"mm_in{i}")
    s["c1"], = conv_fwd(f"conv_a{i}", proj, C_CONV, C_CONV + D, p["conv_dw_w"], row(p["conv_dw_b"]), True, False)
    s["ca"], = tile_fwd(f"ln_silu{i}", f_ln_silu, [(s["c1"], D, 0)], [row(p["conv_ln_g"]), row(p["conv_ln_b"])], [(D, bf16)], tile)
    s["ya"] = matmul(s["ca"], w["w_conv_out"], "nn", f"mm_conv_out{i}")
    s["xpre"], s["xact"] = conv_fwd(f"conv_b{i}", proj, C_XBC, C_XBC, p["ssd_conv_w"], row(p["ssd_conv_b"]), False, True)
    ssd_p = (_pad128(p["ssd_dt_bias"]), _pad128(p["ssd_A_log"]), _pad128(p["ssd_D"]), row(p["ssd_norm_g"]))
    s["yB"], s["states"], landed = ssd_fwd(f"ssd_fwd{i}", s["xact"], proj, *ssd_p,
                                           ex=ex_gather_ici(next_shards) if next_shards else None)
    s["yb"] = matmul(s["yB"], w["w_ssd_out"], "nn", f"mm_ssd_out{i}")
    swa_p = (row(p["attn_q_norm"]), row(p["attn_k_norm"]), _pad128(p["attn_sinks"]))
    s["oc"], gathered = swa_fwd(f"swa_fwd{i}", proj, *swa_p, bias, ex=ex_gather_d2d(landed) if next_shards else None)
    s["yc"] = matmul(s["oc"], w["w_attn_out"], "nn", f"mm_attn_out{i}")
    gate_cols = [(proj, D, C_G // D + k) for k in range(3)]
    s["merged"], = tile_fwd(f"merge{i}", f_merge, gate_cols + [(s["ya"], D, 0), (s["yb"], D, 0), (s["yc"], D, 0)],
                            [row(p["gate_bias"])], [(D, bf16)], tile)
    s["h1"] = h1 = matmul(s["merged"], w["w_mix_out"], "nn", f"mm_mix_out{i}", residual=h)
    s["hx"], = tile_fwd(f"rms_x{i}", f_rms, [(h1, D, 0)], [row(p["norm_xattn"])], [(D, bf16)], tile)
    s["memh"], = tile_fwd(f"rms_mem{i}", f_rms, [(mem, D, 0)], [row(p["norm_mem"])], [(D, bf16)], MEM_LEN)
    s["xq"] = matmul(s["hx"], w["w_xq"], "nn", f"mm_xq{i}")
    s["kv"] = matmul(s["memh"], w["w_xkv"], "nn", f"mm_xkv{i}")
    s["xo"] = xa_fwd(f"xa_fwd{i}", s["xq"], s["kv"], row(p["xattn_q_norm"]), row(p["xattn_k_norm"]))
    s["h2"] = h2 = matmul(s["xo"], w["w_xo"], "nn", f"mm_xo{i}", residual=h1)
    s["um"], = tile_fwd(f"rms_mlp{i}", f_rms, [(h2, D, 0)], [row(p["norm_mlp"])], [(D, bf16)], tile)
    s["up"], s["act"] = matmul(s["um"], w["w_mlp_up"], "nn", f"mm_up{i}", relu2="fwd")
    h3 = matmul(s["act"], w["w_mlp_down"], "nn", f"mm_down{i}", residual=h2)
    return h3, s, gathered


def layer_backward(i, dh3, mem, w, p, bias, s, pending=None, c_arr=None):
    t_len = dh3.shape[0]
    tile = min(ROW_TILE, t_len)
    row = lambda a: a.reshape(1, -1)
    gw, gp = {}, {}
    dup = matmul(dh3, w["w_mlp_down"], "nt", f"mm_down_dx{i}", out_dtype=bf16, relu2=s["up"])
    gw["w_mlp_down"] = matmul(s["act"], dh3, "tn", f"mm_down_dw{i}")
    gw["w_mlp_up"] = matmul(s["um"], dup, "tn", f"mm_up_dw{i}")
    dum = matmul(dup, w["w_mlp_up"], "nt", f"mm_up_dx{i}")
    (dh2,), (g,) = tile_bwd(f"rms_mlp_bwd{i}", f_rms, [(s["h2"], D, 0)], [row(p["norm_mlp"])], [dum], [f32], tile, addend=dh3)
    gp["norm_mlp"] = g
    dxo = matmul(dh2, w["w_xo"], "nt", f"mm_xo_dx{i}")
    gw["w_xo"] = matmul(s["xo"], dh2, "tn", f"mm_xo_dw{i}")
    dxq, dk, dv, gp["xattn_q_norm"], gp["xattn_k_norm"] = xa_bwd(f"xa_bwd{i}", s["xq"], s["kv"], dxo, row(p["xattn_q_norm"]),
                                                                 row(p["xattn_k_norm"]))
    dkv = jnp.concatenate([dk, dv], axis=1)
    gw["w_xq"] = matmul(s["hx"], dxq, "tn", f"mm_xq_dw{i}")
    dhx = matmul(dxq, w["w_xq"], "nt", f"mm_xq_dx{i}")
    gw["w_xkv"] = matmul(s["memh"], dkv, "tn", f"mm_xkv_dw{i}")
    dmemh = matmul(dkv, w["w_xkv"], "nt", f"mm_xkv_dx{i}")
    _, (g,) = tile_bwd(f"rms_mem_bwd{i}", f_rms, [(mem, D, 0)], [row(p["norm_mem"])], [dmemh], [f32], MEM_LEN)
    gp["norm_mem"] = g
    (dh1,), (g,) = tile_bwd(f"rms_x_bwd{i}", f_rms, [(s["h1"], D, 0)], [row(p["norm_xattn"])], [dhx], [f32], tile, addend=dh2)
    gp["norm_xattn"] = g
    proj = s["proj"]
    dmerged = matmul(dh1, w["w_mix_out"], "nt", f"mm_mix_out_dx{i}")
    gw["w_mix_out"] = matmul(s["merged"], dh1, "tn", f"mm_mix_out_dw{i}")
    gate_cols = [(proj, D, C_G // D + k) for k in range(3)]
    (dpa, dpb, dpc, dya, dyb, dyc), (g,) = tile_bwd(
        f"merge_bwd{i}", f_merge, gate_cols + [(s["ya"], D, 0), (s["yb"], D, 0), (s["yc"], D, 0)], [row(p["gate_bias"])],
        [dmerged], [bf16] * 6, tile)
    gp["gate_bias"] = g
    dca = matmul(dya, w["w_conv_out"], "nt", f"mm_conv_out_dx{i}")
    gw["w_conv_out"] = matmul(s["ca"], dya, "tn", f"mm_conv_out_dw{i}")
    (dc1,), (gp["conv_ln_g"], gp["conv_ln_b"]) = tile_bwd(
        f"ln_silu_bwd{i}", f_ln_silu, [(s["c1"], D, 0)], [row(p["conv_ln_g"]), row(p["conv_ln_b"])], [dca], [f32], tile)
    da, dgate, gp["conv_dw_w"], gp["conv_dw_b"] = conv_bwd(f"conv_a_bwd{i}", dc1, None, proj, C_CONV, C_CONV + D,
                                                           p["conv_dw_w"], True, False, bf16)
    dyB = matmul(dyb, w["w_ssd_out"], "nt", f"mm_ssd_out_dx{i}")
    gw["w_ssd_out"] = matmul(s["yB"], dyb, "tn", f"mm_ssd_out_dw{i}")
    doc = matmul(dyc, w["w_attn_out"], "nt", f"mm_attn_out_dx{i}")
    gw["w_attn_out"] = matmul(s["oc"], dyc, "tn", f"mm_attn_out_dw{i}")
    moving = list(pending or []) + pack_other_grads(gw)
    swa_p = (row(p["attn_q_norm"]), row(p["attn_k_norm"]), _pad128(p["attn_sinks"]))
    (dq, dkp, dkc, dvp, dvc, gp["attn_q_norm"], gp["attn_k_norm"], g, dbias), swapped = swa_bwd(
        f"swa_bwd{i}", proj, doc, *swa_p, bias, ex=ex_swap_halves(moving))
    gp["attn_sinks"] = g[:, :ATTN_HEADS]
    shift = lambda a: jnp.concatenate([a[BLK:], jnp.zeros((BLK, a.shape[1]), a.dtype)], axis=0)
    dk_ = (dkc + shift(dkp)).astype(bf16)
    dv_ = (dvc + shift(dvp)).astype(bf16)
    partials = [add_halves(f"add_halves{i}_{k}", g_, got, c_arr) for k, (g_, got) in enumerate(zip(moving, swapped))]
    ssd_p = (_pad128(p["ssd_dt_bias"]), _pad128(p["ssd_A_log"]), _pad128(p["ssd_D"]), row(p["ssd_norm_g"]))
    (dxs, dbm, dcm, dz, ddt4, g1, g2, g3, g4), arrived = ssd_bwd(f"ssd_bwd{i}", s["xact"], proj, s["states"], dyB, *ssd_p,
                                                                  ex=ex_scatter(partials))
    gp["ssd_dt_bias"], gp["ssd_A_log"], gp["ssd_D"], gp["ssd_norm_g"] = g1[:, :SSD_HEADS], g2[:, :SSD_HEADS], g3[:, :SSD_HEADS], g4
    ddt = (ddt4[0] + ddt4[1] + ddt4[2] + ddt4[3]).astype(bf16)
    dxact = jnp.concatenate([dxs, dbm, dcm], axis=1)
    dxbc, gp["ssd_conv_w"], gp["ssd_conv_b"] = conv_bwd(f"conv_b_bwd{i}", dxact, s["xpre"], proj, C_XBC, C_XBC, p["ssd_conv_w"],
                                                        False, True, bf16)
    tail = jnp.concatenate([dk_, dv_, ddt, jnp.zeros((t_len, PCOLS - C_DT - 128), bf16)], axis=1)
    dproj = [da, dgate, dz, dxbc, dq, dpa, dpb, dpc, tail]
    gw["w_in"] = matmul_pieces(dproj, s["u"], "tn", f"mm_in_dw{i}")
    du = matmul_pieces(dproj, w["w_in"], "nt", f"mm_in_dx{i}")
    (dh,), (g,) = tile_bwd(f"rms_mix_bwd{i}", f_rms, [(s["h"], D, 0)], [row(p["norm_mix"])], [du], [f32], tile, addend=dh1)
    gp["norm_mix"] = g
    return dh, gw, gp, dbias, (partials, arrived)


SMALL = ["rel_table", "norm_mix", "gate_bias", "conv_dw_w", "conv_dw_b", "conv_ln_g", "conv_ln_b", "ssd_conv_w", "ssd_conv_b",
         "ssd_dt_bias", "ssd_A_log", "ssd_D", "ssd_norm_g", "attn_q_norm", "attn_k_norm", "attn_sinks", "norm_xattn", "norm_mem",
         "xattn_q_norm", "xattn_k_norm", "norm_mlp"]
SMALL_SHARDED = dict(gate_bias=D, conv_dw_w=D, ssd_conv_w=SSD_XBC)
ORDER = ["rel_table", "norm_mix", "w_in", "gate_bias", "conv_dw_w", "conv_dw_b", "conv_ln_g", "conv_ln_b", "w_conv_out",
         "ssd_conv_w", "ssd_conv_b", "ssd_dt_bias", "ssd_A_log", "ssd_D", "ssd_norm_g", "w_ssd_out", "attn_q_norm", "attn_k_norm",
         "attn_sinks", "w_attn_out", "w_mix_out", "norm_xattn", "norm_mem", "w_xq", "w_xkv", "xattn_q_norm", "xattn_k_norm",
         "w_xo", "norm_mlp", "w_mlp_up", "w_mlp_down"]


def finish_reduction(i, partials, arrived, me_arr):
    mine = [add_partials(f"add_partials{i}_{k}", p, got, me_arr) for k, (p, got) in enumerate(zip(partials, arrived))]
    theirs = run_exchange(f"share_halves{i}", ex_share(mine))
    both = [join_halves(f"join_halves{i}_{k}", m, t) for k, (m, t) in enumerate(zip(mine, theirs))]
    return [b.reshape(2 * b.shape[1], b.shape[2]) for b in both]


def kernel(x, mem, rel_table, norm_mix, w_in, gate_bias, conv_dw_w, conv_dw_b, conv_ln_g, conv_ln_b, w_conv_out, ssd_conv_w, ssd_conv_b, ssd_dt_bias, ssd_A_log, ssd_D, ssd_norm_g, w_ssd_out, attn_q_norm, attn_k_norm, attn_sinks, w_attn_out, w_mix_out, norm_xattn, norm_mem, w_xq, w_xkv, xattn_q_norm, xattn_k_norm, w_xo, norm_mlp, w_mlp_up, w_mlp_down, loss_target, m_rel_table, m_norm_mix, m_w_in, m_gate_bias, m_conv_dw_w, m_conv_dw_b, m_conv_ln_g, m_conv_ln_b, m_w_conv_out, m_ssd_conv_w, m_ssd_conv_b, m_ssd_dt_bias, m_ssd_A_log, m_ssd_D, m_ssd_norm_g, m_w_ssd_out, m_attn_q_norm, m_attn_k_norm, m_attn_sinks, m_w_attn_out, m_w_mix_out, m_norm_xattn, m_norm_mem, m_w_xq, m_w_xkv, m_xattn_q_norm, m_xattn_k_norm, m_w_xo, m_norm_mlp, m_w_mlp_up, m_w_mlp_down, v_rel_table, v_norm_mix, v_w_in, v_gate_bias, v_conv_dw_w, v_conv_dw_b, v_conv_ln_g, v_conv_ln_b, v_w_conv_out, v_ssd_conv_w, v_ssd_conv_b, v_ssd_dt_bias, v_ssd_A_log, v_ssd_D, v_ssd_norm_g, v_w_ssd_out, v_attn_q_norm, v_attn_k_norm, v_attn_sinks, v_w_attn_out, v_w_mix_out, v_norm_xattn, v_norm_mem, v_w_xq, v_w_xkv, v_xattn_q_norm, v_xattn_k_norm, v_w_xo, v_norm_mlp, v_w_mlp_up, v_w_mlp_down):
    a = dict(locals())
    wts = {n: a[n] for n in ORDER}
    ms = {n: a["m_" + n] for n in ORDER}
    vs = {n: a["v_" + n] for n in ORDER}
    depth = norm_mix.shape[0]
    ax, ay, ac = lax.axis_index("x"), lax.axis_index("y"), lax.axis_index("c")
    chip = 2 * ax + ay

    small = {n: wts[n] for n in SMALL if n != "rel_table" and n not in SMALL_SHARDED}
    gather_buf, spans, r0 = [], {}, 0
    for n, width in SMALL_SHARDED.items():
        blk = wts[n]
        q = width // N_CHIPS
        whole = lax.dynamic_update_slice(jnp.zeros(blk.shape[:-1] + (width,), f32), blk, (0, 0, chip * q))
        flat = whole.reshape(-1, 128)
        spans[n] = (r0, flat.shape[0], whole.shape)
        r0 += flat.shape[0]
        gather_buf.append(flat)
    summed = allreduce_small("gather_small_params", jnp.concatenate(gather_buf, axis=0))
    for n, (s0, nr, shp) in spans.items():
        small[n] = summed[s0:s0 + nr].reshape(shp) * 0.5

    c_arr = jnp.reshape(ac, (1,)).astype(jnp.int32)
    me_arr = jnp.reshape(chip, (1,)).astype(jnp.int32)
    x0, mem0 = x[0], mem[0]
    bucket = jnp.asarray(_bucket_table())
    bias = relbias_fwd(rel_table, bucket)

    gathered = run_exchange("gather_d2d_first", ex_gather_d2d(run_exchange("gather_ici_first", ex_gather_ici(pack_shards(wts, 0)))))
    h, saved, ws = x0, [], []
    for i in range(depth):
        w = unpack_gathered(gathered)
        p = {n: small[n][i] for n in small}
        h, s, gathered = layer_forward(i, h, mem0, w, p, bias, pack_shards(wts, i + 1) if i + 1 < depth else None)
        saved.append(s)
        ws.append((w, p))
    grad_x, loss_tile = loss_head(h, loss_target[0], min(ROW_TILE, h.shape[0]))
    loss = lax.psum(loss_tile[0, 0], ("x", "y", "c"))

    reduced, small_grads, dbiases, pending = [[None] * 3 for _ in range(depth)], [None] * depth, [], None
    for i in reversed(range(depth)):
        w, p = ws[i]
        grad_x, gw, gp, dbias, travelled = layer_backward(i, grad_x, mem0, w, p, bias, saved[i], pending, c_arr)
        done = finish_reduction(i, *travelled, me_arr)
        if pending is not None:
            reduced[i + 1][0] = done.pop(0)
        reduced[i][1], reduced[i][2] = done
        small_grads[i] = gp
        dbiases.append(dbias)
        pending = pack_w_in_grad(gw["w_in"])
    swapped = run_exchange("swap_halves_last", ex_swap_halves(pending))
    partials = [add_halves("add_halves_last", pending[0], swapped[0], c_arr)]
    reduced[0][0], = finish_reduction("_last", partials, run_exchange("scatter_last", ex_scatter(partials)), me_arr)
    sg = {n: jnp.stack([small_grads[i][n].reshape(small[n].shape[1:]) for i in range(depth)]) for n in small}
    sg["rel_table"] = relbias_bwd(dbiases, bucket)

    grads, deltas, new_m, new_v = {}, {}, {}, {}
    sources = dict(w_in=(0, 0), w_xkv=(2, 0), **{n: (1, WIDE_OFF[n]) for n in WIDE})
    for n, (k, r0) in sources.items():
        grads[n], deltas[n], new_m[n], new_v[n] = adamw_layers(f"adamw_{n}", wts[n], ms[n], vs[n], [reduced[i][k] for i in range(depth)], r0)

    parts, spans, r0 = [], {}, 0
    for n in SMALL:
        flat = sg[n].reshape(-1)
        nr = -(-flat.shape[0] // 128)
        nr = -(-nr // 8) * 8
        flat = jnp.pad(flat, (0, nr * 128 - flat.shape[0])).reshape(nr, 128)
        spans[n] = (r0, nr, sg[n].shape)
        r0 += nr
        parts.append(flat)
    summed = allreduce_small("allreduce_small_grads", jnp.concatenate(parts, axis=0))
    for n, (s0, nr, shp) in spans.items():
        size = int(np.prod(shp))
        g = summed[s0:s0 + nr].reshape(-1)[:size].reshape(shp)
        if n in SMALL_SHARDED:
            q = SMALL_SHARDED[n] // N_CHIPS
            g = lax.dynamic_slice_in_dim(g, chip * q, q, axis=g.ndim - 1)
        grads[n] = g

    for n in SMALL:
        shp = wts[n].shape
        two = (lambda t: t.reshape(-1, shp[-1]))
        d, nm, nv = adamw(f"adamw_{n}", two(wts[n]), two(grads[n]), two(ms[n]), two(vs[n]))
        deltas[n], new_m[n], new_v[n] = d.reshape(shp), nm.reshape(shp), nv.reshape(shp)

    return (loss, grad_x[None], *[grads[n] for n in ORDER], *[deltas[n] for n in ORDER],
            *[new_m[n] for n in ORDER], *[new_v[n] for n in ORDER])
```

```python
import functools
import math

import numpy as np
import jax
import jax.numpy as jnp
from jax import lax
from jax.experimental import pallas as pl
from jax.experimental.pallas import tpu as pltpu

f32, bf16 = jnp.float32, jnp.bfloat16

D = 1024
EPS = 1e-6
NEG_INF = -1e30
CONV_K = 31
SSD_INNER = 2048
SSD_HEADS = 32
SSD_P = 64
SSD_GROUPS = 4
SSD_N = 128
SSD_K = 4
CHUNK = 128
SSD_XBC = 3072
ATTN_HEADS = 16
ATTN_KV = 4
ATTN_D = 64
BLK = 128
REL_BUCKETS = 32
XH = 4
XD = 256
MLP = 4096
MEM_LEN = 256

OFF_Z, OFF_XBC, OFF_DT, OFF_Q, OFF_K, OFF_V, OFF_GATE, IN_COLS = 2048, 4096, 7168, 7200, 8224, 8480, 8736, 11808
C_CONV, C_Z, C_XBC, C_Q, C_G, C_K, C_V, C_DT, PCOLS = 0, 2048, 4096, 7168, 8192, 11264, 11520, 11776, 12288

ADAM_LR, ADAM_B1, ADAM_B2, ADAM_EPS, ADAM_WD, ADAM_STEP = 0.001, 0.9, 0.999, 1e-08, 0.01, 10

VMEM_LIMIT_BYTES = 56 * 1024 * 1024
N_CHIPS = 4
BIG = ["w_in", "w_conv_out", "w_ssd_out", "w_attn_out", "w_mix_out", "w_xq", "w_xkv", "w_xo", "w_mlp_up", "w_mlp_down"]
WIDE = ["w_conv_out", "w_ssd_out", "w_attn_out", "w_mix_out", "w_xq", "w_xo", "w_mlp_up", "w_mlp_down"]
WIDE_ROWS = dict(w_conv_out=256, w_ssd_out=512, w_attn_out=256, w_mix_out=256, w_xq=256, w_xo=256, w_mlp_up=1024, w_mlp_down=1024)
WIDE_OFF = {n: sum(WIDE_ROWS[m] for m in WIDE[:k]) for k, n in enumerate(WIDE)}
WIDE_TOTAL = sum(WIDE_ROWS.values())
IN_SHARD = IN_COLS // N_CHIPS
IN_PAD = 3072
ORIG_SEGS = [(0, OFF_DT, 0), (OFF_DT, OFF_Q, C_DT), (OFF_Q, OFF_K, C_Q), (OFF_K, OFF_GATE, C_K), (OFF_GATE, IN_COLS, C_G)]


def _params(sem=None):
    return pltpu.CompilerParams(dimension_semantics=sem, vmem_limit_bytes=VMEM_LIMIT_BYTES)


NN, NT, TN = ((1,), (0,)), ((1,), (1,)), ((0,), (0,))


def _dg(a, b, dims):
    return lax.dot_general(a.astype(bf16), b.astype(bf16), (dims, ((), ())), preferred_element_type=f32)


@jax.custom_vjp
def dot_nn(a, b):
    return _dg(a, b, NN)


dot_nn.defvjp(lambda a, b: (_dg(a, b, NN), (a, b)), lambda r, g: (_dg(g, r[1], NT), _dg(r[0], g, TN)))


@jax.custom_vjp
def dot_nt(a, b):
    return _dg(a, b, NT)


dot_nt.defvjp(lambda a, b: (_dg(a, b, NT), (a, b)), lambda r, g: (_dg(g, r[1], NN), _dg(g, r[0], TN)))


@jax.custom_vjp
def dot_tn(a, b):
    return _dg(a, b, TN)


dot_tn.defvjp(lambda a, b: (_dg(a, b, TN), (a, b)), lambda r, g: (_dg(r[1], g, NT), _dg(r[0], g, NN)))


def _tri(n, upper):
    r = lax.broadcasted_iota(jnp.int32, (n, n), 0)
    c = lax.broadcasted_iota(jnp.int32, (n, n), 1)
    return jnp.where((c >= r) if upper else (r >= c), 1.0, 0.0).astype(f32)


def _hdot(a, b):
    return lax.dot_general(a, b, (NN, ((), ())), preferred_element_type=f32, precision=lax.Precision.HIGHEST)


@jax.custom_vjp
def cumsum_rows(x):
    return _hdot(_tri(x.shape[0], False), x)


cumsum_rows.defvjp(lambda x: (_hdot(_tri(x.shape[0], False), x), None),
                   lambda _, g: (_hdot(_tri(g.shape[0], True), g),))


def _pick(n, prefs):
    for p in prefs:
        if n % p == 0:
            return p
    return n


def matmul(a, b, mode, name, out_dtype=f32, residual=None, relu2=None):
    if mode == "nn":
        (m, k), n = a.shape, b.shape[1]
    elif mode == "nt":
        (m, k), n = a.shape, b.shape[0]
    else:
        (k, m), n = a.shape, b.shape[1]
    tm = _pick(m, (1024, 512, 256))
    tn = _pick(n, (1024, 512, 256, 128))
    tk = _pick(k, (1024, 512))
    nk = k // tk
    dims = dict(nn=NN, nt=NT, tn=TN)[mode]
    a_spec = pl.BlockSpec((tk, tm), lambda i, j, l: (l, i)) if mode == "tn" else pl.BlockSpec((tm, tk), lambda i, j, l: (i, l))
    b_spec = pl.BlockSpec((tn, tk), lambda i, j, l: (j, l)) if mode == "nt" else pl.BlockSpec((tk, tn), lambda i, j, l: (l, j))
    o_spec = pl.BlockSpec((tm, tn), lambda i, j, l: (i, j))
    two_outs = isinstance(relu2, str)
    extra = residual if residual is not None else (None if two_outs or relu2 is None else relu2)
    has_extra = extra is not None
    n_out = 2 if two_outs else 1

    def body(*refs):
        a_ref, b_ref = refs[0], refs[1]
        e_ref = refs[2] if has_extra else None
        o_refs = refs[2 + has_extra:2 + has_extra + n_out]
        acc = refs[2 + has_extra + n_out]
        l = pl.program_id(2)

        @pl.when(l == 0)
        def _():
            acc[...] = jnp.zeros(acc.shape, f32)

        acc[...] += _dg(a_ref[...], b_ref[...], dims)

        @pl.when(l == nk - 1)
        def _():
            r = acc[...]
            if residual is not None:
                r = r + e_ref[...]
            elif has_extra:
                r = r * (2.0 * jnp.maximum(e_ref[...], 0.0))
            o_refs[0][...] = r.astype(o_refs[0].dtype)
            if two_outs:
                o_refs[1][...] = jnp.square(jnp.maximum(r, 0.0)).astype(o_refs[1].dtype)

    ins = [a, b] + ([extra] if has_extra else [])
    shapes = [jax.ShapeDtypeStruct((m, n), out_dtype)] + ([jax.ShapeDtypeStruct((m, n), bf16)] if two_outs else [])
    res = pl.pallas_call(
        body, name=name, grid=(m // tm, n // tn, nk),
        in_specs=[a_spec, b_spec] + ([o_spec] if has_extra else []),
        out_specs=[o_spec] * n_out, out_shape=shapes,
        scratch_shapes=[pltpu.VMEM((tm, tn), f32)],
        compiler_params=_params(("parallel", "parallel", "arbitrary")),
    )(*ins)
    return res if two_outs else res[0]


def _tile_in_specs(tiled, params, tile):
    specs = [pl.BlockSpec((tile, cols), functools.partial(lambda i, cb: (i, cb), cb=cb)) for (_, cols, cb) in tiled]
    specs += [pl.BlockSpec(p.shape, lambda i: (0, 0)) for p in params]
    return specs


def tile_fwd(name, f, tiled, params, outs, tile):
    rows = tiled[0][0].shape[0]
    nt, npar = len(tiled), len(params)

    def body(*refs):
        xs = [r[...].astype(f32) for r in refs[:nt + npar]]
        res = f(*xs)
        for o_ref, o in zip(refs[nt + npar:], res):
            o_ref[...] = o.astype(o_ref.dtype)

    return pl.pallas_call(
        body, name=name, grid=(rows // tile,),
        in_specs=_tile_in_specs(tiled, params, tile),
        out_specs=[pl.BlockSpec((tile, c), lambda i: (i, 0)) for c, _ in outs],
        out_shape=[jax.ShapeDtypeStruct((rows, c), dt) for c, dt in outs],
        compiler_params=_params(("parallel",)),
    )(*[t[0] for t in tiled], *params)


def tile_bwd(name, f, tiled, params, cots, d_dtypes, tile, addend=None):
    rows = tiled[0][0].shape[0]
    nt, npar, nc = len(tiled), len(params), len(cots)
    has_add = addend is not None

    def body(*refs):
        i = pl.program_id(0)
        xs = [r[...].astype(f32) for r in refs[:nt + npar]]
        gs = tuple(r[...].astype(f32) for r in refs[nt + npar:nt + npar + nc])
        pos = nt + npar + nc
        add_ref = refs[pos] if has_add else None
        pos += has_add
        dt_refs, dp_refs = refs[pos:pos + nt], refs[pos + nt:pos + nt + npar]
        _, vjp = jax.vjp(f, *xs)
        ds = vjp(gs)
        for k in range(nt):
            d = ds[k]
            if has_add and k == 0:
                d = d + add_ref[...]
            dt_refs[k][...] = d.astype(dt_refs[k].dtype)

        @pl.when(i == 0)
        def _():
            for r in dp_refs:
                r[...] = jnp.zeros(r.shape, f32)

        for k in range(npar):
            dp_refs[k][...] += ds[nt + k]

    in_specs = _tile_in_specs(tiled, params, tile)
    in_specs += [pl.BlockSpec((tile, c.shape[1]), lambda i: (i, 0)) for c in cots]
    ins = [t[0] for t in tiled] + list(params) + list(cots)
    if has_add:
        in_specs.append(pl.BlockSpec((tile, addend.shape[1]), lambda i: (i, 0)))
        ins.append(addend)
    out_specs = [pl.BlockSpec((tile, cols), lambda i: (i, 0)) for (_, cols, _) in tiled]
    out_specs += [pl.BlockSpec(p.shape, lambda i: (0, 0)) for p in params]
    out_shape = [jax.ShapeDtypeStruct((rows, cols), dt) for (_, cols, _), dt in zip(tiled, d_dtypes)]
    out_shape += [jax.ShapeDtypeStruct(p.shape, f32) for p in params]
    res = pl.pallas_call(
        body, name=name, grid=(rows // tile,), in_specs=in_specs, out_specs=out_specs, out_shape=out_shape,
        compiler_params=_params(("arbitrary",)),
    )(*ins)
    return res[:nt], res[nt:]


def f_rms(h, g):
    return (h * lax.rsqrt(jnp.mean(h * h, axis=-1, keepdims=True) + EPS) * g,)


def f_ln_silu(x, g, b):
    mu = jnp.mean(x, axis=-1, keepdims=True)
    xc = x - mu
    y = xc * lax.rsqrt(jnp.mean(xc * xc, axis=-1, keepdims=True) + EPS) * g + b
    return (y * jax.nn.sigmoid(y),)


def f_merge(pa, pb, pc, ya, yb, yc, gb):
    ga = jax.nn.sigmoid(pa + gb[:, 0:D])
    gb_ = jax.nn.sigmoid(pb + gb[:, D:2 * D])
    gc = jax.nn.sigmoid(pc + gb[:, 2 * D:3 * D])
    return (ga * ya + gb_ * yb + gc * yc,)


HALO = 32
CONV_CB = 256
SUB = 128


def _silu_grad(p):
    s = jax.nn.sigmoid(p)
    return s * (1.0 + p * (1.0 - s))


def _fill_shifted(buf, copies, length):
    for k in range(1, 8):
        for r0 in range(0, length - 8, SUB):
            n = min(SUB, length - 8 - r0)
            copies[k - 1, pl.ds(r0, n), :] = buf[pl.ds(r0 + k, n), :]


def _rows(buf, copies, off, n):
    if copies is None or off % 8 == 0:
        return buf[pl.ds(off, n), :]
    return copies[off % 8 - 1, pl.ds(off - off % 8, n), :]


def conv_fwd(name, src, col0, gate_col0, w, b, glu, silu):
    t_len = src.shape[0]
    k_taps, c = w.shape
    tt = min(512, t_len)
    nt_ = t_len // tt
    cb0, gb0 = col0 // CONV_CB, gate_col0 // CONV_CB
    many_taps = k_taps > 8

    def body(*refs):
        pos = 0
        x_cur, x_prev = refs[0], refs[1]
        pos = 2
        if glu:
            g_cur, g_prev = refs[2], refs[3]
            pos = 4
        w_ref, b_ref = refs[pos], refs[pos + 1]
        pre_ref = refs[pos + 2]
        act_ref = refs[pos + 3] if silu else None
        xp = refs[-2] if many_taps else refs[-1]
        xs = refs[-1] if many_taps else None
        t = pl.program_id(1)
        cur = x_cur[...]
        tail = x_prev[...]
        if glu:
            cur = cur * jax.nn.sigmoid(g_cur[...])
            tail = tail * jax.nn.sigmoid(g_prev[...])
        xp[pl.ds(0, HALO), :] = jnp.where(t > 0, tail, 0.0)
        xp[pl.ds(HALO, tt), :] = cur
        if many_taps:
            _fill_shifted(xp, xs, HALO + tt)
        for s in range(tt // SUB):
            acc = jnp.broadcast_to(b_ref[...], (SUB, CONV_CB))
            for j in range(k_taps):
                acc = acc + _rows(xp, xs, s * SUB + HALO - (k_taps - 1) + j, SUB) * w_ref[pl.ds(j, 1), :]
            pre_ref[pl.ds(s * SUB, SUB), :] = acc
            if silu:
                act_ref[pl.ds(s * SUB, SUB), :] = acc * jax.nn.sigmoid(acc)

    blk = lambda off: pl.BlockSpec((tt, CONV_CB), functools.partial(lambda j, t, off: (t, off + j), off=off))
    prev = lambda off: pl.BlockSpec((HALO, CONV_CB), functools.partial(lambda j, t, off: (jnp.maximum(t * (tt // HALO) - 1, 0), off + j), off=off))
    in_specs, ins = [blk(cb0), prev(cb0)], [src, src]
    if glu:
        in_specs += [blk(gb0), prev(gb0)]
        ins += [src, src]
    in_specs += [pl.BlockSpec((k_taps, CONV_CB), lambda j, t: (0, j)), pl.BlockSpec((1, CONV_CB), lambda j, t: (0, j))]
    ins += [w, b]
    o_spec = pl.BlockSpec((tt, CONV_CB), lambda j, t: (t, j))
    n_out = 2 if silu else 1
    res = pl.pallas_call(
        body, name=name, grid=(c // CONV_CB, nt_), in_specs=in_specs, out_specs=[o_spec] * n_out,
        out_shape=[jax.ShapeDtypeStruct((t_len, c), f32)] * n_out,
        scratch_shapes=[pltpu.VMEM((HALO + tt, CONV_CB), f32)] + ([pltpu.VMEM((7, HALO + tt, CONV_CB), f32)] if many_taps else []),
        compiler_params=_params(("parallel", "arbitrary")),
    )(*ins)
    return res


def conv_bwd(name, dy, pre, src, col0, gate_col0, w, glu, silu, out_dtype):
    t_len = src.shape[0]
    k_taps, c = w.shape
    tt = min(512, t_len)
    nt_ = t_len // tt
    cb0, gb0 = col0 // CONV_CB, gate_col0 // CONV_CB
    many_taps = k_taps > 8

    def body(*refs):
        dy_cur, dy_next = refs[0], refs[1]
        pos = 2
        if silu:
            p_cur, p_next = refs[2], refs[3]
            pos = 4
        x_cur, x_prev = refs[pos], refs[pos + 1]
        pos += 2
        if glu:
            g_cur, g_prev = refs[pos], refs[pos + 1]
            pos += 2
        w_ref = refs[pos]
        pos += 1
        n_dx = 2 if glu else 1
        dx_refs = refs[pos:pos + n_dx]
        dw_ref, db_ref = refs[pos + n_dx], refs[pos + n_dx + 1]
        dp, xp = (refs[-4], refs[-3]) if many_taps else (refs[-2], refs[-1])
        ds_, xs = (refs[-2], refs[-1]) if many_taps else (None, None)
        t = pl.program_id(1)

        dcur = dy_cur[...]
        dhead = dy_next[...]
        if silu:
            dcur = dcur * _silu_grad(p_cur[...])
            dhead = dhead * _silu_grad(p_next[...])
        dp[pl.ds(0, tt), :] = dcur
        dp[pl.ds(tt, HALO), :] = jnp.where(t < nt_ - 1, dhead, 0.0)
        cur = x_cur[...]
        tail = x_prev[...]
        if glu:
            cur = cur * jax.nn.sigmoid(g_cur[...])
            tail = tail * jax.nn.sigmoid(g_prev[...])
        xp[pl.ds(0, HALO), :] = jnp.where(t > 0, tail, 0.0)
        xp[pl.ds(HALO, tt), :] = cur

        if many_taps:
            _fill_shifted(dp, ds_, tt + HALO)
            _fill_shifted(xp, xs, HALO + tt)

        @pl.when(t == 0)
        def _():
            dw_ref[...] = jnp.zeros(dw_ref.shape, f32)
            db_ref[...] = jnp.zeros(db_ref.shape, f32)

        for s in range(tt // SUB):
            d_sub = dp[pl.ds(s * SUB, SUB), :]
            acc = jnp.zeros((SUB, CONV_CB), f32)
            for j in range(k_taps):
                acc = acc + _rows(dp, ds_, s * SUB + (k_taps - 1) - j, SUB) * w_ref[pl.ds(j, 1), :]
                x_sub = _rows(xp, xs, s * SUB + HALO - (k_taps - 1) + j, SUB)
                dw_ref[pl.ds(j, 1), :] += jnp.sum(d_sub * x_sub, axis=0, keepdims=True)
            db_ref[...] += jnp.sum(d_sub, axis=0, keepdims=True)
            if glu:
                a = x_cur[pl.ds(s * SUB, SUB), :]
                sg = jax.nn.sigmoid(g_cur[pl.ds(s * SUB, SUB), :])
                dx_refs[0][pl.ds(s * SUB, SUB), :] = (acc * sg).astype(out_dtype)
                dx_refs[1][pl.ds(s * SUB, SUB), :] = (acc * a * sg * (1.0 - sg)).astype(out_dtype)
            else:
                dx_refs[0][pl.ds(s * SUB, SUB), :] = acc.astype(out_dtype)

    blk = lambda off: pl.BlockSpec((tt, CONV_CB), functools.partial(lambda j, t, off: (t, off + j), off=off))
    prev = lambda off: pl.BlockSpec((HALO, CONV_CB), functools.partial(lambda j, t, off: (jnp.maximum(t * (tt // HALO) - 1, 0), off + j), off=off))
    nxt = pl.BlockSpec((HALO, CONV_CB), lambda j, t: (jnp.minimum((t + 1) * (tt // HALO), t_len // HALO - 1), j))
    in_specs, ins = [blk(0), nxt], [dy, dy]
    if silu:
        in_specs += [blk(0), nxt]
        ins += [pre, pre]
    in_specs += [blk(cb0), prev(cb0)]
    ins += [src, src]
    if glu:
        in_specs += [blk(gb0), prev(gb0)]
        ins += [src, src]
    in_specs.append(pl.BlockSpec((k_taps, CONV_CB), lambda j, t: (0, j)))
    ins.append(w)
    n_dx = 2 if glu else 1
    o_spec = pl.BlockSpec((tt, CONV_CB), lambda j, t: (t, j))
    out_specs = [o_spec] * n_dx + [pl.BlockSpec((k_taps, CONV_CB), lambda j, t: (0, j)), pl.BlockSpec((1, CONV_CB), lambda j, t: (0, j))]
    out_shape = [jax.ShapeDtypeStruct((t_len, c), out_dtype)] * n_dx + [jax.ShapeDtypeStruct((k_taps, c), f32), jax.ShapeDtypeStruct((1, c), f32)]
    return pl.pallas_call(
        body, name=name, grid=(c // CONV_CB, nt_), in_specs=in_specs, out_specs=out_specs, out_shape=out_shape,
        scratch_shapes=[pltpu.VMEM((tt + HALO, CONV_CB), f32)] * 2 + ([pltpu.VMEM((7, tt + HALO, CONV_CB), f32)] * 2 if many_taps else []),
        compiler_params=_params(("parallel", "arbitrary")),
    )(*ins)


GH = SSD_HEADS // SSD_GROUPS
GW = GH * SSD_P


def _softplus(x):
    return jnp.maximum(x, 0.0) + jnp.log1p(jnp.exp(-jnp.abs(x)))


def f_ssd(hbase, x, z, bm, cm, dtraw, s_in, dt_bias, a_log, dskip, ng):
    q = x.shape[0]
    dt = _softplus(dtraw + dt_bias)
    da = dt * (-jnp.exp(a_log))
    cs = cumsum_rows(da)
    g_cb = dot_nt(cm, bm)
    lane = lax.broadcasted_iota(jnp.int32, (q, 128), 1)
    lane1 = lax.broadcasted_iota(jnp.int32, (1, 128), 1)
    causal = lax.broadcasted_iota(jnp.int32, (q, q), 0) >= lax.broadcasted_iota(jnp.int32, (q, q), 1)
    last = lax.broadcasted_iota(jnp.int32, (q, GW), 0) == q - 1
    dt_cols, cs_cols, d_cols = [], [], []
    for r in range(GH):
        sel = lane == hbase + r
        dt_cols.append(jnp.sum(jnp.where(sel, dt, 0.0), axis=1, keepdims=True))
        cs_cols.append(jnp.sum(jnp.where(sel, cs, 0.0), axis=1, keepdims=True))
        d_cols.append(jnp.sum(jnp.where(lane1 == hbase + r, dskip, 0.0), axis=1, keepdims=True))
    spread = lambda cols: jnp.concatenate([jnp.broadcast_to(c, (c.shape[0], SSD_P)) for c in cols], axis=1)
    dt_x, cs_x, d_x = spread(dt_cols), spread(cs_cols), spread(d_cols)
    cs_last = jnp.sum(jnp.where(last, cs_x, 0.0), axis=0, keepdims=True)
    xdt = x * dt_x
    y_diag = []
    for r in range(GH):
        m1 = jnp.broadcast_to(cs_cols[r], (q, q))
        decay = jnp.where(causal, jnp.exp(jnp.where(causal, m1 - m1.T, 0.0)), 0.0)
        y_diag.append(dot_nn(g_cb * decay, xdt[:, r * SSD_P:(r + 1) * SSD_P]))
    s_c = dot_tn(bm, xdt * jnp.exp(cs_last - cs_x))
    y_off = dot_nn(cm, s_in) * jnp.exp(cs_x)
    y = jnp.concatenate(y_diag, axis=1) + y_off + x * d_x
    s_out = s_in * jnp.exp(cs_last) + s_c
    y = y * (z * jax.nn.sigmoid(z))
    y = y * lax.rsqrt(jnp.mean(y * y, axis=-1, keepdims=True) + EPS) * ng
    return y, s_out


GPS_FWD, GPS_BWD = 2, 1


def _ssd_in_specs(cmap, gps):
    PW, PN = gps * GW, gps * SSD_N
    return [
        pl.BlockSpec((CHUNK, PW), lambda g, c: (cmap(c), g)),
        pl.BlockSpec((CHUNK, PW), lambda g, c: (cmap(c), C_Z // PW + g)),
        pl.BlockSpec((CHUNK, PN), lambda g, c: (cmap(c), SSD_INNER // PN + g)),
        pl.BlockSpec((CHUNK, PN), lambda g, c: (cmap(c), (SSD_INNER + SSD_GROUPS * SSD_N) // PN + g)),
        pl.BlockSpec((CHUNK, 128), lambda g, c: (cmap(c), C_DT // 128)),
    ]


def _ssd_param_specs(gps):
    return [pl.BlockSpec((1, 128), lambda g, c: (0, 0))] * 3 + [pl.BlockSpec((1, gps * GW), lambda g, c: (0, g))]


def _cols(ref, k, width):
    return ref[:, pl.ds(k * width, width)]


def ssd_fwd(name, xact, proj, dt_bias, a_log, dskip, ng, ex=None):
    t_len = xact.shape[0]
    nc = t_len // CHUNK
    h = _hosted(ex)
    GPS = GPS_FWD
    PW, PN, NPAIR = GPS * GW, GPS * SSD_N, SSD_GROUPS // GPS

    def body(*refs):
        x_ref, z_ref, b_ref, c_ref, dt_ref, p1, p2, p3, p4 = refs[:9]
        y_ref, st_ref = refs[9 + h.n_in:11 + h.n_in]
        s_scr = refs[11 + h.n_in + h.n_out]
        g, c = pl.program_id(0), pl.program_id(1)
        h.at_start(refs, 9, 11 + h.n_in, (g == 0) & (c == 0))

        @pl.when(c == 0)
        def _():
            s_scr[...] = jnp.zeros(s_scr.shape, f32)

        for k in range(GPS):
            s_in = s_scr[k]
            st_ref[k, 0] = s_in
            y, s_out = f_ssd((g * GPS + k) * GH, _cols(x_ref, k, GW), _cols(z_ref, k, GW), _cols(b_ref, k, SSD_N),
                             _cols(c_ref, k, SSD_N), dt_ref[...], s_in, p1[...], p2[...], p3[...], _cols(p4, k, GW))
            y_ref[:, pl.ds(k * GW, GW)] = y.astype(y_ref.dtype)
            s_scr[k] = s_out
        h.at_end(refs, 9, 11 + h.n_in, (g == NPAIR - 1) & (c == nc - 1))

    res = pl.pallas_call(
        body, name=name, grid=(NPAIR, nc),
        in_specs=_ssd_in_specs(lambda c: c, GPS) + _ssd_param_specs(GPS) + h.in_specs,
        out_specs=[pl.BlockSpec((CHUNK, PW), lambda g, c: (c, g)),
                   pl.BlockSpec((GPS, 1, SSD_N, GW), lambda g, c: (g, c, 0, 0))] + h.out_specs,
        out_shape=[jax.ShapeDtypeStruct((t_len, SSD_INNER), bf16),
                   jax.ShapeDtypeStruct((SSD_GROUPS, nc, SSD_N, GW), f32)] + h.out_shape,
        scratch_shapes=[pltpu.VMEM((GPS, SSD_N, GW), f32)] + h.scratch,
        input_output_aliases=h.aliases(9, 2),
        compiler_params=_params(("arbitrary", "arbitrary")),
    )(xact, proj, xact, xact, proj, dt_bias, a_log, dskip, ng, *h.ins)
    return res[0], res[1], list(res[2:])


def ssd_bwd(name, xact, proj, states, dy, dt_bias, a_log, dskip, ng, ex=None):
    t_len = xact.shape[0]
    nc = t_len // CHUNK
    rev = lambda c: nc - 1 - c
    h = _hosted(ex)
    GPS = GPS_BWD
    PW, PN, NPAIR = GPS * GW, GPS * SSD_N, SSD_GROUPS // GPS

    def body(*refs):
        x_ref, z_ref, b_ref, c_ref, dt_ref, st_ref, dy_ref, p1, p2, p3, p4 = refs[:11]
        dx_ref, db_ref, dc_ref, dz_ref, ddt_ref, d1, d2, d3, d4 = refs[11 + h.n_in:20 + h.n_in]
        ds_scr = refs[20 + h.n_in + h.n_out]
        g, c = pl.program_id(0), pl.program_id(1)
        h.at_start(refs, 11, 20 + h.n_in, (g == 0) & (c == 0))

        @pl.when(c == 0)
        def _():
            ds_scr[...] = jnp.zeros(ds_scr.shape, f32)
            d4[...] = jnp.zeros(d4.shape, f32)

        @pl.when((c == 0) & (g == 0))
        def _():
            for r in (d1, d2, d3):
                r[...] = jnp.zeros(r.shape, f32)

        for k in range(GPS):
            fn = functools.partial(f_ssd, (g * GPS + k) * GH)
            _, vjp = jax.vjp(fn, _cols(x_ref, k, GW), _cols(z_ref, k, GW), _cols(b_ref, k, SSD_N), _cols(c_ref, k, SSD_N),
                             dt_ref[...], st_ref[k, 0], p1[...], p2[...], p3[...], _cols(p4, k, GW))
            dx, dz, db, dc, ddt, ds_in, e1, e2, e3, e4 = vjp((_cols(dy_ref, k, GW).astype(f32), ds_scr[k]))
            dx_ref[:, pl.ds(k * GW, GW)] = dx
            dz_ref[:, pl.ds(k * GW, GW)] = dz.astype(dz_ref.dtype)
            db_ref[:, pl.ds(k * SSD_N, SSD_N)] = db
            dc_ref[:, pl.ds(k * SSD_N, SSD_N)] = dc
            ddt_ref[k] = ddt
            ds_scr[k] = ds_in
            d1[...] += e1
            d2[...] += e2
            d3[...] += e3
            d4[:, pl.ds(k * GW, GW)] += e4
        h.at_end(refs, 11, 20 + h.n_in, (g == NPAIR - 1) & (c == nc - 1))

    res = pl.pallas_call(
        body, name=name, grid=(NPAIR, nc),
        in_specs=_ssd_in_specs(rev, GPS) + [
            pl.BlockSpec((GPS, 1, SSD_N, GW), lambda g, c: (g, rev(c), 0, 0)),
            pl.BlockSpec((CHUNK, PW), lambda g, c: (rev(c), g)),
        ] + _ssd_param_specs(GPS) + h.in_specs,
        out_specs=[
            pl.BlockSpec((CHUNK, PW), lambda g, c: (rev(c), g)),
            pl.BlockSpec((CHUNK, PN), lambda g, c: (rev(c), g)),
            pl.BlockSpec((CHUNK, PN), lambda g, c: (rev(c), g)),
            pl.BlockSpec((CHUNK, PW), lambda g, c: (rev(c), g)),
            pl.BlockSpec((GPS, CHUNK, 128), lambda g, c: (g, rev(c), 0)),
        ] + _ssd_param_specs(GPS) + h.out_specs,
        out_shape=[
            jax.ShapeDtypeStruct((t_len, SSD_INNER), f32),
            jax.ShapeDtypeStruct((t_len, SSD_GROUPS * SSD_N), f32),
            jax.ShapeDtypeStruct((t_len, SSD_GROUPS * SSD_N), f32),
            jax.ShapeDtypeStruct((t_len, SSD_INNER), bf16),
            jax.ShapeDtypeStruct((SSD_GROUPS, t_len, 128), f32),
            jax.ShapeDtypeStruct((1, 128), f32), jax.ShapeDtypeStruct((1, 128), f32), jax.ShapeDtypeStruct((1, 128), f32),
            jax.ShapeDtypeStruct((1, SSD_INNER), f32),
        ] + h.out_shape,
        scratch_shapes=[pltpu.VMEM((GPS, SSD_N, GW), f32)] + h.scratch,
        input_output_aliases=h.aliases(11, 9),
        compiler_params=_params(("arbitrary", "arbitrary")),
    )(xact, proj, xact, xact, proj, states, dy, dt_bias, a_log, dskip, ng, *h.ins)
    return list(res[:9]), list(res[9:])


def _head_norm(t, g):
    return t * lax.rsqrt(jnp.mean(t * t, axis=-1, keepdims=True) + EPS) * g


def f_swa(has_prev, q, kp, kc, vp, vc, qg, kg, sinks, *bias):
    rep = ATTN_HEADS // ATTN_KV
    qi = lax.broadcasted_iota(jnp.int32, (rep * BLK, BLK), 0) & (BLK - 1)
    ki = lax.broadcasted_iota(jnp.int32, (rep * BLK, BLK), 1)
    mask_p = (ki > qi) & has_prev
    mask_c = ki <= qi
    lane1 = lax.broadcasted_iota(jnp.int32, (1, 128), 1)
    scale = ATTN_D ** -0.5
    outs = []
    for g in range(ATTN_KV):
        sl = slice(g * ATTN_D, (g + 1) * ATTN_D)
        kpn, kcn = _head_norm(kp[:, sl], kg), _head_norm(kc[:, sl], kg)
        heads = range(g * rep, (g + 1) * rep)
        qn = _head_norm(jnp.concatenate([q[:, h * ATTN_D:(h + 1) * ATTN_D] for h in heads], axis=0), qg)
        lp = jnp.where(mask_p, dot_nt(qn, kpn) * scale + bias[g][:, :BLK], NEG_INF)
        lc = jnp.where(mask_c, dot_nt(qn, kcn) * scale + bias[g][:, BLK:], NEG_INF)
        sink = jnp.concatenate([jnp.broadcast_to(jnp.sum(jnp.where(lane1 == h, sinks, 0.0), axis=1, keepdims=True), (BLK, 1))
                                for h in heads], axis=0)
        m = lax.stop_gradient(jnp.maximum(jnp.maximum(jnp.max(lp, axis=-1, keepdims=True), jnp.max(lc, axis=-1, keepdims=True)), sink))
        pp, pc = jnp.exp(lp - m), jnp.exp(lc - m)
        den = jnp.sum(pp, axis=-1, keepdims=True) + jnp.sum(pc, axis=-1, keepdims=True) + jnp.exp(sink - m)
        o = (dot_nn(pp, vp[:, sl]) + dot_nn(pc, vc[:, sl])) * (1.0 / den)
        outs += [o[r * BLK:(r + 1) * BLK] for r in range(rep)]
    return jnp.concatenate(outs, axis=1)


_SWA_BIAS_SHAPE = (ATTN_KV, ATTN_HEADS // ATTN_KV * BLK, 2 * BLK)
_SWA_BIAS_SPEC = pl.BlockSpec(_SWA_BIAS_SHAPE, lambda i: (0, 0, 0))


def _swa_in_specs():
    prev = lambda i: jnp.maximum(i - 1, 0)
    kw = ATTN_KV * ATTN_D
    return [
        pl.BlockSpec((BLK, D), lambda i: (i, C_Q // D)),
        pl.BlockSpec((BLK, kw), lambda i: (prev(i), C_K // kw)),
        pl.BlockSpec((BLK, kw), lambda i: (i, C_K // kw)),
        pl.BlockSpec((BLK, kw), lambda i: (prev(i), C_V // kw)),
        pl.BlockSpec((BLK, kw), lambda i: (i, C_V // kw)),
        pl.BlockSpec((1, ATTN_D), lambda i: (0, 0)),
        pl.BlockSpec((1, ATTN_D), lambda i: (0, 0)),
        pl.BlockSpec((1, 128), lambda i: (0, 0)),
        _SWA_BIAS_SPEC,
    ]


def swa_fwd(name, proj, qg, kg, sinks, bias, ex=None):
    t_len = proj.shape[0]
    nb = t_len // BLK
    h = _hosted(ex)

    def body(*refs):
        q_ref, kp_ref, kc_ref, vp_ref, vc_ref, qg_ref, kg_ref, s_ref, b_ref = refs[:9]
        o_ref = refs[9 + h.n_in]
        i = pl.program_id(0)
        h.at_start(refs, 9, 10 + h.n_in, i == 0)
        o = f_swa(i > 0, q_ref[...], kp_ref[...], kc_ref[...], vp_ref[...], vc_ref[...], qg_ref[...], kg_ref[...],
                  s_ref[...], *[b_ref[kv] for kv in range(ATTN_KV)])
        o_ref[...] = o.astype(o_ref.dtype)
        h.at_end(refs, 9, 10 + h.n_in, i == nb - 1)

    res = pl.pallas_call(
        body, name=name, grid=(nb,), in_specs=_swa_in_specs() + h.in_specs,
        out_specs=[pl.BlockSpec((BLK, D), lambda i: (i, 0))] + h.out_specs,
        out_shape=[jax.ShapeDtypeStruct((t_len, D), bf16)] + h.out_shape,
        scratch_shapes=h.scratch, input_output_aliases=h.aliases(9, 1),
        compiler_params=_params(("arbitrary",)),
    )(proj, proj, proj, proj, proj, qg, kg, sinks, bias.reshape(_SWA_BIAS_SHAPE), *h.ins)
    return res[0], list(res[1:])


def swa_bwd(name, proj, do, qg, kg, sinks, bias, ex=None):
    t_len = proj.shape[0]
    nb = t_len // BLK
    kw = ATTN_KV * ATTN_D
    h = _hosted(ex)

    def body(*refs):
        q_ref, kp_ref, kc_ref, vp_ref, vc_ref, qg_ref, kg_ref, s_ref, b_ref, do_ref = refs[:10]
        dq_ref, dkp_ref, dkc_ref, dvp_ref, dvc_ref, dqg_ref, dkg_ref, ds_ref, db_ref = refs[10 + h.n_in:19 + h.n_in]
        i = pl.program_id(0)
        h.at_start(refs, 10, 19 + h.n_in, i == 0)

        @pl.when(i == 0)
        def _():
            for r in (dqg_ref, dkg_ref, ds_ref, db_ref):
                r[...] = jnp.zeros(r.shape, f32)

        fn = functools.partial(f_swa, i > 0)
        _, vjp = jax.vjp(fn, q_ref[...], kp_ref[...], kc_ref[...], vp_ref[...], vc_ref[...], qg_ref[...], kg_ref[...],
                         s_ref[...], *[b_ref[kv] for kv in range(ATTN_KV)])
        ds = vjp(do_ref[...].astype(f32))
        dq_ref[...] = ds[0].astype(dq_ref.dtype)
        dkp_ref[...] = ds[1]
        dkc_ref[...] = ds[2]
        dvp_ref[...] = ds[3]
        dvc_ref[...] = ds[4]
        dqg_ref[...] += ds[5]
        dkg_ref[...] += ds[6]
        ds_ref[...] += ds[7]
        for kv in range(ATTN_KV):
            db_ref[kv] += ds[8 + kv]
        h.at_end(refs, 10, 19 + h.n_in, i == nb - 1)

    row = lambda w: pl.BlockSpec((BLK, w), lambda i: (i, 0))
    res = pl.pallas_call(
        body, name=name, grid=(nb,), in_specs=_swa_in_specs() + [row(D)] + h.in_specs,
        out_specs=[row(D), row(kw), row(kw), row(kw), row(kw),
                   pl.BlockSpec((1, ATTN_D), lambda i: (0, 0)), pl.BlockSpec((1, ATTN_D), lambda i: (0, 0)),
                   pl.BlockSpec((1, 128), lambda i: (0, 0)), _SWA_BIAS_SPEC] + h.out_specs,
        out_shape=[jax.ShapeDtypeStruct((t_len, D), bf16)] + [jax.ShapeDtypeStruct((t_len, kw), f32)] * 4
        + [jax.ShapeDtypeStruct((1, ATTN_D), f32)] * 2 + [jax.ShapeDtypeStruct((1, 128), f32),
                                                          jax.ShapeDtypeStruct(_SWA_BIAS_SHAPE, f32)] + h.out_shape,
        scratch_shapes=h.scratch, input_output_aliases=h.aliases(10, 9),
        compiler_params=_params(("arbitrary",)),
    )(proj, proj, proj, proj, proj, qg, kg, sinks, bias.reshape(_SWA_BIAS_SHAPE), do, *h.ins)
    return list(res[:8]) + [res[8].reshape(ATTN_HEADS, BLK, 2 * BLK)], list(res[9:])


def _bucket_table():
    qi = np.arange(BLK)[:, None] + BLK
    kj = np.arange(2 * BLK)[None, :]
    dist = qi - kj
    max_exact = REL_BUCKETS // 2
    d = np.maximum(dist, 1).astype(np.float32)
    large = max_exact + (np.log(d / max_exact) / math.log(128 / max_exact) * (REL_BUCKETS - max_exact)).astype(np.int32)
    large = np.minimum(large, REL_BUCKETS - 1)
    return np.where(dist < max_exact, np.maximum(dist, 0), large).astype(np.int32)


def relbias_fwd(table, bucket):
    def body(t_ref, bk_ref, o_ref):
        bk = bk_ref[...]
        for h in range(ATTN_HEADS):
            acc = jnp.zeros((BLK, 2 * BLK), f32)
            for b in range(REL_BUCKETS):
                acc = jnp.where(bk == b, t_ref[b, h], acc)
            o_ref[h] = acc

    return pl.pallas_call(
        body, name="relbias_fwd", out_shape=jax.ShapeDtypeStruct((ATTN_HEADS, BLK, 2 * BLK), f32),
        in_specs=[pl.BlockSpec(memory_space=pltpu.SMEM), pl.BlockSpec(memory_space=pltpu.VMEM)],
        out_specs=pl.BlockSpec(memory_space=pltpu.VMEM),
    )(table, bucket)


def relbias_bwd(dbias, bucket):
    n = len(dbias)

    def body(*refs):
        bk = refs[n][...]
        o_ref = refs[n + 1]
        row = lax.broadcasted_iota(jnp.int32, (REL_BUCKETS, 128), 0)
        lane = lax.broadcasted_iota(jnp.int32, (REL_BUCKETS, 128), 1)
        res = jnp.zeros((REL_BUCKETS, 128), f32)
        for h in range(ATTN_HEADS):
            tot = refs[0][h]
            for k in range(1, n):
                tot = tot + refs[k][h]
            for b in range(REL_BUCKETS):
                part = jnp.sum(jnp.sum(jnp.where(bk == b, tot, 0.0), axis=1, keepdims=True), axis=0, keepdims=True)
                res = jnp.where((row == b) & (lane == h), part, res)
        o_ref[...] = res

    return pl.pallas_call(
        body, name="relbias_bwd", out_shape=jax.ShapeDtypeStruct((REL_BUCKETS, 128), f32),
        in_specs=[pl.BlockSpec(memory_space=pltpu.VMEM)] * (n + 1),
        out_specs=pl.BlockSpec(memory_space=pltpu.VMEM),
    )(*dbias, bucket)[:, :ATTN_HEADS]


XA_TILE = 512


def f_xattn(q, k, v, qg, kg):
    qn, kn = _head_norm(q, qg), _head_norm(k, kg)
    logits = dot_nt(qn, kn) * (XD ** -0.5)
    p = jnp.exp(logits - lax.stop_gradient(jnp.max(logits, axis=-1, keepdims=True)))
    return dot_nn(p * (1.0 / jnp.sum(p, axis=-1, keepdims=True)), v)


def _xa_in_specs(tq):
    return [
        pl.BlockSpec((tq, XD), lambda h, i: (i, h)),
        pl.BlockSpec((MEM_LEN, XD), lambda h, i: (0, h)),
        pl.BlockSpec((MEM_LEN, XD), lambda h, i: (0, XH + h)),
        pl.BlockSpec((1, XD), lambda h, i: (0, 0)),
        pl.BlockSpec((1, XD), lambda h, i: (0, 0)),
    ]


def xa_fwd(name, xq, kv, qg, kg):
    t_len = xq.shape[0]
    tq = min(XA_TILE, t_len)

    def body(q_ref, k_ref, v_ref, qg_ref, kg_ref, o_ref):
        o_ref[...] = f_xattn(q_ref[...], k_ref[...], v_ref[...], qg_ref[...], kg_ref[...]).astype(o_ref.dtype)

    return pl.pallas_call(
        body, name=name, grid=(XH, t_len // tq), in_specs=_xa_in_specs(tq),
        out_specs=pl.BlockSpec((tq, XD), lambda h, i: (i, h)), out_shape=jax.ShapeDtypeStruct((t_len, D), bf16),
        compiler_params=_params(("parallel", "parallel")),
    )(xq, kv, kv, qg, kg)


def xa_bwd(name, xq, kv, do, qg, kg):
    t_len = xq.shape[0]
    tq = min(XA_TILE, t_len)

    def body(q_ref, k_ref, v_ref, qg_ref, kg_ref, do_ref, dq_ref, dk_ref, dv_ref, dqg_ref, dkg_ref):
        h, i = pl.program_id(0), pl.program_id(1)

        @pl.when(i == 0)
        def _():
            dk_ref[...] = jnp.zeros(dk_ref.shape, f32)
            dv_ref[...] = jnp.zeros(dv_ref.shape, f32)

        @pl.when((i == 0) & (h == 0))
        def _():
            dqg_ref[...] = jnp.zeros(dqg_ref.shape, f32)
            dkg_ref[...] = jnp.zeros(dkg_ref.shape, f32)

        _, vjp = jax.vjp(f_xattn, q_ref[...], k_ref[...], v_ref[...], qg_ref[...], kg_ref[...])
        dq, dk, dv, dqg, dkg = vjp(do_ref[...].astype(f32))
        dq_ref[...] = dq.astype(dq_ref.dtype)
        dk_ref[...] += dk
        dv_ref[...] += dv
        dqg_ref[...] += dqg
        dkg_ref[...] += dkg

    return pl.pallas_call(
        body, name=name, grid=(XH, t_len // tq),
        in_specs=_xa_in_specs(tq) + [pl.BlockSpec((tq, XD), lambda h, i: (i, h))],
        out_specs=[pl.BlockSpec((tq, XD), lambda h, i: (i, h)),
                   pl.BlockSpec((MEM_LEN, XD), lambda h, i: (0, h)), pl.BlockSpec((MEM_LEN, XD), lambda h, i: (0, h)),
                   pl.BlockSpec((1, XD), lambda h, i: (0, 0)), pl.BlockSpec((1, XD), lambda h, i: (0, 0))],
        out_shape=[jax.ShapeDtypeStruct((t_len, D), bf16), jax.ShapeDtypeStruct((MEM_LEN, D), f32),
                   jax.ShapeDtypeStruct((MEM_LEN, D), f32), jax.ShapeDtypeStruct((1, XD), f32), jax.ShapeDtypeStruct((1, XD), f32)],
        compiler_params=_params(("arbitrary", "arbitrary")),
    )(xq, kv, kv, qg, kg, do)


def loss_head(y, target, tile):
    t_len = y.shape[0]

    def body(y_ref, t_ref, dy_ref, l_ref):
        i = pl.program_id(0)

        @pl.when(i == 0)
        def _():
            l_ref[...] = jnp.zeros(l_ref.shape, f32)

        err = y_ref[...] - t_ref[...]
        dy_ref[...] = err * (1.0 / D)
        l_ref[...] += 0.5 * jnp.sum(jnp.sum(err * err, axis=1, keepdims=True), axis=0, keepdims=True) * (1.0 / D)

    row = pl.BlockSpec((tile, D), lambda i: (i, 0))
    return pl.pallas_call(
        body, name="loss_head", grid=(t_len // tile,), in_specs=[row, row],
        out_specs=[row, pl.BlockSpec((8, 128), lambda i: (0, 0))],
        out_shape=[jax.ShapeDtypeStruct((t_len, D), f32), jax.ShapeDtypeStruct((8, 128), f32)],
        compiler_params=_params(("arbitrary",)),
    )(y, target)


ADAM_BLOCK_ELEMS = 512 * 1024


def adamw(name, w, g, m, v):
    rows, cols = w.shape
    tile = rows
    if rows * cols > ADAM_BLOCK_ELEMS:
        tile = _pick(rows, [t for t in (512, 256, 128, 64, 32, 16, 8) if t * cols <= ADAM_BLOCK_ELEMS])

    def body(w_ref, g_ref, m_ref, v_ref, d_ref, nm_ref, nv_ref):
        gg = g_ref[...]
        nm = ADAM_B1 * m_ref[...] + (1.0 - ADAM_B1) * gg
        nv = ADAM_B2 * v_ref[...] + (1.0 - ADAM_B2) * jnp.square(gg)
        m_hat = nm / (1.0 - ADAM_B1 ** ADAM_STEP)
        v_hat = nv / (1.0 - ADAM_B2 ** ADAM_STEP)
        d_ref[...] = -ADAM_LR * (m_hat / (jnp.sqrt(v_hat) + ADAM_EPS) + ADAM_WD * w_ref[...])
        nm_ref[...] = nm
        nv_ref[...] = nv

    spec = pl.BlockSpec((tile, cols), lambda i: (i, 0))
    return pl.pallas_call(
        body, name=name, grid=(rows // tile,), in_specs=[spec] * 4, out_specs=[spec] * 3,
        out_shape=[jax.ShapeDtypeStruct((rows, cols), f32)] * 3, compiler_params=_params(("parallel",)),
    )(w, g, m, v)


def adamw_layers(name, w, m, v, gsrcs, r0):
    depth, rows, cols = w.shape
    tile = max(t for t in range(8, rows + 1, 8) if rows % t == 0 and r0 % t == 0 and t * cols <= ADAM_BLOCK_ELEMS)

    def body(*refs):
        w_ref, m_ref, v_ref = refs[:3]
        g_refs = refs[3:3 + depth]
        go_ref, d_ref, nm_ref, nv_ref = refs[3 + depth:]
        layer = pl.program_id(0)
        gg = g_refs[0][...]
        for k in range(1, depth):
            gg = jnp.where(layer == k, g_refs[k][...], gg)
        nm = ADAM_B1 * m_ref[0] + (1.0 - ADAM_B1) * gg
        nv = ADAM_B2 * v_ref[0] + (1.0 - ADAM_B2) * jnp.square(gg)
        m_hat = nm / (1.0 - ADAM_B1 ** ADAM_STEP)
        v_hat = nv / (1.0 - ADAM_B2 ** ADAM_STEP)
        go_ref[0] = gg
        d_ref[0] = -ADAM_LR * (m_hat / (jnp.sqrt(v_hat) + ADAM_EPS) + ADAM_WD * w_ref[0])
        nm_ref[0] = nm
        nv_ref[0] = nv

    spec = pl.BlockSpec((1, tile, cols), lambda l, i: (l, i, 0))
    g_specs = [pl.BlockSpec((tile, cols), functools.partial(lambda l, i, k: (jnp.where(l == k, r0 // tile + i, 0), 0), k=k))
               for k in range(depth)]
    return pl.pallas_call(
        body, name=name, grid=(depth, rows // tile), in_specs=[spec] * 3 + g_specs, out_specs=[spec] * 4,
        out_shape=[jax.ShapeDtypeStruct(w.shape, f32)] * 4, compiler_params=_params(("parallel", "parallel")),
    )(w, m, v, *gsrcs)


MESH = pl.DeviceIdType.MESH
HBM_SPEC = pl.BlockSpec(memory_space=pltpu.HBM)


def _place():
    x, y, c = lax.axis_index("x"), lax.axis_index("y"), lax.axis_index("c")
    chips = [(1 - x, y), (x, 1 - y), (1 - x, 1 - y)]
    return x, y, c, chips


def _rcopy(src, dst, send_sems, recv_sems, k, to):
    return pltpu.make_async_remote_copy(src_ref=src, dst_ref=dst, send_sem=send_sems.at[k], recv_sem=recv_sems.at[k],
                                        device_id=to, device_id_type=MESH)


class Exchange:
    def __init__(self, ins, outs, n_sems, n_local, start, wait, aliases=None):
        self.ins, self.outs, self.n_sems, self.n_local = list(ins), list(outs), n_sems, n_local
        self.start, self.wait, self.aliases = start, wait, dict(aliases or {})


class _hosted:
    def __init__(self, ex):
        self.ex = ex
        self.ins = list(ex.ins) if ex else []
        self.out_shape = list(ex.outs) if ex else []
        self.n_in, self.n_out = len(self.ins), len(self.out_shape)
        self.in_specs, self.out_specs = [HBM_SPEC] * self.n_in, [HBM_SPEC] * self.n_out
        self.scratch = [pltpu.SemaphoreType.DMA((ex.n_sems,)), pltpu.SemaphoreType.DMA((ex.n_sems,)),
                        pltpu.SemaphoreType.DMA((max(ex.n_local, 1),))] if ex else []

    def aliases(self, first_in, first_out):
        return {first_in + a: first_out + b for a, b in self.ex.aliases.items()} if self.ex else {}

    def _args(self, refs, i0, o0):
        return refs[i0:i0 + self.n_in], refs[o0:o0 + self.n_out], refs[-3], refs[-2], refs[-1]

    def at_start(self, refs, i0, o0, pred):
        if self.ex is not None:
            pl.when(pred)(lambda: self.ex.start(*self._args(refs, i0, o0)))

    def at_end(self, refs, i0, o0, pred):
        if self.ex is not None:
            pl.when(pred)(lambda: self.ex.wait(*self._args(refs, i0, o0)))


def run_exchange(name, ex):
    h = _hosted(ex)

    def body(*refs):
        args = h._args(refs, 0, h.n_in)
        ex.start(*args)
        ex.wait(*args)

    res = pl.pallas_call(
        body, name=name, out_shape=h.out_shape, in_specs=h.in_specs, out_specs=h.out_specs, scratch_shapes=h.scratch,
        input_output_aliases=h.aliases(0, 0), compiler_params=pltpu.CompilerParams(has_side_effects=True),
    )(*h.ins)
    return list(res)


def _halves(ref, axis, c):
    hh = ref.shape[axis] // 2
    return pl.ds(pl.multiple_of(c * hh, 16), hh), pl.ds(pl.multiple_of((1 - c) * hh, 16), hh)


def ex_gather_ici(shards):
    n = len(shards)

    def copies(ins, outs, ssem, rsem, landing):
        x, y, c, chips = _place()
        me = 2 * x + y
        res = []
        for a, (s, o) in enumerate(zip(ins, outs)):
            mine, _ = _halves(s, 0, c)
            for j, (cx, cy) in enumerate(chips):
                slot = 2 * cx + cy if landing else me
                res.append(_rcopy(s.at[mine], o.at[slot, mine], ssem, rsem, 4 * a + j, (cx, cy, c)))
            res.append(_rcopy(s, o.at[me], ssem, rsem, 4 * a + 3, (x, y, 1 - c)))
        return res

    def start(ins, outs, ssem, rsem, lsem):
        for cp in copies(ins, outs, ssem, rsem, False):
            cp.start()

    def wait(ins, outs, ssem, rsem, lsem):
        for cp in copies(ins, outs, ssem, rsem, True):
            cp.wait_recv()
        for cp in copies(ins, outs, ssem, rsem, False):
            cp.wait_send()

    outs = [jax.ShapeDtypeStruct((N_CHIPS,) + s.shape, s.dtype) for s in shards]
    return Exchange(shards, outs, 4 * n, 0, start, wait)


def ex_gather_d2d(gathered):
    n = len(gathered)

    def copies(outs, ssem, rsem, landing):
        x, y, c, chips = _place()
        res = []
        for a, o in enumerate(outs):
            mine, theirs = _halves(o, 1, c)
            for j, (cx, cy) in enumerate(chips):
                rows = o.at[2 * cx + cy, theirs if landing else mine]
                res.append(_rcopy(rows, rows, ssem, rsem, 3 * a + j, (x, y, 1 - c)))
        return res

    def start(ins, outs, ssem, rsem, lsem):
        for cp in copies(outs, ssem, rsem, False):
            cp.start()

    def wait(ins, outs, ssem, rsem, lsem):
        for cp in copies(outs, ssem, rsem, True):
            cp.wait_recv()
        for cp in copies(outs, ssem, rsem, False):
            cp.wait_send()

    outs = [jax.ShapeDtypeStruct(g.shape, g.dtype) for g in gathered]
    return Exchange(gathered, outs, 3 * n, 0, start, wait, aliases={a: a for a in range(n)})


def ex_swap_halves(gs):
    n = len(gs)

    def copies(ins, outs, ssem, rsem):
        x, y, c, _ = _place()
        return [_rcopy(g.at[s, 1 - c], o.at[s], ssem, rsem, N_CHIPS * a + s, (x, y, 1 - c))
                for a, (g, o) in enumerate(zip(ins, outs)) for s in range(N_CHIPS)]

    def start(ins, outs, ssem, rsem, lsem):
        for cp in copies(ins, outs, ssem, rsem):
            cp.start()

    def wait(ins, outs, ssem, rsem, lsem):
        for cp in copies(ins, outs, ssem, rsem):
            cp.wait()

    outs = [jax.ShapeDtypeStruct((N_CHIPS,) + g.shape[2:], g.dtype) for g in gs]
    return Exchange(gs, outs, N_CHIPS * n, 0, start, wait)


def ex_scatter(ps):
    n = len(ps)

    def copies(ins, outs, ssem, rsem):
        x, y, c, chips = _place()
        return [_rcopy(p.at[2 * cx + cy], o.at[j], ssem, rsem, 3 * a + j, (cx, cy, c))
                for a, (p, o) in enumerate(zip(ins, outs)) for j, (cx, cy) in enumerate(chips)]

    def start(ins, outs, ssem, rsem, lsem):
        for cp in copies(ins, outs, ssem, rsem):
            cp.start()

    def wait(ins, outs, ssem, rsem, lsem):
        for cp in copies(ins, outs, ssem, rsem):
            cp.wait()

    outs = [jax.ShapeDtypeStruct((3,) + p.shape[1:], p.dtype) for p in ps]
    return Exchange(ps, outs, 3 * n, 0, start, wait)


def ex_share(rs):
    n = len(rs)

    def copies(ins, outs, ssem, rsem):
        x, y, c, _ = _place()
        return [_rcopy(r, o, ssem, rsem, a, (x, y, 1 - c)) for a, (r, o) in enumerate(zip(ins, outs))]

    def start(ins, outs, ssem, rsem, lsem):
        for cp in copies(ins, outs, ssem, rsem):
            cp.start()

    def wait(ins, outs, ssem, rsem, lsem):
        for cp in copies(ins, outs, ssem, rsem):
            cp.wait()

    outs = [jax.ShapeDtypeStruct(r.shape, r.dtype) for r in rs]
    return Exchange(rs, outs, n, 0, start, wait)


ADD_BLOCK_ELEMS = 384 * 1024


def _add_tile(rows, cols):
    return _pick(rows, [t for t in (1920, 960, 512, 384, 256, 128, 64, 32, 16) if t * cols <= ADD_BLOCK_ELEMS])


def add_halves(name, g, got, c_arr):
    _, _, rows, cols = g.shape
    tile = _add_tile(rows, cols)

    def body(c_ref, g_ref, r_ref, o_ref):
        o_ref[...] = (g_ref[0] + r_ref[...]).astype(o_ref.dtype)

    return pl.pallas_call(
        body, name=name,
        grid_spec=pltpu.PrefetchScalarGridSpec(
            num_scalar_prefetch=1, grid=(N_CHIPS, rows // tile),
            in_specs=[pl.BlockSpec((1, 1, tile, cols), lambda s, i, c_ref: (s, c_ref[0], i, 0)),
                      pl.BlockSpec((1, tile, cols), lambda s, i, c_ref: (s, i, 0))],
            out_specs=pl.BlockSpec((1, tile, cols), lambda s, i, c_ref: (s, i, 0))),
        out_shape=jax.ShapeDtypeStruct(got.shape, bf16), compiler_params=_params(("parallel", "parallel")),
    )(c_arr, g, got)


def add_partials(name, p, got, me_arr):
    _, rows, cols = p.shape
    tile = _add_tile(rows, cols)

    def body(me_ref, p_ref, r_ref, o_ref):
        o_ref[...] = ((p_ref[0].astype(f32) + r_ref[0].astype(f32)) + r_ref[1].astype(f32)) + r_ref[2].astype(f32)

    return pl.pallas_call(
        body, name=name,
        grid_spec=pltpu.PrefetchScalarGridSpec(
            num_scalar_prefetch=1, grid=(rows // tile,),
            in_specs=[pl.BlockSpec((1, tile, cols), lambda i, me_ref: (me_ref[0], i, 0)),
                      pl.BlockSpec((3, tile, cols), lambda i, me_ref: (0, i, 0))],
            out_specs=pl.BlockSpec((tile, cols), lambda i, me_ref: (i, 0))),
        out_shape=jax.ShapeDtypeStruct((rows, cols), f32), compiler_params=_params(("parallel",)),
    )(me_arr, p, got)


def join_halves(name, mine, theirs):
    rows, cols = mine.shape
    tile = _add_tile(rows, cols)

    def body(m_ref, t_ref, o_ref):
        south = lax.axis_index("c") == 0
        o_ref[0] = jnp.where(south, m_ref[...], t_ref[...])
        o_ref[1] = jnp.where(south, t_ref[...], m_ref[...])

    spec = pl.BlockSpec((tile, cols), lambda i: (i, 0))
    return pl.pallas_call(
        body, name=name, grid=(rows // tile,), in_specs=[spec, spec],
        out_specs=pl.BlockSpec((2, tile, cols), lambda i: (0, i, 0)),
        out_shape=jax.ShapeDtypeStruct((2, rows, cols), f32), compiler_params=_params(("parallel",)),
    )(mine, theirs)


N_DEV = 8


def allreduce_small(name, buf):
    m_per = buf.shape[0]

    def body(x_ref, out_ref, all_ref, send_sems, recv_sems, local_sem):
        x, y, c, chips = _place()
        me, sibling = (x, y, c), (x, y, 1 - c)

        def rows(px, py, pc):
            return all_ref.at[pl.ds(pl.multiple_of((4 * px + 2 * py + pc) * m_per, 8), m_per), :]

        def copy(k, block, to, src=None):
            return _rcopy(rows(*block) if src is None else src, rows(*block), send_sems, recv_sems, k, to)

        mine = pltpu.make_async_copy(x_ref, rows(*me), local_sem)
        mine.start()
        first = [copy(0, me, sibling, src=x_ref)]
        first += [copy(1 + j, me, (*chip, c), src=x_ref) for j, chip in enumerate(chips)]
        for cp in first:
            cp.start()
        passed = [copy(4 + j, (*chip, c), sibling) for j, chip in enumerate(chips)]
        for j, chip in enumerate(chips):
            copy(1 + j, (*chip, c), me).wait_recv()
            passed[j].start()
        copy(0, sibling, me).wait_recv()
        for j, chip in enumerate(chips):
            copy(4 + j, (*chip, 1 - c), me).wait_recv()
        for cp in first + passed:
            cp.wait_send()
        mine.wait()
        tot = all_ref[pl.ds(0, m_per), :]
        for d in range(1, N_DEV):
            tot = tot + all_ref[pl.ds(d * m_per, m_per), :]
        out_ref[...] = tot

    return pl.pallas_call(
        body, name=name, out_shape=jax.ShapeDtypeStruct((m_per, 128), f32),
        in_specs=[pl.BlockSpec(memory_space=pltpu.VMEM)], out_specs=pl.BlockSpec(memory_space=pltpu.VMEM),
        scratch_shapes=[pltpu.VMEM((N_DEV * m_per, 128), f32), pltpu.SemaphoreType.DMA((7,)), pltpu.SemaphoreType.DMA((7,)),
                        pltpu.SemaphoreType.DMA],
        compiler_params=pltpu.CompilerParams(has_side_effects=True, vmem_limit_bytes=VMEM_LIMIT_BYTES),
    )(buf)


def pack_shards(ws, layer):
    wide = jnp.concatenate([ws[n][layer].astype(bf16) for n in WIDE], axis=0)
    w_in_t = jnp.pad(jnp.transpose(ws["w_in"][layer]).astype(bf16), ((0, IN_PAD - IN_SHARD), (0, 0)))
    return [w_in_t, wide, ws["w_xkv"][layer].astype(bf16)]


def arrange_w_in(shards):
    parts = []
    for a, b, _ in sorted(ORIG_SEGS, key=lambda seg: seg[2]):
        for s in range(N_CHIPS):
            lo, hi = max(a, s * IN_SHARD), min(b, (s + 1) * IN_SHARD)
            if lo < hi:
                parts.append(shards[s][lo - s * IN_SHARD:hi - s * IN_SHARD])
    parts.append(jnp.zeros((PCOLS - C_DT - (OFF_Q - OFF_DT), shards.shape[2]), shards.dtype))
    return jnp.concatenate(parts, axis=0)


def shard_w_in_grad(g):
    out = []
    for s in range(N_CHIPS):
        for a, b, first in ORIG_SEGS:
            lo, hi = max(a, s * IN_SHARD), min(b, (s + 1) * IN_SHARD)
            if lo < hi:
                out.append(g[first + lo - a:first + hi - a])
        out.append(jnp.zeros((IN_PAD - IN_SHARD, g.shape[1]), g.dtype))
    return jnp.concatenate(out, axis=0).reshape(N_CHIPS, IN_PAD, g.shape[1])


def unpack_gathered(gathered):
    g_in, g_wide, g_xkv = gathered
    out = dict(w_in=arrange_w_in(g_in), w_xkv=jnp.concatenate([g_xkv[s] for s in range(N_CHIPS)], axis=1))
    for n in WIDE:
        piece = g_wide[:, WIDE_OFF[n]:WIDE_OFF[n] + WIDE_ROWS[n], :]
        if n == "w_mlp_up":
            out[n] = jnp.concatenate([piece[s] for s in range(N_CHIPS)], axis=1)
        else:
            out[n] = piece.reshape(N_CHIPS * WIDE_ROWS[n], D)
    return out


def _halved(a):
    return a.reshape(N_CHIPS, 2, a.shape[1] // 2, a.shape[2])


def pack_w_in_grad(g):
    return [_halved(shard_w_in_grad(g))]


def pack_other_grads(gs):
    cols = lambda g, q: jnp.concatenate([g[:, s * q:(s + 1) * q] for s in range(N_CHIPS)], axis=0).reshape(N_CHIPS, g.shape[0], q)
    wide = [cols(gs[n], D) if n == "w_mlp_up" else gs[n].reshape(N_CHIPS, WIDE_ROWS[n], D) for n in WIDE]
    return [_halved(jnp.concatenate(wide, axis=1)), _halved(cols(gs["w_xkv"], 2048 // N_CHIPS))]


ROW_TILE = 256


def _pad128(v):
    return jnp.pad(v.reshape(1, -1), ((0, 0), (0, 128 - v.shape[-1])))


def layer_forward(i, h, mem, w, p, bias, next_shards=None):
    tile = min(ROW_TILE, h.shape[0])
    s = dict(h=h)
    row = lambda a: a.reshape(1, -1)
    s["u"], = tile_fwd(f"rms_mix{i}", f_rms, [(h, D, 0)], [row(p["norm_mix"])], [(D, bf16)], tile)
    s["proj"] = proj = matmul(s["u"], w["w_in"], "nt", f"mm_in{i}")
    s["c1"], = conv_fwd(f"conv_a{i}", proj, C_CONV, C_CONV + D, p["conv_dw_w"], row(p["conv_dw_b"]), True, False)
    s["ca"], = tile_fwd(f"ln_silu{i}", f_ln_silu, [(s["c1"], D, 0)], [row(p["conv_ln_g"]), row(p["conv_ln_b"])], [(D, bf16)], tile)
    s["ya"] = matmul(s["ca"], w["w_conv_out"], "nn", f"mm_conv_out{i}")
    s["xpre"], s["xact"] = conv_fwd(f"conv_b{i}", proj, C_XBC, C_XBC, p["ssd_conv_w"], row(p["ssd_conv_b"]), False, True)
    ssd_p = (_pad128(p["ssd_dt_bias"]), _pad128(p["ssd_A_log"]), _pad128(p["ssd_D"]), row(p["ssd_norm_g"]))
    s["yB"], s["states"], landed = ssd_fwd(f"ssd_fwd{i}", s["xact"], proj, *ssd_p,
                                           ex=ex_gather_ici(next_shards) if next_shards else None)
    s["yb"] = matmul(s["yB"], w["w_ssd_out"], "nn", f"mm_ssd_out{i}")
    swa_p = (row(p["attn_q_norm"]), row(p["attn_k_norm"]), _pad128(p["attn_sinks"]))
    s["oc"], gathered = swa_fwd(f"swa_fwd{i}", proj, *swa_p, bias, ex=ex_gather_d2d(landed) if next_shards else None)
    s["yc"] = matmul(s["oc"], w["w_attn_out"], "nn", f"mm_attn_out{i}")
    gate_cols = [(proj, D, C_G // D + k) for k in range(3)]
    s["merged"], = tile_fwd(f"merge{i}", f_merge, gate_cols + [(s["ya"], D, 0), (s["yb"], D, 0), (s["yc"], D, 0)],
                            [row(p["gate_bias"])], [(D, bf16)], tile)
    s["h1"] = h1 = matmul(s["merged"], w["w_mix_out"], "nn", f"mm_mix_out{i}", residual=h)
    s["hx"], = tile_fwd(f"rms_x{i}", f_rms, [(h1, D, 0)], [row(p["norm_xattn"])], [(D, bf16)], tile)
    s["memh"], = tile_fwd(f"rms_mem{i}", f_rms, [(mem, D, 0)], [row(p["norm_mem"])], [(D, bf16)], MEM_LEN)
    s["xq"] = matmul(s["hx"], w["w_xq"], "nn", f"mm_xq{i}")
    s["kv"] = matmul(s["memh"], w["w_xkv"], "nn", f"mm_xkv{i}")
    s["xo"] = xa_fwd(f"xa_fwd{i}", s["xq"], s["kv"], row(p["xattn_q_norm"]), row(p["xattn_k_norm"]))
    s["h2"] = h2 = matmul(s["xo"], w["w_xo"], "nn", f"mm_xo{i}", residual=h1)
    s["um"], = tile_fwd(f"rms_mlp{i}", f_rms, [(h2, D, 0)], [row(p["norm_mlp"])], [(D, bf16)], tile)
    s["up"], s["act"] = matmul(s["um"], w["w_mlp_up"], "nn", f"mm_up{i}", relu2="fwd")
    h3 = matmul(s["act"], w["w_mlp_down"], "nn", f"mm_down{i}", residual=h2)
    return h3, s, gathered


def layer_backward(i, dh3, mem, w, p, bias, s, pending=None, c_arr=None):
    t_len = dh3.shape[0]
    tile = min(ROW_TILE, t_len)
    row = lambda a: a.reshape(1, -1)
    gw, gp = {}, {}
    dup = matmul(dh3, w["w_mlp_down"], "nt", f"mm_down_dx{i}", out_dtype=bf16, relu2=s["up"])
    gw["w_mlp_down"] = matmul(s["act"], dh3, "tn", f"mm_down_dw{i}")
    gw["w_mlp_up"] = matmul(s["um"], dup, "tn", f"mm_up_dw{i}")
    dum = matmul(dup, w["w_mlp_up"], "nt", f"mm_up_dx{i}")
    (dh2,), (g,) = tile_bwd(f"rms_mlp_bwd{i}", f_rms, [(s["h2"], D, 0)], [row(p["norm_mlp"])], [dum], [f32], tile, addend=dh3)
    gp["norm_mlp"] = g
    dxo = matmul(dh2, w["w_xo"], "nt", f"mm_xo_dx{i}")
    gw["w_xo"] = matmul(s["xo"], dh2, "tn", f"mm_xo_dw{i}")
    dxq, dk, dv, gp["xattn_q_norm"], gp["xattn_k_norm"] = xa_bwd(f"xa_bwd{i}", s["xq"], s["kv"], dxo, row(p["xattn_q_norm"]),
                                                                 row(p["xattn_k_norm"]))
    dkv = jnp.concatenate([dk, dv], axis=1)
    gw["w_xq"] = matmul(s["hx"], dxq, "tn", f"mm_xq_dw{i}")
    dhx = matmul(dxq, w["w_xq"], "nt", f"mm_xq_dx{i}")
    gw["w_xkv"] = matmul(s["memh"], dkv, "tn", f"mm_xkv_dw{i}")
    dmemh = matmul(dkv, w["w_xkv"], "nt", f"mm_xkv_dx{i}")
    _, (g,) = tile_bwd(f"rms_mem_bwd{i}", f_rms, [(mem, D, 0)], [row(p["norm_mem"])], [dmemh], [f32], MEM_LEN)
    gp["norm_mem"] = g
    (dh1,), (g,) = tile_bwd(f"rms_x_bwd{i}", f_rms, [(s["h1"], D, 0)], [row(p["norm_xattn"])], [dhx], [f32], tile, addend=dh2)
    gp["norm_xattn"] = g
    proj = s["proj"]
    dmerged = matmul(dh1, w["w_mix_out"], "nt", f"mm_mix_out_dx{i}")
    gw["w_mix_out"] = matmul(s["merged"], dh1, "tn", f"mm_mix_out_dw{i}")
    gate_cols = [(proj, D, C_G // D + k) for k in range(3)]
    (dpa, dpb, dpc, dya, dyb, dyc), (g,) = tile_bwd(
        f"merge_bwd{i}", f_merge, gate_cols + [(s["ya"], D, 0), (s["yb"], D, 0), (s["yc"], D, 0)], [row(p["gate_bias"])],
        [dmerged], [bf16] * 6, tile)
    gp["gate_bias"] = g
    dca = matmul(dya, w["w_conv_out"], "nt", f"mm_conv_out_dx{i}")
    gw["w_conv_out"] = matmul(s["ca"], dya, "tn", f"mm_conv_out_dw{i}")
    (dc1,), (gp["conv_ln_g"], gp["conv_ln_b"]) = tile_bwd(
        f"ln_silu_bwd{i}", f_ln_silu, [(s["c1"], D, 0)], [row(p["conv_ln_g"]), row(p["conv_ln_b"])], [dca], [f32], tile)
    da, dgate, gp["conv_dw_w"], gp["conv_dw_b"] = conv_bwd(f"conv_a_bwd{i}", dc1, None, proj, C_CONV, C_CONV + D,
                                                           p["conv_dw_w"], True, False, bf16)
    dyB = matmul(dyb, w["w_ssd_out"], "nt", f"mm_ssd_out_dx{i}")
    gw["w_ssd_out"] = matmul(s["yB"], dyb, "tn", f"mm_ssd_out_dw{i}")
    doc = matmul(dyc, w["w_attn_out"], "nt", f"mm_attn_out_dx{i}")
    gw["w_attn_out"] = matmul(s["oc"], dyc, "tn", f"mm_attn_out_dw{i}")
    moving = list(pending or []) + pack_other_grads(gw)
    swa_p = (row(p["attn_q_norm"]), row(p["attn_k_norm"]), _pad128(p["attn_sinks"]))
    (dq, dkp, dkc, dvp, dvc, gp["attn_q_norm"], gp["attn_k_norm"], g, dbias), swapped = swa_bwd(
        f"swa_bwd{i}", proj, doc, *swa_p, bias, ex=ex_swap_halves(moving))
    gp["attn_sinks"] = g[:, :ATTN_HEADS]
    shift = lambda a: jnp.concatenate([a[BLK:], jnp.zeros((BLK, a.shape[1]), a.dtype)], axis=0)
    dk_ = (dkc + shift(dkp)).astype(bf16)
    dv_ = (dvc + shift(dvp)).astype(bf16)
    partials = [add_halves(f"add_halves{i}_{k}", g_, got, c_arr) for k, (g_, got) in enumerate(zip(moving, swapped))]
    ssd_p = (_pad128(p["ssd_dt_bias"]), _pad128(p["ssd_A_log"]), _pad128(p["ssd_D"]), row(p["ssd_norm_g"]))
    (dxs, dbm, dcm, dz, ddt4, g1, g2, g3, g4), arrived = ssd_bwd(f"ssd_bwd{i}", s["xact"], proj, s["states"], dyB, *ssd_p,
                                                                  ex=ex_scatter(partials))
    gp["ssd_dt_bias"], gp["ssd_A_log"], gp["ssd_D"], gp["ssd_norm_g"] = g1[:, :SSD_HEADS], g2[:, :SSD_HEADS], g3[:, :SSD_HEADS], g4
    ddt = (ddt4[0] + ddt4[1] + ddt4[2] + ddt4[3]).astype(bf16)
    dxact = jnp.concatenate([dxs, dbm, dcm], axis=1)
    dxbc, gp["ssd_conv_w"], gp["ssd_conv_b"] = conv_bwd(f"conv_b_bwd{i}", dxact, s["xpre"], proj, C_XBC, C_XBC, p["ssd_conv_w"],
                                                        False, True, bf16)
    dproj = jnp.concatenate([da, dgate, dz, dxbc, dq, dpa, dpb, dpc, dk_, dv_, ddt, jnp.zeros((t_len, PCOLS - C_DT - 128), bf16)], axis=1)
    gw["w_in"] = matmul(dproj, s["u"], "tn", f"mm_in_dw{i}")
    du = matmul(dproj, w["w_in"], "nn", f"mm_in_dx{i}")
    (dh,), (g,) = tile_bwd(f"rms_mix_bwd{i}", f_rms, [(s["h"], D, 0)], [row(p["norm_mix"])], [du], [f32], tile, addend=dh1)
    gp["norm_mix"] = g
    return dh, gw, gp, dbias, (partials, arrived)


SMALL = ["rel_table", "norm_mix", "gate_bias", "conv_dw_w", "conv_dw_b", "conv_ln_g", "conv_ln_b", "ssd_conv_w", "ssd_conv_b",
         "ssd_dt_bias", "ssd_A_log", "ssd_D", "ssd_norm_g", "attn_q_norm", "attn_k_norm", "attn_sinks", "norm_xattn", "norm_mem",
         "xattn_q_norm", "xattn_k_norm", "norm_mlp"]
SMALL_SHARDED = dict(gate_bias=D, conv_dw_w=D, ssd_conv_w=SSD_XBC)
ORDER = ["rel_table", "norm_mix", "w_in", "gate_bias", "conv_dw_w", "conv_dw_b", "conv_ln_g", "conv_ln_b", "w_conv_out",
         "ssd_conv_w", "ssd_conv_b", "ssd_dt_bias", "ssd_A_log", "ssd_D", "ssd_norm_g", "w_ssd_out", "attn_q_norm", "attn_k_norm",
         "attn_sinks", "w_attn_out", "w_mix_out", "norm_xattn", "norm_mem", "w_xq", "w_xkv", "xattn_q_norm", "xattn_k_norm",
         "w_xo", "norm_mlp", "w_mlp_up", "w_mlp_down"]


def finish_reduction(i, partials, arrived, me_arr):
    mine = [add_partials(f"add_partials{i}_{k}", p, got, me_arr) for k, (p, got) in enumerate(zip(partials, arrived))]
    theirs = run_exchange(f"share_halves{i}", ex_share(mine))
    both = [join_halves(f"join_halves{i}_{k}", m, t) for k, (m, t) in enumerate(zip(mine, theirs))]
    return [b.reshape(2 * b.shape[1], b.shape[2]) for b in both]


def kernel(x, mem, rel_table, norm_mix, w_in, gate_bias, conv_dw_w, conv_dw_b, conv_ln_g, conv_ln_b, w_conv_out, ssd_conv_w, ssd_conv_b, ssd_dt_bias, ssd_A_log, ssd_D, ssd_norm_g, w_ssd_out, attn_q_norm, attn_k_norm, attn_sinks, w_attn_out, w_mix_out, norm_xattn, norm_mem, w_xq, w_xkv, xattn_q_norm, xattn_k_norm, w_xo, norm_mlp, w_mlp_up, w_mlp_down, loss_target, m_rel_table, m_norm_mix, m_w_in, m_gate_bias, m_conv_dw_w, m_conv_dw_b, m_conv_ln_g, m_conv_ln_b, m_w_conv_out, m_ssd_conv_w, m_ssd_conv_b, m_ssd_dt_bias, m_ssd_A_log, m_ssd_D, m_ssd_norm_g, m_w_ssd_out, m_attn_q_norm, m_attn_k_norm, m_attn_sinks, m_w_attn_out, m_w_mix_out, m_norm_xattn, m_norm_mem, m_w_xq, m_w_xkv, m_xattn_q_norm, m_xattn_k_norm, m_w_xo, m_norm_mlp, m_w_mlp_up, m_w_mlp_down, v_rel_table, v_norm_mix, v_w_in, v_gate_bias, v_conv_dw_w, v_conv_dw_b, v_conv_ln_g, v_conv_ln_b, v_w_conv_out, v_ssd_conv_w, v_ssd_conv_b, v_ssd_dt_bias, v_ssd_A_log, v_ssd_D, v_ssd_norm_g, v_w_ssd_out, v_attn_q_norm, v_attn_k_norm, v_attn_sinks, v_w_attn_out, v_w_mix_out, v_norm_xattn, v_norm_mem, v_w_xq, v_w_xkv, v_xattn_q_norm, v_xattn_k_norm, v_w_xo, v_norm_mlp, v_w_mlp_up, v_w_mlp_down):
    a = dict(locals())
    wts = {n: a[n] for n in ORDER}
    ms = {n: a["m_" + n] for n in ORDER}
    vs = {n: a["v_" + n] for n in ORDER}
    depth = norm_mix.shape[0]
    ax, ay, ac = lax.axis_index("x"), lax.axis_index("y"), lax.axis_index("c")
    chip = 2 * ax + ay

    small = {n: wts[n] for n in SMALL if n != "rel_table" and n not in SMALL_SHARDED}
    gather_buf, spans, r0 = [], {}, 0
    for n, width in SMALL_SHARDED.items():
        blk = wts[n]
        q = width // N_CHIPS
        whole = lax.dynamic_update_slice(jnp.zeros(blk.shape[:-1] + (width,), f32), blk, (0, 0, chip * q))
        flat = whole.reshape(-1, 128)
        spans[n] = (r0, flat.shape[0], whole.shape)
        r0 += flat.shape[0]
        gather_buf.append(flat)
    summed = allreduce_small("gather_small_params", jnp.concatenate(gather_buf, axis=0))
    for n, (s0, nr, shp) in spans.items():
        small[n] = summed[s0:s0 + nr].reshape(shp) * 0.5

    c_arr = jnp.reshape(ac, (1,)).astype(jnp.int32)
    me_arr = jnp.reshape(chip, (1,)).astype(jnp.int32)
    x0, mem0 = x[0], mem[0]
    bucket = jnp.asarray(_bucket_table())
    bias = relbias_fwd(rel_table, bucket)

    gathered = run_exchange("gather_d2d_first", ex_gather_d2d(run_exchange("gather_ici_first", ex_gather_ici(pack_shards(wts, 0)))))
    h, saved, ws = x0, [], []
    for i in range(depth):
        w = unpack_gathered(gathered)
        p = {n: small[n][i] for n in small}
        h, s, gathered = layer_forward(i, h, mem0, w, p, bias, pack_shards(wts, i + 1) if i + 1 < depth else None)
        saved.append(s)
        ws.append((w, p))
    grad_x, loss_tile = loss_head(h, loss_target[0], min(ROW_TILE, h.shape[0]))
    loss = lax.psum(loss_tile[0, 0], ("x", "y", "c"))

    reduced, small_grads, dbiases, pending = [[None] * 3 for _ in range(depth)], [None] * depth, [], None
    for i in reversed(range(depth)):
        w, p = ws[i]
        grad_x, gw, gp, dbias, travelled = layer_backward(i, grad_x, mem0, w, p, bias, saved[i], pending, c_arr)
        done = finish_reduction(i, *travelled, me_arr)
        if pending is not None:
            reduced[i + 1][0] = done.pop(0)
        reduced[i][1], reduced[i][2] = done
        small_grads[i] = gp
        dbiases.append(dbias)
        pending = pack_w_in_grad(gw["w_in"])
    swapped = run_exchange("swap_halves_last", ex_swap_halves(pending))
    partials = [add_halves("add_halves_last", pending[0], swapped[0], c_arr)]
    reduced[0][0], = finish_reduction("_last", partials, run_exchange("scatter_last", ex_scatter(partials)), me_arr)
    sg = {n: jnp.stack([small_grads[i][n].reshape(small[n].shape[1:]) for i in range(depth)]) for n in small}
    sg["rel_table"] = relbias_bwd(dbiases, bucket)

    grads, deltas, new_m, new_v = {}, {}, {}, {}
    sources = dict(w_in=(0, 0), w_xkv=(2, 0), **{n: (1, WIDE_OFF[n]) for n in WIDE})
    for n, (k, r0) in sources.items():
        flip = (lambda t: jnp.transpose(t, (0, 2, 1))) if n == "w_in" else (lambda t: t)
        res = adamw_layers(f"adamw_{n}", flip(wts[n]), flip(ms[n]), flip(vs[n]), [reduced[i][k] for i in range(depth)], r0)
        grads[n], deltas[n], new_m[n], new_v[n] = [flip(r) for r in res]

    parts, spans, r0 = [], {}, 0
    for n in SMALL:
        flat = sg[n].reshape(-1)
        nr = -(-flat.shape[0] // 128)
        nr = -(-nr // 8) * 8
        flat = jnp.pad(flat, (0, nr * 128 - flat.shape[0])).reshape(nr, 128)
        spans[n] = (r0, nr, sg[n].shape)
        r0 += nr
        parts.append(flat)
    summed = allreduce_small("allreduce_small_grads", jnp.concatenate(parts, axis=0))
    for n, (s0, nr, shp) in spans.items():
        size = int(np.prod(shp))
        g = summed[s0:s0 + nr].reshape(-1)[:size].reshape(shp)
        if n in SMALL_SHARDED:
            q = SMALL_SHARDED[n] // N_CHIPS
            g = lax.dynamic_slice_in_dim(g, chip * q, q, axis=g.ndim - 1)
        grads[n] = g

    for n in SMALL:
        shp = wts[n].shape
        two = (lambda t: t.reshape(-1, shp[-1]))
        d, nm, nv = adamw(f"adamw_{n}", two(wts[n]), two(grads[n]), two(ms[n]), two(vs[n]))
        deltas[n], new_m[n], new_v[n] = d.reshape(shp), nm.reshape(shp), nv.reshape(shp)

    return (loss, grad_x[None], *[grads[n] for n in ORDER], *[deltas[n] for n in ORDER],
            *[new_m[n] for n in ORDER], *[new_v[n] for n in ORDER])
```

```python
import functools
import math

import numpy as np
import jax
import jax.numpy as jnp
from jax import lax
from jax.experimental import pallas as pl
from jax.experimental.pallas import tpu as pltpu

f32, bf16 = jnp.float32, jnp.bfloat16

D = 1024
EPS = 1e-6
NEG_INF = -1e30
CONV_K = 31
SSD_INNER = 2048
SSD_HEADS = 32
SSD_P = 64
SSD_GROUPS = 4
SSD_N = 128
SSD_K = 4
CHUNK = 128
SSD_XBC = 3072
ATTN_HEADS = 16
ATTN_KV = 4
ATTN_D = 64
BLK = 128
REL_BUCKETS = 32
XH = 4
XD = 256
MLP = 4096
MEM_LEN = 256

OFF_Z, OFF_XBC, OFF_DT, OFF_Q, OFF_K, OFF_V, OFF_GATE, IN_COLS = 2048, 4096, 7168, 7200, 8224, 8480, 8736, 11808
C_CONV, C_Z, C_XBC, C_Q, C_G, C_K, C_V, C_DT, PCOLS = 0, 2048, 4096, 7168, 8192, 11264, 11520, 11776, 12288

ADAM_LR, ADAM_B1, ADAM_B2, ADAM_EPS, ADAM_WD, ADAM_STEP = 0.001, 0.9, 0.999, 1e-08, 0.01, 10

VMEM_LIMIT_BYTES = 56 * 1024 * 1024
N_CHIPS = 4
BIG = ["w_in", "w_conv_out", "w_ssd_out", "w_attn_out", "w_mix_out", "w_xq", "w_xkv", "w_xo", "w_mlp_up", "w_mlp_down"]
WIDE = ["w_conv_out", "w_ssd_out", "w_attn_out", "w_mix_out", "w_xq", "w_xo", "w_mlp_up", "w_mlp_down"]
WIDE_ROWS = dict(w_conv_out=256, w_ssd_out=512, w_attn_out=256, w_mix_out=256, w_xq=256, w_xo=256, w_mlp_up=1024, w_mlp_down=1024)
WIDE_OFF = {n: sum(WIDE_ROWS[m] for m in WIDE[:k]) for k, n in enumerate(WIDE)}
WIDE_TOTAL = sum(WIDE_ROWS.values())
IN_SHARD = IN_COLS // N_CHIPS
IN_PAD = 3072
ORIG_SEGS = [(0, OFF_DT, 0), (OFF_DT, OFF_Q, C_DT), (OFF_Q, OFF_K, C_Q), (OFF_K, OFF_GATE, C_K), (OFF_GATE, IN_COLS, C_G)]


def _params(sem=None):
    return pltpu.CompilerParams(dimension_semantics=sem, vmem_limit_bytes=VMEM_LIMIT_BYTES)


NN, NT, TN = ((1,), (0,)), ((1,), (1,)), ((0,), (0,))


def _dg(a, b, dims):
    return lax.dot_general(a.astype(bf16), b.astype(bf16), (dims, ((), ())), preferred_element_type=f32)


@jax.custom_vjp
def dot_nn(a, b):
    return _dg(a, b, NN)


dot_nn.defvjp(lambda a, b: (_dg(a, b, NN), (a, b)), lambda r, g: (_dg(g, r[1], NT), _dg(r[0], g, TN)))


@jax.custom_vjp
def dot_nt(a, b):
    return _dg(a, b, NT)


dot_nt.defvjp(lambda a, b: (_dg(a, b, NT), (a, b)), lambda r, g: (_dg(g, r[1], NN), _dg(g, r[0], TN)))


@jax.custom_vjp
def dot_tn(a, b):
    return _dg(a, b, TN)


dot_tn.defvjp(lambda a, b: (_dg(a, b, TN), (a, b)), lambda r, g: (_dg(r[1], g, NT), _dg(r[0], g, NN)))


def _tri(n, upper):
    r = lax.broadcasted_iota(jnp.int32, (n, n), 0)
    c = lax.broadcasted_iota(jnp.int32, (n, n), 1)
    return jnp.where((c >= r) if upper else (r >= c), 1.0, 0.0).astype(f32)


def _hdot(a, b):
    return lax.dot_general(a, b, (NN, ((), ())), preferred_element_type=f32, precision=lax.Precision.HIGHEST)


@jax.custom_vjp
def cumsum_rows(x):
    return _hdot(_tri(x.shape[0], False), x)


cumsum_rows.defvjp(lambda x: (_hdot(_tri(x.shape[0], False), x), None),
                   lambda _, g: (_hdot(_tri(g.shape[0], True), g),))


def _pick(n, prefs):
    for p in prefs:
        if n % p == 0:
            return p
    return n


def matmul(a, b, mode, name, out_dtype=f32, residual=None, relu2=None):
    if mode == "nn":
        (m, k), n = a.shape, b.shape[1]
    elif mode == "nt":
        (m, k), n = a.shape, b.shape[0]
    else:
        (k, m), n = a.shape, b.shape[1]
    tm = _pick(m, (1024, 512, 256))
    tn = _pick(n, (1024, 512, 256, 128))
    tk = _pick(k, (1024, 512))
    nk = k // tk
    dims = dict(nn=NN, nt=NT, tn=TN)[mode]
    a_spec = pl.BlockSpec((tk, tm), lambda i, j, l: (l, i)) if mode == "tn" else pl.BlockSpec((tm, tk), lambda i, j, l: (i, l))
    b_spec = pl.BlockSpec((tn, tk), lambda i, j, l: (j, l)) if mode == "nt" else pl.BlockSpec((tk, tn), lambda i, j, l: (l, j))
    o_spec = pl.BlockSpec((tm, tn), lambda i, j, l: (i, j))
    two_outs = isinstance(relu2, str)
    extra = residual if residual is not None else (None if two_outs or relu2 is None else relu2)
    has_extra = extra is not None
    n_out = 2 if two_outs else 1

    def body(*refs):
        a_ref, b_ref = refs[0], refs[1]
        e_ref = refs[2] if has_extra else None
        o_refs = refs[2 + has_extra:2 + has_extra + n_out]
        acc = refs[2 + has_extra + n_out]
        l = pl.program_id(2)

        @pl.when(l == 0)
        def _():
            acc[...] = jnp.zeros(acc.shape, f32)

        acc[...] += _dg(a_ref[...], b_ref[...], dims)

        @pl.when(l == nk - 1)
        def _():
            r = acc[...]
            if residual is not None:
                r = r + e_ref[...]
            elif has_extra:
                r = r * (2.0 * jnp.maximum(e_ref[...], 0.0))
            o_refs[0][...] = r.astype(o_refs[0].dtype)
            if two_outs:
                o_refs[1][...] = jnp.square(jnp.maximum(r, 0.0)).astype(o_refs[1].dtype)

    ins = [a, b] + ([extra] if has_extra else [])
    shapes = [jax.ShapeDtypeStruct((m, n), out_dtype)] + ([jax.ShapeDtypeStruct((m, n), bf16)] if two_outs else [])
    res = pl.pallas_call(
        body, name=name, grid=(m // tm, n // tn, nk),
        in_specs=[a_spec, b_spec] + ([o_spec] if has_extra else []),
        out_specs=[o_spec] * n_out, out_shape=shapes,
        scratch_shapes=[pltpu.VMEM((tm, tn), f32)],
        compiler_params=_params(("parallel", "parallel", "arbitrary")),
    )(*ins)
    return res if two_outs else res[0]


def _tile_in_specs(tiled, params, tile):
    specs = [pl.BlockSpec((tile, cols), functools.partial(lambda i, cb: (i, cb), cb=cb)) for (_, cols, cb) in tiled]
    specs += [pl.BlockSpec(p.shape, lambda i: (0, 0)) for p in params]
    return specs


def tile_fwd(name, f, tiled, params, outs, tile):
    rows = tiled[0][0].shape[0]
    nt, npar = len(tiled), len(params)

    def body(*refs):
        xs = [r[...].astype(f32) for r in refs[:nt + npar]]
        res = f(*xs)
        for o_ref, o in zip(refs[nt + npar:], res):
            o_ref[...] = o.astype(o_ref.dtype)

    return pl.pallas_call(
        body, name=name, grid=(rows // tile,),
        in_specs=_tile_in_specs(tiled, params, tile),
        out_specs=[pl.BlockSpec((tile, c), lambda i: (i, 0)) for c, _ in outs],
        out_shape=[jax.ShapeDtypeStruct((rows, c), dt) for c, dt in outs],
        compiler_params=_params(("parallel",)),
    )(*[t[0] for t in tiled], *params)


def tile_bwd(name, f, tiled, params, cots, d_dtypes, tile, addend=None):
    rows = tiled[0][0].shape[0]
    nt, npar, nc = len(tiled), len(params), len(cots)
    has_add = addend is not None

    def body(*refs):
        i = pl.program_id(0)
        xs = [r[...].astype(f32) for r in refs[:nt + npar]]
        gs = tuple(r[...].astype(f32) for r in refs[nt + npar:nt + npar + nc])
        pos = nt + npar + nc
        add_ref = refs[pos] if has_add else None
        pos += has_add
        dt_refs, dp_refs = refs[pos:pos + nt], refs[pos + nt:pos + nt + npar]
        _, vjp = jax.vjp(f, *xs)
        ds = vjp(gs)
        for k in range(nt):
            d = ds[k]
            if has_add and k == 0:
                d = d + add_ref[...]
            dt_refs[k][...] = d.astype(dt_refs[k].dtype)

        @pl.when(i == 0)
        def _():
            for r in dp_refs:
                r[...] = jnp.zeros(r.shape, f32)

        for k in range(npar):
            dp_refs[k][...] += ds[nt + k]

    in_specs = _tile_in_specs(tiled, params, tile)
    in_specs += [pl.BlockSpec((tile, c.shape[1]), lambda i: (i, 0)) for c in cots]
    ins = [t[0] for t in tiled] + list(params) + list(cots)
    if has_add:
        in_specs.append(pl.BlockSpec((tile, addend.shape[1]), lambda i: (i, 0)))
        ins.append(addend)
    out_specs = [pl.BlockSpec((tile, cols), lambda i: (i, 0)) for (_, cols, _) in tiled]
    out_specs += [pl.BlockSpec(p.shape, lambda i: (0, 0)) for p in params]
    out_shape = [jax.ShapeDtypeStruct((rows, cols), dt) for (_, cols, _), dt in zip(tiled, d_dtypes)]
    out_shape += [jax.ShapeDtypeStruct(p.shape, f32) for p in params]
    res = pl.pallas_call(
        body, name=name, grid=(rows // tile,), in_specs=in_specs, out_specs=out_specs, out_shape=out_shape,
        compiler_params=_params(("arbitrary",)),
    )(*ins)
    return res[:nt], res[nt:]


def f_rms(h, g):
    return (h * lax.rsqrt(jnp.mean(h * h, axis=-1, keepdims=True) + EPS) * g,)


def f_ln_silu(x, g, b):
    mu = jnp.mean(x, axis=-1, keepdims=True)
    xc = x - mu
    y = xc * lax.rsqrt(jnp.mean(xc * xc, axis=-1, keepdims=True) + EPS) * g + b
    return (y * jax.nn.sigmoid(y),)


def f_merge(pa, pb, pc, ya, yb, yc, gb):
    ga = jax.nn.sigmoid(pa + gb[:, 0:D])
    gb_ = jax.nn.sigmoid(pb + gb[:, D:2 * D])
    gc = jax.nn.sigmoid(pc + gb[:, 2 * D:3 * D])
    return (ga * ya + gb_ * yb + gc * yc,)


HALO = 32
CONV_CB = 256
SUB = 128


def _silu_grad(p):
    s = jax.nn.sigmoid(p)
    return s * (1.0 + p * (1.0 - s))


def _fill_shifted(buf, copies, length):
    for k in range(1, 8):
        for r0 in range(0, length - 8, SUB):
            n = min(SUB, length - 8 - r0)
            copies[k - 1, pl.ds(r0, n), :] = buf[pl.ds(r0 + k, n), :]


def _rows(buf, copies, off, n):
    if copies is None or off % 8 == 0:
        return buf[pl.ds(off, n), :]
    return copies[off % 8 - 1, pl.ds(off - off % 8, n), :]


def conv_fwd(name, src, col0, gate_col0, w, b, glu, silu):
    t_len = src.shape[0]
    k_taps, c = w.shape
    tt = min(512, t_len)
    nt_ = t_len // tt
    cb0, gb0 = col0 // CONV_CB, gate_col0 // CONV_CB
    many_taps = k_taps > 8

    def body(*refs):
        pos = 0
        x_cur, x_prev = refs[0], refs[1]
        pos = 2
        if glu:
            g_cur, g_prev = refs[2], refs[3]
            pos = 4
        w_ref, b_ref = refs[pos], refs[pos + 1]
        pre_ref = refs[pos + 2]
        act_ref = refs[pos + 3] if silu else None
        xp = refs[-2] if many_taps else refs[-1]
        xs = refs[-1] if many_taps else None
        t = pl.program_id(1)
        cur = x_cur[...]
        tail = x_prev[...]
        if glu:
            cur = cur * jax.nn.sigmoid(g_cur[...])
            tail = tail * jax.nn.sigmoid(g_prev[...])
        xp[pl.ds(0, HALO), :] = jnp.where(t > 0, tail, 0.0)
        xp[pl.ds(HALO, tt), :] = cur
        if many_taps:
            _fill_shifted(xp, xs, HALO + tt)
        for s in range(tt // SUB):
            acc = jnp.broadcast_to(b_ref[...], (SUB, CONV_CB))
            for j in range(k_taps):
                acc = acc + _rows(xp, xs, s * SUB + HALO - (k_taps - 1) + j, SUB) * w_ref[pl.ds(j, 1), :]
            pre_ref[pl.ds(s * SUB, SUB), :] = acc
            if silu:
                act_ref[pl.ds(s * SUB, SUB), :] = acc * jax.nn.sigmoid(acc)

    blk = lambda off: pl.BlockSpec((tt, CONV_CB), functools.partial(lambda j, t, off: (t, off + j), off=off))
    prev = lambda off: pl.BlockSpec((HALO, CONV_CB), functools.partial(lambda j, t, off: (jnp.maximum(t * (tt // HALO) - 1, 0), off + j), off=off))
    in_specs, ins = [blk(cb0), prev(cb0)], [src, src]
    if glu:
        in_specs += [blk(gb0), prev(gb0)]
        ins += [src, src]
    in_specs += [pl.BlockSpec((k_taps, CONV_CB), lambda j, t: (0, j)), pl.BlockSpec((1, CONV_CB), lambda j, t: (0, j))]
    ins += [w, b]
    o_spec = pl.BlockSpec((tt, CONV_CB), lambda j, t: (t, j))
    n_out = 2 if silu else 1
    res = pl.pallas_call(
        body, name=name, grid=(c // CONV_CB, nt_), in_specs=in_specs, out_specs=[o_spec] * n_out,
        out_shape=[jax.ShapeDtypeStruct((t_len, c), f32)] * n_out,
        scratch_shapes=[pltpu.VMEM((HALO + tt, CONV_CB), f32)] + ([pltpu.VMEM((7, HALO + tt, CONV_CB), f32)] if many_taps else []),
        compiler_params=_params(("parallel", "arbitrary")),
    )(*ins)
    return res


def conv_bwd(name, dy, pre, src, col0, gate_col0, w, glu, silu, out_dtype):
    t_len = src.shape[0]
    k_taps, c = w.shape
    tt = min(512, t_len)
    nt_ = t_len // tt
    cb0, gb0 = col0 // CONV_CB, gate_col0 // CONV_CB
    many_taps = k_taps > 8

    def body(*refs):
        dy_cur, dy_next = refs[0], refs[1]
        pos = 2
        if silu:
            p_cur, p_next = refs[2], refs[3]
            pos = 4
        x_cur, x_prev = refs[pos], refs[pos + 1]
        pos += 2
        if glu:
            g_cur, g_prev = refs[pos], refs[pos + 1]
            pos += 2
        w_ref = refs[pos]
        pos += 1
        n_dx = 2 if glu else 1
        dx_refs = refs[pos:pos + n_dx]
        dw_ref, db_ref = refs[pos + n_dx], refs[pos + n_dx + 1]
        dp, xp = (refs[-4], refs[-3]) if many_taps else (refs[-2], refs[-1])
        ds_, xs = (refs[-2], refs[-1]) if many_taps else (None, None)
        t = pl.program_id(1)

        dcur = dy_cur[...]
        dhead = dy_next[...]
        if silu:
            dcur = dcur * _silu_grad(p_cur[...])
            dhead = dhead * _silu_grad(p_next[...])
        dp[pl.ds(0, tt), :] = dcur
        dp[pl.ds(tt, HALO), :] = jnp.where(t < nt_ - 1, dhead, 0.0)
        cur = x_cur[...]
        tail = x_prev[...]
        if glu:
            cur = cur * jax.nn.sigmoid(g_cur[...])
            tail = tail * jax.nn.sigmoid(g_prev[...])
        xp[pl.ds(0, HALO), :] = jnp.where(t > 0, tail, 0.0)
        xp[pl.ds(HALO, tt), :] = cur

        if many_taps:
            _fill_shifted(dp, ds_, tt + HALO)
            _fill_shifted(xp, xs, HALO + tt)

        @pl.when(t == 0)
        def _():
            dw_ref[...] = jnp.zeros(dw_ref.shape, f32)
            db_ref[...] = jnp.zeros(db_ref.shape, f32)

        for s in range(tt // SUB):
            d_sub = dp[pl.ds(s * SUB, SUB), :]
            acc = jnp.zeros((SUB, CONV_CB), f32)
            for j in range(k_taps):
                acc = acc + _rows(dp, ds_, s * SUB + (k_taps - 1) - j, SUB) * w_ref[pl.ds(j, 1), :]
                x_sub = _rows(xp, xs, s * SUB + HALO - (k_taps - 1) + j, SUB)
                dw_ref[pl.ds(j, 1), :] += jnp.sum(d_sub * x_sub, axis=0, keepdims=True)
            db_ref[...] += jnp.sum(d_sub, axis=0, keepdims=True)
            if glu:
                a = x_cur[pl.ds(s * SUB, SUB), :]
                sg = jax.nn.sigmoid(g_cur[pl.ds(s * SUB, SUB), :])
                dx_refs[0][pl.ds(s * SUB, SUB), :] = (acc * sg).astype(out_dtype)
                dx_refs[1][pl.ds(s * SUB, SUB), :] = (acc * a * sg * (1.0 - sg)).astype(out_dtype)
            else:
                dx_refs[0][pl.ds(s * SUB, SUB), :] = acc.astype(out_dtype)

    blk = lambda off: pl.BlockSpec((tt, CONV_CB), functools.partial(lambda j, t, off: (t, off + j), off=off))
    prev = lambda off: pl.BlockSpec((HALO, CONV_CB), functools.partial(lambda j, t, off: (jnp.maximum(t * (tt // HALO) - 1, 0), off + j), off=off))
    nxt = pl.BlockSpec((HALO, CONV_CB), lambda j, t: (jnp.minimum((t + 1) * (tt // HALO), t_len // HALO - 1), j))
    in_specs, ins = [blk(0), nxt], [dy, dy]
    if silu:
        in_specs += [blk(0), nxt]
        ins += [pre, pre]
    in_specs += [blk(cb0), prev(cb0)]
    ins += [src, src]
    if glu:
        in_specs += [blk(gb0), prev(gb0)]
        ins += [src, src]
    in_specs.append(pl.BlockSpec((k_taps, CONV_CB), lambda j, t: (0, j)))
    ins.append(w)
    n_dx = 2 if glu else 1
    o_spec = pl.BlockSpec((tt, CONV_CB), lambda j, t: (t, j))
    out_specs = [o_spec] * n_dx + [pl.BlockSpec((k_taps, CONV_CB), lambda j, t: (0, j)), pl.BlockSpec((1, CONV_CB), lambda j, t: (0, j))]
    out_shape = [jax.ShapeDtypeStruct((t_len, c), out_dtype)] * n_dx + [jax.ShapeDtypeStruct((k_taps, c), f32), jax.ShapeDtypeStruct((1, c), f32)]
    return pl.pallas_call(
        body, name=name, grid=(c // CONV_CB, nt_), in_specs=in_specs, out_specs=out_specs, out_shape=out_shape,
        scratch_shapes=[pltpu.VMEM((tt + HALO, CONV_CB), f32)] * 2 + ([pltpu.VMEM((7, tt + HALO, CONV_CB), f32)] * 2 if many_taps else []),
        compiler_params=_params(("parallel", "arbitrary")),
    )(*ins)


GH = SSD_HEADS // SSD_GROUPS
GW = GH * SSD_P


def _softplus(x):
    return jnp.maximum(x, 0.0) + jnp.log1p(jnp.exp(-jnp.abs(x)))


def f_ssd(hbase, x, z, bm, cm, dtraw, s_in, dt_bias, a_log, dskip, ng):
    q = x.shape[0]
    dt = _softplus(dtraw + dt_bias)
    da = dt * (-jnp.exp(a_log))
    cs = cumsum_rows(da)
    g_cb = dot_nt(cm, bm)
    lane = lax.broadcasted_iota(jnp.int32, (q, 128), 1)
    lane1 = lax.broadcasted_iota(jnp.int32, (1, 128), 1)
    causal = lax.broadcasted_iota(jnp.int32, (q, q), 0) >= lax.broadcasted_iota(jnp.int32, (q, q), 1)
    last = lax.broadcasted_iota(jnp.int32, (q, GW), 0) == q - 1
    dt_cols, cs_cols, d_cols = [], [], []
    for r in range(GH):
        sel = lane == hbase + r
        dt_cols.append(jnp.sum(jnp.where(sel, dt, 0.0), axis=1, keepdims=True))
        cs_cols.append(jnp.sum(jnp.where(sel, cs, 0.0), axis=1, keepdims=True))
        d_cols.append(jnp.sum(jnp.where(lane1 == hbase + r, dskip, 0.0), axis=1, keepdims=True))
    spread = lambda cols: jnp.concatenate([jnp.broadcast_to(c, (c.shape[0], SSD_P)) for c in cols], axis=1)
    dt_x, cs_x, d_x = spread(dt_cols), spread(cs_cols), spread(d_cols)
    cs_last = jnp.sum(jnp.where(last, cs_x, 0.0), axis=0, keepdims=True)
    xdt = x * dt_x
    y_diag = []
    for r in range(GH):
        m1 = jnp.broadcast_to(cs_cols[r], (q, q))
        decay = jnp.where(causal, jnp.exp(jnp.where(causal, m1 - m1.T, 0.0)), 0.0)
        y_diag.append(dot_nn(g_cb * decay, xdt[:, r * SSD_P:(r + 1) * SSD_P]))
    s_c = dot_tn(bm, xdt * jnp.exp(cs_last - cs_x))
    y_off = dot_nn(cm, s_in) * jnp.exp(cs_x)
    y = jnp.concatenate(y_diag, axis=1) + y_off + x * d_x
    s_out = s_in * jnp.exp(cs_last) + s_c
    y = y * (z * jax.nn.sigmoid(z))
    y = y * lax.rsqrt(jnp.mean(y * y, axis=-1, keepdims=True) + EPS) * ng
    return y, s_out


GPS_FWD, GPS_BWD = 2, 1


def _ssd_in_specs(cmap, gps):
    PW, PN = gps * GW, gps * SSD_N
    return [
        pl.BlockSpec((CHUNK, PW), lambda g, c: (cmap(c), g)),
        pl.BlockSpec((CHUNK, PW), lambda g, c: (cmap(c), C_Z // PW + g)),
        pl.BlockSpec((CHUNK, PN), lambda g, c: (cmap(c), SSD_INNER // PN + g)),
        pl.BlockSpec((CHUNK, PN), lambda g, c: (cmap(c), (SSD_INNER + SSD_GROUPS * SSD_N) // PN + g)),
        pl.BlockSpec((CHUNK, 128), lambda g, c: (cmap(c), C_DT // 128)),
    ]


def _ssd_param_specs(gps):
    return [pl.BlockSpec((1, 128), lambda g, c: (0, 0))] * 3 + [pl.BlockSpec((1, gps * GW), lambda g, c: (0, g))]


def _cols(ref, k, width):
    return ref[:, pl.ds(k * width, width)]


def ssd_fwd(name, xact, proj, dt_bias, a_log, dskip, ng, ex=None):
    t_len = xact.shape[0]
    nc = t_len // CHUNK
    h = _hosted(ex)
    GPS = GPS_FWD
    PW, PN, NPAIR = GPS * GW, GPS * SSD_N, SSD_GROUPS // GPS

    def body(*refs):
        x_ref, z_ref, b_ref, c_ref, dt_ref, p1, p2, p3, p4 = refs[:9]
        y_ref, st_ref = refs[9 + h.n_in:11 + h.n_in]
        s_scr = refs[11 + h.n_in + h.n_out]
        g, c = pl.program_id(0), pl.program_id(1)
        h.at_start(refs, 9, 11 + h.n_in, (g == 0) & (c == 0))

        @pl.when(c == 0)
        def _():
            s_scr[...] = jnp.zeros(s_scr.shape, f32)

        for k in range(GPS):
            s_in = s_scr[k]
            st_ref[k, 0] = s_in
            y, s_out = f_ssd((g * GPS + k) * GH, _cols(x_ref, k, GW), _cols(z_ref, k, GW), _cols(b_ref, k, SSD_N),
                             _cols(c_ref, k, SSD_N), dt_ref[...], s_in, p1[...], p2[...], p3[...], _cols(p4, k, GW))
            y_ref[:, pl.ds(k * GW, GW)] = y.astype(y_ref.dtype)
            s_scr[k] = s_out
        h.at_end(refs, 9, 11 + h.n_in, (g == NPAIR - 1) & (c == nc - 1))

    res = pl.pallas_call(
        body, name=name, grid=(NPAIR, nc),
        in_specs=_ssd_in_specs(lambda c: c, GPS) + _ssd_param_specs(GPS) + h.in_specs,
        out_specs=[pl.BlockSpec((CHUNK, PW), lambda g, c: (c, g)),
                   pl.BlockSpec((GPS, 1, SSD_N, GW), lambda g, c: (g, c, 0, 0))] + h.out_specs,
        out_shape=[jax.ShapeDtypeStruct((t_len, SSD_INNER), bf16),
                   jax.ShapeDtypeStruct((SSD_GROUPS, nc, SSD_N, GW), f32)] + h.out_shape,
        scratch_shapes=[pltpu.VMEM((GPS, SSD_N, GW), f32)] + h.scratch,
        input_output_aliases=h.aliases(9, 2),
        compiler_params=_params(("arbitrary", "arbitrary")),
    )(xact, proj, xact, xact, proj, dt_bias, a_log, dskip, ng, *h.ins)
    return res[0], res[1], list(res[2:])


def ssd_bwd(name, xact, proj, states, dy, dt_bias, a_log, dskip, ng, ex=None):
    t_len = xact.shape[0]
    nc = t_len // CHUNK
    rev = lambda c: nc - 1 - c
    h = _hosted(ex)
    GPS = GPS_BWD
    PW, PN, NPAIR = GPS * GW, GPS * SSD_N, SSD_GROUPS // GPS

    def body(*refs):
        x_ref, z_ref, b_ref, c_ref, dt_ref, st_ref, dy_ref, p1, p2, p3, p4 = refs[:11]
        dx_ref, db_ref, dc_ref, dz_ref, ddt_ref, d1, d2, d3, d4 = refs[11 + h.n_in:20 + h.n_in]
        ds_scr = refs[20 + h.n_in + h.n_out]
        g, c = pl.program_id(0), pl.program_id(1)
        h.at_start(refs, 11, 20 + h.n_in, (g == 0) & (c == 0))

        @pl.when(c == 0)
        def _():
            ds_scr[...] = jnp.zeros(ds_scr.shape, f32)
            d4[...] = jnp.zeros(d4.shape, f32)

        @pl.when((c == 0) & (g == 0))
        def _():
            for r in (d1, d2, d3):
                r[...] = jnp.zeros(r.shape, f32)

        for k in range(GPS):
            fn = functools.partial(f_ssd, (g * GPS + k) * GH)
            _, vjp = jax.vjp(fn, _cols(x_ref, k, GW), _cols(z_ref, k, GW), _cols(b_ref, k, SSD_N), _cols(c_ref, k, SSD_N),
                             dt_ref[...], st_ref[k, 0], p1[...], p2[...], p3[...], _cols(p4, k, GW))
            dx, dz, db, dc, ddt, ds_in, e1, e2, e3, e4 = vjp((_cols(dy_ref, k, GW).astype(f32), ds_scr[k]))
            dx_ref[:, pl.ds(k * GW, GW)] = dx
            dz_ref[:, pl.ds(k * GW, GW)] = dz.astype(dz_ref.dtype)
            db_ref[:, pl.ds(k * SSD_N, SSD_N)] = db
            dc_ref[:, pl.ds(k * SSD_N, SSD_N)] = dc
            ddt_ref[k] = ddt
            ds_scr[k] = ds_in
            d1[...] += e1
            d2[...] += e2
            d3[...] += e3
            d4[:, pl.ds(k * GW, GW)] += e4
        h.at_end(refs, 11, 20 + h.n_in, (g == NPAIR - 1) & (c == nc - 1))

    res = pl.pallas_call(
        body, name=name, grid=(NPAIR, nc),
        in_specs=_ssd_in_specs(rev, GPS) + [
            pl.BlockSpec((GPS, 1, SSD_N, GW), lambda g, c: (g, rev(c), 0, 0)),
            pl.BlockSpec((CHUNK, PW), lambda g, c: (rev(c), g)),
        ] + _ssd_param_specs(GPS) + h.in_specs,
        out_specs=[
            pl.BlockSpec((CHUNK, PW), lambda g, c: (rev(c), g)),
            pl.BlockSpec((CHUNK, PN), lambda g, c: (rev(c), g)),
            pl.BlockSpec((CHUNK, PN), lambda g, c: (rev(c), g)),
            pl.BlockSpec((CHUNK, PW), lambda g, c: (rev(c), g)),
            pl.BlockSpec((GPS, CHUNK, 128), lambda g, c: (g, rev(c), 0)),
        ] + _ssd_param_specs(GPS) + h.out_specs,
        out_shape=[
            jax.ShapeDtypeStruct((t_len, SSD_INNER), f32),
            jax.ShapeDtypeStruct((t_len, SSD_GROUPS * SSD_N), f32),
            jax.ShapeDtypeStruct((t_len, SSD_GROUPS * SSD_N), f32),
            jax.ShapeDtypeStruct((t_len, SSD_INNER), bf16),
            jax.ShapeDtypeStruct((SSD_GROUPS, t_len, 128), f32),
            jax.ShapeDtypeStruct((1, 128), f32), jax.ShapeDtypeStruct((1, 128), f32), jax.ShapeDtypeStruct((1, 128), f32),
            jax.ShapeDtypeStruct((1, SSD_INNER), f32),
        ] + h.out_shape,
        scratch_shapes=[pltpu.VMEM((GPS, SSD_N, GW), f32)] + h.scratch,
        input_output_aliases=h.aliases(11, 9),
        compiler_params=_params(("arbitrary", "arbitrary")),
    )(xact, proj, xact, xact, proj, states, dy, dt_bias, a_log, dskip, ng, *h.ins)
    return list(res[:9]), list(res[9:])


def _head_norm(t, g):
    return t * lax.rsqrt(jnp.mean(t * t, axis=-1, keepdims=True) + EPS) * g


def f_swa(has_prev, q, kp, kc, vp, vc, qg, kg, sinks, *bias):
    rep = ATTN_HEADS // ATTN_KV
    qi = lax.broadcasted_iota(jnp.int32, (rep * BLK, BLK), 0) & (BLK - 1)
    ki = lax.broadcasted_iota(jnp.int32, (rep * BLK, BLK), 1)
    mask_p = (ki > qi) & has_prev
    mask_c = ki <= qi
    lane1 = lax.broadcasted_iota(jnp.int32, (1, 128), 1)
    scale = ATTN_D ** -0.5
    outs = []
    for g in range(ATTN_KV):
        sl = slice(g * ATTN_D, (g + 1) * ATTN_D)
        kpn, kcn = _head_norm(kp[:, sl], kg), _head_norm(kc[:, sl], kg)
        heads = range(g * rep, (g + 1) * rep)
        qn = _head_norm(jnp.concatenate([q[:, h * ATTN_D:(h + 1) * ATTN_D] for h in heads], axis=0), qg)
        lp = jnp.where(mask_p, dot_nt(qn, kpn) * scale + bias[g][:, :BLK], NEG_INF)
        lc = jnp.where(mask_c, dot_nt(qn, kcn) * scale + bias[g][:, BLK:], NEG_INF)
        sink = jnp.concatenate([jnp.broadcast_to(jnp.sum(jnp.where(lane1 == h, sinks, 0.0), axis=1, keepdims=True), (BLK, 1))
                                for h in heads], axis=0)
        m = lax.stop_gradient(jnp.maximum(jnp.maximum(jnp.max(lp, axis=-1, keepdims=True), jnp.max(lc, axis=-1, keepdims=True)), sink))
        pp, pc = jnp.exp(lp - m), jnp.exp(lc - m)
        den = jnp.sum(pp, axis=-1, keepdims=True) + jnp.sum(pc, axis=-1, keepdims=True) + jnp.exp(sink - m)
        o = (dot_nn(pp, vp[:, sl]) + dot_nn(pc, vc[:, sl])) * (1.0 / den)
        outs += [o[r * BLK:(r + 1) * BLK] for r in range(rep)]
    return jnp.concatenate(outs, axis=1)


_SWA_BIAS_SHAPE = (ATTN_KV, ATTN_HEADS // ATTN_KV * BLK, 2 * BLK)
_SWA_BIAS_SPEC = pl.BlockSpec(_SWA_BIAS_SHAPE, lambda i: (0, 0, 0))


def _swa_in_specs():
    prev = lambda i: jnp.maximum(i - 1, 0)
    kw = ATTN_KV * ATTN_D
    return [
        pl.BlockSpec((BLK, D), lambda i: (i, C_Q // D)),
        pl.BlockSpec((BLK, kw), lambda i: (prev(i), C_K // kw)),
        pl.BlockSpec((BLK, kw), lambda i: (i, C_K // kw)),
        pl.BlockSpec((BLK, kw), lambda i: (prev(i), C_V // kw)),
        pl.BlockSpec((BLK, kw), lambda i: (i, C_V // kw)),
        pl.BlockSpec((1, ATTN_D), lambda i: (0, 0)),
        pl.BlockSpec((1, ATTN_D), lambda i: (0, 0)),
        pl.BlockSpec((1, 128), lambda i: (0, 0)),
        _SWA_BIAS_SPEC,
    ]


def swa_fwd(name, proj, qg, kg, sinks, bias, ex=None):
    t_len = proj.shape[0]
    nb = t_len // BLK
    h = _hosted(ex)

    def body(*refs):
        q_ref, kp_ref, kc_ref, vp_ref, vc_ref, qg_ref, kg_ref, s_ref, b_ref = refs[:9]
        o_ref = refs[9 + h.n_in]
        i = pl.program_id(0)
        h.at_start(refs, 9, 10 + h.n_in, i == 0)
        o = f_swa(i > 0, q_ref[...], kp_ref[...], kc_ref[...], vp_ref[...], vc_ref[...], qg_ref[...], kg_ref[...],
                  s_ref[...], *[b_ref[kv] for kv in range(ATTN_KV)])
        o_ref[...] = o.astype(o_ref.dtype)
        h.at_end(refs, 9, 10 + h.n_in, i == nb - 1)

    res = pl.pallas_call(
        body, name=name, grid=(nb,), in_specs=_swa_in_specs() + h.in_specs,
        out_specs=[pl.BlockSpec((BLK, D), lambda i: (i, 0))] + h.out_specs,
        out_shape=[jax.ShapeDtypeStruct((t_len, D), bf16)] + h.out_shape,
        scratch_shapes=h.scratch, input_output_aliases=h.aliases(9, 1),
        compiler_params=_params(("arbitrary",)),
    )(proj, proj, proj, proj, proj, qg, kg, sinks, bias.reshape(_SWA_BIAS_SHAPE), *h.ins)
    return res[0], list(res[1:])


def swa_bwd(name, proj, do, qg, kg, sinks, bias, ex=None):
    t_len = proj.shape[0]
    nb = t_len // BLK
    kw = ATTN_KV * ATTN_D
    h = _hosted(ex)

    def body(*refs):
        q_ref, kp_ref, kc_ref, vp_ref, vc_ref, qg_ref, kg_ref, s_ref, b_ref, do_ref = refs[:10]
        dq_ref, dkp_ref, dkc_ref, dvp_ref, dvc_ref, dqg_ref, dkg_ref, ds_ref, db_ref = refs[10 + h.n_in:19 + h.n_in]
        i = pl.program_id(0)
        h.at_start(refs, 10, 19 + h.n_in, i == 0)

        @pl.when(i == 0)
        def _():
            for r in (dqg_ref, dkg_ref, ds_ref, db_ref):
                r[...] = jnp.zeros(r.shape, f32)

        fn = functools.partial(f_swa, i > 0)
        _, vjp = jax.vjp(fn, q_ref[...], kp_ref[...], kc_ref[...], vp_ref[...], vc_ref[...], qg_ref[...], kg_ref[...],
                         s_ref[...], *[b_ref[kv] for kv in range(ATTN_KV)])
        ds = vjp(do_ref[...].astype(f32))
        dq_ref[...] = ds[0].astype(dq_ref.dtype)
        dkp_ref[...] = ds[1]
        dkc_ref[...] = ds[2]
        dvp_ref[...] = ds[3]
        dvc_ref[...] = ds[4]
        dqg_ref[...] += ds[5]
        dkg_ref[...] += ds[6]
        ds_ref[...] += ds[7]
        for kv in range(ATTN_KV):
            db_ref[kv] += ds[8 + kv]
        h.at_end(refs, 10, 19 + h.n_in, i == nb - 1)

    row = lambda w: pl.BlockSpec((BLK, w), lambda i: (i, 0))
    res = pl.pallas_call(
        body, name=name, grid=(nb,), in_specs=_swa_in_specs() + [row(D)] + h.in_specs,
        out_specs=[row(D), row(kw), row(kw), row(kw), row(kw),
                   pl.BlockSpec((1, ATTN_D), lambda i: (0, 0)), pl.BlockSpec((1, ATTN_D), lambda i: (0, 0)),
                   pl.BlockSpec((1, 128), lambda i: (0, 0)), _SWA_BIAS_SPEC] + h.out_specs,
        out_shape=[jax.ShapeDtypeStruct((t_len, D), bf16)] + [jax.ShapeDtypeStruct((t_len, kw), f32)] * 4
        + [jax.ShapeDtypeStruct((1, ATTN_D), f32)] * 2 + [jax.ShapeDtypeStruct((1, 128), f32),
                                                          jax.ShapeDtypeStruct(_SWA_BIAS_SHAPE, f32)] + h.out_shape,
        scratch_shapes=h.scratch, input_output_aliases=h.aliases(10, 9),
        compiler_params=_params(("arbitrary",)),
    )(proj, proj, proj, proj, proj, qg, kg, sinks, bias.reshape(_SWA_BIAS_SHAPE), do, *h.ins)
    return list(res[:8]) + [res[8].reshape(ATTN_HEADS, BLK, 2 * BLK)], list(res[9:])


def _bucket_table():
    qi = np.arange(BLK)[:, None] + BLK
    kj = np.arange(2 * BLK)[None, :]
    dist = qi - kj
    max_exact = REL_BUCKETS // 2
    d = np.maximum(dist, 1).astype(np.float32)
    large = max_exact + (np.log(d / max_exact) / math.log(128 / max_exact) * (REL_BUCKETS - max_exact)).astype(np.int32)
    large = np.minimum(large, REL_BUCKETS - 1)
    return np.where(dist < max_exact, np.maximum(dist, 0), large).astype(np.int32)


def relbias_fwd(table, bucket):
    def body(t_ref, bk_ref, o_ref):
        bk = bk_ref[...]
        for h in range(ATTN_HEADS):
            acc = jnp.zeros((BLK, 2 * BLK), f32)
            for b in range(REL_BUCKETS):
                acc = jnp.where(bk == b, t_ref[b, h], acc)
            o_ref[h] = acc

    return pl.pallas_call(
        body, name="relbias_fwd", out_shape=jax.ShapeDtypeStruct((ATTN_HEADS, BLK, 2 * BLK), f32),
        in_specs=[pl.BlockSpec(memory_space=pltpu.SMEM), pl.BlockSpec(memory_space=pltpu.VMEM)],
        out_specs=pl.BlockSpec(memory_space=pltpu.VMEM),
    )(table, bucket)


def relbias_bwd(dbias, bucket):
    n = len(dbias)

    def body(*refs):
        bk = refs[n][...]
        o_ref = refs[n + 1]
        row = lax.broadcasted_iota(jnp.int32, (REL_BUCKETS, 128), 0)
        lane = lax.broadcasted_iota(jnp.int32, (REL_BUCKETS, 128), 1)
        res = jnp.zeros((REL_BUCKETS, 128), f32)
        for h in range(ATTN_HEADS):
            tot = refs[0][h]
            for k in range(1, n):
                tot = tot + refs[k][h]
            for b in range(REL_BUCKETS):
                part = jnp.sum(jnp.sum(jnp.where(bk == b, tot, 0.0), axis=1, keepdims=True), axis=0, keepdims=True)
                res = jnp.where((row == b) & (lane == h), part, res)
        o_ref[...] = res

    return pl.pallas_call(
        body, name="relbias_bwd", out_shape=jax.ShapeDtypeStruct((REL_BUCKETS, 128), f32),
        in_specs=[pl.BlockSpec(memory_space=pltpu.VMEM)] * (n + 1),
        out_specs=pl.BlockSpec(memory_space=pltpu.VMEM),
    )(*dbias, bucket)[:, :ATTN_HEADS]


XA_TILE = 512


def f_xattn(q, k, v, qg, kg):
    qn, kn = _head_norm(q, qg), _head_norm(k, kg)
    logits = dot_nt(qn, kn) * (XD ** -0.5)
    p = jnp.exp(logits - lax.stop_gradient(jnp.max(logits, axis=-1, keepdims=True)))
    return dot_nn(p * (1.0 / jnp.sum(p, axis=-1, keepdims=True)), v)


def _xa_in_specs(tq):
    return [
        pl.BlockSpec((tq, XD), lambda h, i: (i, h)),
        pl.BlockSpec((MEM_LEN, XD), lambda h, i: (0, h)),
        pl.BlockSpec((MEM_LEN, XD), lambda h, i: (0, XH + h)),
        pl.BlockSpec((1, XD), lambda h, i: (0, 0)),
        pl.BlockSpec((1, XD), lambda h, i: (0, 0)),
    ]


def xa_fwd(name, xq, kv, qg, kg):
    t_len = xq.shape[0]
    tq = min(XA_TILE, t_len)

    def body(q_ref, k_ref, v_ref, qg_ref, kg_ref, o_ref):
        o_ref[...] = f_xattn(q_ref[...], k_ref[...], v_ref[...], qg_ref[...], kg_ref[...]).astype(o_ref.dtype)

    return pl.pallas_call(
        body, name=name, grid=(XH, t_len // tq), in_specs=_xa_in_specs(tq),
        out_specs=pl.BlockSpec((tq, XD), lambda h, i: (i, h)), out_shape=jax.ShapeDtypeStruct((t_len, D), bf16),
        compiler_params=_params(("parallel", "parallel")),
    )(xq, kv, kv, qg, kg)


def xa_bwd(name, xq, kv, do, qg, kg):
    t_len = xq.shape[0]
    tq = min(XA_TILE, t_len)

    def body(q_ref, k_ref, v_ref, qg_ref, kg_ref, do_ref, dq_ref, dk_ref, dv_ref, dqg_ref, dkg_ref):
        h, i = pl.program_id(0), pl.program_id(1)

        @pl.when(i == 0)
        def _():
            dk_ref[...] = jnp.zeros(dk_ref.shape, f32)
            dv_ref[...] = jnp.zeros(dv_ref.shape, f32)

        @pl.when((i == 0) & (h == 0))
        def _():
            dqg_ref[...] = jnp.zeros(dqg_ref.shape, f32)
            dkg_ref[...] = jnp.zeros(dkg_ref.shape, f32)

        _, vjp = jax.vjp(f_xattn, q_ref[...], k_ref[...], v_ref[...], qg_ref[...], kg_ref[...])
        dq, dk, dv, dqg, dkg = vjp(do_ref[...].astype(f32))
        dq_ref[...] = dq.astype(dq_ref.dtype)
        dk_ref[...] += dk
        dv_ref[...] += dv
        dqg_ref[...] += dqg
        dkg_ref[...] += dkg

    return pl.pallas_call(
        body, name=name, grid=(XH, t_len // tq),
        in_specs=_xa_in_specs(tq) + [pl.BlockSpec((tq, XD), lambda h, i: (i, h))],
        out_specs=[pl.BlockSpec((tq, XD), lambda h, i: (i, h)),
                   pl.BlockSpec((MEM_LEN, XD), lambda h, i: (0, h)), pl.BlockSpec((MEM_LEN, XD), lambda h, i: (0, h)),
                   pl.BlockSpec((1, XD), lambda h, i: (0, 0)), pl.BlockSpec((1, XD), lambda h, i: (0, 0))],
        out_shape=[jax.ShapeDtypeStruct((t_len, D), bf16), jax.ShapeDtypeStruct((MEM_LEN, D), f32),
                   jax.ShapeDtypeStruct((MEM_LEN, D), f32), jax.ShapeDtypeStruct((1, XD), f32), jax.ShapeDtypeStruct((1, XD), f32)],
        compiler_params=_params(("arbitrary", "arbitrary")),
    )(xq, kv, kv, qg, kg, do)


def loss_head(y, target, tile):
    t_len = y.shape[0]

    def body(y_ref, t_ref, dy_ref, l_ref):
        i = pl.program_id(0)

        @pl.when(i == 0)
        def _():
            l_ref[...] = jnp.zeros(l_ref.shape, f32)

        err = y_ref[...] - t_ref[...]
        dy_ref[...] = err * (1.0 / D)
        l_ref[...] += 0.5 * jnp.sum(jnp.sum(err * err, axis=1, keepdims=True), axis=0, keepdims=True) * (1.0 / D)

    row = pl.BlockSpec((tile, D), lambda i: (i, 0))
    return pl.pallas_call(
        body, name="loss_head", grid=(t_len // tile,), in_specs=[row, row],
        out_specs=[row, pl.BlockSpec((8, 128), lambda i: (0, 0))],
        out_shape=[jax.ShapeDtypeStruct((t_len, D), f32), jax.ShapeDtypeStruct((8, 128), f32)],
        compiler_params=_params(("arbitrary",)),
    )(y, target)


ADAM_BLOCK_ELEMS = 512 * 1024


def adamw(name, w, g, m, v):
    rows, cols = w.shape
    tile = rows
    if rows * cols > ADAM_BLOCK_ELEMS:
        tile = _pick(rows, [t for t in (512, 256, 128, 64, 32, 16, 8) if t * cols <= ADAM_BLOCK_ELEMS])

    def body(w_ref, g_ref, m_ref, v_ref, d_ref, nm_ref, nv_ref):
        gg = g_ref[...]
        nm = ADAM_B1 * m_ref[...] + (1.0 - ADAM_B1) * gg
        nv = ADAM_B2 * v_ref[...] + (1.0 - ADAM_B2) * jnp.square(gg)
        m_hat = nm / (1.0 - ADAM_B1 ** ADAM_STEP)
        v_hat = nv / (1.0 - ADAM_B2 ** ADAM_STEP)
        d_ref[...] = -ADAM_LR * (m_hat / (jnp.sqrt(v_hat) + ADAM_EPS) + ADAM_WD * w_ref[...])
        nm_ref[...] = nm
        nv_ref[...] = nv

    spec = pl.BlockSpec((tile, cols), lambda i: (i, 0))
    return pl.pallas_call(
        body, name=name, grid=(rows // tile,), in_specs=[spec] * 4, out_specs=[spec] * 3,
        out_shape=[jax.ShapeDtypeStruct((rows, cols), f32)] * 3, compiler_params=_params(("parallel",)),
    )(w, g, m, v)


def adamw_layers(name, w, m, v, gsrcs, r0):
    depth, rows, cols = w.shape
    tile = max(t for t in range(8, rows + 1, 8) if rows % t == 0 and r0 % t == 0 and t * cols <= ADAM_BLOCK_ELEMS)

    def body(*refs):
        w_ref, m_ref, v_ref = refs[:3]
        g_refs = refs[3:3 + depth]
        go_ref, d_ref, nm_ref, nv_ref = refs[3 + depth:]
        layer = pl.program_id(0)
        gg = g_refs[0][...]
        for k in range(1, depth):
            gg = jnp.where(layer == k, g_refs[k][...], gg)
        nm = ADAM_B1 * m_ref[0] + (1.0 - ADAM_B1) * gg
        nv = ADAM_B2 * v_ref[0] + (1.0 - ADAM_B2) * jnp.square(gg)
        m_hat = nm / (1.0 - ADAM_B1 ** ADAM_STEP)
        v_hat = nv / (1.0 - ADAM_B2 ** ADAM_STEP)
        go_ref[0] = gg
        d_ref[0] = -ADAM_LR * (m_hat / (jnp.sqrt(v_hat) + ADAM_EPS) + ADAM_WD * w_ref[0])
        nm_ref[0] = nm
        nv_ref[0] = nv

    spec = pl.BlockSpec((1, tile, cols), lambda l, i: (l, i, 0))
    g_specs = [pl.BlockSpec((tile, cols), functools.partial(lambda l, i, k: (jnp.where(l == k, r0 // tile + i, 0), 0), k=k))
               for k in range(depth)]
    return pl.pallas_call(
        body, name=name, grid=(depth, rows // tile), in_specs=[spec] * 3 + g_specs, out_specs=[spec] * 4,
        out_shape=[jax.ShapeDtypeStruct(w.shape, f32)] * 4, compiler_params=_params(("parallel", "parallel")),
    )(w, m, v, *gsrcs)


MESH = pl.DeviceIdType.MESH
HBM_SPEC = pl.BlockSpec(memory_space=pltpu.HBM)


def _place():
    x, y, c = lax.axis_index("x"), lax.axis_index("y"), lax.axis_index("c")
    chips = [(1 - x, y), (x, 1 - y), (1 - x, 1 - y)]
    return x, y, c, chips


def _rcopy(src, dst, send_sems, recv_sems, k, to):
    return pltpu.make_async_remote_copy(src_ref=src, dst_ref=dst, send_sem=send_sems.at[k], recv_sem=recv_sems.at[k],
                                        device_id=to, device_id_type=MESH)


class Exchange:
    def __init__(self, ins, outs, n_sems, n_local, start, wait, aliases=None):
        self.ins, self.outs, self.n_sems, self.n_local = list(ins), list(outs), n_sems, n_local
        self.start, self.wait, self.aliases = start, wait, dict(aliases or {})


class _hosted:
    def __init__(self, ex):
        self.ex = ex
        self.ins = list(ex.ins) if ex else []
        self.out_shape = list(ex.outs) if ex else []
        self.n_in, self.n_out = len(self.ins), len(self.out_shape)
        self.in_specs, self.out_specs = [HBM_SPEC] * self.n_in, [HBM_SPEC] * self.n_out
        self.scratch = [pltpu.SemaphoreType.DMA((ex.n_sems,)), pltpu.SemaphoreType.DMA((ex.n_sems,)),
                        pltpu.SemaphoreType.DMA((max(ex.n_local, 1),))] if ex else []

    def aliases(self, first_in, first_out):
        return {first_in + a: first_out + b for a, b in self.ex.aliases.items()} if self.ex else {}

    def _args(self, refs, i0, o0):
        return refs[i0:i0 + self.n_in], refs[o0:o0 + self.n_out], refs[-3], refs[-2], refs[-1]

    def at_start(self, refs, i0, o0, pred):
        if self.ex is not None:
            pl.when(pred)(lambda: self.ex.start(*self._args(refs, i0, o0)))

    def at_end(self, refs, i0, o0, pred):
        if self.ex is not None:
            pl.when(pred)(lambda: self.ex.wait(*self._args(refs, i0, o0)))


def run_exchange(name, ex):
    h = _hosted(ex)

    def body(*refs):
        args = h._args(refs, 0, h.n_in)
        ex.start(*args)
        ex.wait(*args)

    res = pl.pallas_call(
        body, name=name, out_shape=h.out_shape, in_specs=h.in_specs, out_specs=h.out_specs, scratch_shapes=h.scratch,
        input_output_aliases=h.aliases(0, 0), compiler_params=pltpu.CompilerParams(has_side_effects=True),
    )(*h.ins)
    return list(res)


def _halves(ref, axis, c):
    hh = ref.shape[axis] // 2
    return pl.ds(pl.multiple_of(c * hh, 16), hh), pl.ds(pl.multiple_of((1 - c) * hh, 16), hh)


def ex_gather_ici(shards):
    n = len(shards)

    def copies(ins, outs, ssem, rsem, landing):
        x, y, c, chips = _place()
        me = 2 * x + y
        res = []
        for a, (s, o) in enumerate(zip(ins, outs)):
            mine, _ = _halves(s, 0, c)
            for j, (cx, cy) in enumerate(chips[:2]):
                slot = 2 * cx + cy if landing else me
                res.append(_rcopy(s.at[mine], o.at[slot, mine], ssem, rsem, 3 * a + j, (cx, cy, c)))
            res.append(_rcopy(s, o.at[me], ssem, rsem, 3 * a + 2, (x, y, 1 - c)))
        return res

    def start(ins, outs, ssem, rsem, lsem):
        for cp in copies(ins, outs, ssem, rsem, False):
            cp.start()

    def wait(ins, outs, ssem, rsem, lsem):
        for cp in copies(ins, outs, ssem, rsem, True):
            cp.wait_recv()
        for cp in copies(ins, outs, ssem, rsem, False):
            cp.wait_send()

    outs = [jax.ShapeDtypeStruct((N_CHIPS,) + s.shape, s.dtype) for s in shards]
    return Exchange(shards, outs, 3 * n, 0, start, wait)


def ex_gather_d2d(gathered):
    n = len(gathered)

    def copies(outs, ssem, rsem, landing):
        x, y, c, chips = _place()
        res = []
        for a, o in enumerate(outs):
            mine, theirs = _halves(o, 1, c)
            for j, (cx, cy) in enumerate(chips[:2]):
                rows = o.at[2 * cx + cy, theirs if landing else mine]
                res.append(_rcopy(rows, rows, ssem, rsem, 3 * a + j, (x, y, 1 - c)))
            fx, fy = lax.rem(x + 1 - c, 2), lax.rem(y + c, 2)
            tx, ty = lax.rem(x + c, 2), lax.rem(y + 1 - c, 2)
            slot = 2 * (1 - x) + (1 - y) if landing else 2 * fx + fy
            rows = o.at[slot, mine]
            res.append(_rcopy(rows, rows, ssem, rsem, 3 * a + 2, (tx, ty, c)))
        return res

    def start(ins, outs, ssem, rsem, lsem):
        for cp in copies(outs, ssem, rsem, False):
            cp.start()

    def wait(ins, outs, ssem, rsem, lsem):
        for cp in copies(outs, ssem, rsem, True):
            cp.wait_recv()
        for cp in copies(outs, ssem, rsem, False):
            cp.wait_send()

    outs = [jax.ShapeDtypeStruct(g.shape, g.dtype) for g in gathered]
    return Exchange(gathered, outs, 3 * n, 0, start, wait, aliases={a: a for a in range(n)})


def ex_gather_diag(gathered):
    n = len(gathered)

    def copies(outs, ssem, rsem, landing):
        x, y, c, _ = _place()
        res = []
        for a, o in enumerate(outs):
            mine, theirs = _halves(o, 1, c)
            rows = o.at[2 * (1 - x) + (1 - y), theirs if landing else mine]
            res.append(_rcopy(rows, rows, ssem, rsem, a, (x, y, 1 - c)))
        return res

    def start(ins, outs, ssem, rsem, lsem):
        for cp in copies(outs, ssem, rsem, False):
            cp.start()

    def wait(ins, outs, ssem, rsem, lsem):
        for cp in copies(outs, ssem, rsem, True):
            cp.wait_recv()
        for cp in copies(outs, ssem, rsem, False):
            cp.wait_send()

    outs = [jax.ShapeDtypeStruct(g.shape, g.dtype) for g in gathered]
    return Exchange(gathered, outs, n, 0, start, wait, aliases={a: a for a in range(n)})


def ex_swap_halves(gs):
    n = len(gs)

    def copies(ins, outs, ssem, rsem):
        x, y, c, _ = _place()
        return [_rcopy(g.at[s, 1 - c], o.at[s], ssem, rsem, N_CHIPS * a + s, (x, y, 1 - c))
                for a, (g, o) in enumerate(zip(ins, outs)) for s in range(N_CHIPS)]

    def start(ins, outs, ssem, rsem, lsem):
        for cp in copies(ins, outs, ssem, rsem):
            cp.start()

    def wait(ins, outs, ssem, rsem, lsem):
        for cp in copies(ins, outs, ssem, rsem):
            cp.wait()

    outs = [jax.ShapeDtypeStruct((N_CHIPS,) + g.shape[2:], g.dtype) for g in gs]
    return Exchange(gs, outs, N_CHIPS * n, 0, start, wait)


def ex_scatter(ps):
    n = len(ps)

    def copies(ins, outs, ssem, rsem):
        x, y, c, chips = _place()
        return [_rcopy(p.at[2 * cx + cy], o.at[j], ssem, rsem, 3 * a + j, (cx, cy, c))
                for a, (p, o) in enumerate(zip(ins, outs)) for j, (cx, cy) in enumerate(chips)]

    def start(ins, outs, ssem, rsem, lsem):
        for cp in copies(ins, outs, ssem, rsem):
            cp.start()

    def wait(ins, outs, ssem, rsem, lsem):
        for cp in copies(ins, outs, ssem, rsem):
            cp.wait()

    outs = [jax.ShapeDtypeStruct((3,) + p.shape[1:], p.dtype) for p in ps]
    return Exchange(ps, outs, 3 * n, 0, start, wait)


def ex_share(rs):
    n = len(rs)

    def copies(ins, outs, ssem, rsem):
        x, y, c, _ = _place()
        return [_rcopy(r, o, ssem, rsem, a, (x, y, 1 - c)) for a, (r, o) in enumerate(zip(ins, outs))]

    def start(ins, outs, ssem, rsem, lsem):
        for cp in copies(ins, outs, ssem, rsem):
            cp.start()

    def wait(ins, outs, ssem, rsem, lsem):
        for cp in copies(ins, outs, ssem, rsem):
            cp.wait()

    outs = [jax.ShapeDtypeStruct(r.shape, r.dtype) for r in rs]
    return Exchange(rs, outs, n, 0, start, wait)


ADD_BLOCK_ELEMS = 384 * 1024


def _add_tile(rows, cols):
    return _pick(rows, [t for t in (1920, 960, 512, 384, 256, 128, 64, 32, 16) if t * cols <= ADD_BLOCK_ELEMS])


def add_halves(name, g, got, c_arr):
    _, _, rows, cols = g.shape
    tile = _add_tile(rows, cols)

    def body(c_ref, g_ref, r_ref, o_ref):
        o_ref[...] = (g_ref[0] + r_ref[...]).astype(o_ref.dtype)

    return pl.pallas_call(
        body, name=name,
        grid_spec=pltpu.PrefetchScalarGridSpec(
            num_scalar_prefetch=1, grid=(N_CHIPS, rows // tile),
            in_specs=[pl.BlockSpec((1, 1, tile, cols), lambda s, i, c_ref: (s, c_ref[0], i, 0)),
                      pl.BlockSpec((1, tile, cols), lambda s, i, c_ref: (s, i, 0))],
            out_specs=pl.BlockSpec((1, tile, cols), lambda s, i, c_ref: (s, i, 0))),
        out_shape=jax.ShapeDtypeStruct(got.shape, bf16), compiler_params=_params(("parallel", "parallel")),
    )(c_arr, g, got)


def add_partials(name, p, got, me_arr):
    _, rows, cols = p.shape
    tile = _add_tile(rows, cols)

    def body(me_ref, p_ref, r_ref, o_ref):
        o_ref[...] = ((p_ref[0].astype(f32) + r_ref[0].astype(f32)) + r_ref[1].astype(f32)) + r_ref[2].astype(f32)

    return pl.pallas_call(
        body, name=name,
        grid_spec=pltpu.PrefetchScalarGridSpec(
            num_scalar_prefetch=1, grid=(rows // tile,),
            in_specs=[pl.BlockSpec((1, tile, cols), lambda i, me_ref: (me_ref[0], i, 0)),
                      pl.BlockSpec((3, tile, cols), lambda i, me_ref: (0, i, 0))],
            out_specs=pl.BlockSpec((tile, cols), lambda i, me_ref: (i, 0))),
        out_shape=jax.ShapeDtypeStruct((rows, cols), f32), compiler_params=_params(("parallel",)),
    )(me_arr, p, got)


def join_halves(name, mine, theirs):
    rows, cols = mine.shape
    tile = _add_tile(rows, cols)

    def body(m_ref, t_ref, o_ref):
        south = lax.axis_index("c") == 0
        o_ref[0] = jnp.where(south, m_ref[...], t_ref[...])
        o_ref[1] = jnp.where(south, t_ref[...], m_ref[...])

    spec = pl.BlockSpec((tile, cols), lambda i: (i, 0))
    return pl.pallas_call(
        body, name=name, grid=(rows // tile,), in_specs=[spec, spec],
        out_specs=pl.BlockSpec((2, tile, cols), lambda i: (0, i, 0)),
        out_shape=jax.ShapeDtypeStruct((2, rows, cols), f32), compiler_params=_params(("parallel",)),
    )(mine, theirs)


N_DEV = 8


def allreduce_small(name, buf):
    m_per = buf.shape[0]

    def body(x_ref, out_ref, all_ref, send_sems, recv_sems, local_sem):
        x, y, c, chips = _place()
        me, sibling = (x, y, c), (x, y, 1 - c)

        def rows(px, py, pc):
            return all_ref.at[pl.ds(pl.multiple_of((4 * px + 2 * py + pc) * m_per, 8), m_per), :]

        def copy(k, block, to, src=None):
            return _rcopy(rows(*block) if src is None else src, rows(*block), send_sems, recv_sems, k, to)

        mine = pltpu.make_async_copy(x_ref, rows(*me), local_sem)
        mine.start()
        first = [copy(0, me, sibling, src=x_ref)]
        first += [copy(1 + j, me, (*chip, c), src=x_ref) for j, chip in enumerate(chips)]
        for cp in first:
            cp.start()
        passed = [copy(4 + j, (*chip, c), sibling) for j, chip in enumerate(chips)]
        for j, chip in enumerate(chips):
            copy(1 + j, (*chip, c), me).wait_recv()
            passed[j].start()
        copy(0, sibling, me).wait_recv()
        for j, chip in enumerate(chips):
            copy(4 + j, (*chip, 1 - c), me).wait_recv()
        for cp in first + passed:
            cp.wait_send()
        mine.wait()
        tot = all_ref[pl.ds(0, m_per), :]
        for d in range(1, N_DEV):
            tot = tot + all_ref[pl.ds(d * m_per, m_per), :]
        out_ref[...] = tot

    return pl.pallas_call(
        body, name=name, out_shape=jax.ShapeDtypeStruct((m_per, 128), f32),
        in_specs=[pl.BlockSpec(memory_space=pltpu.VMEM)], out_specs=pl.BlockSpec(memory_space=pltpu.VMEM),
        scratch_shapes=[pltpu.VMEM((N_DEV * m_per, 128), f32), pltpu.SemaphoreType.DMA((7,)), pltpu.SemaphoreType.DMA((7,)),
                        pltpu.SemaphoreType.DMA],
        compiler_params=pltpu.CompilerParams(has_side_effects=True, vmem_limit_bytes=VMEM_LIMIT_BYTES),
    )(buf)


def pack_shards(ws, layer):
    wide = jnp.concatenate([ws[n][layer].astype(bf16) for n in WIDE], axis=0)
    w_in_t = jnp.pad(jnp.transpose(ws["w_in"][layer]).astype(bf16), ((0, IN_PAD - IN_SHARD), (0, 0)))
    return [w_in_t, wide, ws["w_xkv"][layer].astype(bf16)]


def arrange_w_in(shards):
    parts = []
    for a, b, _ in sorted(ORIG_SEGS, key=lambda seg: seg[2]):
        for s in range(N_CHIPS):
            lo, hi = max(a, s * IN_SHARD), min(b, (s + 1) * IN_SHARD)
            if lo < hi:
                parts.append(shards[s][lo - s * IN_SHARD:hi - s * IN_SHARD])
    parts.append(jnp.zeros((PCOLS - C_DT - (OFF_Q - OFF_DT), shards.shape[2]), shards.dtype))
    return jnp.concatenate(parts, axis=0)


def shard_w_in_grad(g):
    out = []
    for s in range(N_CHIPS):
        for a, b, first in ORIG_SEGS:
            lo, hi = max(a, s * IN_SHARD), min(b, (s + 1) * IN_SHARD)
            if lo < hi:
                out.append(g[first + lo - a:first + hi - a])
        out.append(jnp.zeros((IN_PAD - IN_SHARD, g.shape[1]), g.dtype))
    return jnp.concatenate(out, axis=0).reshape(N_CHIPS, IN_PAD, g.shape[1])


def unpack_gathered(gathered):
    g_in, g_wide, g_xkv = gathered
    out = dict(w_in=arrange_w_in(g_in), w_xkv=jnp.concatenate([g_xkv[s] for s in range(N_CHIPS)], axis=1))
    for n in WIDE:
        piece = g_wide[:, WIDE_OFF[n]:WIDE_OFF[n] + WIDE_ROWS[n], :]
        if n == "w_mlp_up":
            out[n] = jnp.concatenate([piece[s] for s in range(N_CHIPS)], axis=1)
        else:
            out[n] = piece.reshape(N_CHIPS * WIDE_ROWS[n], D)
    return out


def _halved(a):
    return a.reshape(N_CHIPS, 2, a.shape[1] // 2, a.shape[2])


def pack_w_in_grad(g):
    return [_halved(shard_w_in_grad(g))]


def pack_other_grads(gs):
    cols = lambda g, q: jnp.concatenate([g[:, s * q:(s + 1) * q] for s in range(N_CHIPS)], axis=0).reshape(N_CHIPS, g.shape[0], q)
    wide = [cols(gs[n], D) if n == "w_mlp_up" else gs[n].reshape(N_CHIPS, WIDE_ROWS[n], D) for n in WIDE]
    return [_halved(jnp.concatenate(wide, axis=1)), _halved(cols(gs["w_xkv"], 2048 // N_CHIPS))]


ROW_TILE = 256


def _pad128(v):
    return jnp.pad(v.reshape(1, -1), ((0, 0), (0, 128 - v.shape[-1])))


def layer_forward(i, h, mem, w, p, bias, next_shards=None):
    tile = min(ROW_TILE, h.shape[0])
    s = dict(h=h)
    row = lambda a: a.reshape(1, -1)
    s["u"], = tile_fwd(f"rms_mix{i}", f_rms, [(h, D, 0)], [row(p["norm_mix"])], [(D, bf16)], tile)
    s["proj"] = proj = matmul(s["u"], w["w_in"], "nt", f"mm_in{i}")
    s["c1"], = conv_fwd(f"conv_a{i}", proj, C_CONV, C_CONV + D, p["conv_dw_w"], row(p["conv_dw_b"]), True, False)
    s["ca"], = tile_fwd(f"ln_silu{i}", f_ln_silu, [(s["c1"], D, 0)], [row(p["conv_ln_g"]), row(p["conv_ln_b"])], [(D, bf16)], tile)
    s["ya"] = matmul(s["ca"], w["w_conv_out"], "nn", f"mm_conv_out{i}")
    s["xpre"], s["xact"] = conv_fwd(f"conv_b{i}", proj, C_XBC, C_XBC, p["ssd_conv_w"], row(p["ssd_conv_b"]), False, True)
    ssd_p = (_pad128(p["ssd_dt_bias"]), _pad128(p["ssd_A_log"]), _pad128(p["ssd_D"]), row(p["ssd_norm_g"]))
    s["yB"], s["states"], landed = ssd_fwd(f"ssd_fwd{i}", s["xact"], proj, *ssd_p,
                                           ex=ex_gather_ici(next_shards) if next_shards else None)
    s["yb"] = matmul(s["yB"], w["w_ssd_out"], "nn", f"mm_ssd_out{i}")
    swa_p = (row(p["attn_q_norm"]), row(p["attn_k_norm"]), _pad128(p["attn_sinks"]))
    s["oc"], gathered = swa_fwd(f"swa_fwd{i}", proj, *swa_p, bias, ex=ex_gather_d2d(landed) if next_shards else None)
    if next_shards:
        gathered = run_exchange(f"gather_diag{i + 1}", ex_gather_diag(gathered))
    s["yc"] = matmul(s["oc"], w["w_attn_out"], "nn", f"mm_attn_out{i}")
    gate_cols = [(proj, D, C_G // D + k) for k in range(3)]
    s["merged"], = tile_fwd(f"merge{i}", f_merge, gate_cols + [(s["ya"], D, 0), (s["yb"], D, 0), (s["yc"], D, 0)],
                            [row(p["gate_bias"])], [(D, bf16)], tile)
    s["h1"] = h1 = matmul(s["merged"], w["w_mix_out"], "nn", f"mm_mix_out{i}", residual=h)
    s["hx"], = tile_fwd(f"rms_x{i}", f_rms, [(h1, D, 0)], [row(p["norm_xattn"])], [(D, bf16)], tile)
    s["memh"], = tile_fwd(f"rms_mem{i}", f_rms, [(mem, D, 0)], [row(p["norm_mem"])], [(D, bf16)], MEM_LEN)
    s["xq"] = matmul(s["hx"], w["w_xq"], "nn", f"mm_xq{i}")
    s["kv"] = matmul(s["memh"], w["w_xkv"], "nn", f"mm_xkv{i}")
    s["xo"] = xa_fwd(f"xa_fwd{i}", s["xq"], s["kv"], row(p["xattn_q_norm"]), row(p["xattn_k_norm"]))
    s["h2"] = h2 = matmul(s["xo"], w["w_xo"], "nn", f"mm_xo{i}", residual=h1)
    s["um"], = tile_fwd(f"rms_mlp{i}", f_rms, [(h2, D, 0)], [row(p["norm_mlp"])], [(D, bf16)], tile)
    s["up"], s["act"] = matmul(s["um"], w["w_mlp_up"], "nn", f"mm_up{i}", relu2="fwd")
    h3 = matmul(s["act"], w["w_mlp_down"], "nn", f"mm_down{i}", residual=h2)
    return h3, s, gathered


def layer_backward(i, dh3, mem, w, p, bias, s, pending=None, c_arr=None):
    t_len = dh3.shape[0]
    tile = min(ROW_TILE, t_len)
    row = lambda a: a.reshape(1, -1)
    gw, gp = {}, {}
    dup = matmul(dh3, w["w_mlp_down"], "nt", f"mm_down_dx{i}", out_dtype=bf16, relu2=s["up"])
    gw["w_mlp_down"] = matmul(s["act"], dh3, "tn", f"mm_down_dw{i}")
    gw["w_mlp_up"] = matmul(s["um"], dup, "tn", f"mm_up_dw{i}")
    dum = matmul(dup, w["w_mlp_up"], "nt", f"mm_up_dx{i}")
    (dh2,), (g,) = tile_bwd(f"rms_mlp_bwd{i}", f_rms, [(s["h2"], D, 0)], [row(p["norm_mlp"])], [dum], [f32], tile, addend=dh3)
    gp["norm_mlp"] = g
    dxo = matmul(dh2, w["w_xo"], "nt", f"mm_xo_dx{i}")
    gw["w_xo"] = matmul(s["xo"], dh2, "tn", f"mm_xo_dw{i}")
    dxq, dk, dv, gp["xattn_q_norm"], gp["xattn_k_norm"] = xa_bwd(f"xa_bwd{i}", s["xq"], s["kv"], dxo, row(p["xattn_q_norm"]),
                                                                 row(p["xattn_k_norm"]))
    dkv = jnp.concatenate([dk, dv], axis=1)
    gw["w_xq"] = matmul(s["hx"], dxq, "tn", f"mm_xq_dw{i}")
    dhx = matmul(dxq, w["w_xq"], "nt", f"mm_xq_dx{i}")
    gw["w_xkv"] = matmul(s["memh"], dkv, "tn", f"mm_xkv_dw{i}")
    dmemh = matmul(dkv, w["w_xkv"], "nt", f"mm_xkv_dx{i}")
    _, (g,) = tile_bwd(f"rms_mem_bwd{i}", f_rms, [(mem, D, 0)], [row(p["norm_mem"])], [dmemh], [f32], MEM_LEN)
    gp["norm_mem"] = g
    (dh1,), (g,) = tile_bwd(f"rms_x_bwd{i}", f_rms, [(s["h1"], D, 0)], [row(p["norm_xattn"])], [dhx], [f32], tile, addend=dh2)
    gp["norm_xattn"] = g
    proj = s["proj"]
    dmerged = matmul(dh1, w["w_mix_out"], "nt", f"mm_mix_out_dx{i}")
    gw["w_mix_out"] = matmul(s["merged"], dh1, "tn", f"mm_mix_out_dw{i}")
    gate_cols = [(proj, D, C_G // D + k) for k in range(3)]
    (dpa, dpb, dpc, dya, dyb, dyc), (g,) = tile_bwd(
        f"merge_bwd{i}", f_merge, gate_cols + [(s["ya"], D, 0), (s["yb"], D, 0), (s["yc"], D, 0)], [row(p["gate_bias"])],
        [dmerged], [bf16] * 6, tile)
    gp["gate_bias"] = g
    dca = matmul(dya, w["w_conv_out"], "nt", f"mm_conv_out_dx{i}")
    gw["w_conv_out"] = matmul(s["ca"], dya, "tn", f"mm_conv_out_dw{i}")
    (dc1,), (gp["conv_ln_g"], gp["conv_ln_b"]) = tile_bwd(
        f"ln_silu_bwd{i}", f_ln_silu, [(s["c1"], D, 0)], [row(p["conv_ln_g"]), row(p["conv_ln_b"])], [dca], [f32], tile)
    da, dgate, gp["conv_dw_w"], gp["conv_dw_b"] = conv_bwd(f"conv_a_bwd{i}", dc1, None, proj, C_CONV, C_CONV + D,
                                                           p["conv_dw_w"], True, False, bf16)
    dyB = matmul(dyb, w["w_ssd_out"], "nt", f"mm_ssd_out_dx{i}")
    gw["w_ssd_out"] = matmul(s["yB"], dyb, "tn", f"mm_ssd_out_dw{i}")
    doc = matmul(dyc, w["w_attn_out"], "nt", f"mm_attn_out_dx{i}")
    gw["w_attn_out"] = matmul(s["oc"], dyc, "tn", f"mm_attn_out_dw{i}")
    moving = list(pending or []) + pack_other_grads(gw)
    swa_p = (row(p["attn_q_norm"]), row(p["attn_k_norm"]), _pad128(p["attn_sinks"]))
    (dq, dkp, dkc, dvp, dvc, gp["attn_q_norm"], gp["attn_k_norm"], g, dbias), swapped = swa_bwd(
        f"swa_bwd{i}", proj, doc, *swa_p, bias, ex=ex_swap_halves(moving))
    gp["attn_sinks"] = g[:, :ATTN_HEADS]
    shift = lambda a: jnp.concatenate([a[BLK:], jnp.zeros((BLK, a.shape[1]), a.dtype)], axis=0)
    dk_ = (dkc + shift(dkp)).astype(bf16)
    dv_ = (dvc + shift(dvp)).astype(bf16)
    partials = [add_halves(f"add_halves{i}_{k}", g_, got, c_arr) for k, (g_, got) in enumerate(zip(moving, swapped))]
    ssd_p = (_pad128(p["ssd_dt_bias"]), _pad128(p["ssd_A_log"]), _pad128(p["ssd_D"]), row(p["ssd_norm_g"]))
    (dxs, dbm, dcm, dz, ddt4, g1, g2, g3, g4), arrived = ssd_bwd(f"ssd_bwd{i}", s["xact"], proj, s["states"], dyB, *ssd_p,
                                                                  ex=ex_scatter(partials))
    gp["ssd_dt_bias"], gp["ssd_A_log"], gp["ssd_D"], gp["ssd_norm_g"] = g1[:, :SSD_HEADS], g2[:, :SSD_HEADS], g3[:, :SSD_HEADS], g4
    ddt = (ddt4[0] + ddt4[1] + ddt4[2] + ddt4[3]).astype(bf16)
    dxact = jnp.concatenate([dxs, dbm, dcm], axis=1)
    dxbc, gp["ssd_conv_w"], gp["ssd_conv_b"] = conv_bwd(f"conv_b_bwd{i}", dxact, s["xpre"], proj, C_XBC, C_XBC, p["ssd_conv_w"],
                                                        False, True, bf16)
    dproj = jnp.concatenate([da, dgate, dz, dxbc, dq, dpa, dpb, dpc, dk_, dv_, ddt, jnp.zeros((t_len, PCOLS - C_DT - 128), bf16)], axis=1)
    gw["w_in"] = matmul(dproj, s["u"], "tn", f"mm_in_dw{i}")
    du = matmul(dproj, w["w_in"], "nn", f"mm_in_dx{i}")
    (dh,), (g,) = tile_bwd(f"rms_mix_bwd{i}", f_rms, [(s["h"], D, 0)], [row(p["norm_mix"])], [du], [f32], tile, addend=dh1)
    gp["norm_mix"] = g
    return dh, gw, gp, dbias, (partials, arrived)


SMALL = ["rel_table", "norm_mix", "gate_bias", "conv_dw_w", "conv_dw_b", "conv_ln_g", "conv_ln_b", "ssd_conv_w", "ssd_conv_b",
         "ssd_dt_bias", "ssd_A_log", "ssd_D", "ssd_norm_g", "attn_q_norm", "attn_k_norm", "attn_sinks", "norm_xattn", "norm_mem",
         "xattn_q_norm", "xattn_k_norm", "norm_mlp"]
SMALL_SHARDED = dict(gate_bias=D, conv_dw_w=D, ssd_conv_w=SSD_XBC)
ORDER = ["rel_table", "norm_mix", "w_in", "gate_bias", "conv_dw_w", "conv_dw_b", "conv_ln_g", "conv_ln_b", "w_conv_out",
         "ssd_conv_w", "ssd_conv_b", "ssd_dt_bias", "ssd_A_log", "ssd_D", "ssd_norm_g", "w_ssd_out", "attn_q_norm", "attn_k_norm",
         "attn_sinks", "w_attn_out", "w_mix_out", "norm_xattn", "norm_mem", "w_xq", "w_xkv", "xattn_q_norm", "xattn_k_norm",
         "w_xo", "norm_mlp", "w_mlp_up", "w_mlp_down"]


def finish_reduction(i, partials, arrived, me_arr):
    mine = [add_partials(f"add_partials{i}_{k}", p, got, me_arr) for k, (p, got) in enumerate(zip(partials, arrived))]
    theirs = run_exchange(f"share_halves{i}", ex_share(mine))
    both = [join_halves(f"join_halves{i}_{k}", m, t) for k, (m, t) in enumerate(zip(mine, theirs))]
    return [b.reshape(2 * b.shape[1], b.shape[2]) for b in both]


def kernel(x, mem, rel_table, norm_mix, w_in, gate_bias, conv_dw_w, conv_dw_b, conv_ln_g, conv_ln_b, w_conv_out, ssd_conv_w, ssd_conv_b, ssd_dt_bias, ssd_A_log, ssd_D, ssd_norm_g, w_ssd_out, attn_q_norm, attn_k_norm, attn_sinks, w_attn_out, w_mix_out, norm_xattn, norm_mem, w_xq, w_xkv, xattn_q_norm, xattn_k_norm, w_xo, norm_mlp, w_mlp_up, w_mlp_down, loss_target, m_rel_table, m_norm_mix, m_w_in, m_gate_bias, m_conv_dw_w, m_conv_dw_b, m_conv_ln_g, m_conv_ln_b, m_w_conv_out, m_ssd_conv_w, m_ssd_conv_b, m_ssd_dt_bias, m_ssd_A_log, m_ssd_D, m_ssd_norm_g, m_w_ssd_out, m_attn_q_norm, m_attn_k_norm, m_attn_sinks, m_w_attn_out, m_w_mix_out, m_norm_xattn, m_norm_mem, m_w_xq, m_w_xkv, m_xattn_q_norm, m_xattn_k_norm, m_w_xo, m_norm_mlp, m_w_mlp_up, m_w_mlp_down, v_rel_table, v_norm_mix, v_w_in, v_gate_bias, v_conv_dw_w, v_conv_dw_b, v_conv_ln_g, v_conv_ln_b, v_w_conv_out, v_ssd_conv_w, v_ssd_conv_b, v_ssd_dt_bias, v_ssd_A_log, v_ssd_D, v_ssd_norm_g, v_w_ssd_out, v_attn_q_norm, v_attn_k_norm, v_attn_sinks, v_w_attn_out, v_w_mix_out, v_norm_xattn, v_norm_mem, v_w_xq, v_w_xkv, v_xattn_q_norm, v_xattn_k_norm, v_w_xo, v_norm_mlp, v_w_mlp_up, v_w_mlp_down):
    a = dict(locals())
    wts = {n: a[n] for n in ORDER}
    ms = {n: a["m_" + n] for n in ORDER}
    vs = {n: a["v_" + n] for n in ORDER}
    depth = norm_mix.shape[0]
    ax, ay, ac = lax.axis_index("x"), lax.axis_index("y"), lax.axis_index("c")
    chip = 2 * ax + ay

    small = {n: wts[n] for n in SMALL if n != "rel_table" and n not in SMALL_SHARDED}
    gather_buf, spans, r0 = [], {}, 0
    for n, width in SMALL_SHARDED.items():
        blk = wts[n]
        q = width // N_CHIPS
        whole = lax.dynamic_update_slice(jnp.zeros(blk.shape[:-1] + (width,), f32), blk, (0, 0, chip * q))
        flat = whole.reshape(-1, 128)
        spans[n] = (r0, flat.shape[0], whole.shape)
        r0 += flat.shape[0]
        gather_buf.append(flat)
    summed = allreduce_small("gather_small_params", jnp.concatenate(gather_buf, axis=0))
    for n, (s0, nr, shp) in spans.items():
        small[n] = summed[s0:s0 + nr].reshape(shp) * 0.5

    c_arr = jnp.reshape(ac, (1,)).astype(jnp.int32)
    me_arr = jnp.reshape(chip, (1,)).astype(jnp.int32)
    x0, mem0 = x[0], mem[0]
    bucket = jnp.asarray(_bucket_table())
    bias = relbias_fwd(rel_table, bucket)

    gathered = run_exchange("gather_ici_first", ex_gather_ici(pack_shards(wts, 0)))
    gathered = run_exchange("gather_diag0", ex_gather_diag(run_exchange("gather_d2d_first", ex_gather_d2d(gathered))))
    h, saved, ws = x0, [], []
    for i in range(depth):
        w = unpack_gathered(gathered)
        p = {n: small[n][i] for n in small}
        h, s, gathered = layer_forward(i, h, mem0, w, p, bias, pack_shards(wts, i + 1) if i + 1 < depth else None)
        saved.append(s)
        ws.append((w, p))
    grad_x, loss_tile = loss_head(h, loss_target[0], min(ROW_TILE, h.shape[0]))
    loss = lax.psum(loss_tile[0, 0], ("x", "y", "c"))

    reduced, small_grads, dbiases, pending = [[None] * 3 for _ in range(depth)], [None] * depth, [], None
    for i in reversed(range(depth)):
        w, p = ws[i]
        grad_x, gw, gp, dbias, travelled = layer_backward(i, grad_x, mem0, w, p, bias, saved[i], pending, c_arr)
        done = finish_reduction(i, *travelled, me_arr)
        if pending is not None:
            reduced[i + 1][0] = done.pop(0)
        reduced[i][1], reduced[i][2] = done
        small_grads[i] = gp
        dbiases.append(dbias)
        pending = pack_w_in_grad(gw["w_in"])
    swapped = run_exchange("swap_halves_last", ex_swap_halves(pending))
    partials = [add_halves("add_halves_last", pending[0], swapped[0], c_arr)]
    reduced[0][0], = finish_reduction("_last", partials, run_exchange("scatter_last", ex_scatter(partials)), me_arr)
    sg = {n: jnp.stack([small_grads[i][n].reshape(small[n].shape[1:]) for i in range(depth)]) for n in small}
    sg["rel_table"] = relbias_bwd(dbiases, bucket)

    grads, deltas, new_m, new_v = {}, {}, {}, {}
    sources = dict(w_in=(0, 0), w_xkv=(2, 0), **{n: (1, WIDE_OFF[n]) for n in WIDE})
    for n, (k, r0) in sources.items():
        flip = (lambda t: jnp.transpose(t, (0, 2, 1))) if n == "w_in" else (lambda t: t)
        res = adamw_layers(f"adamw_{n}", flip(wts[n]), flip(ms[n]), flip(vs[n]), [reduced[i][k] for i in range(depth)], r0)
        grads[n], deltas[n], new_m[n], new_v[n] = [flip(r) for r in res]

    parts, spans, r0 = [], {}, 0
    for n in SMALL:
        flat = sg[n].reshape(-1)
        nr = -(-flat.shape[0] // 128)
        nr = -(-nr // 8) * 8
        flat = jnp.pad(flat, (0, nr * 128 - flat.shape[0])).reshape(nr, 128)
        spans[n] = (r0, nr, sg[n].shape)
        r0 += nr
        parts.append(flat)
    summed = allreduce_small("allreduce_small_grads", jnp.concatenate(parts, axis=0))
    for n, (s0, nr, shp) in spans.items():
        size = int(np.prod(shp))
        g = summed[s0:s0 + nr].reshape(-1)[:size].reshape(shp)
        if n in SMALL_SHARDED:
            q = SMALL_SHARDED[n] // N_CHIPS
            g = lax.dynamic_slice_in_dim(g, chip * q, q, axis=g.ndim - 1)
        grads[n] = g

    for n in SMALL:
        shp = wts[n].shape
        two = (lambda t: t.reshape(-1, shp[-1]))
        d, nm, nv = adamw(f"adamw_{n}", two(wts[n]), two(grads[n]), two(ms[n]), two(vs[n]))
        deltas[n], new_m[n], new_v[n] = d.reshape(shp), nm.reshape(shp), nv.reshape(shp)

    return (loss, grad_x[None], *[grads[n] for n in ORDER], *[deltas[n] for n in ORDER],
            *[new_m[n] for n in ORDER], *[new_v[n] for n in ORDER])
```

```python
import functools
import math

import numpy as np
import jax
import jax.numpy as jnp
from jax import lax
from jax.experimental import pallas as pl
from jax.experimental.pallas import tpu as pltpu

f32, bf16 = jnp.float32, jnp.bfloat16

D = 1024
EPS = 1e-6
NEG_INF = -1e30
CONV_K = 31
SSD_INNER = 2048
SSD_HEADS = 32
SSD_P = 64
SSD_GROUPS = 4
SSD_N = 128
SSD_K = 4
CHUNK = 128
SSD_XBC = 3072
ATTN_HEADS = 16
ATTN_KV = 4
ATTN_D = 64
BLK = 128
REL_BUCKETS = 32
XH = 4
XD = 256
MLP = 4096
MEM_LEN = 256

OFF_Z, OFF_XBC, OFF_DT, OFF_Q, OFF_K, OFF_V, OFF_GATE, IN_COLS = 2048, 4096, 7168, 7200, 8224, 8480, 8736, 11808
C_CONV, C_Z, C_XBC, C_Q, C_G, C_K, C_V, C_DT, PCOLS = 0, 2048, 4096, 7168, 8192, 11264, 11520, 11776, 12288

ADAM_LR, ADAM_B1, ADAM_B2, ADAM_EPS, ADAM_WD, ADAM_STEP = 0.001, 0.9, 0.999, 1e-08, 0.01, 10

VMEM_LIMIT_BYTES = 56 * 1024 * 1024
N_CHIPS = 4
BIG = ["w_in", "w_conv_out", "w_ssd_out", "w_attn_out", "w_mix_out", "w_xq", "w_xkv", "w_xo", "w_mlp_up", "w_mlp_down"]
WIDE = ["w_conv_out", "w_ssd_out", "w_attn_out", "w_mix_out", "w_xq", "w_xo", "w_mlp_up", "w_mlp_down"]
WIDE_ROWS = dict(w_conv_out=256, w_ssd_out=512, w_attn_out=256, w_mix_out=256, w_xq=256, w_xo=256, w_mlp_up=1024, w_mlp_down=1024)
WIDE_OFF = {n: sum(WIDE_ROWS[m] for m in WIDE[:k]) for k, n in enumerate(WIDE)}
WIDE_TOTAL = sum(WIDE_ROWS.values())
IN_SHARD = IN_COLS // N_CHIPS
IN_PAD = 3072
ORIG_SEGS = [(0, OFF_DT, 0), (OFF_DT, OFF_Q, C_DT), (OFF_Q, OFF_K, C_Q), (OFF_K, OFF_GATE, C_K), (OFF_GATE, IN_COLS, C_G)]


def _params(sem=None):
    return pltpu.CompilerParams(dimension_semantics=sem, vmem_limit_bytes=VMEM_LIMIT_BYTES)


NN, NT, TN = ((1,), (0,)), ((1,), (1,)), ((0,), (0,))


def _dg(a, b, dims):
    return lax.dot_general(a.astype(bf16), b.astype(bf16), (dims, ((), ())), preferred_element_type=f32)


@jax.custom_vjp
def dot_nn(a, b):
    return _dg(a, b, NN)


dot_nn.defvjp(lambda a, b: (_dg(a, b, NN), (a, b)), lambda r, g: (_dg(g, r[1], NT), _dg(r[0], g, TN)))


@jax.custom_vjp
def dot_nt(a, b):
    return _dg(a, b, NT)


dot_nt.defvjp(lambda a, b: (_dg(a, b, NT), (a, b)), lambda r, g: (_dg(g, r[1], NN), _dg(g, r[0], TN)))


@jax.custom_vjp
def dot_tn(a, b):
    return _dg(a, b, TN)


dot_tn.defvjp(lambda a, b: (_dg(a, b, TN), (a, b)), lambda r, g: (_dg(r[1], g, NT), _dg(r[0], g, NN)))


def _tri(n, upper):
    r = lax.broadcasted_iota(jnp.int32, (n, n), 0)
    c = lax.broadcasted_iota(jnp.int32, (n, n), 1)
    return jnp.where((c >= r) if upper else (r >= c), 1.0, 0.0).astype(f32)


def _hdot(a, b):
    return lax.dot_general(a, b, (NN, ((), ())), preferred_element_type=f32, precision=lax.Precision.HIGHEST)


@jax.custom_vjp
def cumsum_rows(x):
    return _hdot(_tri(x.shape[0], False), x)


cumsum_rows.defvjp(lambda x: (_hdot(_tri(x.shape[0], False), x), None),
                   lambda _, g: (_hdot(_tri(g.shape[0], True), g),))


def _pick(n, prefs):
    for p in prefs:
        if n % p == 0:
            return p
    return n


def matmul(a, b, mode, name, out_dtype=f32, residual=None, relu2=None):
    if mode == "nn":
        (m, k), n = a.shape, b.shape[1]
    elif mode == "nt":
        (m, k), n = a.shape, b.shape[0]
    else:
        (k, m), n = a.shape, b.shape[1]
    tm = _pick(m, (1024, 512, 256))
    tn = _pick(n, (1024, 512, 256, 128))
    tk = _pick(k, (2048, 1024, 512))
    nk = k // tk
    dims = dict(nn=NN, nt=NT, tn=TN)[mode]
    a_spec = pl.BlockSpec((tk, tm), lambda i, j, l: (l, i)) if mode == "tn" else pl.BlockSpec((tm, tk), lambda i, j, l: (i, l))
    b_spec = pl.BlockSpec((tn, tk), lambda i, j, l: (j, l)) if mode == "nt" else pl.BlockSpec((tk, tn), lambda i, j, l: (l, j))
    o_spec = pl.BlockSpec((tm, tn), lambda i, j, l: (i, j))
    two_outs = isinstance(relu2, str)
    extra = residual if residual is not None else (None if two_outs or relu2 is None else relu2)
    has_extra = extra is not None
    n_out = 2 if two_outs else 1

    def body(*refs):
        a_ref, b_ref = refs[0], refs[1]
        e_ref = refs[2] if has_extra else None
        o_refs = refs[2 + has_extra:2 + has_extra + n_out]
        acc = refs[2 + has_extra + n_out]
        l = pl.program_id(2)

        @pl.when(l == 0)
        def _():
            acc[...] = jnp.zeros(acc.shape, f32)

        acc[...] += _dg(a_ref[...], b_ref[...], dims)

        @pl.when(l == nk - 1)
        def _():
            r = acc[...]
            if residual is not None:
                r = r + e_ref[...]
            elif has_extra:
                r = r * (2.0 * jnp.maximum(e_ref[...], 0.0))
            o_refs[0][...] = r.astype(o_refs[0].dtype)
            if two_outs:
                o_refs[1][...] = jnp.square(jnp.maximum(r, 0.0)).astype(o_refs[1].dtype)

    ins = [a, b] + ([extra] if has_extra else [])
    shapes = [jax.ShapeDtypeStruct((m, n), out_dtype)] + ([jax.ShapeDtypeStruct((m, n), bf16)] if two_outs else [])
    res = pl.pallas_call(
        body, name=name, grid=(m // tm, n // tn, nk),
        in_specs=[a_spec, b_spec] + ([o_spec] if has_extra else []),
        out_specs=[o_spec] * n_out, out_shape=shapes,
        scratch_shapes=[pltpu.VMEM((tm, tn), f32)],
        compiler_params=_params(("parallel", "parallel", "arbitrary")),
    )(*ins)
    return res if two_outs else res[0]


def _tile_in_specs(tiled, params, tile):
    specs = [pl.BlockSpec((tile, cols), functools.partial(lambda i, cb: (i, cb), cb=cb)) for (_, cols, cb) in tiled]
    specs += [pl.BlockSpec(p.shape, lambda i: (0, 0)) for p in params]
    return specs


def tile_fwd(name, f, tiled, params, outs, tile):
    rows = tiled[0][0].shape[0]
    nt, npar = len(tiled), len(params)

    def body(*refs):
        xs = [r[...].astype(f32) for r in refs[:nt + npar]]
        res = f(*xs)
        for o_ref, o in zip(refs[nt + npar:], res):
            o_ref[...] = o.astype(o_ref.dtype)

    return pl.pallas_call(
        body, name=name, grid=(rows // tile,),
        in_specs=_tile_in_specs(tiled, params, tile),
        out_specs=[pl.BlockSpec((tile, c), lambda i: (i, 0)) for c, _ in outs],
        out_shape=[jax.ShapeDtypeStruct((rows, c), dt) for c, dt in outs],
        compiler_params=_params(("parallel",)),
    )(*[t[0] for t in tiled], *params)


def tile_bwd(name, f, tiled, params, cots, d_dtypes, tile, addend=None):
    rows = tiled[0][0].shape[0]
    nt, npar, nc = len(tiled), len(params), len(cots)
    has_add = addend is not None

    def body(*refs):
        i = pl.program_id(0)
        xs = [r[...].astype(f32) for r in refs[:nt + npar]]
        gs = tuple(r[...].astype(f32) for r in refs[nt + npar:nt + npar + nc])
        pos = nt + npar + nc
        add_ref = refs[pos] if has_add else None
        pos += has_add
        dt_refs, dp_refs = refs[pos:pos + nt], refs[pos + nt:pos + nt + npar]
        _, vjp = jax.vjp(f, *xs)
        ds = vjp(gs)
        for k in range(nt):
            d = ds[k]
            if has_add and k == 0:
                d = d + add_ref[...]
            dt_refs[k][...] = d.astype(dt_refs[k].dtype)

        @pl.when(i == 0)
        def _():
            for r in dp_refs:
                r[...] = jnp.zeros(r.shape, f32)

        for k in range(npar):
            dp_refs[k][...] += ds[nt + k]

    in_specs = _tile_in_specs(tiled, params, tile)
    in_specs += [pl.BlockSpec((tile, c.shape[1]), lambda i: (i, 0)) for c in cots]
    ins = [t[0] for t in tiled] + list(params) + list(cots)
    if has_add:
        in_specs.append(pl.BlockSpec((tile, addend.shape[1]), lambda i: (i, 0)))
        ins.append(addend)
    out_specs = [pl.BlockSpec((tile, cols), lambda i: (i, 0)) for (_, cols, _) in tiled]
    out_specs += [pl.BlockSpec(p.shape, lambda i: (0, 0)) for p in params]
    out_shape = [jax.ShapeDtypeStruct((rows, cols), dt) for (_, cols, _), dt in zip(tiled, d_dtypes)]
    out_shape += [jax.ShapeDtypeStruct(p.shape, f32) for p in params]
    res = pl.pallas_call(
        body, name=name, grid=(rows // tile,), in_specs=in_specs, out_specs=out_specs, out_shape=out_shape,
        compiler_params=_params(("arbitrary",)),
    )(*ins)
    return res[:nt], res[nt:]


def f_rms(h, g):
    return (h * lax.rsqrt(jnp.mean(h * h, axis=-1, keepdims=True) + EPS) * g,)


def f_ln_silu(x, g, b):
    mu = jnp.mean(x, axis=-1, keepdims=True)
    xc = x - mu
    y = xc * lax.rsqrt(jnp.mean(xc * xc, axis=-1, keepdims=True) + EPS) * g + b
    return (y * jax.nn.sigmoid(y),)


def f_merge(pa, pb, pc, ya, yb, yc, gb):
    ga = jax.nn.sigmoid(pa + gb[:, 0:D])
    gb_ = jax.nn.sigmoid(pb + gb[:, D:2 * D])
    gc = jax.nn.sigmoid(pc + gb[:, 2 * D:3 * D])
    return (ga * ya + gb_ * yb + gc * yc,)


HALO = 32
CONV_CB = 256
SUB = 128


def _silu_grad(p):
    s = jax.nn.sigmoid(p)
    return s * (1.0 + p * (1.0 - s))


def _fill_shifted(buf, copies, length):
    for k in range(1, 8):
        for r0 in range(0, length - 8, SUB):
            n = min(SUB, length - 8 - r0)
            copies[k - 1, pl.ds(r0, n), :] = buf[pl.ds(r0 + k, n), :]


def _rows(buf, copies, off, n):
    if copies is None or off % 8 == 0:
        return buf[pl.ds(off, n), :]
    return copies[off % 8 - 1, pl.ds(off - off % 8, n), :]


def conv_fwd(name, src, col0, gate_col0, w, b, glu, silu):
    t_len = src.shape[0]
    k_taps, c = w.shape
    tt = min(512, t_len)
    nt_ = t_len // tt
    cb0, gb0 = col0 // CONV_CB, gate_col0 // CONV_CB
    many_taps = k_taps > 8

    def body(*refs):
        pos = 0
        x_cur, x_prev = refs[0], refs[1]
        pos = 2
        if glu:
            g_cur, g_prev = refs[2], refs[3]
            pos = 4
        w_ref, b_ref = refs[pos], refs[pos + 1]
        pre_ref = refs[pos + 2]
        act_ref = refs[pos + 3] if silu else None
        xp = refs[-2] if many_taps else refs[-1]
        xs = refs[-1] if many_taps else None
        t = pl.program_id(1)
        cur = x_cur[...]
        tail = x_prev[...]
        if glu:
            cur = cur * jax.nn.sigmoid(g_cur[...])
            tail = tail * jax.nn.sigmoid(g_prev[...])
        xp[pl.ds(0, HALO), :] = jnp.where(t > 0, tail, 0.0)
        xp[pl.ds(HALO, tt), :] = cur
        if many_taps:
            _fill_shifted(xp, xs, HALO + tt)
        for s in range(tt // SUB):
            acc = jnp.broadcast_to(b_ref[...], (SUB, CONV_CB))
            for j in range(k_taps):
                acc = acc + _rows(xp, xs, s * SUB + HALO - (k_taps - 1) + j, SUB) * w_ref[pl.ds(j, 1), :]
            pre_ref[pl.ds(s * SUB, SUB), :] = acc
            if silu:
                act_ref[pl.ds(s * SUB, SUB), :] = acc * jax.nn.sigmoid(acc)

    blk = lambda off: pl.BlockSpec((tt, CONV_CB), functools.partial(lambda j, t, off: (t, off + j), off=off))
    prev = lambda off: pl.BlockSpec((HALO, CONV_CB), functools.partial(lambda j, t, off: (jnp.maximum(t * (tt // HALO) - 1, 0), off + j), off=off))
    in_specs, ins = [blk(cb0), prev(cb0)], [src, src]
    if glu:
        in_specs += [blk(gb0), prev(gb0)]
        ins += [src, src]
    in_specs += [pl.BlockSpec((k_taps, CONV_CB), lambda j, t: (0, j)), pl.BlockSpec((1, CONV_CB), lambda j, t: (0, j))]
    ins += [w, b]
    o_spec = pl.BlockSpec((tt, CONV_CB), lambda j, t: (t, j))
    n_out = 2 if silu else 1
    res = pl.pallas_call(
        body, name=name, grid=(c // CONV_CB, nt_), in_specs=in_specs, out_specs=[o_spec] * n_out,
        out_shape=[jax.ShapeDtypeStruct((t_len, c), f32)] * n_out,
        scratch_shapes=[pltpu.VMEM((HALO + tt, CONV_CB), f32)] + ([pltpu.VMEM((7, HALO + tt, CONV_CB), f32)] if many_taps else []),
        compiler_params=_params(("parallel", "arbitrary")),
    )(*ins)
    return res


def conv_bwd(name, dy, pre, src, col0, gate_col0, w, glu, silu, out_dtype):
    t_len = src.shape[0]
    k_taps, c = w.shape
    tt = min(512, t_len)
    nt_ = t_len // tt
    cb0, gb0 = col0 // CONV_CB, gate_col0 // CONV_CB
    many_taps = k_taps > 8

    def body(*refs):
        dy_cur, dy_next = refs[0], refs[1]
        pos = 2
        if silu:
            p_cur, p_next = refs[2], refs[3]
            pos = 4
        x_cur, x_prev = refs[pos], refs[pos + 1]
        pos += 2
        if glu:
            g_cur, g_prev = refs[pos], refs[pos + 1]
            pos += 2
        w_ref = refs[pos]
        pos += 1
        n_dx = 2 if glu else 1
        dx_refs = refs[pos:pos + n_dx]
        dw_ref, db_ref = refs[pos + n_dx], refs[pos + n_dx + 1]
        dp, xp = (refs[-4], refs[-3]) if many_taps else (refs[-2], refs[-1])
        ds_, xs = (refs[-2], refs[-1]) if many_taps else (None, None)
        t = pl.program_id(1)

        dcur = dy_cur[...]
        dhead = dy_next[...]
        if silu:
            dcur = dcur * _silu_grad(p_cur[...])
            dhead = dhead * _silu_grad(p_next[...])
        dp[pl.ds(0, tt), :] = dcur
        dp[pl.ds(tt, HALO), :] = jnp.where(t < nt_ - 1, dhead, 0.0)
        cur = x_cur[...]
        tail = x_prev[...]
        if glu:
            cur = cur * jax.nn.sigmoid(g_cur[...])
            tail = tail * jax.nn.sigmoid(g_prev[...])
        xp[pl.ds(0, HALO), :] = jnp.where(t > 0, tail, 0.0)
        xp[pl.ds(HALO, tt), :] = cur

        if many_taps:
            _fill_shifted(dp, ds_, tt + HALO)
            _fill_shifted(xp, xs, HALO + tt)

        @pl.when(t == 0)
        def _():
            dw_ref[...] = jnp.zeros(dw_ref.shape, f32)
            db_ref[...] = jnp.zeros(db_ref.shape, f32)

        for s in range(tt // SUB):
            d_sub = dp[pl.ds(s * SUB, SUB), :]
            acc = jnp.zeros((SUB, CONV_CB), f32)
            for j in range(k_taps):
                acc = acc + _rows(dp, ds_, s * SUB + (k_taps - 1) - j, SUB) * w_ref[pl.ds(j, 1), :]
                x_sub = _rows(xp, xs, s * SUB + HALO - (k_taps - 1) + j, SUB)
                dw_ref[pl.ds(j, 1), :] += jnp.sum(d_sub * x_sub, axis=0, keepdims=True)
            db_ref[...] += jnp.sum(d_sub, axis=0, keepdims=True)
            if glu:
                a = x_cur[pl.ds(s * SUB, SUB), :]
                sg = jax.nn.sigmoid(g_cur[pl.ds(s * SUB, SUB), :])
                dx_refs[0][pl.ds(s * SUB, SUB), :] = (acc * sg).astype(out_dtype)
                dx_refs[1][pl.ds(s * SUB, SUB), :] = (acc * a * sg * (1.0 - sg)).astype(out_dtype)
            else:
                dx_refs[0][pl.ds(s * SUB, SUB), :] = acc.astype(out_dtype)

    blk = lambda off: pl.BlockSpec((tt, CONV_CB), functools.partial(lambda j, t, off: (t, off + j), off=off))
    prev = lambda off: pl.BlockSpec((HALO, CONV_CB), functools.partial(lambda j, t, off: (jnp.maximum(t * (tt // HALO) - 1, 0), off + j), off=off))
    nxt = pl.BlockSpec((HALO, CONV_CB), lambda j, t: (jnp.minimum((t + 1) * (tt // HALO), t_len // HALO - 1), j))
    in_specs, ins = [blk(0), nxt], [dy, dy]
    if silu:
        in_specs += [blk(0), nxt]
        ins += [pre, pre]
    in_specs += [blk(cb0), prev(cb0)]
    ins += [src, src]
    if glu:
        in_specs += [blk(gb0), prev(gb0)]
        ins += [src, src]
    in_specs.append(pl.BlockSpec((k_taps, CONV_CB), lambda j, t: (0, j)))
    ins.append(w)
    n_dx = 2 if glu else 1
    o_spec = pl.BlockSpec((tt, CONV_CB), lambda j, t: (t, j))
    out_specs = [o_spec] * n_dx + [pl.BlockSpec((k_taps, CONV_CB), lambda j, t: (0, j)), pl.BlockSpec((1, CONV_CB), lambda j, t: (0, j))]
    out_shape = [jax.ShapeDtypeStruct((t_len, c), out_dtype)] * n_dx + [jax.ShapeDtypeStruct((k_taps, c), f32), jax.ShapeDtypeStruct((1, c), f32)]
    return pl.pallas_call(
        body, name=name, grid=(c // CONV_CB, nt_), in_specs=in_specs, out_specs=out_specs, out_shape=out_shape,
        scratch_shapes=[pltpu.VMEM((tt + HALO, CONV_CB), f32)] * 2 + ([pltpu.VMEM((7, tt + HALO, CONV_CB), f32)] * 2 if many_taps else []),
        compiler_params=_params(("parallel", "arbitrary")),
    )(*ins)


GH = SSD_HEADS // SSD_GROUPS
GW = GH * SSD_P


def _softplus(x):
    return jnp.maximum(x, 0.0) + jnp.log1p(jnp.exp(-jnp.abs(x)))


def f_ssd(hbase, x, z, bm, cm, dtraw, s_in, dt_bias, a_log, dskip, ng):
    q = x.shape[0]
    dt = _softplus(dtraw + dt_bias)
    da = dt * (-jnp.exp(a_log))
    cs = cumsum_rows(da)
    g_cb = dot_nt(cm, bm)
    lane = lax.broadcasted_iota(jnp.int32, (q, 128), 1)
    lane1 = lax.broadcasted_iota(jnp.int32, (1, 128), 1)
    causal = lax.broadcasted_iota(jnp.int32, (q, q), 0) >= lax.broadcasted_iota(jnp.int32, (q, q), 1)
    last = lax.broadcasted_iota(jnp.int32, (q, GW), 0) == q - 1
    dt_cols, cs_cols, d_cols = [], [], []
    for r in range(GH):
        sel = lane == hbase + r
        dt_cols.append(jnp.sum(jnp.where(sel, dt, 0.0), axis=1, keepdims=True))
        cs_cols.append(jnp.sum(jnp.where(sel, cs, 0.0), axis=1, keepdims=True))
        d_cols.append(jnp.sum(jnp.where(lane1 == hbase + r, dskip, 0.0), axis=1, keepdims=True))
    spread = lambda cols: jnp.concatenate([jnp.broadcast_to(c, (c.shape[0], SSD_P)) for c in cols], axis=1)
    dt_x, cs_x, d_x = spread(dt_cols), spread(cs_cols), spread(d_cols)
    cs_last = jnp.sum(jnp.where(last, cs_x, 0.0), axis=0, keepdims=True)
    xdt = x * dt_x
    y_diag = []
    for r in range(GH):
        m1 = jnp.broadcast_to(cs_cols[r], (q, q))
        decay = jnp.where(causal, jnp.exp(jnp.where(causal, m1 - m1.T, 0.0)), 0.0)
        y_diag.append(dot_nn(g_cb * decay, xdt[:, r * SSD_P:(r + 1) * SSD_P]))
    s_c = dot_tn(bm, xdt * jnp.exp(cs_last - cs_x))
    y_off = dot_nn(cm, s_in) * jnp.exp(cs_x)
    y = jnp.concatenate(y_diag, axis=1) + y_off + x * d_x
    s_out = s_in * jnp.exp(cs_last) + s_c
    y = y * (z * jax.nn.sigmoid(z))
    y = y * lax.rsqrt(jnp.mean(y * y, axis=-1, keepdims=True) + EPS) * ng
    return y, s_out


GPS_FWD, GPS_BWD = 2, 1


def _ssd_in_specs(cmap, gps):
    PW, PN = gps * GW, gps * SSD_N
    return [
        pl.BlockSpec((CHUNK, PW), lambda g, c: (cmap(c), g)),
        pl.BlockSpec((CHUNK, PW), lambda g, c: (cmap(c), C_Z // PW + g)),
        pl.BlockSpec((CHUNK, PN), lambda g, c: (cmap(c), SSD_INNER // PN + g)),
        pl.BlockSpec((CHUNK, PN), lambda g, c: (cmap(c), (SSD_INNER + SSD_GROUPS * SSD_N) // PN + g)),
        pl.BlockSpec((CHUNK, 128), lambda g, c: (cmap(c), C_DT // 128)),
    ]


def _ssd_param_specs(gps):
    return [pl.BlockSpec((1, 128), lambda g, c: (0, 0))] * 3 + [pl.BlockSpec((1, gps * GW), lambda g, c: (0, g))]


def _cols(ref, k, width):
    return ref[:, pl.ds(k * width, width)]


def ssd_fwd(name, xact, proj, dt_bias, a_log, dskip, ng, ex=None):
    t_len = xact.shape[0]
    nc = t_len // CHUNK
    h = _hosted(ex)
    GPS = GPS_FWD
    PW, PN, NPAIR = GPS * GW, GPS * SSD_N, SSD_GROUPS // GPS

    def body(*refs):
        x_ref, z_ref, b_ref, c_ref, dt_ref, p1, p2, p3, p4 = refs[:9]
        y_ref, st_ref = refs[9 + h.n_in:11 + h.n_in]
        s_scr = refs[11 + h.n_in + h.n_out]
        g, c = pl.program_id(0), pl.program_id(1)
        h.at_start(refs, 9, 11 + h.n_in, (g == 0) & (c == 0))

        @pl.when(c == 0)
        def _():
            s_scr[...] = jnp.zeros(s_scr.shape, f32)

        for k in range(GPS):
            s_in = s_scr[k]
            st_ref[k, 0] = s_in
            y, s_out = f_ssd((g * GPS + k) * GH, _cols(x_ref, k, GW), _cols(z_ref, k, GW), _cols(b_ref, k, SSD_N),
                             _cols(c_ref, k, SSD_N), dt_ref[...], s_in, p1[...], p2[...], p3[...], _cols(p4, k, GW))
            y_ref[:, pl.ds(k * GW, GW)] = y.astype(y_ref.dtype)
            s_scr[k] = s_out
        h.at_end(refs, 9, 11 + h.n_in, (g == NPAIR - 1) & (c == nc - 1))

    res = pl.pallas_call(
        body, name=name, grid=(NPAIR, nc),
        in_specs=_ssd_in_specs(lambda c: c, GPS) + _ssd_param_specs(GPS) + h.in_specs,
        out_specs=[pl.BlockSpec((CHUNK, PW), lambda g, c: (c, g)),
                   pl.BlockSpec((GPS, 1, SSD_N, GW), lambda g, c: (g, c, 0, 0))] + h.out_specs,
        out_shape=[jax.ShapeDtypeStruct((t_len, SSD_INNER), bf16),
                   jax.ShapeDtypeStruct((SSD_GROUPS, nc, SSD_N, GW), f32)] + h.out_shape,
        scratch_shapes=[pltpu.VMEM((GPS, SSD_N, GW), f32)] + h.scratch,
        input_output_aliases=h.aliases(9, 2),
        compiler_params=_params(("arbitrary", "arbitrary")),
    )(xact, proj, xact, xact, proj, dt_bias, a_log, dskip, ng, *h.ins)
    return res[0], res[1], list(res[2:])


def ssd_bwd(name, xact, proj, states, dy, dt_bias, a_log, dskip, ng, ex=None):
    t_len = xact.shape[0]
    nc = t_len // CHUNK
    rev = lambda c: nc - 1 - c
    h = _hosted(ex)
    GPS = GPS_BWD
    PW, PN, NPAIR = GPS * GW, GPS * SSD_N, SSD_GROUPS // GPS

    def body(*refs):
        x_ref, z_ref, b_ref, c_ref, dt_ref, st_ref, dy_ref, p1, p2, p3, p4 = refs[:11]
        dx_ref, db_ref, dc_ref, dz_ref, ddt_ref, d1, d2, d3, d4 = refs[11 + h.n_in:20 + h.n_in]
        ds_scr = refs[20 + h.n_in + h.n_out]
        g, c = pl.program_id(0), pl.program_id(1)
        h.at_start(refs, 11, 20 + h.n_in, (g == 0) & (c == 0))

        @pl.when(c == 0)
        def _():
            ds_scr[...] = jnp.zeros(ds_scr.shape, f32)
            d4[...] = jnp.zeros(d4.shape, f32)

        @pl.when((c == 0) & (g == 0))
        def _():
            for r in (d1, d2, d3):
                r[...] = jnp.zeros(r.shape, f32)

        for k in range(GPS):
            fn = functools.partial(f_ssd, (g * GPS + k) * GH)
            _, vjp = jax.vjp(fn, _cols(x_ref, k, GW), _cols(z_ref, k, GW), _cols(b_ref, k, SSD_N), _cols(c_ref, k, SSD_N),
                             dt_ref[...], st_ref[k, 0], p1[...], p2[...], p3[...], _cols(p4, k, GW))
            dx, dz, db, dc, ddt, ds_in, e1, e2, e3, e4 = vjp((_cols(dy_ref, k, GW).astype(f32), ds_scr[k]))
            dx_ref[:, pl.ds(k * GW, GW)] = dx
            dz_ref[:, pl.ds(k * GW, GW)] = dz.astype(dz_ref.dtype)
            db_ref[:, pl.ds(k * SSD_N, SSD_N)] = db
            dc_ref[:, pl.ds(k * SSD_N, SSD_N)] = dc
            ddt_ref[k] = ddt
            ds_scr[k] = ds_in
            d1[...] += e1
            d2[...] += e2
            d3[...] += e3
            d4[:, pl.ds(k * GW, GW)] += e4
        h.at_end(refs, 11, 20 + h.n_in, (g == NPAIR - 1) & (c == nc - 1))

    res = pl.pallas_call(
        body, name=name, grid=(NPAIR, nc),
        in_specs=_ssd_in_specs(rev, GPS) + [
            pl.BlockSpec((GPS, 1, SSD_N, GW), lambda g, c: (g, rev(c), 0, 0)),
            pl.BlockSpec((CHUNK, PW), lambda g, c: (rev(c), g)),
        ] + _ssd_param_specs(GPS) + h.in_specs,
        out_specs=[
            pl.BlockSpec((CHUNK, PW), lambda g, c: (rev(c), g)),
            pl.BlockSpec((CHUNK, PN), lambda g, c: (rev(c), g)),
            pl.BlockSpec((CHUNK, PN), lambda g, c: (rev(c), g)),
            pl.BlockSpec((CHUNK, PW), lambda g, c: (rev(c), g)),
            pl.BlockSpec((GPS, CHUNK, 128), lambda g, c: (g, rev(c), 0)),
        ] + _ssd_param_specs(GPS) + h.out_specs,
        out_shape=[
            jax.ShapeDtypeStruct((t_len, SSD_INNER), f32),
            jax.ShapeDtypeStruct((t_len, SSD_GROUPS * SSD_N), f32),
            jax.ShapeDtypeStruct((t_len, SSD_GROUPS * SSD_N), f32),
            jax.ShapeDtypeStruct((t_len, SSD_INNER), bf16),
            jax.ShapeDtypeStruct((SSD_GROUPS, t_len, 128), f32),
            jax.ShapeDtypeStruct((1, 128), f32), jax.ShapeDtypeStruct((1, 128), f32), jax.ShapeDtypeStruct((1, 128), f32),
            jax.ShapeDtypeStruct((1, SSD_INNER), f32),
        ] + h.out_shape,
        scratch_shapes=[pltpu.VMEM((GPS, SSD_N, GW), f32)] + h.scratch,
        input_output_aliases=h.aliases(11, 9),
        compiler_params=_params(("arbitrary", "arbitrary")),
    )(xact, proj, xact, xact, proj, states, dy, dt_bias, a_log, dskip, ng, *h.ins)
    return list(res[:9]), list(res[9:])


def _head_norm(t, g):
    return t * lax.rsqrt(jnp.mean(t * t, axis=-1, keepdims=True) + EPS) * g


def f_swa(has_prev, q, kp, kc, vp, vc, qg, kg, sinks, *bias):
    rep = ATTN_HEADS // ATTN_KV
    qi = lax.broadcasted_iota(jnp.int32, (rep * BLK, BLK), 0) & (BLK - 1)
    ki = lax.broadcasted_iota(jnp.int32, (rep * BLK, BLK), 1)
    mask_p = (ki > qi) & has_prev
    mask_c = ki <= qi
    lane1 = lax.broadcasted_iota(jnp.int32, (1, 128), 1)
    scale = ATTN_D ** -0.5
    outs = []
    for g in range(ATTN_KV):
        sl = slice(g * ATTN_D, (g + 1) * ATTN_D)
        kpn, kcn = _head_norm(kp[:, sl], kg), _head_norm(kc[:, sl], kg)
        heads = range(g * rep, (g + 1) * rep)
        qn = _head_norm(jnp.concatenate([q[:, h * ATTN_D:(h + 1) * ATTN_D] for h in heads], axis=0), qg)
        lp = jnp.where(mask_p, dot_nt(qn, kpn) * scale + bias[g][:, :BLK], NEG_INF)
        lc = jnp.where(mask_c, dot_nt(qn, kcn) * scale + bias[g][:, BLK:], NEG_INF)
        sink = jnp.concatenate([jnp.broadcast_to(jnp.sum(jnp.where(lane1 == h, sinks, 0.0), axis=1, keepdims=True), (BLK, 1))
                                for h in heads], axis=0)
        m = lax.stop_gradient(jnp.maximum(jnp.maximum(jnp.max(lp, axis=-1, keepdims=True), jnp.max(lc, axis=-1, keepdims=True)), sink))
        pp, pc = jnp.exp(lp - m), jnp.exp(lc - m)
        den = jnp.sum(pp, axis=-1, keepdims=True) + jnp.sum(pc, axis=-1, keepdims=True) + jnp.exp(sink - m)
        o = (dot_nn(pp, vp[:, sl]) + dot_nn(pc, vc[:, sl])) * (1.0 / den)
        outs += [o[r * BLK:(r + 1) * BLK] for r in range(rep)]
    return jnp.concatenate(outs, axis=1)


_SWA_BIAS_SHAPE = (ATTN_KV, ATTN_HEADS // ATTN_KV * BLK, 2 * BLK)
_SWA_BIAS_SPEC = pl.BlockSpec(_SWA_BIAS_SHAPE, lambda i: (0, 0, 0))


def _swa_in_specs():
    prev = lambda i: jnp.maximum(i - 1, 0)
    kw = ATTN_KV * ATTN_D
    return [
        pl.BlockSpec((BLK, D), lambda i: (i, C_Q // D)),
        pl.BlockSpec((BLK, kw), lambda i: (prev(i), C_K // kw)),
        pl.BlockSpec((BLK, kw), lambda i: (i, C_K // kw)),
        pl.BlockSpec((BLK, kw), lambda i: (prev(i), C_V // kw)),
        pl.BlockSpec((BLK, kw), lambda i: (i, C_V // kw)),
        pl.BlockSpec((1, ATTN_D), lambda i: (0, 0)),
        pl.BlockSpec((1, ATTN_D), lambda i: (0, 0)),
        pl.BlockSpec((1, 128), lambda i: (0, 0)),
        _SWA_BIAS_SPEC,
    ]


def swa_fwd(name, proj, qg, kg, sinks, bias, ex=None):
    t_len = proj.shape[0]
    nb = t_len // BLK
    h = _hosted(ex)

    def body(*refs):
        q_ref, kp_ref, kc_ref, vp_ref, vc_ref, qg_ref, kg_ref, s_ref, b_ref = refs[:9]
        o_ref = refs[9 + h.n_in]
        i = pl.program_id(0)
        h.at_start(refs, 9, 10 + h.n_in, i == 0)
        o = f_swa(i > 0, q_ref[...], kp_ref[...], kc_ref[...], vp_ref[...], vc_ref[...], qg_ref[...], kg_ref[...],
                  s_ref[...], *[b_ref[kv] for kv in range(ATTN_KV)])
        o_ref[...] = o.astype(o_ref.dtype)
        h.at_end(refs, 9, 10 + h.n_in, i == nb - 1)

    res = pl.pallas_call(
        body, name=name, grid=(nb,), in_specs=_swa_in_specs() + h.in_specs,
        out_specs=[pl.BlockSpec((BLK, D), lambda i: (i, 0))] + h.out_specs,
        out_shape=[jax.ShapeDtypeStruct((t_len, D), bf16)] + h.out_shape,
        scratch_shapes=h.scratch, input_output_aliases=h.aliases(9, 1),
        compiler_params=_params(("arbitrary",)),
    )(proj, proj, proj, proj, proj, qg, kg, sinks, bias.reshape(_SWA_BIAS_SHAPE), *h.ins)
    return res[0], list(res[1:])


def swa_bwd(name, proj, do, qg, kg, sinks, bias, ex=None):
    t_len = proj.shape[0]
    nb = t_len // BLK
    kw = ATTN_KV * ATTN_D
    h = _hosted(ex)

    def body(*refs):
        q_ref, kp_ref, kc_ref, vp_ref, vc_ref, qg_ref, kg_ref, s_ref, b_ref, do_ref = refs[:10]
        dq_ref, dkp_ref, dkc_ref, dvp_ref, dvc_ref, dqg_ref, dkg_ref, ds_ref, db_ref = refs[10 + h.n_in:19 + h.n_in]
        i = pl.program_id(0)
        h.at_start(refs, 10, 19 + h.n_in, i == 0)

        @pl.when(i == 0)
        def _():
            for r in (dqg_ref, dkg_ref, ds_ref, db_ref):
                r[...] = jnp.zeros(r.shape, f32)

        fn = functools.partial(f_swa, i > 0)
        _, vjp = jax.vjp(fn, q_ref[...], kp_ref[...], kc_ref[...], vp_ref[...], vc_ref[...], qg_ref[...], kg_ref[...],
                         s_ref[...], *[b_ref[kv] for kv in range(ATTN_KV)])
        ds = vjp(do_ref[...].astype(f32))
        dq_ref[...] = ds[0].astype(dq_ref.dtype)
        dkp_ref[...] = ds[1]
        dkc_ref[...] = ds[2]
        dvp_ref[...] = ds[3]
        dvc_ref[...] = ds[4]
        dqg_ref[...] += ds[5]
        dkg_ref[...] += ds[6]
        ds_ref[...] += ds[7]
        for kv in range(ATTN_KV):
            db_ref[kv] += ds[8 + kv]
        h.at_end(refs, 10, 19 + h.n_in, i == nb - 1)

    row = lambda w: pl.BlockSpec((BLK, w), lambda i: (i, 0))
    res = pl.pallas_call(
        body, name=name, grid=(nb,), in_specs=_swa_in_specs() + [row(D)] + h.in_specs,
        out_specs=[row(D), row(kw), row(kw), row(kw), row(kw),
                   pl.BlockSpec((1, ATTN_D), lambda i: (0, 0)), pl.BlockSpec((1, ATTN_D), lambda i: (0, 0)),
                   pl.BlockSpec((1, 128), lambda i: (0, 0)), _SWA_BIAS_SPEC] + h.out_specs,
        out_shape=[jax.ShapeDtypeStruct((t_len, D), bf16)] + [jax.ShapeDtypeStruct((t_len, kw), f32)] * 4
        + [jax.ShapeDtypeStruct((1, ATTN_D), f32)] * 2 + [jax.ShapeDtypeStruct((1, 128), f32),
                                                          jax.ShapeDtypeStruct(_SWA_BIAS_SHAPE, f32)] + h.out_shape,
        scratch_shapes=h.scratch, input_output_aliases=h.aliases(10, 9),
        compiler_params=_params(("arbitrary",)),
    )(proj, proj, proj, proj, proj, qg, kg, sinks, bias.reshape(_SWA_BIAS_SHAPE), do, *h.ins)
    return list(res[:8]) + [res[8].reshape(ATTN_HEADS, BLK, 2 * BLK)], list(res[9:])


def _bucket_table():
    qi = np.arange(BLK)[:, None] + BLK
    kj = np.arange(2 * BLK)[None, :]
    dist = qi - kj
    max_exact = REL_BUCKETS // 2
    d = np.maximum(dist, 1).astype(np.float32)
    large = max_exact + (np.log(d / max_exact) / math.log(128 / max_exact) * (REL_BUCKETS - max_exact)).astype(np.int32)
    large = np.minimum(large, REL_BUCKETS - 1)
    return np.where(dist < max_exact, np.maximum(dist, 0), large).astype(np.int32)


def relbias_fwd(table, bucket):
    def body(t_ref, bk_ref, o_ref):
        bk = bk_ref[...]
        for h in range(ATTN_HEADS):
            acc = jnp.zeros((BLK, 2 * BLK), f32)
            for b in range(REL_BUCKETS):
                acc = jnp.where(bk == b, t_ref[b, h], acc)
            o_ref[h] = acc

    return pl.pallas_call(
        body, name="relbias_fwd", out_shape=jax.ShapeDtypeStruct((ATTN_HEADS, BLK, 2 * BLK), f32),
        in_specs=[pl.BlockSpec(memory_space=pltpu.SMEM), pl.BlockSpec(memory_space=pltpu.VMEM)],
        out_specs=pl.BlockSpec(memory_space=pltpu.VMEM),
    )(table, bucket)


def relbias_bwd(dbias, bucket):
    n = len(dbias)

    def body(*refs):
        bk = refs[n][...]
        o_ref = refs[n + 1]
        row = lax.broadcasted_iota(jnp.int32, (REL_BUCKETS, 128), 0)
        lane = lax.broadcasted_iota(jnp.int32, (REL_BUCKETS, 128), 1)
        res = jnp.zeros((REL_BUCKETS, 128), f32)
        for h in range(ATTN_HEADS):
            tot = refs[0][h]
            for k in range(1, n):
                tot = tot + refs[k][h]
            for b in range(REL_BUCKETS):
                part = jnp.sum(jnp.sum(jnp.where(bk == b, tot, 0.0), axis=1, keepdims=True), axis=0, keepdims=True)
                res = jnp.where((row == b) & (lane == h), part, res)
        o_ref[...] = res

    return pl.pallas_call(
        body, name="relbias_bwd", out_shape=jax.ShapeDtypeStruct((REL_BUCKETS, 128), f32),
        in_specs=[pl.BlockSpec(memory_space=pltpu.VMEM)] * (n + 1),
        out_specs=pl.BlockSpec(memory_space=pltpu.VMEM),
    )(*dbias, bucket)[:, :ATTN_HEADS]


XA_TILE = 512


def f_xattn(q, k, v, qg, kg):
    qn, kn = _head_norm(q, qg), _head_norm(k, kg)
    logits = dot_nt(qn, kn) * (XD ** -0.5)
    p = jnp.exp(logits - lax.stop_gradient(jnp.max(logits, axis=-1, keepdims=True)))
    return dot_nn(p * (1.0 / jnp.sum(p, axis=-1, keepdims=True)), v)


def _xa_in_specs(tq):
    return [
        pl.BlockSpec((tq, XD), lambda h, i: (i, h)),
        pl.BlockSpec((MEM_LEN, XD), lambda h, i: (0, h)),
        pl.BlockSpec((MEM_LEN, XD), lambda h, i: (0, XH + h)),
        pl.BlockSpec((1, XD), lambda h, i: (0, 0)),
        pl.BlockSpec((1, XD), lambda h, i: (0, 0)),
    ]


def xa_fwd(name, xq, kv, qg, kg):
    t_len = xq.shape[0]
    tq = min(XA_TILE, t_len)

    def body(q_ref, k_ref, v_ref, qg_ref, kg_ref, o_ref):
        o_ref[...] = f_xattn(q_ref[...], k_ref[...], v_ref[...], qg_ref[...], kg_ref[...]).astype(o_ref.dtype)

    return pl.pallas_call(
        body, name=name, grid=(XH, t_len // tq), in_specs=_xa_in_specs(tq),
        out_specs=pl.BlockSpec((tq, XD), lambda h, i: (i, h)), out_shape=jax.ShapeDtypeStruct((t_len, D), bf16),
        compiler_params=_params(("parallel", "parallel")),
    )(xq, kv, kv, qg, kg)


def xa_bwd(name, xq, kv, do, qg, kg):
    t_len = xq.shape[0]
    tq = min(XA_TILE, t_len)

    def body(q_ref, k_ref, v_ref, qg_ref, kg_ref, do_ref, dq_ref, dk_ref, dv_ref, dqg_ref, dkg_ref):
        h, i = pl.program_id(0), pl.program_id(1)

        @pl.when(i == 0)
        def _():
            dk_ref[...] = jnp.zeros(dk_ref.shape, f32)
            dv_ref[...] = jnp.zeros(dv_ref.shape, f32)

        @pl.when((i == 0) & (h == 0))
        def _():
            dqg_ref[...] = jnp.zeros(dqg_ref.shape, f32)
            dkg_ref[...] = jnp.zeros(dkg_ref.shape, f32)

        _, vjp = jax.vjp(f_xattn, q_ref[...], k_ref[...], v_ref[...], qg_ref[...], kg_ref[...])
        dq, dk, dv, dqg, dkg = vjp(do_ref[...].astype(f32))
        dq_ref[...] = dq.astype(dq_ref.dtype)
        dk_ref[...] += dk
        dv_ref[...] += dv
        dqg_ref[...] += dqg
        dkg_ref[...] += dkg

    return pl.pallas_call(
        body, name=name, grid=(XH, t_len // tq),
        in_specs=_xa_in_specs(tq) + [pl.BlockSpec((tq, XD), lambda h, i: (i, h))],
        out_specs=[pl.BlockSpec((tq, XD), lambda h, i: (i, h)),
                   pl.BlockSpec((MEM_LEN, XD), lambda h, i: (0, h)), pl.BlockSpec((MEM_LEN, XD), lambda h, i: (0, h)),
                   pl.BlockSpec((1, XD), lambda h, i: (0, 0)), pl.BlockSpec((1, XD), lambda h, i: (0, 0))],
        out_shape=[jax.ShapeDtypeStruct((t_len, D), bf16), jax.ShapeDtypeStruct((MEM_LEN, D), f32),
                   jax.ShapeDtypeStruct((MEM_LEN, D), f32), jax.ShapeDtypeStruct((1, XD), f32), jax.ShapeDtypeStruct((1, XD), f32)],
        compiler_params=_params(("arbitrary", "arbitrary")),
    )(xq, kv, kv, qg, kg, do)


def loss_head(y, target, tile):
    t_len = y.shape[0]

    def body(y_ref, t_ref, dy_ref, l_ref):
        i = pl.program_id(0)

        @pl.when(i == 0)
        def _():
            l_ref[...] = jnp.zeros(l_ref.shape, f32)

        err = y_ref[...] - t_ref[...]
        dy_ref[...] = err * (1.0 / D)
        l_ref[...] += 0.5 * jnp.sum(jnp.sum(err * err, axis=1, keepdims=True), axis=0, keepdims=True) * (1.0 / D)

    row = pl.BlockSpec((tile, D), lambda i: (i, 0))
    return pl.pallas_call(
        body, name="loss_head", grid=(t_len // tile,), in_specs=[row, row],
        out_specs=[row, pl.BlockSpec((8, 128), lambda i: (0, 0))],
        out_shape=[jax.ShapeDtypeStruct((t_len, D), f32), jax.ShapeDtypeStruct((8, 128), f32)],
        compiler_params=_params(("arbitrary",)),
    )(y, target)


ADAM_BLOCK_ELEMS = 512 * 1024


def adamw(name, w, g, m, v):
    rows, cols = w.shape
    tile = rows
    if rows * cols > ADAM_BLOCK_ELEMS:
        tile = _pick(rows, [t for t in (512, 256, 128, 64, 32, 16, 8) if t * cols <= ADAM_BLOCK_ELEMS])

    def body(w_ref, g_ref, m_ref, v_ref, d_ref, nm_ref, nv_ref):
        gg = g_ref[...]
        nm = ADAM_B1 * m_ref[...] + (1.0 - ADAM_B1) * gg
        nv = ADAM_B2 * v_ref[...] + (1.0 - ADAM_B2) * jnp.square(gg)
        m_hat = nm / (1.0 - ADAM_B1 ** ADAM_STEP)
        v_hat = nv / (1.0 - ADAM_B2 ** ADAM_STEP)
        d_ref[...] = -ADAM_LR * (m_hat / (jnp.sqrt(v_hat) + ADAM_EPS) + ADAM_WD * w_ref[...])
        nm_ref[...] = nm
        nv_ref[...] = nv

    spec = pl.BlockSpec((tile, cols), lambda i: (i, 0))
    return pl.pallas_call(
        body, name=name, grid=(rows // tile,), in_specs=[spec] * 4, out_specs=[spec] * 3,
        out_shape=[jax.ShapeDtypeStruct((rows, cols), f32)] * 3, compiler_params=_params(("parallel",)),
    )(w, g, m, v)


def adamw_layers(name, w, m, v, gsrcs, r0):
    depth, rows, cols = w.shape
    tile = max(t for t in range(8, rows + 1, 8) if rows % t == 0 and r0 % t == 0 and t * cols <= ADAM_BLOCK_ELEMS)

    def body(*refs):
        w_ref, m_ref, v_ref = refs[:3]
        g_refs = refs[3:3 + depth]
        go_ref, d_ref, nm_ref, nv_ref = refs[3 + depth:]
        layer = pl.program_id(0)
        gg = g_refs[0][...]
        for k in range(1, depth):
            gg = jnp.where(layer == k, g_refs[k][...], gg)
        nm = ADAM_B1 * m_ref[0] + (1.0 - ADAM_B1) * gg
        nv = ADAM_B2 * v_ref[0] + (1.0 - ADAM_B2) * jnp.square(gg)
        m_hat = nm / (1.0 - ADAM_B1 ** ADAM_STEP)
        v_hat = nv / (1.0 - ADAM_B2 ** ADAM_STEP)
        go_ref[0] = gg
        d_ref[0] = -ADAM_LR * (m_hat / (jnp.sqrt(v_hat) + ADAM_EPS) + ADAM_WD * w_ref[0])
        nm_ref[0] = nm
        nv_ref[0] = nv

    spec = pl.BlockSpec((1, tile, cols), lambda l, i: (l, i, 0))
    g_specs = [pl.BlockSpec((tile, cols), functools.partial(lambda l, i, k: (jnp.where(l == k, r0 // tile + i, 0), 0), k=k))
               for k in range(depth)]
    return pl.pallas_call(
        body, name=name, grid=(depth, rows // tile), in_specs=[spec] * 3 + g_specs, out_specs=[spec] * 4,
        out_shape=[jax.ShapeDtypeStruct(w.shape, f32)] * 4, compiler_params=_params(("parallel", "parallel")),
    )(w, m, v, *gsrcs)


MESH = pl.DeviceIdType.MESH
HBM_SPEC = pl.BlockSpec(memory_space=pltpu.HBM)


def _place():
    x, y, c = lax.axis_index("x"), lax.axis_index("y"), lax.axis_index("c")
    chips = [(1 - x, y), (x, 1 - y), (1 - x, 1 - y)]
    return x, y, c, chips


def _rcopy(src, dst, send_sems, recv_sems, k, to):
    return pltpu.make_async_remote_copy(src_ref=src, dst_ref=dst, send_sem=send_sems.at[k], recv_sem=recv_sems.at[k],
                                        device_id=to, device_id_type=MESH)


class Exchange:
    def __init__(self, ins, outs, n_sems, n_local, start, wait, aliases=None):
        self.ins, self.outs, self.n_sems, self.n_local = list(ins), list(outs), n_sems, n_local
        self.start, self.wait, self.aliases = start, wait, dict(aliases or {})


class _hosted:
    def __init__(self, ex):
        self.ex = ex
        self.ins = list(ex.ins) if ex else []
        self.out_shape = list(ex.outs) if ex else []
        self.n_in, self.n_out = len(self.ins), len(self.out_shape)
        self.in_specs, self.out_specs = [HBM_SPEC] * self.n_in, [HBM_SPEC] * self.n_out
        self.scratch = [pltpu.SemaphoreType.DMA((ex.n_sems,)), pltpu.SemaphoreType.DMA((ex.n_sems,)),
                        pltpu.SemaphoreType.DMA((max(ex.n_local, 1),))] if ex else []

    def aliases(self, first_in, first_out):
        return {first_in + a: first_out + b for a, b in self.ex.aliases.items()} if self.ex else {}

    def _args(self, refs, i0, o0):
        return refs[i0:i0 + self.n_in], refs[o0:o0 + self.n_out], refs[-3], refs[-2], refs[-1]

    def at_start(self, refs, i0, o0, pred):
        if self.ex is not None:
            pl.when(pred)(lambda: self.ex.start(*self._args(refs, i0, o0)))

    def at_end(self, refs, i0, o0, pred):
        if self.ex is not None:
            pl.when(pred)(lambda: self.ex.wait(*self._args(refs, i0, o0)))


def run_exchange(name, ex):
    h = _hosted(ex)

    def body(*refs):
        args = h._args(refs, 0, h.n_in)
        ex.start(*args)
        ex.wait(*args)

    res = pl.pallas_call(
        body, name=name, out_shape=h.out_shape, in_specs=h.in_specs, out_specs=h.out_specs, scratch_shapes=h.scratch,
        input_output_aliases=h.aliases(0, 0), compiler_params=pltpu.CompilerParams(has_side_effects=True),
    )(*h.ins)
    return list(res)


def _halves(ref, axis, c):
    hh = ref.shape[axis] // 2
    return pl.ds(pl.multiple_of(c * hh, 16), hh), pl.ds(pl.multiple_of((1 - c) * hh, 16), hh)


def ex_gather_ici(shards):
    n = len(shards)

    def copies(ins, outs, ssem, rsem, landing):
        x, y, c, chips = _place()
        me = 2 * x + y
        res = []
        for a, (s, o) in enumerate(zip(ins, outs)):
            mine, _ = _halves(s, 0, c)
            for j, (cx, cy) in enumerate(chips[:2]):
                slot = 2 * cx + cy if landing else me
                res.append(_rcopy(s.at[mine], o.at[slot, mine], ssem, rsem, 3 * a + j, (cx, cy, c)))
            res.append(_rcopy(s, o.at[me], ssem, rsem, 3 * a + 2, (x, y, 1 - c)))
        return res

    def start(ins, outs, ssem, rsem, lsem):
        for cp in copies(ins, outs, ssem, rsem, False):
            cp.start()

    def wait(ins, outs, ssem, rsem, lsem):
        for cp in copies(ins, outs, ssem, rsem, True):
            cp.wait_recv()
        for cp in copies(ins, outs, ssem, rsem, False):
            cp.wait_send()

    outs = [jax.ShapeDtypeStruct((N_CHIPS,) + s.shape, s.dtype) for s in shards]
    return Exchange(shards, outs, 3 * n, 0, start, wait)


def ex_gather_d2d(gathered):
    n = len(gathered)

    def copies(outs, ssem, rsem, landing):
        x, y, c, chips = _place()
        res = []
        for a, o in enumerate(outs):
            mine, theirs = _halves(o, 1, c)
            for j, (cx, cy) in enumerate(chips[:2]):
                rows = o.at[2 * cx + cy, theirs if landing else mine]
                res.append(_rcopy(rows, rows, ssem, rsem, 3 * a + j, (x, y, 1 - c)))
            fx, fy = lax.rem(x + 1 - c, 2), lax.rem(y + c, 2)
            tx, ty = lax.rem(x + c, 2), lax.rem(y + 1 - c, 2)
            slot = 2 * (1 - x) + (1 - y) if landing else 2 * fx + fy
            rows = o.at[slot, mine]
            res.append(_rcopy(rows, rows, ssem, rsem, 3 * a + 2, (tx, ty, c)))
        return res

    def start(ins, outs, ssem, rsem, lsem):
        for cp in copies(outs, ssem, rsem, False):
            cp.start()

    def wait(ins, outs, ssem, rsem, lsem):
        for cp in copies(outs, ssem, rsem, True):
            cp.wait_recv()
        for cp in copies(outs, ssem, rsem, False):
            cp.wait_send()

    outs = [jax.ShapeDtypeStruct(g.shape, g.dtype) for g in gathered]
    return Exchange(gathered, outs, 3 * n, 0, start, wait, aliases={a: a for a in range(n)})


def ex_gather_diag(gathered):
    n = len(gathered)

    def copies(outs, ssem, rsem, landing):
        x, y, c, _ = _place()
        res = []
        for a, o in enumerate(outs):
            mine, theirs = _halves(o, 1, c)
            rows = o.at[2 * (1 - x) + (1 - y), theirs if landing else mine]
            res.append(_rcopy(rows, rows, ssem, rsem, a, (x, y, 1 - c)))
        return res

    def start(ins, outs, ssem, rsem, lsem):
        for cp in copies(outs, ssem, rsem, False):
            cp.start()

    def wait(ins, outs, ssem, rsem, lsem):
        for cp in copies(outs, ssem, rsem, True):
            cp.wait_recv()
        for cp in copies(outs, ssem, rsem, False):
            cp.wait_send()

    outs = [jax.ShapeDtypeStruct(g.shape, g.dtype) for g in gathered]
    return Exchange(gathered, outs, n, 0, start, wait, aliases={a: a for a in range(n)})


def ex_swap_halves(gs):
    n = len(gs)

    def copies(ins, outs, ssem, rsem):
        x, y, c, _ = _place()
        return [_rcopy(g.at[s, 1 - c], o.at[s], ssem, rsem, N_CHIPS * a + s, (x, y, 1 - c))
                for a, (g, o) in enumerate(zip(ins, outs)) for s in range(N_CHIPS)]

    def start(ins, outs, ssem, rsem, lsem):
        for cp in copies(ins, outs, ssem, rsem):
            cp.start()

    def wait(ins, outs, ssem, rsem, lsem):
        for cp in copies(ins, outs, ssem, rsem):
            cp.wait()

    outs = [jax.ShapeDtypeStruct((N_CHIPS,) + g.shape[2:], g.dtype) for g in gs]
    return Exchange(gs, outs, N_CHIPS * n, 0, start, wait)


def ex_scatter(ps):
    n = len(ps)

    def copies(ins, outs, ssem, rsem):
        x, y, c, chips = _place()
        return [_rcopy(p.at[2 * cx + cy], o.at[j], ssem, rsem, 3 * a + j, (cx, cy, c))
                for a, (p, o) in enumerate(zip(ins, outs)) for j, (cx, cy) in enumerate(chips)]

    def start(ins, outs, ssem, rsem, lsem):
        for cp in copies(ins, outs, ssem, rsem):
            cp.start()

    def wait(ins, outs, ssem, rsem, lsem):
        for cp in copies(ins, outs, ssem, rsem):
            cp.wait()

    outs = [jax.ShapeDtypeStruct((3,) + p.shape[1:], p.dtype) for p in ps]
    return Exchange(ps, outs, 3 * n, 0, start, wait)


def ex_share(rs):
    n = len(rs)

    def copies(ins, outs, ssem, rsem):
        x, y, c, _ = _place()
        return [_rcopy(r, o, ssem, rsem, a, (x, y, 1 - c)) for a, (r, o) in enumerate(zip(ins, outs))]

    def start(ins, outs, ssem, rsem, lsem):
        for cp in copies(ins, outs, ssem, rsem):
            cp.start()

    def wait(ins, outs, ssem, rsem, lsem):
        for cp in copies(ins, outs, ssem, rsem):
            cp.wait()

    outs = [jax.ShapeDtypeStruct(r.shape, r.dtype) for r in rs]
    return Exchange(rs, outs, n, 0, start, wait)


ADD_BLOCK_ELEMS = 384 * 1024


def _add_tile(rows, cols):
    return _pick(rows, [t for t in (1920, 960, 512, 384, 256, 128, 64, 32, 16) if t * cols <= ADD_BLOCK_ELEMS])


def add_halves(name, g, got, c_arr):
    _, _, rows, cols = g.shape
    tile = _add_tile(rows, cols)

    def body(c_ref, g_ref, r_ref, o_ref):
        o_ref[...] = (g_ref[0] + r_ref[...]).astype(o_ref.dtype)

    return pl.pallas_call(
        body, name=name,
        grid_spec=pltpu.PrefetchScalarGridSpec(
            num_scalar_prefetch=1, grid=(N_CHIPS, rows // tile),
            in_specs=[pl.BlockSpec((1, 1, tile, cols), lambda s, i, c_ref: (s, c_ref[0], i, 0)),
                      pl.BlockSpec((1, tile, cols), lambda s, i, c_ref: (s, i, 0))],
            out_specs=pl.BlockSpec((1, tile, cols), lambda s, i, c_ref: (s, i, 0))),
        out_shape=jax.ShapeDtypeStruct(got.shape, bf16), compiler_params=_params(("parallel", "parallel")),
    )(c_arr, g, got)


def add_partials(name, p, got, me_arr):
    _, rows, cols = p.shape
    tile = _add_tile(rows, cols)

    def body(me_ref, p_ref, r_ref, o_ref):
        o_ref[...] = ((p_ref[0].astype(f32) + r_ref[0].astype(f32)) + r_ref[1].astype(f32)) + r_ref[2].astype(f32)

    return pl.pallas_call(
        body, name=name,
        grid_spec=pltpu.PrefetchScalarGridSpec(
            num_scalar_prefetch=1, grid=(rows // tile,),
            in_specs=[pl.BlockSpec((1, tile, cols), lambda i, me_ref: (me_ref[0], i, 0)),
                      pl.BlockSpec((3, tile, cols), lambda i, me_ref: (0, i, 0))],
            out_specs=pl.BlockSpec((tile, cols), lambda i, me_ref: (i, 0))),
        out_shape=jax.ShapeDtypeStruct((rows, cols), f32), compiler_params=_params(("parallel",)),
    )(me_arr, p, got)


def join_halves(name, mine, theirs):
    rows, cols = mine.shape
    tile = _add_tile(rows, cols)

    def body(m_ref, t_ref, o_ref):
        south = lax.axis_index("c") == 0
        o_ref[0] = jnp.where(south, m_ref[...], t_ref[...])
        o_ref[1] = jnp.where(south, t_ref[...], m_ref[...])

    spec = pl.BlockSpec((tile, cols), lambda i: (i, 0))
    return pl.pallas_call(
        body, name=name, grid=(rows // tile,), in_specs=[spec, spec],
        out_specs=pl.BlockSpec((2, tile, cols), lambda i: (0, i, 0)),
        out_shape=jax.ShapeDtypeStruct((2, rows, cols), f32), compiler_params=_params(("parallel",)),
    )(mine, theirs)


N_DEV = 8


def allreduce_small(name, buf):
    m_per = buf.shape[0]

    def body(x_ref, out_ref, all_ref, send_sems, recv_sems, local_sem):
        x, y, c, chips = _place()
        me, sibling = (x, y, c), (x, y, 1 - c)

        def rows(px, py, pc):
            return all_ref.at[pl.ds(pl.multiple_of((4 * px + 2 * py + pc) * m_per, 8), m_per), :]

        def copy(k, block, to, src=None):
            return _rcopy(rows(*block) if src is None else src, rows(*block), send_sems, recv_sems, k, to)

        mine = pltpu.make_async_copy(x_ref, rows(*me), local_sem)
        mine.start()
        first = [copy(0, me, sibling, src=x_ref)]
        first += [copy(1 + j, me, (*chip, c), src=x_ref) for j, chip in enumerate(chips)]
        for cp in first:
            cp.start()
        passed = [copy(4 + j, (*chip, c), sibling) for j, chip in enumerate(chips)]
        for j, chip in enumerate(chips):
            copy(1 + j, (*chip, c), me).wait_recv()
            passed[j].start()
        copy(0, sibling, me).wait_recv()
        for j, chip in enumerate(chips):
            copy(4 + j, (*chip, 1 - c), me).wait_recv()
        for cp in first + passed:
            cp.wait_send()
        mine.wait()
        tot = all_ref[pl.ds(0, m_per), :]
        for d in range(1, N_DEV):
            tot = tot + all_ref[pl.ds(d * m_per, m_per), :]
        out_ref[...] = tot

    return pl.pallas_call(
        body, name=name, out_shape=jax.ShapeDtypeStruct((m_per, 128), f32),
        in_specs=[pl.BlockSpec(memory_space=pltpu.VMEM)], out_specs=pl.BlockSpec(memory_space=pltpu.VMEM),
        scratch_shapes=[pltpu.VMEM((N_DEV * m_per, 128), f32), pltpu.SemaphoreType.DMA((7,)), pltpu.SemaphoreType.DMA((7,)),
                        pltpu.SemaphoreType.DMA],
        compiler_params=pltpu.CompilerParams(has_side_effects=True, vmem_limit_bytes=VMEM_LIMIT_BYTES),
    )(buf)


def pack_shards(ws, layer):
    wide = jnp.concatenate([ws[n][layer].astype(bf16) for n in WIDE], axis=0)
    w_in_t = jnp.pad(jnp.transpose(ws["w_in"][layer]).astype(bf16), ((0, IN_PAD - IN_SHARD), (0, 0)))
    return [w_in_t, wide, ws["w_xkv"][layer].astype(bf16)]


def arrange_w_in(shards):
    parts = []
    for a, b, _ in sorted(ORIG_SEGS, key=lambda seg: seg[2]):
        for s in range(N_CHIPS):
            lo, hi = max(a, s * IN_SHARD), min(b, (s + 1) * IN_SHARD)
            if lo < hi:
                parts.append(shards[s][lo - s * IN_SHARD:hi - s * IN_SHARD])
    parts.append(jnp.zeros((PCOLS - C_DT - (OFF_Q - OFF_DT), shards.shape[2]), shards.dtype))
    return jnp.concatenate(parts, axis=0)


def shard_w_in_grad(g):
    out = []
    for s in range(N_CHIPS):
        for a, b, first in ORIG_SEGS:
            lo, hi = max(a, s * IN_SHARD), min(b, (s + 1) * IN_SHARD)
            if lo < hi:
                out.append(g[first + lo - a:first + hi - a])
        out.append(jnp.zeros((IN_PAD - IN_SHARD, g.shape[1]), g.dtype))
    return jnp.concatenate(out, axis=0).reshape(N_CHIPS, IN_PAD, g.shape[1])


def unpack_gathered(gathered):
    g_in, g_wide, g_xkv = gathered
    out = dict(w_in=arrange_w_in(g_in), w_xkv=jnp.concatenate([g_xkv[s] for s in range(N_CHIPS)], axis=1))
    for n in WIDE:
        piece = g_wide[:, WIDE_OFF[n]:WIDE_OFF[n] + WIDE_ROWS[n], :]
        if n == "w_mlp_up":
            out[n] = jnp.concatenate([piece[s] for s in range(N_CHIPS)], axis=1)
        else:
            out[n] = piece.reshape(N_CHIPS * WIDE_ROWS[n], D)
    return out


def _halved(a):
    return a.reshape(N_CHIPS, 2, a.shape[1] // 2, a.shape[2])


def pack_w_in_grad(g):
    return [_halved(shard_w_in_grad(g))]


def pack_other_grads(gs):
    cols = lambda g, q: jnp.concatenate([g[:, s * q:(s + 1) * q] for s in range(N_CHIPS)], axis=0).reshape(N_CHIPS, g.shape[0], q)
    wide = [cols(gs[n], D) if n == "w_mlp_up" else gs[n].reshape(N_CHIPS, WIDE_ROWS[n], D) for n in WIDE]
    return [_halved(jnp.concatenate(wide, axis=1)), _halved(cols(gs["w_xkv"], 2048 // N_CHIPS))]


ROW_TILE = 256


def _pad128(v):
    return jnp.pad(v.reshape(1, -1), ((0, 0), (0, 128 - v.shape[-1])))


def layer_forward(i, h, mem, w, p, bias, next_shards=None):
    tile = min(ROW_TILE, h.shape[0])
    s = dict(h=h)
    row = lambda a: a.reshape(1, -1)
    s["u"], = tile_fwd(f"rms_mix{i}", f_rms, [(h, D, 0)], [row(p["norm_mix"])], [(D, bf16)], tile)
    s["proj"] = proj = matmul(s["u"], w["w_in"], "nt", f"mm_in{i}")
    s["c1"], = conv_fwd(f"conv_a{i}", proj, C_CONV, C_CONV + D, p["conv_dw_w"], row(p["conv_dw_b"]), True, False)
    s["ca"], = tile_fwd(f"ln_silu{i}", f_ln_silu, [(s["c1"], D, 0)], [row(p["conv_ln_g"]), row(p["conv_ln_b"])], [(D, bf16)], tile)
    s["ya"] = matmul(s["ca"], w["w_conv_out"], "nn", f"mm_conv_out{i}")
    s["xpre"], s["xact"] = conv_fwd(f"conv_b{i}", proj, C_XBC, C_XBC, p["ssd_conv_w"], row(p["ssd_conv_b"]), False, True)
    ssd_p = (_pad128(p["ssd_dt_bias"]), _pad128(p["ssd_A_log"]), _pad128(p["ssd_D"]), row(p["ssd_norm_g"]))
    s["yB"], s["states"], landed = ssd_fwd(f"ssd_fwd{i}", s["xact"], proj, *ssd_p,
                                           ex=ex_gather_ici(next_shards) if next_shards else None)
    s["yb"] = matmul(s["yB"], w["w_ssd_out"], "nn", f"mm_ssd_out{i}")
    swa_p = (row(p["attn_q_norm"]), row(p["attn_k_norm"]), _pad128(p["attn_sinks"]))
    s["oc"], gathered = swa_fwd(f"swa_fwd{i}", proj, *swa_p, bias, ex=ex_gather_d2d(landed) if next_shards else None)
    if next_shards:
        gathered = run_exchange(f"gather_diag{i + 1}", ex_gather_diag(gathered))
    s["yc"] = matmul(s["oc"], w["w_attn_out"], "nn", f"mm_attn_out{i}")
    gate_cols = [(proj, D, C_G // D + k) for k in range(3)]
    s["merged"], = tile_fwd(f"merge{i}", f_merge, gate_cols + [(s["ya"], D, 0), (s["yb"], D, 0), (s["yc"], D, 0)],
                            [row(p["gate_bias"])], [(D, bf16)], tile)
    s["h1"] = h1 = matmul(s["merged"], w["w_mix_out"], "nn", f"mm_mix_out{i}", residual=h)
    s["hx"], = tile_fwd(f"rms_x{i}", f_rms, [(h1, D, 0)], [row(p["norm_xattn"])], [(D, bf16)], tile)
    s["memh"], = tile_fwd(f"rms_mem{i}", f_rms, [(mem, D, 0)], [row(p["norm_mem"])], [(D, bf16)], MEM_LEN)
    s["xq"] = matmul(s["hx"], w["w_xq"], "nn", f"mm_xq{i}")
    s["kv"] = matmul(s["memh"], w["w_xkv"], "nn", f"mm_xkv{i}")
    s["xo"] = xa_fwd(f"xa_fwd{i}", s["xq"], s["kv"], row(p["xattn_q_norm"]), row(p["xattn_k_norm"]))
    s["h2"] = h2 = matmul(s["xo"], w["w_xo"], "nn", f"mm_xo{i}", residual=h1)
    s["um"], = tile_fwd(f"rms_mlp{i}", f_rms, [(h2, D, 0)], [row(p["norm_mlp"])], [(D, bf16)], tile)
    s["up"], s["act"] = matmul(s["um"], w["w_mlp_up"], "nn", f"mm_up{i}", relu2="fwd")
    h3 = matmul(s["act"], w["w_mlp_down"], "nn", f"mm_down{i}", residual=h2)
    return h3, s, gathered


def layer_backward(i, dh3, mem, w, p, bias, s, pending=None, c_arr=None):
    t_len = dh3.shape[0]
    tile = min(ROW_TILE, t_len)
    row = lambda a: a.reshape(1, -1)
    gw, gp = {}, {}
    dup = matmul(dh3, w["w_mlp_down"], "nt", f"mm_down_dx{i}", out_dtype=bf16, relu2=s["up"])
    gw["w_mlp_down"] = matmul(s["act"], dh3, "tn", f"mm_down_dw{i}")
    gw["w_mlp_up"] = matmul(s["um"], dup, "tn", f"mm_up_dw{i}")
    dum = matmul(dup, w["w_mlp_up"], "nt", f"mm_up_dx{i}")
    (dh2,), (g,) = tile_bwd(f"rms_mlp_bwd{i}", f_rms, [(s["h2"], D, 0)], [row(p["norm_mlp"])], [dum], [f32], tile, addend=dh3)
    gp["norm_mlp"] = g
    dxo = matmul(dh2, w["w_xo"], "nt", f"mm_xo_dx{i}")
    gw["w_xo"] = matmul(s["xo"], dh2, "tn", f"mm_xo_dw{i}")
    dxq, dk, dv, gp["xattn_q_norm"], gp["xattn_k_norm"] = xa_bwd(f"xa_bwd{i}", s["xq"], s["kv"], dxo, row(p["xattn_q_norm"]),
                                                                 row(p["xattn_k_norm"]))
    dkv = jnp.concatenate([dk, dv], axis=1)
    gw["w_xq"] = matmul(s["hx"], dxq, "tn", f"mm_xq_dw{i}")
    dhx = matmul(dxq, w["w_xq"], "nt", f"mm_xq_dx{i}")
    gw["w_xkv"] = matmul(s["memh"], dkv, "tn", f"mm_xkv_dw{i}")
    dmemh = matmul(dkv, w["w_xkv"], "nt", f"mm_xkv_dx{i}")
    _, (g,) = tile_bwd(f"rms_mem_bwd{i}", f_rms, [(mem, D, 0)], [row(p["norm_mem"])], [dmemh], [f32], MEM_LEN)
    gp["norm_mem"] = g
    (dh1,), (g,) = tile_bwd(f"rms_x_bwd{i}", f_rms, [(s["h1"], D, 0)], [row(p["norm_xattn"])], [dhx], [f32], tile, addend=dh2)
    gp["norm_xattn"] = g
    proj = s["proj"]
    dmerged = matmul(dh1, w["w_mix_out"], "nt", f"mm_mix_out_dx{i}")
    gw["w_mix_out"] = matmul(s["merged"], dh1, "tn", f"mm_mix_out_dw{i}")
    gate_cols = [(proj, D, C_G // D + k) for k in range(3)]
    (dpa, dpb, dpc, dya, dyb, dyc), (g,) = tile_bwd(
        f"merge_bwd{i}", f_merge, gate_cols + [(s["ya"], D, 0), (s["yb"], D, 0), (s["yc"], D, 0)], [row(p["gate_bias"])],
        [dmerged], [bf16] * 6, tile)
    gp["gate_bias"] = g
    dca = matmul(dya, w["w_conv_out"], "nt", f"mm_conv_out_dx{i}")
    gw["w_conv_out"] = matmul(s["ca"], dya, "tn", f"mm_conv_out_dw{i}")
    (dc1,), (gp["conv_ln_g"], gp["conv_ln_b"]) = tile_bwd(
        f"ln_silu_bwd{i}", f_ln_silu, [(s["c1"], D, 0)], [row(p["conv_ln_g"]), row(p["conv_ln_b"])], [dca], [f32], tile)
    da, dgate, gp["conv_dw_w"], gp["conv_dw_b"] = conv_bwd(f"conv_a_bwd{i}", dc1, None, proj, C_CONV, C_CONV + D,
                                                           p["conv_dw_w"], True, False, bf16)
    dyB = matmul(dyb, w["w_ssd_out"], "nt", f"mm_ssd_out_dx{i}")
    gw["w_ssd_out"] = matmul(s["yB"], dyb, "tn", f"mm_ssd_out_dw{i}")
    doc = matmul(dyc, w["w_attn_out"], "nt", f"mm_attn_out_dx{i}")
    gw["w_attn_out"] = matmul(s["oc"], dyc, "tn", f"mm_attn_out_dw{i}")
    moving = list(pending or []) + pack_other_grads(gw)
    swa_p = (row(p["attn_q_norm"]), row(p["attn_k_norm"]), _pad128(p["attn_sinks"]))
    (dq, dkp, dkc, dvp, dvc, gp["attn_q_norm"], gp["attn_k_norm"], g, dbias), swapped = swa_bwd(
        f"swa_bwd{i}", proj, doc, *swa_p, bias, ex=ex_swap_halves(moving))
    gp["attn_sinks"] = g[:, :ATTN_HEADS]
    shift = lambda a: jnp.concatenate([a[BLK:], jnp.zeros((BLK, a.shape[1]), a.dtype)], axis=0)
    dk_ = (dkc + shift(dkp)).astype(bf16)
    dv_ = (dvc + shift(dvp)).astype(bf16)
    partials = [add_halves(f"add_halves{i}_{k}", g_, got, c_arr) for k, (g_, got) in enumerate(zip(moving, swapped))]
    ssd_p = (_pad128(p["ssd_dt_bias"]), _pad128(p["ssd_A_log"]), _pad128(p["ssd_D"]), row(p["ssd_norm_g"]))
    (dxs, dbm, dcm, dz, ddt4, g1, g2, g3, g4), arrived = ssd_bwd(f"ssd_bwd{i}", s["xact"], proj, s["states"], dyB, *ssd_p,
                                                                  ex=ex_scatter(partials))
    gp["ssd_dt_bias"], gp["ssd_A_log"], gp["ssd_D"], gp["ssd_norm_g"] = g1[:, :SSD_HEADS], g2[:, :SSD_HEADS], g3[:, :SSD_HEADS], g4
    ddt = (ddt4[0] + ddt4[1] + ddt4[2] + ddt4[3]).astype(bf16)
    dxact = jnp.concatenate([dxs, dbm, dcm], axis=1)
    dxbc, gp["ssd_conv_w"], gp["ssd_conv_b"] = conv_bwd(f"conv_b_bwd{i}", dxact, s["xpre"], proj, C_XBC, C_XBC, p["ssd_conv_w"],
                                                        False, True, bf16)
    dproj = jnp.concatenate([da, dgate, dz, dxbc, dq, dpa, dpb, dpc, dk_, dv_, ddt, jnp.zeros((t_len, PCOLS - C_DT - 128), bf16)], axis=1)
    gw["w_in"] = matmul(dproj, s["u"], "tn", f"mm_in_dw{i}")
    du = matmul(dproj, w["w_in"], "nn", f"mm_in_dx{i}")
    (dh,), (g,) = tile_bwd(f"rms_mix_bwd{i}", f_rms, [(s["h"], D, 0)], [row(p["norm_mix"])], [du], [f32], tile, addend=dh1)
    gp["norm_mix"] = g
    return dh, gw, gp, dbias, (partials, arrived)


SMALL = ["rel_table", "norm_mix", "gate_bias", "conv_dw_w", "conv_dw_b", "conv_ln_g", "conv_ln_b", "ssd_conv_w", "ssd_conv_b",
         "ssd_dt_bias", "ssd_A_log", "ssd_D", "ssd_norm_g", "attn_q_norm", "attn_k_norm", "attn_sinks", "norm_xattn", "norm_mem",
         "xattn_q_norm", "xattn_k_norm", "norm_mlp"]
SMALL_SHARDED = dict(gate_bias=D, conv_dw_w=D, ssd_conv_w=SSD_XBC)
ORDER = ["rel_table", "norm_mix", "w_in", "gate_bias", "conv_dw_w", "conv_dw_b", "conv_ln_g", "conv_ln_b", "w_conv_out",
         "ssd_conv_w", "ssd_conv_b", "ssd_dt_bias", "ssd_A_log", "ssd_D", "ssd_norm_g", "w_ssd_out", "attn_q_norm", "attn_k_norm",
         "attn_sinks", "w_attn_out", "w_mix_out", "norm_xattn", "norm_mem", "w_xq", "w_xkv", "xattn_q_norm", "xattn_k_norm",
         "w_xo", "norm_mlp", "w_mlp_up", "w_mlp_down"]


def finish_reduction(i, partials, arrived, me_arr):
    mine = [add_partials(f"add_partials{i}_{k}", p, got, me_arr) for k, (p, got) in enumerate(zip(partials, arrived))]
    theirs = run_exchange(f"share_halves{i}", ex_share(mine))
    both = [join_halves(f"join_halves{i}_{k}", m, t) for k, (m, t) in enumerate(zip(mine, theirs))]
    return [b.reshape(2 * b.shape[1], b.shape[2]) for b in both]


def kernel(x, mem, rel_table, norm_mix, w_in, gate_bias, conv_dw_w, conv_dw_b, conv_ln_g, conv_ln_b, w_conv_out, ssd_conv_w, ssd_conv_b, ssd_dt_bias, ssd_A_log, ssd_D, ssd_norm_g, w_ssd_out, attn_q_norm, attn_k_norm, attn_sinks, w_attn_out, w_mix_out, norm_xattn, norm_mem, w_xq, w_xkv, xattn_q_norm, xattn_k_norm, w_xo, norm_mlp, w_mlp_up, w_mlp_down, loss_target, m_rel_table, m_norm_mix, m_w_in, m_gate_bias, m_conv_dw_w, m_conv_dw_b, m_conv_ln_g, m_conv_ln_b, m_w_conv_out, m_ssd_conv_w, m_ssd_conv_b, m_ssd_dt_bias, m_ssd_A_log, m_ssd_D, m_ssd_norm_g, m_w_ssd_out, m_attn_q_norm, m_attn_k_norm, m_attn_sinks, m_w_attn_out, m_w_mix_out, m_norm_xattn, m_norm_mem, m_w_xq, m_w_xkv, m_xattn_q_norm, m_xattn_k_norm, m_w_xo, m_norm_mlp, m_w_mlp_up, m_w_mlp_down, v_rel_table, v_norm_mix, v_w_in, v_gate_bias, v_conv_dw_w, v_conv_dw_b, v_conv_ln_g, v_conv_ln_b, v_w_conv_out, v_ssd_conv_w, v_ssd_conv_b, v_ssd_dt_bias, v_ssd_A_log, v_ssd_D, v_ssd_norm_g, v_w_ssd_out, v_attn_q_norm, v_attn_k_norm, v_attn_sinks, v_w_attn_out, v_w_mix_out, v_norm_xattn, v_norm_mem, v_w_xq, v_w_xkv, v_xattn_q_norm, v_xattn_k_norm, v_w_xo, v_norm_mlp, v_w_mlp_up, v_w_mlp_down):
    a = dict(locals())
    wts = {n: a[n] for n in ORDER}
    ms = {n: a["m_" + n] for n in ORDER}
    vs = {n: a["v_" + n] for n in ORDER}
    depth = norm_mix.shape[0]
    ax, ay, ac = lax.axis_index("x"), lax.axis_index("y"), lax.axis_index("c")
    chip = 2 * ax + ay

    small = {n: wts[n] for n in SMALL if n != "rel_table" and n not in SMALL_SHARDED}
    gather_buf, spans, r0 = [], {}, 0
    for n, width in SMALL_SHARDED.items():
        blk = wts[n]
        q = width // N_CHIPS
        whole = lax.dynamic_update_slice(jnp.zeros(blk.shape[:-1] + (width,), f32), blk, (0, 0, chip * q))
        flat = whole.reshape(-1, 128)
        spans[n] = (r0, flat.shape[0], whole.shape)
        r0 += flat.shape[0]
        gather_buf.append(flat)
    summed = allreduce_small("gather_small_params", jnp.concatenate(gather_buf, axis=0))
    for n, (s0, nr, shp) in spans.items():
        small[n] = summed[s0:s0 + nr].reshape(shp) * 0.5

    c_arr = jnp.reshape(ac, (1,)).astype(jnp.int32)
    me_arr = jnp.reshape(chip, (1,)).astype(jnp.int32)
    x0, mem0 = x[0], mem[0]
    bucket = jnp.asarray(_bucket_table())
    bias = relbias_fwd(rel_table, bucket)

    gathered = run_exchange("gather_ici_first", ex_gather_ici(pack_shards(wts, 0)))
    gathered = run_exchange("gather_diag0", ex_gather_diag(run_exchange("gather_d2d_first", ex_gather_d2d(gathered))))
    h, saved, ws = x0, [], []
    for i in range(depth):
        w = unpack_gathered(gathered)
        p = {n: small[n][i] for n in small}
        h, s, gathered = layer_forward(i, h, mem0, w, p, bias, pack_shards(wts, i + 1) if i + 1 < depth else None)
        saved.append(s)
        ws.append((w, p))
    grad_x, loss_tile = loss_head(h, loss_target[0], min(ROW_TILE, h.shape[0]))
    loss = lax.psum(loss_tile[0, 0], ("x", "y", "c"))

    reduced, small_grads, dbiases, pending = [[None] * 3 for _ in range(depth)], [None] * depth, [], None
    for i in reversed(range(depth)):
        w, p = ws[i]
        grad_x, gw, gp, dbias, travelled = layer_backward(i, grad_x, mem0, w, p, bias, saved[i], pending, c_arr)
        done = finish_reduction(i, *travelled, me_arr)
        if pending is not None:
            reduced[i + 1][0] = done.pop(0)
        reduced[i][1], reduced[i][2] = done
        small_grads[i] = gp
        dbiases.append(dbias)
        pending = pack_w_in_grad(gw["w_in"])
    swapped = run_exchange("swap_halves_last", ex_swap_halves(pending))
    partials = [add_halves("add_halves_last", pending[0], swapped[0], c_arr)]
    reduced[0][0], = finish_reduction("_last", partials, run_exchange("scatter_last", ex_scatter(partials)), me_arr)
    sg = {n: jnp.stack([small_grads[i][n].reshape(small[n].shape[1:]) for i in range(depth)]) for n in small}
    sg["rel_table"] = relbias_bwd(dbiases, bucket)

    grads, deltas, new_m, new_v = {}, {}, {}, {}
    sources = dict(w_in=(0, 0), w_xkv=(2, 0), **{n: (1, WIDE_OFF[n]) for n in WIDE})
    for n, (k, r0) in sources.items():
        flip = (lambda t: jnp.transpose(t, (0, 2, 1))) if n == "w_in" else (lambda t: t)
        res = adamw_layers(f"adamw_{n}", flip(wts[n]), flip(ms[n]), flip(vs[n]), [reduced[i][k] for i in range(depth)], r0)
        grads[n], deltas[n], new_m[n], new_v[n] = [flip(r) for r in res]

    parts, spans, r0 = [], {}, 0
    for n in SMALL:
        flat = sg[n].reshape(-1)
        nr = -(-flat.shape[0] // 128)
        nr = -(-nr // 8) * 8
        flat = jnp.pad(flat, (0, nr * 128 - flat.shape[0])).reshape(nr, 128)
        spans[n] = (r0, nr, sg[n].shape)
        r0 += nr
        parts.append(flat)
    summed = allreduce_small("allreduce_small_grads", jnp.concatenate(parts, axis=0))
    for n, (s0, nr, shp) in spans.items():
        size = int(np.prod(shp))
        g = summed[s0:s0 + nr].reshape(-1)[:size].reshape(shp)
        if n in SMALL_SHARDED:
            q = SMALL_SHARDED[n] // N_CHIPS
            g = lax.dynamic_slice_in_dim(g, chip * q, q, axis=g.ndim - 1)
        grads[n] = g

    for n in SMALL:
        shp = wts[n].shape
        two = (lambda t: t.reshape(-1, shp[-1]))
        d, nm, nv = adamw(f"adamw_{n}", two(wts[n]), two(grads[n]), two(ms[n]), two(vs[n]))
        deltas[n], new_m[n], new_v[n] = d.reshape(shp), nm.reshape(shp), nv.reshape(shp)

    return (loss, grad_x[None], *[grads[n] for n in ORDER], *[deltas[n] for n in ORDER],
            *[new_m[n] for n in ORDER], *[new_v[n] for n in ORDER])
```

```python
import functools
import math

import numpy as np
import jax
import jax.numpy as jnp
from jax import lax
from jax.experimental import pallas as pl
from jax.experimental.pallas import tpu as pltpu

f32, bf16 = jnp.float32, jnp.bfloat16

D = 1024
EPS = 1e-6
NEG_INF = -1e30
CONV_K = 31
SSD_INNER = 2048
SSD_HEADS = 32
SSD_P = 64
SSD_GROUPS = 4
SSD_N = 128
SSD_K = 4
CHUNK = 128
SSD_XBC = 3072
ATTN_HEADS = 16
ATTN_KV = 4
ATTN_D = 64
BLK = 128
REL_BUCKETS = 32
XH = 4
XD = 256
MLP = 4096
MEM_LEN = 256

OFF_Z, OFF_XBC, OFF_DT, OFF_Q, OFF_K, OFF_V, OFF_GATE, IN_COLS = 2048, 4096, 7168, 7200, 8224, 8480, 8736, 11808
C_CONV, C_Z, C_XBC, C_Q, C_G, C_K, C_V, C_DT, PCOLS = 0, 2048, 4096, 7168, 8192, 11264, 11520, 11776, 12288

ADAM_LR, ADAM_B1, ADAM_B2, ADAM_EPS, ADAM_WD, ADAM_STEP = 0.001, 0.9, 0.999, 1e-08, 0.01, 10

VMEM_LIMIT_BYTES = 56 * 1024 * 1024
N_CHIPS = 4
BIG = ["w_in", "w_conv_out", "w_ssd_out", "w_attn_out", "w_mix_out", "w_xq", "w_xkv", "w_xo", "w_mlp_up", "w_mlp_down"]
WIDE = ["w_conv_out", "w_ssd_out", "w_attn_out", "w_mix_out", "w_xq", "w_xo", "w_mlp_up", "w_mlp_down"]
WIDE_ROWS = dict(w_conv_out=256, w_ssd_out=512, w_attn_out=256, w_mix_out=256, w_xq=256, w_xo=256, w_mlp_up=1024, w_mlp_down=1024)
WIDE_OFF = {n: sum(WIDE_ROWS[m] for m in WIDE[:k]) for k, n in enumerate(WIDE)}
WIDE_TOTAL = sum(WIDE_ROWS.values())
IN_SHARD = IN_COLS // N_CHIPS
IN_PAD = 3072
ORIG_SEGS = [(0, OFF_DT, 0), (OFF_DT, OFF_Q, C_DT), (OFF_Q, OFF_K, C_Q), (OFF_K, OFF_GATE, C_K), (OFF_GATE, IN_COLS, C_G)]


def _params(sem=None):
    return pltpu.CompilerParams(dimension_semantics=sem, vmem_limit_bytes=VMEM_LIMIT_BYTES)


NN, NT, TN = ((1,), (0,)), ((1,), (1,)), ((0,), (0,))


def _dg(a, b, dims):
    return lax.dot_general(a.astype(bf16), b.astype(bf16), (dims, ((), ())), preferred_element_type=f32)


@jax.custom_vjp
def dot_nn(a, b):
    return _dg(a, b, NN)


dot_nn.defvjp(lambda a, b: (_dg(a, b, NN), (a, b)), lambda r, g: (_dg(g, r[1], NT), _dg(r[0], g, TN)))


@jax.custom_vjp
def dot_nt(a, b):
    return _dg(a, b, NT)


dot_nt.defvjp(lambda a, b: (_dg(a, b, NT), (a, b)), lambda r, g: (_dg(g, r[1], NN), _dg(g, r[0], TN)))


@jax.custom_vjp
def dot_tn(a, b):
    return _dg(a, b, TN)


dot_tn.defvjp(lambda a, b: (_dg(a, b, TN), (a, b)), lambda r, g: (_dg(r[1], g, NT), _dg(r[0], g, NN)))


def _tri(n, upper):
    r = lax.broadcasted_iota(jnp.int32, (n, n), 0)
    c = lax.broadcasted_iota(jnp.int32, (n, n), 1)
    return jnp.where((c >= r) if upper else (r >= c), 1.0, 0.0).astype(f32)


def _hdot(a, b):
    return lax.dot_general(a, b, (NN, ((), ())), preferred_element_type=f32, precision=lax.Precision.HIGHEST)


@jax.custom_vjp
def cumsum_rows(x):
    return _hdot(_tri(x.shape[0], False), x)


cumsum_rows.defvjp(lambda x: (_hdot(_tri(x.shape[0], False), x), None),
                   lambda _, g: (_hdot(_tri(g.shape[0], True), g),))


def _pick(n, prefs):
    for p in prefs:
        if n % p == 0:
            return p
    return n


def matmul(a, b, mode, name, out_dtype=f32, residual=None, relu2=None):
    if mode == "nn":
        (m, k), n = a.shape, b.shape[1]
    elif mode == "nt":
        (m, k), n = a.shape, b.shape[0]
    else:
        (k, m), n = a.shape, b.shape[1]
    tm = _pick(m, (1024, 512, 256))
    tn = _pick(n, (1024, 512, 256, 128))
    tk = _pick(k, (2048, 1024, 512))
    nk = k // tk
    dims = dict(nn=NN, nt=NT, tn=TN)[mode]
    a_spec = pl.BlockSpec((tk, tm), lambda i, j, l: (l, i)) if mode == "tn" else pl.BlockSpec((tm, tk), lambda i, j, l: (i, l))
    b_spec = pl.BlockSpec((tn, tk), lambda i, j, l: (j, l)) if mode == "nt" else pl.BlockSpec((tk, tn), lambda i, j, l: (l, j))
    o_spec = pl.BlockSpec((tm, tn), lambda i, j, l: (i, j))
    two_outs = isinstance(relu2, str)
    extra = residual if residual is not None else (None if two_outs or relu2 is None else relu2)
    has_extra = extra is not None
    n_out = 2 if two_outs else 1

    def body(*refs):
        a_ref, b_ref = refs[0], refs[1]
        e_ref = refs[2] if has_extra else None
        o_refs = refs[2 + has_extra:2 + has_extra + n_out]
        acc = refs[2 + has_extra + n_out]
        l = pl.program_id(2)

        prod = _dg(a_ref[...], b_ref[...], dims)

        def finish(r):
            if residual is not None:
                r = r + e_ref[...]
            elif has_extra:
                r = r * (2.0 * jnp.maximum(e_ref[...], 0.0))
            o_refs[0][...] = r.astype(o_refs[0].dtype)
            if two_outs:
                o_refs[1][...] = jnp.square(jnp.maximum(r, 0.0)).astype(o_refs[1].dtype)

        if nk == 1:
            finish(prod)
        else:
            @pl.when(l == 0)
            def _():
                acc[...] = prod

            @pl.when((l > 0) & (l < nk - 1))
            def _():
                acc[...] += prod

            @pl.when(l == nk - 1)
            def _():
                finish(acc[...] + prod)

    ins = [a, b] + ([extra] if has_extra else [])
    shapes = [jax.ShapeDtypeStruct((m, n), out_dtype)] + ([jax.ShapeDtypeStruct((m, n), bf16)] if two_outs else [])
    res = pl.pallas_call(
        body, name=name, grid=(m // tm, n // tn, nk),
        in_specs=[a_spec, b_spec] + ([o_spec] if has_extra else []),
        out_specs=[o_spec] * n_out, out_shape=shapes,
        scratch_shapes=[pltpu.VMEM((tm, tn), f32)],
        compiler_params=_params(("parallel", "parallel", "arbitrary")),
    )(*ins)
    return res if two_outs else res[0]


def _tile_in_specs(tiled, params, tile):
    specs = [pl.BlockSpec((tile, cols), functools.partial(lambda i, cb: (i, cb), cb=cb)) for (_, cols, cb) in tiled]
    specs += [pl.BlockSpec(p.shape, lambda i: (0, 0)) for p in params]
    return specs


def tile_fwd(name, f, tiled, params, outs, tile):
    rows = tiled[0][0].shape[0]
    nt, npar = len(tiled), len(params)

    def body(*refs):
        xs = [r[...].astype(f32) for r in refs[:nt + npar]]
        res = f(*xs)
        for o_ref, o in zip(refs[nt + npar:], res):
            o_ref[...] = o.astype(o_ref.dtype)

    return pl.pallas_call(
        body, name=name, grid=(rows // tile,),
        in_specs=_tile_in_specs(tiled, params, tile),
        out_specs=[pl.BlockSpec((tile, c), lambda i: (i, 0)) for c, _ in outs],
        out_shape=[jax.ShapeDtypeStruct((rows, c), dt) for c, dt in outs],
        compiler_params=_params(("parallel",)),
    )(*[t[0] for t in tiled], *params)


def tile_bwd(name, f, tiled, params, cots, d_dtypes, tile, addend=None):
    rows = tiled[0][0].shape[0]
    nt, npar, nc = len(tiled), len(params), len(cots)
    has_add = addend is not None

    def body(*refs):
        i = pl.program_id(0)
        xs = [r[...].astype(f32) for r in refs[:nt + npar]]
        gs = tuple(r[...].astype(f32) for r in refs[nt + npar:nt + npar + nc])
        pos = nt + npar + nc
        add_ref = refs[pos] if has_add else None
        pos += has_add
        dt_refs, dp_refs = refs[pos:pos + nt], refs[pos + nt:pos + nt + npar]
        _, vjp = jax.vjp(f, *xs)
        ds = vjp(gs)
        for k in range(nt):
            d = ds[k]
            if has_add and k == 0:
                d = d + add_ref[...]
            dt_refs[k][...] = d.astype(dt_refs[k].dtype)

        @pl.when(i == 0)
        def _():
            for r in dp_refs:
                r[...] = jnp.zeros(r.shape, f32)

        for k in range(npar):
            dp_refs[k][...] += ds[nt + k]

    in_specs = _tile_in_specs(tiled, params, tile)
    in_specs += [pl.BlockSpec((tile, c.shape[1]), lambda i: (i, 0)) for c in cots]
    ins = [t[0] for t in tiled] + list(params) + list(cots)
    if has_add:
        in_specs.append(pl.BlockSpec((tile, addend.shape[1]), lambda i: (i, 0)))
        ins.append(addend)
    out_specs = [pl.BlockSpec((tile, cols), lambda i: (i, 0)) for (_, cols, _) in tiled]
    out_specs += [pl.BlockSpec(p.shape, lambda i: (0, 0)) for p in params]
    out_shape = [jax.ShapeDtypeStruct((rows, cols), dt) for (_, cols, _), dt in zip(tiled, d_dtypes)]
    out_shape += [jax.ShapeDtypeStruct(p.shape, f32) for p in params]
    res = pl.pallas_call(
        body, name=name, grid=(rows // tile,), in_specs=in_specs, out_specs=out_specs, out_shape=out_shape,
        compiler_params=_params(("arbitrary",)),
    )(*ins)
    return res[:nt], res[nt:]


def f_rms(h, g):
    return (h * lax.rsqrt(jnp.mean(h * h, axis=-1, keepdims=True) + EPS) * g,)


def f_ln_silu(x, g, b):
    mu = jnp.mean(x, axis=-1, keepdims=True)
    xc = x - mu
    y = xc * lax.rsqrt(jnp.mean(xc * xc, axis=-1, keepdims=True) + EPS) * g + b
    return (y * jax.nn.sigmoid(y),)


def f_merge(pa, pb, pc, ya, yb, yc, gb):
    ga = jax.nn.sigmoid(pa + gb[:, 0:D])
    gb_ = jax.nn.sigmoid(pb + gb[:, D:2 * D])
    gc = jax.nn.sigmoid(pc + gb[:, 2 * D:3 * D])
    return (ga * ya + gb_ * yb + gc * yc,)


HALO = 32
CONV_CB = 256
SUB = 128


def _silu_grad(p):
    s = jax.nn.sigmoid(p)
    return s * (1.0 + p * (1.0 - s))


def _fill_shifted(buf, copies, length):
    for k in range(1, 8):
        for r0 in range(0, length - 8, SUB):
            n = min(SUB, length - 8 - r0)
            copies[k - 1, pl.ds(r0, n), :] = buf[pl.ds(r0 + k, n), :]


def _rows(buf, copies, off, n):
    if copies is None or off % 8 == 0:
        return buf[pl.ds(off, n), :]
    return copies[off % 8 - 1, pl.ds(off - off % 8, n), :]


def conv_fwd(name, src, col0, gate_col0, w, b, glu, silu):
    t_len = src.shape[0]
    k_taps, c = w.shape
    tt = min(512, t_len)
    nt_ = t_len // tt
    cb0, gb0 = col0 // CONV_CB, gate_col0 // CONV_CB
    many_taps = k_taps > 8

    def body(*refs):
        pos = 0
        x_cur, x_prev = refs[0], refs[1]
        pos = 2
        if glu:
            g_cur, g_prev = refs[2], refs[3]
            pos = 4
        w_ref, b_ref = refs[pos], refs[pos + 1]
        pre_ref = refs[pos + 2]
        act_ref = refs[pos + 3] if silu else None
        xp = refs[-2] if many_taps else refs[-1]
        xs = refs[-1] if many_taps else None
        t = pl.program_id(1)
        cur = x_cur[...]
        tail = x_prev[...]
        if glu:
            cur = cur * jax.nn.sigmoid(g_cur[...])
            tail = tail * jax.nn.sigmoid(g_prev[...])
        xp[pl.ds(0, HALO), :] = jnp.where(t > 0, tail, 0.0)
        xp[pl.ds(HALO, tt), :] = cur
        if many_taps:
            _fill_shifted(xp, xs, HALO + tt)
        for s in range(tt // SUB):
            acc = jnp.broadcast_to(b_ref[...], (SUB, CONV_CB))
            for j in range(k_taps):
                acc = acc + _rows(xp, xs, s * SUB + HALO - (k_taps - 1) + j, SUB) * w_ref[pl.ds(j, 1), :]
            pre_ref[pl.ds(s * SUB, SUB), :] = acc
            if silu:
                act_ref[pl.ds(s * SUB, SUB), :] = acc * jax.nn.sigmoid(acc)

    blk = lambda off: pl.BlockSpec((tt, CONV_CB), functools.partial(lambda j, t, off: (t, off + j), off=off))
    prev = lambda off: pl.BlockSpec((HALO, CONV_CB), functools.partial(lambda j, t, off: (jnp.maximum(t * (tt // HALO) - 1, 0), off + j), off=off))
    in_specs, ins = [blk(cb0), prev(cb0)], [src, src]
    if glu:
        in_specs += [blk(gb0), prev(gb0)]
        ins += [src, src]
    in_specs += [pl.BlockSpec((k_taps, CONV_CB), lambda j, t: (0, j)), pl.BlockSpec((1, CONV_CB), lambda j, t: (0, j))]
    ins += [w, b]
    o_spec = pl.BlockSpec((tt, CONV_CB), lambda j, t: (t, j))
    n_out = 2 if silu else 1
    res = pl.pallas_call(
        body, name=name, grid=(c // CONV_CB, nt_), in_specs=in_specs, out_specs=[o_spec] * n_out,
        out_shape=[jax.ShapeDtypeStruct((t_len, c), f32)] * n_out,
        scratch_shapes=[pltpu.VMEM((HALO + tt, CONV_CB), f32)] + ([pltpu.VMEM((7, HALO + tt, CONV_CB), f32)] if many_taps else []),
        compiler_params=_params(("parallel", "arbitrary")),
    )(*ins)
    return res


def conv_bwd(name, dy, pre, src, col0, gate_col0, w, glu, silu, out_dtype):
    t_len = src.shape[0]
    k_taps, c = w.shape
    tt = min(512, t_len)
    nt_ = t_len // tt
    cb0, gb0 = col0 // CONV_CB, gate_col0 // CONV_CB
    many_taps = k_taps > 8

    def body(*refs):
        dy_cur, dy_next = refs[0], refs[1]
        pos = 2
        if silu:
            p_cur, p_next = refs[2], refs[3]
            pos = 4
        x_cur, x_prev = refs[pos], refs[pos + 1]
        pos += 2
        if glu:
            g_cur, g_prev = refs[pos], refs[pos + 1]
            pos += 2
        w_ref = refs[pos]
        pos += 1
        n_dx = 2 if glu else 1
        dx_refs = refs[pos:pos + n_dx]
        dw_ref, db_ref = refs[pos + n_dx], refs[pos + n_dx + 1]
        dp, xp = (refs[-4], refs[-3]) if many_taps else (refs[-2], refs[-1])
        ds_, xs = (refs[-2], refs[-1]) if many_taps else (None, None)
        t = pl.program_id(1)

        dcur = dy_cur[...]
        dhead = dy_next[...]
        if silu:
            dcur = dcur * _silu_grad(p_cur[...])
            dhead = dhead * _silu_grad(p_next[...])
        dp[pl.ds(0, tt), :] = dcur
        dp[pl.ds(tt, HALO), :] = jnp.where(t < nt_ - 1, dhead, 0.0)
        cur = x_cur[...]
        tail = x_prev[...]
        if glu:
            cur = cur * jax.nn.sigmoid(g_cur[...])
            tail = tail * jax.nn.sigmoid(g_prev[...])
        xp[pl.ds(0, HALO), :] = jnp.where(t > 0, tail, 0.0)
        xp[pl.ds(HALO, tt), :] = cur

        if many_taps:
            _fill_shifted(dp, ds_, tt + HALO)
            _fill_shifted(xp, xs, HALO + tt)

        @pl.when(t == 0)
        def _():
            dw_ref[...] = jnp.zeros(dw_ref.shape, f32)
            db_ref[...] = jnp.zeros(db_ref.shape, f32)

        for s in range(tt // SUB):
            d_sub = dp[pl.ds(s * SUB, SUB), :]
            acc = jnp.zeros((SUB, CONV_CB), f32)
            for j in range(k_taps):
                acc = acc + _rows(dp, ds_, s * SUB + (k_taps - 1) - j, SUB) * w_ref[pl.ds(j, 1), :]
                x_sub = _rows(xp, xs, s * SUB + HALO - (k_taps - 1) + j, SUB)
                dw_ref[pl.ds(j, 1), :] += jnp.sum(d_sub * x_sub, axis=0, keepdims=True)
            db_ref[...] += jnp.sum(d_sub, axis=0, keepdims=True)
            if glu:
                a = x_cur[pl.ds(s * SUB, SUB), :]
                sg = jax.nn.sigmoid(g_cur[pl.ds(s * SUB, SUB), :])
                dx_refs[0][pl.ds(s * SUB, SUB), :] = (acc * sg).astype(out_dtype)
                dx_refs[1][pl.ds(s * SUB, SUB), :] = (acc * a * sg * (1.0 - sg)).astype(out_dtype)
            else:
                dx_refs[0][pl.ds(s * SUB, SUB), :] = acc.astype(out_dtype)

    blk = lambda off: pl.BlockSpec((tt, CONV_CB), functools.partial(lambda j, t, off: (t, off + j), off=off))
    prev = lambda off: pl.BlockSpec((HALO, CONV_CB), functools.partial(lambda j, t, off: (jnp.maximum(t * (tt // HALO) - 1, 0), off + j), off=off))
    nxt = pl.BlockSpec((HALO, CONV_CB), lambda j, t: (jnp.minimum((t + 1) * (tt // HALO), t_len // HALO - 1), j))
    in_specs, ins = [blk(0), nxt], [dy, dy]
    if silu:
        in_specs += [blk(0), nxt]
        ins += [pre, pre]
    in_specs += [blk(cb0), prev(cb0)]
    ins += [src, src]
    if glu:
        in_specs += [blk(gb0), prev(gb0)]
        ins += [src, src]
    in_specs.append(pl.BlockSpec((k_taps, CONV_CB), lambda j, t: (0, j)))
    ins.append(w)
    n_dx = 2 if glu else 1
    o_spec = pl.BlockSpec((tt, CONV_CB), lambda j, t: (t, j))
    out_specs = [o_spec] * n_dx + [pl.BlockSpec((k_taps, CONV_CB), lambda j, t: (0, j)), pl.BlockSpec((1, CONV_CB), lambda j, t: (0, j))]
    out_shape = [jax.ShapeDtypeStruct((t_len, c), out_dtype)] * n_dx + [jax.ShapeDtypeStruct((k_taps, c), f32), jax.ShapeDtypeStruct((1, c), f32)]
    return pl.pallas_call(
        body, name=name, grid=(c // CONV_CB, nt_), in_specs=in_specs, out_specs=out_specs, out_shape=out_shape,
        scratch_shapes=[pltpu.VMEM((tt + HALO, CONV_CB), f32)] * 2 + ([pltpu.VMEM((7, tt + HALO, CONV_CB), f32)] * 2 if many_taps else []),
        compiler_params=_params(("parallel", "arbitrary")),
    )(*ins)


GH = SSD_HEADS // SSD_GROUPS
GW = GH * SSD_P


def _softplus(x):
    return jnp.maximum(x, 0.0) + jnp.log1p(jnp.exp(-jnp.abs(x)))


def f_ssd(hbase, x, z, bm, cm, dtraw, s_in, dt_bias, a_log, dskip, ng):
    q = x.shape[0]
    dt = _softplus(dtraw + dt_bias)
    da = dt * (-jnp.exp(a_log))
    cs = cumsum_rows(da)
    g_cb = dot_nt(cm, bm)
    lane = lax.broadcasted_iota(jnp.int32, (q, 128), 1)
    lane1 = lax.broadcasted_iota(jnp.int32, (1, 128), 1)
    causal = lax.broadcasted_iota(jnp.int32, (q, q), 0) >= lax.broadcasted_iota(jnp.int32, (q, q), 1)
    last = lax.broadcasted_iota(jnp.int32, (q, GW), 0) == q - 1
    dt_cols, cs_cols, d_cols = [], [], []
    for r in range(GH):
        sel = lane == hbase + r
        dt_cols.append(jnp.sum(jnp.where(sel, dt, 0.0), axis=1, keepdims=True))
        cs_cols.append(jnp.sum(jnp.where(sel, cs, 0.0), axis=1, keepdims=True))
        d_cols.append(jnp.sum(jnp.where(lane1 == hbase + r, dskip, 0.0), axis=1, keepdims=True))
    spread = lambda cols: jnp.concatenate([jnp.broadcast_to(c, (c.shape[0], SSD_P)) for c in cols], axis=1)
    dt_x, cs_x, d_x = spread(dt_cols), spread(cs_cols), spread(d_cols)
    cs_last = jnp.sum(jnp.where(last, cs_x, 0.0), axis=0, keepdims=True)
    xdt = x * dt_x
    y_diag = []
    for r in range(GH):
        m1 = jnp.broadcast_to(cs_cols[r], (q, q))
        decay = jnp.where(causal, jnp.exp(jnp.where(causal, m1 - m1.T, 0.0)), 0.0)
        y_diag.append(dot_nn(g_cb * decay, xdt[:, r * SSD_P:(r + 1) * SSD_P]))
    s_c = dot_tn(bm, xdt * jnp.exp(cs_last - cs_x))
    y_off = dot_nn(cm, s_in) * jnp.exp(cs_x)
    y = jnp.concatenate(y_diag, axis=1) + y_off + x * d_x
    s_out = s_in * jnp.exp(cs_last) + s_c
    y = y * (z * jax.nn.sigmoid(z))
    y = y * lax.rsqrt(jnp.mean(y * y, axis=-1, keepdims=True) + EPS) * ng
    return y, s_out


GPS_FWD, GPS_BWD = 2, 1


def _ssd_in_specs(cmap, gps):
    PW, PN = gps * GW, gps * SSD_N
    return [
        pl.BlockSpec((CHUNK, PW), lambda g, c: (cmap(c), g)),
        pl.BlockSpec((CHUNK, PW), lambda g, c: (cmap(c), C_Z // PW + g)),
        pl.BlockSpec((CHUNK, PN), lambda g, c: (cmap(c), SSD_INNER // PN + g)),
        pl.BlockSpec((CHUNK, PN), lambda g, c: (cmap(c), (SSD_INNER + SSD_GROUPS * SSD_N) // PN + g)),
        pl.BlockSpec((CHUNK, 128), lambda g, c: (cmap(c), C_DT // 128)),
    ]


def _ssd_param_specs(gps):
    return [pl.BlockSpec((1, 128), lambda g, c: (0, 0))] * 3 + [pl.BlockSpec((1, gps * GW), lambda g, c: (0, g))]


def _cols(ref, k, width):
    return ref[:, pl.ds(k * width, width)]


def ssd_fwd(name, xact, proj, dt_bias, a_log, dskip, ng, ex=None):
    t_len = xact.shape[0]
    nc = t_len // CHUNK
    h = _hosted(ex)
    GPS = GPS_FWD
    PW, PN, NPAIR = GPS * GW, GPS * SSD_N, SSD_GROUPS // GPS

    def body(*refs):
        x_ref, z_ref, b_ref, c_ref, dt_ref, p1, p2, p3, p4 = refs[:9]
        y_ref, st_ref = refs[9 + h.n_in:11 + h.n_in]
        s_scr = refs[11 + h.n_in + h.n_out]
        g, c = pl.program_id(0), pl.program_id(1)
        h.at_start(refs, 9, 11 + h.n_in, (g == 0) & (c == 0))

        @pl.when(c == 0)
        def _():
            s_scr[...] = jnp.zeros(s_scr.shape, f32)

        for k in range(GPS):
            s_in = s_scr[k]
            st_ref[k, 0] = s_in
            y, s_out = f_ssd((g * GPS + k) * GH, _cols(x_ref, k, GW), _cols(z_ref, k, GW), _cols(b_ref, k, SSD_N),
                             _cols(c_ref, k, SSD_N), dt_ref[...], s_in, p1[...], p2[...], p3[...], _cols(p4, k, GW))
            y_ref[:, pl.ds(k * GW, GW)] = y.astype(y_ref.dtype)
            s_scr[k] = s_out
        h.at_end(refs, 9, 11 + h.n_in, (g == NPAIR - 1) & (c == nc - 1))

    res = pl.pallas_call(
        body, name=name, grid=(NPAIR, nc),
        in_specs=_ssd_in_specs(lambda c: c, GPS) + _ssd_param_specs(GPS) + h.in_specs,
        out_specs=[pl.BlockSpec((CHUNK, PW), lambda g, c: (c, g)),
                   pl.BlockSpec((GPS, 1, SSD_N, GW), lambda g, c: (g, c, 0, 0))] + h.out_specs,
        out_shape=[jax.ShapeDtypeStruct((t_len, SSD_INNER), bf16),
                   jax.ShapeDtypeStruct((SSD_GROUPS, nc, SSD_N, GW), f32)] + h.out_shape,
        scratch_shapes=[pltpu.VMEM((GPS, SSD_N, GW), f32)] + h.scratch,
        input_output_aliases=h.aliases(9, 2),
        compiler_params=_params(("arbitrary", "arbitrary")),
    )(xact, proj, xact, xact, proj, dt_bias, a_log, dskip, ng, *h.ins)
    return res[0], res[1], list(res[2:])


def ssd_bwd(name, xact, proj, states, dy, dt_bias, a_log, dskip, ng, ex=None):
    t_len = xact.shape[0]
    nc = t_len // CHUNK
    rev = lambda c: nc - 1 - c
    h = _hosted(ex)
    GPS = GPS_BWD
    PW, PN, NPAIR = GPS * GW, GPS * SSD_N, SSD_GROUPS // GPS

    def body(*refs):
        x_ref, z_ref, b_ref, c_ref, dt_ref, st_ref, dy_ref, p1, p2, p3, p4 = refs[:11]
        dx_ref, db_ref, dc_ref, dz_ref, ddt_ref, d1, d2, d3, d4 = refs[11 + h.n_in:20 + h.n_in]
        ds_scr = refs[20 + h.n_in + h.n_out]
        g, c = pl.program_id(0), pl.program_id(1)
        h.at_start(refs, 11, 20 + h.n_in, (g == 0) & (c == 0))

        @pl.when(c == 0)
        def _():
            ds_scr[...] = jnp.zeros(ds_scr.shape, f32)
            d4[...] = jnp.zeros(d4.shape, f32)

        @pl.when((c == 0) & (g == 0))
        def _():
            for r in (d1, d2, d3):
                r[...] = jnp.zeros(r.shape, f32)

        for k in range(GPS):
            fn = functools.partial(f_ssd, (g * GPS + k) * GH)
            _, vjp = jax.vjp(fn, _cols(x_ref, k, GW), _cols(z_ref, k, GW), _cols(b_ref, k, SSD_N), _cols(c_ref, k, SSD_N),
                             dt_ref[...], st_ref[k, 0], p1[...], p2[...], p3[...], _cols(p4, k, GW))
            dx, dz, db, dc, ddt, ds_in, e1, e2, e3, e4 = vjp((_cols(dy_ref, k, GW).astype(f32), ds_scr[k]))
            dx_ref[:, pl.ds(k * GW, GW)] = dx
            dz_ref[:, pl.ds(k * GW, GW)] = dz.astype(dz_ref.dtype)
            db_ref[:, pl.ds(k * SSD_N, SSD_N)] = db
            dc_ref[:, pl.ds(k * SSD_N, SSD_N)] = dc
            ddt_ref[k] = ddt
            ds_scr[k] = ds_in
            d1[...] += e1
            d2[...] += e2
            d3[...] += e3
            d4[:, pl.ds(k * GW, GW)] += e4
        h.at_end(refs, 11, 20 + h.n_in, (g == NPAIR - 1) & (c == nc - 1))

    res = pl.pallas_call(
        body, name=name, grid=(NPAIR, nc),
        in_specs=_ssd_in_specs(rev, GPS) + [
            pl.BlockSpec((GPS, 1, SSD_N, GW), lambda g, c: (g, rev(c), 0, 0)),
            pl.BlockSpec((CHUNK, PW), lambda g, c: (rev(c), g)),
        ] + _ssd_param_specs(GPS) + h.in_specs,
        out_specs=[
            pl.BlockSpec((CHUNK, PW), lambda g, c: (rev(c), g)),
            pl.BlockSpec((CHUNK, PN), lambda g, c: (rev(c), g)),
            pl.BlockSpec((CHUNK, PN), lambda g, c: (rev(c), g)),
            pl.BlockSpec((CHUNK, PW), lambda g, c: (rev(c), g)),
            pl.BlockSpec((GPS, CHUNK, 128), lambda g, c: (g, rev(c), 0)),
        ] + _ssd_param_specs(GPS) + h.out_specs,
        out_shape=[
            jax.ShapeDtypeStruct((t_len, SSD_INNER), f32),
            jax.ShapeDtypeStruct((t_len, SSD_GROUPS * SSD_N), f32),
            jax.ShapeDtypeStruct((t_len, SSD_GROUPS * SSD_N), f32),
            jax.ShapeDtypeStruct((t_len, SSD_INNER), bf16),
            jax.ShapeDtypeStruct((SSD_GROUPS, t_len, 128), f32),
            jax.ShapeDtypeStruct((1, 128), f32), jax.ShapeDtypeStruct((1, 128), f32), jax.ShapeDtypeStruct((1, 128), f32),
            jax.ShapeDtypeStruct((1, SSD_INNER), f32),
        ] + h.out_shape,
        scratch_shapes=[pltpu.VMEM((GPS, SSD_N, GW), f32)] + h.scratch,
        input_output_aliases=h.aliases(11, 9),
        compiler_params=_params(("arbitrary", "arbitrary")),
    )(xact, proj, xact, xact, proj, states, dy, dt_bias, a_log, dskip, ng, *h.ins)
    return list(res[:9]), list(res[9:])


def _head_norm(t, g):
    return t * lax.rsqrt(jnp.mean(t * t, axis=-1, keepdims=True) + EPS) * g


def f_swa(has_prev, q, kp, kc, vp, vc, qg, kg, sinks, *bias):
    rep = ATTN_HEADS // ATTN_KV
    qi = lax.broadcasted_iota(jnp.int32, (rep * BLK, BLK), 0) & (BLK - 1)
    ki = lax.broadcasted_iota(jnp.int32, (rep * BLK, BLK), 1)
    mask_p = (ki > qi) & has_prev
    mask_c = ki <= qi
    lane1 = lax.broadcasted_iota(jnp.int32, (1, 128), 1)
    scale = ATTN_D ** -0.5
    outs = []
    for g in range(ATTN_KV):
        sl = slice(g * ATTN_D, (g + 1) * ATTN_D)
        kpn, kcn = _head_norm(kp[:, sl], kg), _head_norm(kc[:, sl], kg)
        heads = range(g * rep, (g + 1) * rep)
        qn = _head_norm(jnp.concatenate([q[:, h * ATTN_D:(h + 1) * ATTN_D] for h in heads], axis=0), qg)
        lp = jnp.where(mask_p, dot_nt(qn, kpn) * scale + bias[g][:, :BLK], NEG_INF)
        lc = jnp.where(mask_c, dot_nt(qn, kcn) * scale + bias[g][:, BLK:], NEG_INF)
        sink = jnp.concatenate([jnp.broadcast_to(jnp.sum(jnp.where(lane1 == h, sinks, 0.0), axis=1, keepdims=True), (BLK, 1))
                                for h in heads], axis=0)
        m = lax.stop_gradient(jnp.maximum(jnp.maximum(jnp.max(lp, axis=-1, keepdims=True), jnp.max(lc, axis=-1, keepdims=True)), sink))
        pp, pc = jnp.exp(lp - m), jnp.exp(lc - m)
        den = jnp.sum(pp, axis=-1, keepdims=True) + jnp.sum(pc, axis=-1, keepdims=True) + jnp.exp(sink - m)
        o = (dot_nn(pp, vp[:, sl]) + dot_nn(pc, vc[:, sl])) * (1.0 / den)
        outs += [o[r * BLK:(r + 1) * BLK] for r in range(rep)]
    return jnp.concatenate(outs, axis=1)


_SWA_BIAS_SHAPE = (ATTN_KV, ATTN_HEADS // ATTN_KV * BLK, 2 * BLK)
_SWA_BIAS_SPEC = pl.BlockSpec(_SWA_BIAS_SHAPE, lambda i: (0, 0, 0))


def _swa_in_specs():
    prev = lambda i: jnp.maximum(i - 1, 0)
    kw = ATTN_KV * ATTN_D
    return [
        pl.BlockSpec((BLK, D), lambda i: (i, C_Q // D)),
        pl.BlockSpec((BLK, kw), lambda i: (prev(i), C_K // kw)),
        pl.BlockSpec((BLK, kw), lambda i: (i, C_K // kw)),
        pl.BlockSpec((BLK, kw), lambda i: (prev(i), C_V // kw)),
        pl.BlockSpec((BLK, kw), lambda i: (i, C_V // kw)),
        pl.BlockSpec((1, ATTN_D), lambda i: (0, 0)),
        pl.BlockSpec((1, ATTN_D), lambda i: (0, 0)),
        pl.BlockSpec((1, 128), lambda i: (0, 0)),
        _SWA_BIAS_SPEC,
    ]


def swa_fwd(name, proj, qg, kg, sinks, bias, ex=None):
    t_len = proj.shape[0]
    nb = t_len // BLK
    h = _hosted(ex)

    def body(*refs):
        q_ref, kp_ref, kc_ref, vp_ref, vc_ref, qg_ref, kg_ref, s_ref, b_ref = refs[:9]
        o_ref = refs[9 + h.n_in]
        i = pl.program_id(0)
        h.at_start(refs, 9, 10 + h.n_in, i == 0)
        o = f_swa(i > 0, q_ref[...], kp_ref[...], kc_ref[...], vp_ref[...], vc_ref[...], qg_ref[...], kg_ref[...],
                  s_ref[...], *[b_ref[kv] for kv in range(ATTN_KV)])
        o_ref[...] = o.astype(o_ref.dtype)
        h.at_end(refs, 9, 10 + h.n_in, i == nb - 1)

    res = pl.pallas_call(
        body, name=name, grid=(nb,), in_specs=_swa_in_specs() + h.in_specs,
        out_specs=[pl.BlockSpec((BLK, D), lambda i: (i, 0))] + h.out_specs,
        out_shape=[jax.ShapeDtypeStruct((t_len, D), bf16)] + h.out_shape,
        scratch_shapes=h.scratch, input_output_aliases=h.aliases(9, 1),
        compiler_params=_params(("arbitrary",)),
    )(proj, proj, proj, proj, proj, qg, kg, sinks, bias.reshape(_SWA_BIAS_SHAPE), *h.ins)
    return res[0], list(res[1:])


def swa_bwd(name, proj, do, qg, kg, sinks, bias, ex=None):
    t_len = proj.shape[0]
    nb = t_len // BLK
    kw = ATTN_KV * ATTN_D
    h = _hosted(ex)

    def body(*refs):
        q_ref, kp_ref, kc_ref, vp_ref, vc_ref, qg_ref, kg_ref, s_ref, b_ref, do_ref = refs[:10]
        dq_ref, dkp_ref, dkc_ref, dvp_ref, dvc_ref, dqg_ref, dkg_ref, ds_ref, db_ref = refs[10 + h.n_in:19 + h.n_in]
        i = pl.program_id(0)
        h.at_start(refs, 10, 19 + h.n_in, i == 0)

        @pl.when(i == 0)
        def _():
            for r in (dqg_ref, dkg_ref, ds_ref, db_ref):
                r[...] = jnp.zeros(r.shape, f32)

        fn = functools.partial(f_swa, i > 0)
        _, vjp = jax.vjp(fn, q_ref[...], kp_ref[...], kc_ref[...], vp_ref[...], vc_ref[...], qg_ref[...], kg_ref[...],
                         s_ref[...], *[b_ref[kv] for kv in range(ATTN_KV)])
        ds = vjp(do_ref[...].astype(f32))
        dq_ref[...] = ds[0].astype(dq_ref.dtype)
        dkp_ref[...] = ds[1]
        dkc_ref[...] = ds[2]
        dvp_ref[...] = ds[3]
        dvc_ref[...] = ds[4]
        dqg_ref[...] += ds[5]
        dkg_ref[...] += ds[6]
        ds_ref[...] += ds[7]
        for kv in range(ATTN_KV):
            db_ref[kv] += ds[8 + kv]
        h.at_end(refs, 10, 19 + h.n_in, i == nb - 1)

    row = lambda w: pl.BlockSpec((BLK, w), lambda i: (i, 0))
    res = pl.pallas_call(
        body, name=name, grid=(nb,), in_specs=_swa_in_specs() + [row(D)] + h.in_specs,
        out_specs=[row(D), row(kw), row(kw), row(kw), row(kw),
                   pl.BlockSpec((1, ATTN_D), lambda i: (0, 0)), pl.BlockSpec((1, ATTN_D), lambda i: (0, 0)),
                   pl.BlockSpec((1, 128), lambda i: (0, 0)), _SWA_BIAS_SPEC] + h.out_specs,
        out_shape=[jax.ShapeDtypeStruct((t_len, D), bf16)] + [jax.ShapeDtypeStruct((t_len, kw), f32)] * 4
        + [jax.ShapeDtypeStruct((1, ATTN_D), f32)] * 2 + [jax.ShapeDtypeStruct((1, 128), f32),
                                                          jax.ShapeDtypeStruct(_SWA_BIAS_SHAPE, f32)] + h.out_shape,
        scratch_shapes=h.scratch, input_output_aliases=h.aliases(10, 9),
        compiler_params=_params(("arbitrary",)),
    )(proj, proj, proj, proj, proj, qg, kg, sinks, bias.reshape(_SWA_BIAS_SHAPE), do, *h.ins)
    return list(res[:8]) + [res[8].reshape(ATTN_HEADS, BLK, 2 * BLK)], list(res[9:])


def _bucket_table():
    qi = np.arange(BLK)[:, None] + BLK
    kj = np.arange(2 * BLK)[None, :]
    dist = qi - kj
    max_exact = REL_BUCKETS // 2
    d = np.maximum(dist, 1).astype(np.float32)
    large = max_exact + (np.log(d / max_exact) / math.log(128 / max_exact) * (REL_BUCKETS - max_exact)).astype(np.int32)
    large = np.minimum(large, REL_BUCKETS - 1)
    return np.where(dist < max_exact, np.maximum(dist, 0), large).astype(np.int32)


def relbias_fwd(table, bucket):
    def body(t_ref, bk_ref, o_ref):
        bk = bk_ref[...]
        for h in range(ATTN_HEADS):
            acc = jnp.zeros((BLK, 2 * BLK), f32)
            for b in range(REL_BUCKETS):
                acc = jnp.where(bk == b, t_ref[b, h], acc)
            o_ref[h] = acc

    return pl.pallas_call(
        body, name="relbias_fwd", out_shape=jax.ShapeDtypeStruct((ATTN_HEADS, BLK, 2 * BLK), f32),
        in_specs=[pl.BlockSpec(memory_space=pltpu.SMEM), pl.BlockSpec(memory_space=pltpu.VMEM)],
        out_specs=pl.BlockSpec(memory_space=pltpu.VMEM),
    )(table, bucket)


def relbias_bwd(dbias, bucket):
    n = len(dbias)

    def body(*refs):
        bk = refs[n][...]
        o_ref = refs[n + 1]
        row = lax.broadcasted_iota(jnp.int32, (REL_BUCKETS, 128), 0)
        lane = lax.broadcasted_iota(jnp.int32, (REL_BUCKETS, 128), 1)
        res = jnp.zeros((REL_BUCKETS, 128), f32)
        for h in range(ATTN_HEADS):
            tot = refs[0][h]
            for k in range(1, n):
                tot = tot + refs[k][h]
            for b in range(REL_BUCKETS):
                part = jnp.sum(jnp.sum(jnp.where(bk == b, tot, 0.0), axis=1, keepdims=True), axis=0, keepdims=True)
                res = jnp.where((row == b) & (lane == h), part, res)
        o_ref[...] = res

    return pl.pallas_call(
        body, name="relbias_bwd", out_shape=jax.ShapeDtypeStruct((REL_BUCKETS, 128), f32),
        in_specs=[pl.BlockSpec(memory_space=pltpu.VMEM)] * (n + 1),
        out_specs=pl.BlockSpec(memory_space=pltpu.VMEM),
    )(*dbias, bucket)[:, :ATTN_HEADS]


XA_TILE = 512


def f_xattn(q, k, v, qg, kg):
    qn, kn = _head_norm(q, qg), _head_norm(k, kg)
    logits = dot_nt(qn, kn) * (XD ** -0.5)
    p = jnp.exp(logits - lax.stop_gradient(jnp.max(logits, axis=-1, keepdims=True)))
    return dot_nn(p * (1.0 / jnp.sum(p, axis=-1, keepdims=True)), v)


def _xa_in_specs(tq):
    return [
        pl.BlockSpec((tq, XD), lambda h, i: (i, h)),
        pl.BlockSpec((MEM_LEN, XD), lambda h, i: (0, h)),
        pl.BlockSpec((MEM_LEN, XD), lambda h, i: (0, XH + h)),
        pl.BlockSpec((1, XD), lambda h, i: (0, 0)),
        pl.BlockSpec((1, XD), lambda h, i: (0, 0)),
    ]


def xa_fwd(name, xq, kv, qg, kg):
    t_len = xq.shape[0]
    tq = min(XA_TILE, t_len)

    def body(q_ref, k_ref, v_ref, qg_ref, kg_ref, o_ref):
        o_ref[...] = f_xattn(q_ref[...], k_ref[...], v_ref[...], qg_ref[...], kg_ref[...]).astype(o_ref.dtype)

    return pl.pallas_call(
        body, name=name, grid=(XH, t_len // tq), in_specs=_xa_in_specs(tq),
        out_specs=pl.BlockSpec((tq, XD), lambda h, i: (i, h)), out_shape=jax.ShapeDtypeStruct((t_len, D), bf16),
        compiler_params=_params(("parallel", "parallel")),
    )(xq, kv, kv, qg, kg)


def xa_bwd(name, xq, kv, do, qg, kg):
    t_len = xq.shape[0]
    tq = min(XA_TILE, t_len)

    def body(q_ref, k_ref, v_ref, qg_ref, kg_ref, do_ref, dq_ref, dk_ref, dv_ref, dqg_ref, dkg_ref):
        h, i = pl.program_id(0), pl.program_id(1)

        @pl.when(i == 0)
        def _():
            dk_ref[...] = jnp.zeros(dk_ref.shape, f32)
            dv_ref[...] = jnp.zeros(dv_ref.shape, f32)

        @pl.when((i == 0) & (h == 0))
        def _():
            dqg_ref[...] = jnp.zeros(dqg_ref.shape, f32)
            dkg_ref[...] = jnp.zeros(dkg_ref.shape, f32)

        _, vjp = jax.vjp(f_xattn, q_ref[...], k_ref[...], v_ref[...], qg_ref[...], kg_ref[...])
        dq, dk, dv, dqg, dkg = vjp(do_ref[...].astype(f32))
        dq_ref[...] = dq.astype(dq_ref.dtype)
        dk_ref[...] += dk
        dv_ref[...] += dv
        dqg_ref[...] += dqg
        dkg_ref[...] += dkg

    return pl.pallas_call(
        body, name=name, grid=(XH, t_len // tq),
        in_specs=_xa_in_specs(tq) + [pl.BlockSpec((tq, XD), lambda h, i: (i, h))],
        out_specs=[pl.BlockSpec((tq, XD), lambda h, i: (i, h)),
                   pl.BlockSpec((MEM_LEN, XD), lambda h, i: (0, h)), pl.BlockSpec((MEM_LEN, XD), lambda h, i: (0, h)),
                   pl.BlockSpec((1, XD), lambda h, i: (0, 0)), pl.BlockSpec((1, XD), lambda h, i: (0, 0))],
        out_shape=[jax.ShapeDtypeStruct((t_len, D), bf16), jax.ShapeDtypeStruct((MEM_LEN, D), f32),
                   jax.ShapeDtypeStruct((MEM_LEN, D), f32), jax.ShapeDtypeStruct((1, XD), f32), jax.ShapeDtypeStruct((1, XD), f32)],
        compiler_params=_params(("arbitrary", "arbitrary")),
    )(xq, kv, kv, qg, kg, do)


def loss_head(y, target, tile):
    t_len = y.shape[0]

    def body(y_ref, t_ref, dy_ref, l_ref):
        i = pl.program_id(0)

        @pl.when(i == 0)
        def _():
            l_ref[...] = jnp.zeros(l_ref.shape, f32)

        err = y_ref[...] - t_ref[...]
        dy_ref[...] = err * (1.0 / D)
        l_ref[...] += 0.5 * jnp.sum(jnp.sum(err * err, axis=1, keepdims=True), axis=0, keepdims=True) * (1.0 / D)

    row = pl.BlockSpec((tile, D), lambda i: (i, 0))
    return pl.pallas_call(
        body, name="loss_head", grid=(t_len // tile,), in_specs=[row, row],
        out_specs=[row, pl.BlockSpec((8, 128), lambda i: (0, 0))],
        out_shape=[jax.ShapeDtypeStruct((t_len, D), f32), jax.ShapeDtypeStruct((8, 128), f32)],
        compiler_params=_params(("arbitrary",)),
    )(y, target)


ADAM_BLOCK_ELEMS = 512 * 1024


def adamw(name, w, g, m, v):
    rows, cols = w.shape
    tile = rows
    if rows * cols > ADAM_BLOCK_ELEMS:
        tile = _pick(rows, [t for t in (512, 256, 128, 64, 32, 16, 8) if t * cols <= ADAM_BLOCK_ELEMS])

    def body(w_ref, g_ref, m_ref, v_ref, d_ref, nm_ref, nv_ref):
        gg = g_ref[...]
        nm = ADAM_B1 * m_ref[...] + (1.0 - ADAM_B1) * gg
        nv = ADAM_B2 * v_ref[...] + (1.0 - ADAM_B2) * jnp.square(gg)
        m_hat = nm / (1.0 - ADAM_B1 ** ADAM_STEP)
        v_hat = nv / (1.0 - ADAM_B2 ** ADAM_STEP)
        d_ref[...] = -ADAM_LR * (m_hat / (jnp.sqrt(v_hat) + ADAM_EPS) + ADAM_WD * w_ref[...])
        nm_ref[...] = nm
        nv_ref[...] = nv

    spec = pl.BlockSpec((tile, cols), lambda i: (i, 0))
    return pl.pallas_call(
        body, name=name, grid=(rows // tile,), in_specs=[spec] * 4, out_specs=[spec] * 3,
        out_shape=[jax.ShapeDtypeStruct((rows, cols), f32)] * 3, compiler_params=_params(("parallel",)),
    )(w, g, m, v)


def adamw_layers(name, w, m, v, gsrcs, r0):
    depth, rows, cols = w.shape
    tile = max(t for t in range(8, rows + 1, 8) if rows % t == 0 and r0 % t == 0 and t * cols <= ADAM_BLOCK_ELEMS)

    def body(*refs):
        w_ref, m_ref, v_ref = refs[:3]
        g_refs = refs[3:3 + depth]
        go_ref, d_ref, nm_ref, nv_ref = refs[3 + depth:]
        layer = pl.program_id(0)
        gg = g_refs[0][...]
        for k in range(1, depth):
            gg = jnp.where(layer == k, g_refs[k][...], gg)
        nm = ADAM_B1 * m_ref[0] + (1.0 - ADAM_B1) * gg
        nv = ADAM_B2 * v_ref[0] + (1.0 - ADAM_B2) * jnp.square(gg)
        m_hat = nm / (1.0 - ADAM_B1 ** ADAM_STEP)
        v_hat = nv / (1.0 - ADAM_B2 ** ADAM_STEP)
        go_ref[0] = gg
        d_ref[0] = -ADAM_LR * (m_hat / (jnp.sqrt(v_hat) + ADAM_EPS) + ADAM_WD * w_ref[0])
        nm_ref[0] = nm
        nv_ref[0] = nv

    spec = pl.BlockSpec((1, tile, cols), lambda l, i: (l, i, 0))
    g_specs = [pl.BlockSpec((tile, cols), functools.partial(lambda l, i, k: (jnp.where(l == k, r0 // tile + i, 0), 0), k=k))
               for k in range(depth)]
    return pl.pallas_call(
        body, name=name, grid=(depth, rows // tile), in_specs=[spec] * 3 + g_specs, out_specs=[spec] * 4,
        out_shape=[jax.ShapeDtypeStruct(w.shape, f32)] * 4, compiler_params=_params(("parallel", "parallel")),
    )(w, m, v, *gsrcs)


MESH = pl.DeviceIdType.MESH
HBM_SPEC = pl.BlockSpec(memory_space=pltpu.HBM)


def _place():
    x, y, c = lax.axis_index("x"), lax.axis_index("y"), lax.axis_index("c")
    chips = [(1 - x, y), (x, 1 - y), (1 - x, 1 - y)]
    return x, y, c, chips


def _rcopy(src, dst, send_sems, recv_sems, k, to):
    return pltpu.make_async_remote_copy(src_ref=src, dst_ref=dst, send_sem=send_sems.at[k], recv_sem=recv_sems.at[k],
                                        device_id=to, device_id_type=MESH)


class Exchange:
    def __init__(self, ins, outs, n_sems, n_local, start, wait, aliases=None):
        self.ins, self.outs, self.n_sems, self.n_local = list(ins), list(outs), n_sems, n_local
        self.start, self.wait, self.aliases = start, wait, dict(aliases or {})


class _hosted:
    def __init__(self, ex):
        self.ex = ex
        self.ins = list(ex.ins) if ex else []
        self.out_shape = list(ex.outs) if ex else []
        self.n_in, self.n_out = len(self.ins), len(self.out_shape)
        self.in_specs, self.out_specs = [HBM_SPEC] * self.n_in, [HBM_SPEC] * self.n_out
        self.scratch = [pltpu.SemaphoreType.DMA((ex.n_sems,)), pltpu.SemaphoreType.DMA((ex.n_sems,)),
                        pltpu.SemaphoreType.DMA((max(ex.n_local, 1),))] if ex else []

    def aliases(self, first_in, first_out):
        return {first_in + a: first_out + b for a, b in self.ex.aliases.items()} if self.ex else {}

    def _args(self, refs, i0, o0):
        return refs[i0:i0 + self.n_in], refs[o0:o0 + self.n_out], refs[-3], refs[-2], refs[-1]

    def at_start(self, refs, i0, o0, pred):
        if self.ex is not None:
            pl.when(pred)(lambda: self.ex.start(*self._args(refs, i0, o0)))

    def at_end(self, refs, i0, o0, pred):
        if self.ex is not None:
            pl.when(pred)(lambda: self.ex.wait(*self._args(refs, i0, o0)))


def run_exchange(name, ex):
    h = _hosted(ex)

    def body(*refs):
        args = h._args(refs, 0, h.n_in)
        ex.start(*args)
        ex.wait(*args)

    res = pl.pallas_call(
        body, name=name, out_shape=h.out_shape, in_specs=h.in_specs, out_specs=h.out_specs, scratch_shapes=h.scratch,
        input_output_aliases=h.aliases(0, 0), compiler_params=pltpu.CompilerParams(has_side_effects=True),
    )(*h.ins)
    return list(res)


def _halves(ref, axis, c):
    hh = ref.shape[axis] // 2
    return pl.ds(pl.multiple_of(c * hh, 16), hh), pl.ds(pl.multiple_of((1 - c) * hh, 16), hh)


def ex_gather_ici(shards):
    n = len(shards)

    def copies(ins, outs, ssem, rsem, landing):
        x, y, c, chips = _place()
        me = 2 * x + y
        res = []
        for a, (s, o) in enumerate(zip(ins, outs)):
            mine, _ = _halves(s, 0, c)
            for j, (cx, cy) in enumerate(chips[:2]):
                slot = 2 * cx + cy if landing else me
                res.append(_rcopy(s.at[mine], o.at[slot, mine], ssem, rsem, 3 * a + j, (cx, cy, c)))
            res.append(_rcopy(s, o.at[me], ssem, rsem, 3 * a + 2, (x, y, 1 - c)))
        return res

    def start(ins, outs, ssem, rsem, lsem):
        for cp in copies(ins, outs, ssem, rsem, False):
            cp.start()

    def wait(ins, outs, ssem, rsem, lsem):
        for cp in copies(ins, outs, ssem, rsem, True):
            cp.wait_recv()
        for cp in copies(ins, outs, ssem, rsem, False):
            cp.wait_send()

    outs = [jax.ShapeDtypeStruct((N_CHIPS,) + s.shape, s.dtype) for s in shards]
    return Exchange(shards, outs, 3 * n, 0, start, wait)


def ex_gather_d2d(gathered):
    n = len(gathered)

    def copies(outs, ssem, rsem, landing):
        x, y, c, chips = _place()
        res = []
        for a, o in enumerate(outs):
            mine, theirs = _halves(o, 1, c)
            for j, (cx, cy) in enumerate(chips[:2]):
                rows = o.at[2 * cx + cy, theirs if landing else mine]
                res.append(_rcopy(rows, rows, ssem, rsem, 3 * a + j, (x, y, 1 - c)))
            fx, fy = lax.rem(x + 1 - c, 2), lax.rem(y + c, 2)
            tx, ty = lax.rem(x + c, 2), lax.rem(y + 1 - c, 2)
            slot = 2 * (1 - x) + (1 - y) if landing else 2 * fx + fy
            rows = o.at[slot, mine]
            res.append(_rcopy(rows, rows, ssem, rsem, 3 * a + 2, (tx, ty, c)))
        return res

    def start(ins, outs, ssem, rsem, lsem):
        for cp in copies(outs, ssem, rsem, False):
            cp.start()

    def wait(ins, outs, ssem, rsem, lsem):
        for cp in copies(outs, ssem, rsem, True):
            cp.wait_recv()
        for cp in copies(outs, ssem, rsem, False):
            cp.wait_send()

    outs = [jax.ShapeDtypeStruct(g.shape, g.dtype) for g in gathered]
    return Exchange(gathered, outs, 3 * n, 0, start, wait, aliases={a: a for a in range(n)})


def ex_gather_diag(gathered):
    n = len(gathered)

    def copies(outs, ssem, rsem, landing):
        x, y, c, _ = _place()
        res = []
        for a, o in enumerate(outs):
            mine, theirs = _halves(o, 1, c)
            rows = o.at[2 * (1 - x) + (1 - y), theirs if landing else mine]
            res.append(_rcopy(rows, rows, ssem, rsem, a, (x, y, 1 - c)))
        return res

    def start(ins, outs, ssem, rsem, lsem):
        for cp in copies(outs, ssem, rsem, False):
            cp.start()

    def wait(ins, outs, ssem, rsem, lsem):
        for cp in copies(outs, ssem, rsem, True):
            cp.wait_recv()
        for cp in copies(outs, ssem, rsem, False):
            cp.wait_send()

    outs = [jax.ShapeDtypeStruct(g.shape, g.dtype) for g in gathered]
    return Exchange(gathered, outs, n, 0, start, wait, aliases={a: a for a in range(n)})


def ex_swap_halves(gs):
    n = len(gs)

    def copies(ins, outs, ssem, rsem):
        x, y, c, _ = _place()
        return [_rcopy(g.at[s, 1 - c], o.at[s], ssem, rsem, N_CHIPS * a + s, (x, y, 1 - c))
                for a, (g, o) in enumerate(zip(ins, outs)) for s in range(N_CHIPS)]

    def start(ins, outs, ssem, rsem, lsem):
        for cp in copies(ins, outs, ssem, rsem):
            cp.start()

    def wait(ins, outs, ssem, rsem, lsem):
        for cp in copies(ins, outs, ssem, rsem):
            cp.wait()

    outs = [jax.ShapeDtypeStruct((N_CHIPS,) + g.shape[2:], g.dtype) for g in gs]
    return Exchange(gs, outs, N_CHIPS * n, 0, start, wait)


def ex_scatter(ps):
    n = len(ps)

    def copies(ins, outs, ssem, rsem):
        x, y, c, chips = _place()
        return [_rcopy(p.at[2 * cx + cy], o.at[j], ssem, rsem, 3 * a + j, (cx, cy, c))
                for a, (p, o) in enumerate(zip(ins, outs)) for j, (cx, cy) in enumerate(chips)]

    def start(ins, outs, ssem, rsem, lsem):
        for cp in copies(ins, outs, ssem, rsem):
            cp.start()

    def wait(ins, outs, ssem, rsem, lsem):
        for cp in copies(ins, outs, ssem, rsem):
            cp.wait()

    outs = [jax.ShapeDtypeStruct((3,) + p.shape[1:], p.dtype) for p in ps]
    return Exchange(ps, outs, 3 * n, 0, start, wait)


def ex_share(rs):
    n = len(rs)

    def copies(ins, outs, ssem, rsem):
        x, y, c, _ = _place()
        return [_rcopy(r, o, ssem, rsem, a, (x, y, 1 - c)) for a, (r, o) in enumerate(zip(ins, outs))]

    def start(ins, outs, ssem, rsem, lsem):
        for cp in copies(ins, outs, ssem, rsem):
            cp.start()

    def wait(ins, outs, ssem, rsem, lsem):
        for cp in copies(ins, outs, ssem, rsem):
            cp.wait()

    outs = [jax.ShapeDtypeStruct(r.shape, r.dtype) for r in rs]
    return Exchange(rs, outs, n, 0, start, wait)


ADD_BLOCK_ELEMS = 384 * 1024


def _add_tile(rows, cols):
    return _pick(rows, [t for t in (1920, 960, 512, 384, 256, 128, 64, 32, 16) if t * cols <= ADD_BLOCK_ELEMS])


def add_halves(name, g, got, c_arr):
    _, _, rows, cols = g.shape
    tile = _add_tile(rows, cols)

    def body(c_ref, g_ref, r_ref, o_ref):
        o_ref[...] = (g_ref[0] + r_ref[...]).astype(o_ref.dtype)

    return pl.pallas_call(
        body, name=name,
        grid_spec=pltpu.PrefetchScalarGridSpec(
            num_scalar_prefetch=1, grid=(N_CHIPS, rows // tile),
            in_specs=[pl.BlockSpec((1, 1, tile, cols), lambda s, i, c_ref: (s, c_ref[0], i, 0)),
                      pl.BlockSpec((1, tile, cols), lambda s, i, c_ref: (s, i, 0))],
            out_specs=pl.BlockSpec((1, tile, cols), lambda s, i, c_ref: (s, i, 0))),
        out_shape=jax.ShapeDtypeStruct(got.shape, bf16), compiler_params=_params(("parallel", "parallel")),
    )(c_arr, g, got)


def add_partials(name, p, got, me_arr):
    _, rows, cols = p.shape
    tile = _add_tile(rows, cols)

    def body(me_ref, p_ref, r_ref, o_ref):
        o_ref[...] = ((p_ref[0].astype(f32) + r_ref[0].astype(f32)) + r_ref[1].astype(f32)) + r_ref[2].astype(f32)

    return pl.pallas_call(
        body, name=name,
        grid_spec=pltpu.PrefetchScalarGridSpec(
            num_scalar_prefetch=1, grid=(rows // tile,),
            in_specs=[pl.BlockSpec((1, tile, cols), lambda i, me_ref: (me_ref[0], i, 0)),
                      pl.BlockSpec((3, tile, cols), lambda i, me_ref: (0, i, 0))],
            out_specs=pl.BlockSpec((tile, cols), lambda i, me_ref: (i, 0))),
        out_shape=jax.ShapeDtypeStruct((rows, cols), f32), compiler_params=_params(("parallel",)),
    )(me_arr, p, got)


def join_halves(name, mine, theirs):
    rows, cols = mine.shape
    tile = _add_tile(rows, cols)

    def body(m_ref, t_ref, o_ref):
        south = lax.axis_index("c") == 0
        o_ref[0] = jnp.where(south, m_ref[...], t_ref[...])
        o_ref[1] = jnp.where(south, t_ref[...], m_ref[...])

    spec = pl.BlockSpec((tile, cols), lambda i: (i, 0))
    return pl.pallas_call(
        body, name=name, grid=(rows // tile,), in_specs=[spec, spec],
        out_specs=pl.BlockSpec((2, tile, cols), lambda i: (0, i, 0)),
        out_shape=jax.ShapeDtypeStruct((2, rows, cols), f32), compiler_params=_params(("parallel",)),
    )(mine, theirs)


N_DEV = 8


def allreduce_small(name, buf):
    m_per = buf.shape[0]

    def body(x_ref, out_ref, all_ref, send_sems, recv_sems, local_sem):
        x, y, c, chips = _place()
        me, sibling = (x, y, c), (x, y, 1 - c)

        def rows(px, py, pc):
            return all_ref.at[pl.ds(pl.multiple_of((4 * px + 2 * py + pc) * m_per, 8), m_per), :]

        def copy(k, block, to, src=None):
            return _rcopy(rows(*block) if src is None else src, rows(*block), send_sems, recv_sems, k, to)

        mine = pltpu.make_async_copy(x_ref, rows(*me), local_sem)
        mine.start()
        first = [copy(0, me, sibling, src=x_ref)]
        first += [copy(1 + j, me, (*chip, c), src=x_ref) for j, chip in enumerate(chips)]
        for cp in first:
            cp.start()
        passed = [copy(4 + j, (*chip, c), sibling) for j, chip in enumerate(chips)]
        for j, chip in enumerate(chips):
            copy(1 + j, (*chip, c), me).wait_recv()
            passed[j].start()
        copy(0, sibling, me).wait_recv()
        for j, chip in enumerate(chips):
            copy(4 + j, (*chip, 1 - c), me).wait_recv()
        for cp in first + passed:
            cp.wait_send()
        mine.wait()
        tot = all_ref[pl.ds(0, m_per), :]
        for d in range(1, N_DEV):
            tot = tot + all_ref[pl.ds(d * m_per, m_per), :]
        out_ref[...] = tot

    return pl.pallas_call(
        body, name=name, out_shape=jax.ShapeDtypeStruct((m_per, 128), f32),
        in_specs=[pl.BlockSpec(memory_space=pltpu.VMEM)], out_specs=pl.BlockSpec(memory_space=pltpu.VMEM),
        scratch_shapes=[pltpu.VMEM((N_DEV * m_per, 128), f32), pltpu.SemaphoreType.DMA((7,)), pltpu.SemaphoreType.DMA((7,)),
                        pltpu.SemaphoreType.DMA],
        compiler_params=pltpu.CompilerParams(has_side_effects=True, vmem_limit_bytes=VMEM_LIMIT_BYTES),
    )(buf)


def pack_shards(ws, layer):
    wide = jnp.concatenate([ws[n][layer].astype(bf16) for n in WIDE], axis=0)
    w_in_t = jnp.pad(jnp.transpose(ws["w_in"][layer]).astype(bf16), ((0, IN_PAD - IN_SHARD), (0, 0)))
    return [w_in_t, wide, ws["w_xkv"][layer].astype(bf16)]


def arrange_w_in(shards):
    parts = []
    for a, b, _ in sorted(ORIG_SEGS, key=lambda seg: seg[2]):
        for s in range(N_CHIPS):
            lo, hi = max(a, s * IN_SHARD), min(b, (s + 1) * IN_SHARD)
            if lo < hi:
                parts.append(shards[s][lo - s * IN_SHARD:hi - s * IN_SHARD])
    parts.append(jnp.zeros((PCOLS - C_DT - (OFF_Q - OFF_DT), shards.shape[2]), shards.dtype))
    return jnp.concatenate(parts, axis=0)


def shard_w_in_grad(g):
    out = []
    for s in range(N_CHIPS):
        for a, b, first in ORIG_SEGS:
            lo, hi = max(a, s * IN_SHARD), min(b, (s + 1) * IN_SHARD)
            if lo < hi:
                out.append(g[first + lo - a:first + hi - a])
        out.append(jnp.zeros((IN_PAD - IN_SHARD, g.shape[1]), g.dtype))
    return jnp.concatenate(out, axis=0).reshape(N_CHIPS, IN_PAD, g.shape[1])


def unpack_gathered(gathered):
    g_in, g_wide, g_xkv = gathered
    out = dict(w_in=arrange_w_in(g_in), w_xkv=jnp.concatenate([g_xkv[s] for s in range(N_CHIPS)], axis=1))
    for n in WIDE:
        piece = g_wide[:, WIDE_OFF[n]:WIDE_OFF[n] + WIDE_ROWS[n], :]
        if n == "w_mlp_up":
            out[n] = jnp.concatenate([piece[s] for s in range(N_CHIPS)], axis=1)
        else:
            out[n] = piece.reshape(N_CHIPS * WIDE_ROWS[n], D)
    return out


def _halved(a):
    return a.reshape(N_CHIPS, 2, a.shape[1] // 2, a.shape[2])


def pack_w_in_grad(g):
    return [_halved(shard_w_in_grad(g))]


def pack_other_grads(gs):
    cols = lambda g, q: jnp.concatenate([g[:, s * q:(s + 1) * q] for s in range(N_CHIPS)], axis=0).reshape(N_CHIPS, g.shape[0], q)
    wide = [cols(gs[n], D) if n == "w_mlp_up" else gs[n].reshape(N_CHIPS, WIDE_ROWS[n], D) for n in WIDE]
    return [_halved(jnp.concatenate(wide, axis=1)), _halved(cols(gs["w_xkv"], 2048 // N_CHIPS))]


ROW_TILE = 256


def _pad128(v):
    return jnp.pad(v.reshape(1, -1), ((0, 0), (0, 128 - v.shape[-1])))


def layer_forward(i, h, mem, w, p, bias, next_shards=None):
    tile = min(ROW_TILE, h.shape[0])
    s = dict(h=h)
    row = lambda a: a.reshape(1, -1)
    s["u"], = tile_fwd(f"rms_mix{i}", f_rms, [(h, D, 0)], [row(p["norm_mix"])], [(D, bf16)], tile)
    s["proj"] = proj = matmul(s["u"], w["w_in"], "nt", f"mm_in{i}")
    s["c1"], = conv_fwd(f"conv_a{i}", proj, C_CONV, C_CONV + D, p["conv_dw_w"], row(p["conv_dw_b"]), True, False)
    s["ca"], = tile_fwd(f"ln_silu{i}", f_ln_silu, [(s["c1"], D, 0)], [row(p["conv_ln_g"]), row(p["conv_ln_b"])], [(D, bf16)], tile)
    s["ya"] = matmul(s["ca"], w["w_conv_out"], "nn", f"mm_conv_out{i}")
    s["xpre"], s["xact"] = conv_fwd(f"conv_b{i}", proj, C_XBC, C_XBC, p["ssd_conv_w"], row(p["ssd_conv_b"]), False, True)
    ssd_p = (_pad128(p["ssd_dt_bias"]), _pad128(p["ssd_A_log"]), _pad128(p["ssd_D"]), row(p["ssd_norm_g"]))
    s["yB"], s["states"], landed = ssd_fwd(f"ssd_fwd{i}", s["xact"], proj, *ssd_p,
                                           ex=ex_gather_ici(next_shards) if next_shards else None)
    s["yb"] = matmul(s["yB"], w["w_ssd_out"], "nn", f"mm_ssd_out{i}")
    swa_p = (row(p["attn_q_norm"]), row(p["attn_k_norm"]), _pad128(p["attn_sinks"]))
    s["oc"], gathered = swa_fwd(f"swa_fwd{i}", proj, *swa_p, bias, ex=ex_gather_d2d(landed) if next_shards else None)
    if next_shards:
        gathered = run_exchange(f"gather_diag{i + 1}", ex_gather_diag(gathered))
    s["yc"] = matmul(s["oc"], w["w_attn_out"], "nn", f"mm_attn_out{i}")
    gate_cols = [(proj, D, C_G // D + k) for k in range(3)]
    s["merged"], = tile_fwd(f"merge{i}", f_merge, gate_cols + [(s["ya"], D, 0), (s["yb"], D, 0), (s["yc"], D, 0)],
                            [row(p["gate_bias"])], [(D, bf16)], tile)
    s["h1"] = h1 = matmul(s["merged"], w["w_mix_out"], "nn", f"mm_mix_out{i}", residual=h)
    s["hx"], = tile_fwd(f"rms_x{i}", f_rms, [(h1, D, 0)], [row(p["norm_xattn"])], [(D, bf16)], tile)
    s["memh"], = tile_fwd(f"rms_mem{i}", f_rms, [(mem, D, 0)], [row(p["norm_mem"])], [(D, bf16)], MEM_LEN)
    s["xq"] = matmul(s["hx"], w["w_xq"], "nn", f"mm_xq{i}")
    s["kv"] = matmul(s["memh"], w["w_xkv"], "nn", f"mm_xkv{i}")
    s["xo"] = xa_fwd(f"xa_fwd{i}", s["xq"], s["kv"], row(p["xattn_q_norm"]), row(p["xattn_k_norm"]))
    s["h2"] = h2 = matmul(s["xo"], w["w_xo"], "nn", f"mm_xo{i}", residual=h1)
    s["um"], = tile_fwd(f"rms_mlp{i}", f_rms, [(h2, D, 0)], [row(p["norm_mlp"])], [(D, bf16)], tile)
    s["up"], s["act"] = matmul(s["um"], w["w_mlp_up"], "nn", f"mm_up{i}", relu2="fwd")
    h3 = matmul(s["act"], w["w_mlp_down"], "nn", f"mm_down{i}", residual=h2)
    return h3, s, gathered


def layer_backward(i, dh3, mem, w, p, bias, s, pending=None, c_arr=None):
    t_len = dh3.shape[0]
    tile = min(ROW_TILE, t_len)
    row = lambda a: a.reshape(1, -1)
    gw, gp = {}, {}
    dup = matmul(dh3, w["w_mlp_down"], "nt", f"mm_down_dx{i}", out_dtype=bf16, relu2=s["up"])
    gw["w_mlp_down"] = matmul(s["act"], dh3, "tn", f"mm_down_dw{i}")
    gw["w_mlp_up"] = matmul(s["um"], dup, "tn", f"mm_up_dw{i}")
    dum = matmul(dup, w["w_mlp_up"], "nt", f"mm_up_dx{i}")
    (dh2,), (g,) = tile_bwd(f"rms_mlp_bwd{i}", f_rms, [(s["h2"], D, 0)], [row(p["norm_mlp"])], [dum], [f32], tile, addend=dh3)
    gp["norm_mlp"] = g
    dxo = matmul(dh2, w["w_xo"], "nt", f"mm_xo_dx{i}")
    gw["w_xo"] = matmul(s["xo"], dh2, "tn", f"mm_xo_dw{i}")
    dxq, dk, dv, gp["xattn_q_norm"], gp["xattn_k_norm"] = xa_bwd(f"xa_bwd{i}", s["xq"], s["kv"], dxo, row(p["xattn_q_norm"]),
                                                                 row(p["xattn_k_norm"]))
    dkv = jnp.concatenate([dk, dv], axis=1)
    gw["w_xq"] = matmul(s["hx"], dxq, "tn", f"mm_xq_dw{i}")
    dhx = matmul(dxq, w["w_xq"], "nt", f"mm_xq_dx{i}")
    gw["w_xkv"] = matmul(s["memh"], dkv, "tn", f"mm_xkv_dw{i}")
    dmemh = matmul(dkv, w["w_xkv"], "nt", f"mm_xkv_dx{i}")
    _, (g,) = tile_bwd(f"rms_mem_bwd{i}", f_rms, [(mem, D, 0)], [row(p["norm_mem"])], [dmemh], [f32], MEM_LEN)
    gp["norm_mem"] = g
    (dh1,), (g,) = tile_bwd(f"rms_x_bwd{i}", f_rms, [(s["h1"], D, 0)], [row(p["norm_xattn"])], [dhx], [f32], tile, addend=dh2)
    gp["norm_xattn"] = g
    proj = s["proj"]
    dmerged = matmul(dh1, w["w_mix_out"], "nt", f"mm_mix_out_dx{i}")
    gw["w_mix_out"] = matmul(s["merged"], dh1, "tn", f"mm_mix_out_dw{i}")
    gate_cols = [(proj, D, C_G // D + k) for k in range(3)]
    (dpa, dpb, dpc, dya, dyb, dyc), (g,) = tile_bwd(
        f"merge_bwd{i}", f_merge, gate_cols + [(s["ya"], D, 0), (s["yb"], D, 0), (s["yc"], D, 0)], [row(p["gate_bias"])],
        [dmerged], [bf16] * 6, tile)
    gp["gate_bias"] = g
    dca = matmul(dya, w["w_conv_out"], "nt", f"mm_conv_out_dx{i}")
    gw["w_conv_out"] = matmul(s["ca"], dya, "tn", f"mm_conv_out_dw{i}")
    (dc1,), (gp["conv_ln_g"], gp["conv_ln_b"]) = tile_bwd(
        f"ln_silu_bwd{i}", f_ln_silu, [(s["c1"], D, 0)], [row(p["conv_ln_g"]), row(p["conv_ln_b"])], [dca], [f32], tile)
    da, dgate, gp["conv_dw_w"], gp["conv_dw_b"] = conv_bwd(f"conv_a_bwd{i}", dc1, None, proj, C_CONV, C_CONV + D,
                                                           p["conv_dw_w"], True, False, bf16)
    dyB = matmul(dyb, w["w_ssd_out"], "nt", f"mm_ssd_out_dx{i}")
    gw["w_ssd_out"] = matmul(s["yB"], dyb, "tn", f"mm_ssd_out_dw{i}")
    doc = matmul(dyc, w["w_attn_out"], "nt", f"mm_attn_out_dx{i}")
    gw["w_attn_out"] = matmul(s["oc"], dyc, "tn", f"mm_attn_out_dw{i}")
    moving = list(pending or []) + pack_other_grads(gw)
    swa_p = (row(p["attn_q_norm"]), row(p["attn_k_norm"]), _pad128(p["attn_sinks"]))
    (dq, dkp, dkc, dvp, dvc, gp["attn_q_norm"], gp["attn_k_norm"], g, dbias), swapped = swa_bwd(
        f"swa_bwd{i}", proj, doc, *swa_p, bias, ex=ex_swap_halves(moving))
    gp["attn_sinks"] = g[:, :ATTN_HEADS]
    shift = lambda a: jnp.concatenate([a[BLK:], jnp.zeros((BLK, a.shape[1]), a.dtype)], axis=0)
    dk_ = (dkc + shift(dkp)).astype(bf16)
    dv_ = (dvc + shift(dvp)).astype(bf16)
    partials = [add_halves(f"add_halves{i}_{k}", g_, got, c_arr) for k, (g_, got) in enumerate(zip(moving, swapped))]
    ssd_p = (_pad128(p["ssd_dt_bias"]), _pad128(p["ssd_A_log"]), _pad128(p["ssd_D"]), row(p["ssd_norm_g"]))
    (dxs, dbm, dcm, dz, ddt4, g1, g2, g3, g4), arrived = ssd_bwd(f"ssd_bwd{i}", s["xact"], proj, s["states"], dyB, *ssd_p,
                                                                  ex=ex_scatter(partials))
    gp["ssd_dt_bias"], gp["ssd_A_log"], gp["ssd_D"], gp["ssd_norm_g"] = g1[:, :SSD_HEADS], g2[:, :SSD_HEADS], g3[:, :SSD_HEADS], g4
    ddt = (ddt4[0] + ddt4[1] + ddt4[2] + ddt4[3]).astype(bf16)
    dxact = jnp.concatenate([dxs, dbm, dcm], axis=1)
    dxbc, gp["ssd_conv_w"], gp["ssd_conv_b"] = conv_bwd(f"conv_b_bwd{i}", dxact, s["xpre"], proj, C_XBC, C_XBC, p["ssd_conv_w"],
                                                        False, True, bf16)
    dproj = jnp.concatenate([da, dgate, dz, dxbc, dq, dpa, dpb, dpc, dk_, dv_, ddt, jnp.zeros((t_len, PCOLS - C_DT - 128), bf16)], axis=1)
    gw["w_in"] = matmul(dproj, s["u"], "tn", f"mm_in_dw{i}")
    du = matmul(dproj, w["w_in"], "nn", f"mm_in_dx{i}")
    (dh,), (g,) = tile_bwd(f"rms_mix_bwd{i}", f_rms, [(s["h"], D, 0)], [row(p["norm_mix"])], [du], [f32], tile, addend=dh1)
    gp["norm_mix"] = g
    return dh, gw, gp, dbias, (partials, arrived)


SMALL = ["rel_table", "norm_mix", "gate_bias", "conv_dw_w", "conv_dw_b", "conv_ln_g", "conv_ln_b", "ssd_conv_w", "ssd_conv_b",
         "ssd_dt_bias", "ssd_A_log", "ssd_D", "ssd_norm_g", "attn_q_norm", "attn_k_norm", "attn_sinks", "norm_xattn", "norm_mem",
         "xattn_q_norm", "xattn_k_norm", "norm_mlp"]
SMALL_SHARDED = dict(gate_bias=D, conv_dw_w=D, ssd_conv_w=SSD_XBC)
ORDER = ["rel_table", "norm_mix", "w_in", "gate_bias", "conv_dw_w", "conv_dw_b", "conv_ln_g", "conv_ln_b", "w_conv_out",
         "ssd_conv_w", "ssd_conv_b", "ssd_dt_bias", "ssd_A_log", "ssd_D", "ssd_norm_g", "w_ssd_out", "attn_q_norm", "attn_k_norm",
         "attn_sinks", "w_attn_out", "w_mix_out", "norm_xattn", "norm_mem", "w_xq", "w_xkv", "xattn_q_norm", "xattn_k_norm",
         "w_xo", "norm_mlp", "w_mlp_up", "w_mlp_down"]


def finish_reduction(i, partials, arrived, me_arr):
    mine = [add_partials(f"add_partials{i}_{k}", p, got, me_arr) for k, (p, got) in enumerate(zip(partials, arrived))]
    theirs = run_exchange(f"share_halves{i}", ex_share(mine))
    both = [join_halves(f"join_halves{i}_{k}", m, t) for k, (m, t) in enumerate(zip(mine, theirs))]
    return [b.reshape(2 * b.shape[1], b.shape[2]) for b in both]


def kernel(x, mem, rel_table, norm_mix, w_in, gate_bias, conv_dw_w, conv_dw_b, conv_ln_g, conv_ln_b, w_conv_out, ssd_conv_w, ssd_conv_b, ssd_dt_bias, ssd_A_log, ssd_D, ssd_norm_g, w_ssd_out, attn_q_norm, attn_k_norm, attn_sinks, w_attn_out, w_mix_out, norm_xattn, norm_mem, w_xq, w_xkv, xattn_q_norm, xattn_k_norm, w_xo, norm_mlp, w_mlp_up, w_mlp_down, loss_target, m_rel_table, m_norm_mix, m_w_in, m_gate_bias, m_conv_dw_w, m_conv_dw_b, m_conv_ln_g, m_conv_ln_b, m_w_conv_out, m_ssd_conv_w, m_ssd_conv_b, m_ssd_dt_bias, m_ssd_A_log, m_ssd_D, m_ssd_norm_g, m_w_ssd_out, m_attn_q_norm, m_attn_k_norm, m_attn_sinks, m_w_attn_out, m_w_mix_out, m_norm_xattn, m_norm_mem, m_w_xq, m_w_xkv, m_xattn_q_norm, m_xattn_k_norm, m_w_xo, m_norm_mlp, m_w_mlp_up, m_w_mlp_down, v_rel_table, v_norm_mix, v_w_in, v_gate_bias, v_conv_dw_w, v_conv_dw_b, v_conv_ln_g, v_conv_ln_b, v_w_conv_out, v_ssd_conv_w, v_ssd_conv_b, v_ssd_dt_bias, v_ssd_A_log, v_ssd_D, v_ssd_norm_g, v_w_ssd_out, v_attn_q_norm, v_attn_k_norm, v_attn_sinks, v_w_attn_out, v_w_mix_out, v_norm_xattn, v_norm_mem, v_w_xq, v_w_xkv, v_xattn_q_norm, v_xattn_k_norm, v_w_xo, v_norm_mlp, v_w_mlp_up, v_w_mlp_down):
    a = dict(locals())
    wts = {n: a[n] for n in ORDER}
    ms = {n: a["m_" + n] for n in ORDER}
    vs = {n: a["v_" + n] for n in ORDER}
    depth = norm_mix.shape[0]
    ax, ay, ac = lax.axis_index("x"), lax.axis_index("y"), lax.axis_index("c")
    chip = 2 * ax + ay

    small = {n: wts[n] for n in SMALL if n != "rel_table" and n not in SMALL_SHARDED}
    gather_buf, spans, r0 = [], {}, 0
    for n, width in SMALL_SHARDED.items():
        blk = wts[n]
        q = width // N_CHIPS
        whole = lax.dynamic_update_slice(jnp.zeros(blk.shape[:-1] + (width,), f32), blk, (0, 0, chip * q))
        flat = whole.reshape(-1, 128)
        spans[n] = (r0, flat.shape[0], whole.shape)
        r0 += flat.shape[0]
        gather_buf.append(flat)
    summed = allreduce_small("gather_small_params", jnp.concatenate(gather_buf, axis=0))
    for n, (s0, nr, shp) in spans.items():
        small[n] = summed[s0:s0 + nr].reshape(shp) * 0.5

    c_arr = jnp.reshape(ac, (1,)).astype(jnp.int32)
    me_arr = jnp.reshape(chip, (1,)).astype(jnp.int32)
    x0, mem0 = x[0], mem[0]
    bucket = jnp.asarray(_bucket_table())
    bias = relbias_fwd(rel_table, bucket)

    gathered = run_exchange("gather_ici_first", ex_gather_ici(pack_shards(wts, 0)))
    gathered = run_exchange("gather_diag0", ex_gather_diag(run_exchange("gather_d2d_first", ex_gather_d2d(gathered))))
    h, saved, ws = x0, [], []
    for i in range(depth):
        w = unpack_gathered(gathered)
        p = {n: small[n][i] for n in small}
        h, s, gathered = layer_forward(i, h, mem0, w, p, bias, pack_shards(wts, i + 1) if i + 1 < depth else None)
        saved.append(s)
        ws.append((w, p))
    grad_x, loss_tile = loss_head(h, loss_target[0], min(ROW_TILE, h.shape[0]))
    loss = lax.psum(loss_tile[0, 0], ("x", "y", "c"))

    reduced, small_grads, dbiases, pending = [[None] * 3 for _ in range(depth)], [None] * depth, [], None
    for i in reversed(range(depth)):
        w, p = ws[i]
        grad_x, gw, gp, dbias, travelled = layer_backward(i, grad_x, mem0, w, p, bias, saved[i], pending, c_arr)
        done = finish_reduction(i, *travelled, me_arr)
        if pending is not None:
            reduced[i + 1][0] = done.pop(0)
        reduced[i][1], reduced[i][2] = done
        small_grads[i] = gp
        dbiases.append(dbias)
        pending = pack_w_in_grad(gw["w_in"])
    swapped = run_exchange("swap_halves_last", ex_swap_halves(pending))
    partials = [add_halves("add_halves_last", pending[0], swapped[0], c_arr)]
    reduced[0][0], = finish_reduction("_last", partials, run_exchange("scatter_last", ex_scatter(partials)), me_arr)
    sg = {n: jnp.stack([small_grads[i][n].reshape(small[n].shape[1:]) for i in range(depth)]) for n in small}
    sg["rel_table"] = relbias_bwd(dbiases, bucket)

    grads, deltas, new_m, new_v = {}, {}, {}, {}
    sources = dict(w_in=(0, 0), w_xkv=(2, 0), **{n: (1, WIDE_OFF[n]) for n in WIDE})
    for n, (k, r0) in sources.items():
        flip = (lambda t: jnp.transpose(t, (0, 2, 1))) if n == "w_in" else (lambda t: t)
        res = adamw_layers(f"adamw_{n}", flip(wts[n]), flip(ms[n]), flip(vs[n]), [reduced[i][k] for i in range(depth)], r0)
        grads[n], deltas[n], new_m[n], new_v[n] = [flip(r) for r in res]

    parts, spans, r0 = [], {}, 0
    for n in SMALL:
        flat = sg[n].reshape(-1)
        nr = -(-flat.shape[0] // 128)
        nr = -(-nr // 8) * 8
        flat = jnp.pad(flat, (0, nr * 128 - flat.shape[0])).reshape(nr, 128)
        spans[n] = (r0, nr, sg[n].shape)
        r0 += nr
        parts.append(flat)
    summed = allreduce_small("allreduce_small_grads", jnp.concatenate(parts, axis=0))
    for n, (s0, nr, shp) in spans.items():
        size = int(np.prod(shp))
        g = summed[s0:s0 + nr].reshape(-1)[:size].reshape(shp)
        if n in SMALL_SHARDED:
            q = SMALL_SHARDED[n] // N_CHIPS
            g = lax.dynamic_slice_in_dim(g, chip * q, q, axis=g.ndim - 1)
        grads[n] = g

    for n in SMALL:
        shp = wts[n].shape
        two = (lambda t: t.reshape(-1, shp[-1]))
        d, nm, nv = adamw(f"adamw_{n}", two(wts[n]), two(grads[n]), two(ms[n]), two(vs[n]))
        deltas[n], new_m[n], new_v[n] = d.reshape(shp), nm.reshape(shp), nv.reshape(shp)

    return (loss, grad_x[None], *[grads[n] for n in ORDER], *[deltas[n] for n in ORDER],
            *[new_m[n] for n in ORDER], *[new_v[n] for n in ORDER])
```
